```python
import jax
import jax.numpy as jnp
from jax import lax
import numpy as np

D_MODEL = 1024
BATCH = 32
SEQ = 2048
DEPTH = 1

GRID_W = 64
CTX_LEN = 256
N_MOD = 6
F_GROUPS = 4
F_GROUP_DIM = 128
F_WIDTH = F_GROUPS * F_GROUP_DIM
HG_HEADS = 4
HG_DK = 128
HG_DV = 128
HG_KEY_WIDTH = HG_HEADS * HG_DK
HG_WIDTH = HG_HEADS * HG_DV
CHUNK = 64
N_BRANCHES = 2
SPLIT_POINTS = (F_WIDTH, F_WIDTH + HG_KEY_WIDTH, F_WIDTH + 2 * HG_KEY_WIDTH, F_WIDTH + 3 * HG_KEY_WIDTH, F_WIDTH + 3 * HG_KEY_WIDTH + HG_WIDTH, F_WIDTH + 3 * HG_KEY_WIDTH + 2 * HG_WIDTH)
IN_WIDTH = SPLIT_POINTS[-1] + N_BRANCHES * D_MODEL
N_EXPERTS = 32
TOP_K = 4
D_EXPERT = D_MODEL
SWIGLU_LIMIT = 7.0
SWIGLU_ALPHA = 1.702
MOE_BLOCK = 256
DEEPNORM_ALPHA = (2.0 * DEPTH) ** 0.25
DEEPNORM_BETA = (8.0 * DEPTH) ** -0.25
LN_EPS = 1e-5
RMS_EPS = 1e-6

kernel_name = 'hybrid_fourier_hgrn2_moe_dit_block'


def _layer_norm(x, g, b):
    xf = x.astype(jnp.float32)
    mu = jnp.mean(xf, axis=-1, keepdims=True)
    xc = xf - mu
    var = jnp.mean(xc * xc, axis=-1, keepdims=True)
    return (xc * lax.rsqrt(var + LN_EPS) * g.astype(jnp.float32) + b.astype(jnp.float32)).astype(x.dtype)


def _split_in(p):
    return jnp.split(p, SPLIT_POINTS, axis=-1)


def _fourier_latent(t):
    b, length, _ = t.shape
    rows = length // GRID_W
    g = t.astype(jnp.float32).reshape(b, rows, GRID_W, F_GROUPS, F_GROUP_DIM)
    y = jnp.fft.fftn(g, axes=(1, 2, 4), norm='ortho').real
    return y.reshape(b, length, F_WIDTH).astype(t.dtype)


def _fourier_context(t):
    b, length, _ = t.shape
    g = t.astype(jnp.float32).reshape(b, length, F_GROUPS, F_GROUP_DIM)
    y = jnp.fft.fftn(g, axes=(1, 3), norm='ortho').real
    return y.reshape(b, length, F_WIDTH).astype(t.dtype)


def _to_heads(t, d):
    b, length, _ = t.shape
    return t.reshape(b, length, HG_HEADS, d).transpose(0, 2, 1, 3)


def _forget_gate(z, lb):
    f = lb + (1.0 - lb) * jax.nn.sigmoid(z.astype(jnp.float32))
    return _to_heads(1.0 - f, HG_DK), _to_heads(jnp.log(f), HG_DK)


def _gla_chunked(q, k, v, log_f, s0):
    b, h, length, dk = q.shape
    dv = v.shape[-1]
    n = length // CHUNK
    q, k, log_f = (t.reshape(b, h, n, CHUNK, dk) for t in (q, k, log_f))
    v = v.reshape(b, h, n, CHUNK, dv)
    cum = jnp.cumsum(log_f, axis=3)
    cum_last = cum[:, :, :, -1:, :]
    q_dec = q * jnp.exp(cum)
    k_inv = k * jnp.exp(-cum)
    k_end = k * jnp.exp(cum_last - cum)
    scores = jnp.einsum('bhncd,bhnsd->bhncs', q_dec, k_inv)
    tri = jnp.tril(jnp.ones((CHUNK, CHUNK), dtype=bool))
    scores = jnp.where(tri, scores, 0.0)
    o_intra = jnp.einsum('bhncs,bhnse->bhnce', scores, v)
    chunk_decay = jnp.exp(cum_last[:, :, :, 0, :])

    def step(state, inp):
        q_c, k_c, v_c, g_c = inp
        o_c = jnp.einsum('bhcd,bhde->bhce', q_c, state)
        state = g_c[..., None] * state + jnp.einsum('bhcd,bhce->bhde', k_c, v_c)
        return state, o_c

    xs = tuple(jnp.moveaxis(t, 2, 0) for t in (q_dec, k_end, v, chunk_decay))
    s_final, o_inter = lax.scan(step, s0, xs)
    o = o_intra + jnp.moveaxis(o_inter, 0, 2)
    return o.reshape(b, h, length, dv), s_final


def _hgrn2_bidir(pq, pzf, pzb, pv, lb, s0_f, s0_b):
    q = _to_heads(jax.nn.silu(pq.astype(jnp.float32)), HG_DK)
    v = _to_heads(pv.astype(jnp.float32), HG_DV)
    k_f, lf_f = _forget_gate(pzf, lb[0])
    k_b, lf_b = _forget_gate(pzb, lb[1])
    o_f, s_f = _gla_chunked(q, k_f, v, lf_f, s0_f)
    rev = lambda t: jnp.flip(t, axis=2)
    o_b, s_b = _gla_chunked(rev(q), rev(k_b), rev(v), rev(lf_b), s0_b)
    return o_f + rev(o_b), s_f, s_b


def _hgrn2_readout(o, pg, norm_g):
    b, _, length, _ = o.shape
    o = o.transpose(0, 2, 1, 3)
    o = o * lax.rsqrt(jnp.mean(o * o, axis=-1, keepdims=True) + RMS_EPS) * norm_g.astype(jnp.float32).reshape(HG_HEADS, HG_DV)
    y = o.reshape(b, length, HG_WIDTH) * jax.nn.silu(pg.astype(jnp.float32))
    return y.astype(pg.dtype)


def _merge(y_f, y_h, gates, w_fo, w_ho, w_o):
    g_f, g_h = jnp.split(gates, N_BRANCHES, axis=-1)
    m = jax.nn.sigmoid(g_f) * (y_f @ w_fo) + jax.nn.sigmoid(g_h) * (y_h @ w_ho)
    return m @ w_o


def _moe(h, w_router, b_router, w_gu, b_gu, w_down, b_down):
    n_tok, d = h.shape
    logits = (h @ w_router).astype(jnp.float32) + b_router.astype(jnp.float32)
    top_val, top_idx = lax.top_k(logits, TOP_K)
    top_w = jax.nn.softmax(top_val, axis=-1)
    n_asg = n_tok * TOP_K
    e_flat = top_idx.reshape(n_asg)
    order = jnp.argsort(e_flat)
    e_sorted = e_flat[order]
    tok_sorted = (order // TOP_K).astype(jnp.int32)
    w_sorted = top_w.reshape(n_asg)[order]
    counts = jnp.bincount(e_flat, length=N_EXPERTS)
    starts = jnp.cumsum(counts) - counts
    padded = (counts + MOE_BLOCK - 1) // MOE_BLOCK * MOE_BLOCK
    pad_ends = jnp.cumsum(padded)
    dest = pad_ends[e_sorted] - padded[e_sorted] + jnp.arange(n_asg) - starts[e_sorted]
    n_blocks = -(-n_asg // MOE_BLOCK) + N_EXPERTS
    n_rows = n_blocks * MOE_BLOCK
    row_tok = jnp.full((n_rows,), n_tok, dtype=jnp.int32).at[dest].set(tok_sorted)
    row_w = jnp.zeros((n_rows,), jnp.float32).at[dest].set(w_sorted)
    block_expert = jnp.minimum(jnp.searchsorted(pad_ends, jnp.arange(n_blocks) * MOE_BLOCK, side='right'), N_EXPERTS - 1)
    h_pad = jnp.concatenate([h, jnp.zeros((1, d), h.dtype)], axis=0)

    def expert_block(args):
        toks, e = args
        gu = h_pad[toks] @ w_gu[e] + b_gu[e]
        gate, up = jnp.split(gu, 2, axis=-1)
        gate = jnp.minimum(gate, SWIGLU_LIMIT)
        up = jnp.clip(up, -SWIGLU_LIMIT, SWIGLU_LIMIT)
        act = (up + 1.0) * gate * jax.nn.sigmoid(SWIGLU_ALPHA * gate)
        return act @ w_down[e] + b_down[e]

    out = lax.map(expert_block, (row_tok.reshape(n_blocks, MOE_BLOCK), block_expert))
    out = out.reshape(n_rows, d).astype(jnp.float32) * row_w[:, None]
    y = jax.ops.segment_sum(out, row_tok, num_segments=n_tok + 1)[:n_tok]
    return y.astype(h.dtype)


def setup_inputs(seed: int = 0) -> dict:
    key = jax.random.key(seed)
    ks = jax.random.split(key, 24)
    nrm = lambda k, shape, scale: jax.random.normal(k, shape, jnp.float32) * scale
    return {
        'x': nrm(ks[0], (BATCH, SEQ, D_MODEL), 1.0),
        'c': nrm(ks[1], (BATCH, D_MODEL), 1.0),
        'ctx': nrm(ks[2], (BATCH, CTX_LEN, D_MODEL), 1.0),
        'c_ctx': nrm(ks[3], (D_MODEL,), 1.0),
        'w_ada': nrm(ks[4], (DEPTH, D_MODEL, N_MOD * D_MODEL), 0.5 * D_MODEL ** -0.5),
        'b_ada': nrm(ks[5], (DEPTH, N_MOD * D_MODEL), 0.02),
        'w_in': nrm(ks[6], (DEPTH, D_MODEL, IN_WIDTH), D_MODEL ** -0.5),
        'lb_raw': nrm(ks[7], (DEPTH + 1, 2, HG_KEY_WIDTH), 0.1),
        'hg_norm_g': 1.0 + nrm(ks[8], (DEPTH, HG_WIDTH), 0.02),
        'w_four_out': nrm(ks[9], (DEPTH, F_WIDTH, D_MODEL), F_WIDTH ** -0.5),
        'w_hg_out': nrm(ks[10], (DEPTH, HG_WIDTH, D_MODEL), HG_WIDTH ** -0.5),
        'w_o': nrm(ks[11], (DEPTH, D_MODEL, D_MODEL), DEEPNORM_BETA * D_MODEL ** -0.5),
        'ln1_g': 1.0 + nrm(ks[12], (DEPTH, D_MODEL), 0.02),
        'ln1_b': nrm(ks[13], (DEPTH, D_MODEL), 0.02),
        'w_router': nrm(ks[14], (DEPTH, D_MODEL, N_EXPERTS), D_MODEL ** -0.5),
        'b_router': nrm(ks[15], (DEPTH, N_EXPERTS), 0.01),
        'w_gate_up': nrm(ks[16], (DEPTH, N_EXPERTS, D_MODEL, 2 * D_EXPERT), D_MODEL ** -0.5),
        'b_gate_up': nrm(ks[17], (DEPTH, N_EXPERTS, 2 * D_EXPERT), 0.02),
        'w_down': nrm(ks[18], (DEPTH, N_EXPERTS, D_EXPERT, D_MODEL), DEEPNORM_BETA * D_EXPERT ** -0.5),
        'b_down': nrm(ks[19], (DEPTH, N_EXPERTS, D_MODEL), 0.02),
        'ln2_g': 1.0 + nrm(ks[20], (DEPTH, D_MODEL), 0.02),
        'ln2_b': nrm(ks[21], (DEPTH, D_MODEL), 0.02),
    }


def reference(x, c, ctx, c_ctx, w_ada, b_ada, w_in, lb_raw, hg_norm_g, w_four_out, w_hg_out, w_o, ln1_g, ln1_b, w_router, b_router, w_gate_up, b_gate_up, w_down, b_down, ln2_g, ln2_b):
    b = x.shape[0]
    lower_bounds = jnp.cumsum(jax.nn.softmax(lb_raw.astype(jnp.float32), axis=0), axis=0)
    c_act = jax.nn.silu(c)
    c_ctx_act = jax.nn.silu(c_ctx)
    xc = ctx
    for l in range(DEPTH):
        last = l == DEPTH - 1
        shift1, scale1, gate1, shift2, scale2, gate2 = (t[:, None, :] for t in jnp.split(c_act @ w_ada[l] + b_ada[l], N_MOD, axis=-1))
        cshift1, cscale1, cgate1, cshift2, cscale2, cgate2 = jnp.split(c_ctx_act @ w_ada[l] + b_ada[l], N_MOD, axis=-1)
        lb = lower_bounds[l]
        u = x * (1.0 + scale1) + shift1
        uc = xc * (1.0 + cscale1) + cshift1
        pf, pq, pzf, pzb, pv, pg, gates = _split_in(u @ w_in[l])
        cpf, cpq, cpzf, cpzb, cpv, cpg, cgates = _split_in(uc @ w_in[l])
        zero_state = jnp.zeros((b, HG_HEADS, HG_DK, HG_DV), jnp.float32)
        oc, s_f, s_b = _hgrn2_bidir(cpq, cpzf, cpzb, cpv, lb, zero_state, zero_state)
        o, _, _ = _hgrn2_bidir(pq, pzf, pzb, pv, lb, s_f, s_b)
        mix = _merge(_fourier_latent(pf), _hgrn2_readout(o, pg, hg_norm_g[l]), gates, w_four_out[l], w_hg_out[l], w_o[l])
        x = _layer_norm(DEEPNORM_ALPHA * x + gate1 * mix, ln1_g[l], ln1_b[l])
        u2 = x * (1.0 + scale2) + shift2
        ff = _moe(u2.reshape(-1, D_MODEL), w_router[l], b_router[l], w_gate_up[l], b_gate_up[l], w_down[l], b_down[l]).reshape(x.shape)
        x = _layer_norm(DEEPNORM_ALPHA * x + gate2 * ff, ln2_g[l], ln2_b[l])
        if not last:
            mix_c = _merge(_fourier_context(cpf), _hgrn2_readout(oc, cpg, hg_norm_g[l]), cgates, w_four_out[l], w_hg_out[l], w_o[l])
            xc = _layer_norm(DEEPNORM_ALPHA * xc + cgate1 * mix_c, ln1_g[l], ln1_b[l])
            u2c = xc * (1.0 + cscale2) + cshift2
            ffc = _moe(u2c.reshape(-1, D_MODEL), w_router[l], b_router[l], w_gate_up[l], b_gate_up[l], w_down[l], b_down[l]).reshape(xc.shape)
            xc = _layer_norm(DEEPNORM_ALPHA * xc + cgate2 * ffc, ln2_g[l], ln2_b[l])
    return x
```

```python
import functools
import math

import jax
import jax.numpy as jnp
import numpy as np
from jax import lax
from jax.experimental import pallas as pl
from jax.experimental.pallas import tpu as pltpu

F32 = jnp.float32
BF16 = jnp.bfloat16

GRID_W = 64
N_MOD = 6
F_GROUPS = 4
F_GROUP_DIM = 128
HG_HEADS = 4
HG_D = 128
CHUNK = 64
N_EXPERTS = 32
TOP_K = 4
SWIGLU_LIMIT = 7.0
SWIGLU_ALPHA = 1.702
LN_EPS = 1e-5
RMS_EPS = 1e-6
DEPTH = 1
DEEPNORM_ALPHA = (2.0 * DEPTH) ** 0.25

LANES = 128
MOE_ROWS = 512
VMEM_LIMIT = 56 * 1024 * 1024


def _cparams(sem, vmem=None):
    return pltpu.CompilerParams(dimension_semantics=sem, vmem_limit_bytes=vmem)


def _dot(a, b):
    return jnp.dot(a, b, preferred_element_type=F32)


def _sigmoid(x):
    return 1.0 / (1.0 + jnp.exp(-x))


def _ada_kernel(c_ref, w_ref, b_ref, o_ref):
    c = c_ref[...]
    a = c * _sigmoid(c)
    o_ref[...] = jnp.dot(a, w_ref[...], preferred_element_type=F32,
                         precision=lax.Precision.HIGHEST) + b_ref[...]


def _ada(c_rows, w, b):
    r, d = c_rows.shape
    n = w.shape[1]
    tn = 512
    return pl.pallas_call(
        _ada_kernel,
        grid=(n // tn,),
        in_specs=[pl.BlockSpec((r, d), lambda j: (0, 0)),
                  pl.BlockSpec((d, tn), lambda j: (0, j)),
                  pl.BlockSpec((1, tn), lambda j: (0, j))],
        out_specs=pl.BlockSpec((r, tn), lambda j: (0, j)),
        out_shape=jax.ShapeDtypeStruct((r, n), F32),
        compiler_params=_cparams(("arbitrary",)),
        name="ada_mod",
    )(c_rows, w, b.reshape(1, n))


def _inproj_kernel(x_ref, sc_ref, sh_ref, w_ref, *o_refs, col_chunks):
    u = (x_ref[0] * (1.0 + sc_ref[0]) + sh_ref[0]).astype(BF16)
    for o_ref, chunks in zip(o_refs, col_chunks):
        for (w0, o0, n) in chunks:
            o_ref[0, :, o0:o0 + n] = _dot(u, w_ref[:, w0:w0 + n]).astype(o_ref.dtype)


def _inproj(x, scale, shift, w_bf16, outs, tm):
    b, s, d = x.shape
    n_w = w_bf16.shape[1]
    col_chunks = []
    for (c0, width, _) in outs:
        step = min(width, 512)
        col_chunks.append(tuple((c0 + o, o, step) for o in range(0, width, step)))
    kern = functools.partial(_inproj_kernel, col_chunks=tuple(col_chunks))
    return pl.pallas_call(
        kern,
        grid=(b, s // tm),
        in_specs=[pl.BlockSpec((1, tm, d), lambda i, j: (i, j, 0)),
                  pl.BlockSpec((1, 1, d), lambda i, j: (i, 0, 0)),
                  pl.BlockSpec((1, 1, d), lambda i, j: (i, 0, 0)),
                  pl.BlockSpec((d, n_w), lambda i, j: (0, 0))],
        out_specs=[pl.BlockSpec((1, tm, width), lambda i, j: (i, j, 0)) for (_, width, _) in outs],
        out_shape=[jax.ShapeDtypeStruct((b, s, width), dt) for (_, width, dt) in outs],
        compiler_params=_cparams(("parallel", "parallel"), VMEM_LIMIT),
        name="in_proj",
    )(x, scale, shift, w_bf16)


def _tri_cumsum(tri, lf):
    hi = lf.astype(BF16)
    r1 = lf - hi.astype(F32)
    mid = r1.astype(BF16)
    lo = (r1 - mid.astype(F32)).astype(BF16)
    return _dot(tri, hi) + _dot(tri, mid) + _dot(tri, lo)


def _gla_chunk(z, v, lb, state_t, tri, tot_row, q=None, keep=None):
    f = lb + (1.0 - lb) * _sigmoid(z)
    k = 1.0 - f
    cum = _tri_cumsum(tri, jnp.log(f))
    tot = cum[tot_row:tot_row + 1, :]
    k_inv = k * jnp.exp(-cum)
    k_end = (k_inv * jnp.exp(tot)).astype(BF16)
    o = None
    if q is not None:
        q_dec = (q * jnp.exp(cum)).astype(BF16)
        scores = lax.dot_general(q_dec, k_inv.astype(BF16), (((1,), (1,)), ((), ())),
                                 preferred_element_type=F32)
        scores = jnp.where(keep, scores, 0.0).astype(BF16)
        o = _dot(scores, v) + lax.dot_general(q_dec, state_t.astype(BF16), (((1,), (1,)), ((), ())),
                                              preferred_element_type=F32)
    upd = lax.dot_general(v, k_end, (((0,), (0,)), ((), ())), preferred_element_type=F32)
    return o, state_t * jnp.exp(tot) + upd


def _hgrn_kernel(q_ref, zf_ref, zb_ref, v_ref, g_ref, czf_ref, czb_ref, cv_ref, lb_ref, ng_ref,
                 y_ref, of_ref, ob_ref, *, seq, ctx_len):
    c = CHUNK
    n_lat = seq // c
    n_ctx = ctx_len // c
    row = lax.broadcasted_iota(jnp.int32, (c, c), 0)
    col = lax.broadcasted_iota(jnp.int32, (c, c), 1)
    lower = row >= col
    upper = row <= col
    tri_f = jnp.where(lower, 1.0, 0.0).astype(BF16)
    tri_b = jnp.where(upper, 1.0, 0.0).astype(BF16)
    lb_f = lb_ref[0:1, :]
    lb_b = lb_ref[1:2, :]

    def ctx_body(i, carry):
        s_f, s_b = carry
        sl_f = pl.ds(pl.multiple_of(i * c, c), c)
        sl_b = pl.ds(pl.multiple_of((n_ctx - 1 - i) * c, c), c)
        _, s_f = _gla_chunk(czf_ref[0, sl_f, :], cv_ref[0, sl_f, :], lb_f, s_f, tri_f, c - 1)
        _, s_b = _gla_chunk(czb_ref[0, sl_b, :], cv_ref[0, sl_b, :], lb_b, s_b, tri_b, 0)
        return s_f, s_b

    zero = jnp.zeros((HG_D, HG_D), F32)
    carry = lax.fori_loop(0, n_ctx, ctx_body, (zero, zero))

    def lat_body(i, carry):
        s_f, s_b = carry
        sl_f = pl.ds(pl.multiple_of(i * c, c), c)
        sl_b = pl.ds(pl.multiple_of((n_lat - 1 - i) * c, c), c)
        qf = q_ref[0, sl_f, :].astype(F32)
        qb = q_ref[0, sl_b, :].astype(F32)
        qf = qf * _sigmoid(qf)
        qb = qb * _sigmoid(qb)
        o_f, s_f = _gla_chunk(zf_ref[0, sl_f, :], v_ref[0, sl_f, :], lb_f, s_f, tri_f, c - 1, qf, lower)
        o_b, s_b = _gla_chunk(zb_ref[0, sl_b, :], v_ref[0, sl_b, :], lb_b, s_b, tri_b, 0, qb, upper)
        of_ref[sl_f, :] = o_f
        ob_ref[sl_b, :] = o_b
        return s_f, s_b

    lax.fori_loop(0, n_lat, lat_body, carry)

    ng = ng_ref[...]
    rt = 256

    def out_body(i, _):
        sl = pl.ds(pl.multiple_of(i * rt, rt), rt)
        o = of_ref[sl, :] + ob_ref[sl, :]
        o = o * lax.rsqrt(jnp.mean(o * o, axis=-1, keepdims=True) + RMS_EPS) * ng
        g = g_ref[0, sl, :].astype(F32)
        y_ref[0, sl, :] = (o * (g * _sigmoid(g))).astype(y_ref.dtype)
        return 0

    lax.fori_loop(0, seq // rt, out_body, 0)


def _hgrn(q, z, v, g, cz, cv, lb, ng):
    b, s, _ = q.shape
    lc = cz.shape[1]
    h = HG_HEADS
    d = HG_D
    kern = functools.partial(_hgrn_kernel, seq=s, ctx_len=lc)
    head = lambda i, j: (i, 0, j)
    return pl.pallas_call(
        kern,
        grid=(b, h),
        in_specs=[pl.BlockSpec((1, s, d), head),
                  pl.BlockSpec((1, s, d), head),
                  pl.BlockSpec((1, s, d), lambda i, j: (i, 0, j + h)),
                  pl.BlockSpec((1, s, d), head),
                  pl.BlockSpec((1, s, d), head),
                  pl.BlockSpec((1, lc, d), head),
                  pl.BlockSpec((1, lc, d), lambda i, j: (i, 0, j + h)),
                  pl.BlockSpec((1, lc, d), head),
                  pl.BlockSpec((2, d), lambda i, j: (0, j)),
                  pl.BlockSpec((1, d), lambda i, j: (0, j))],
        out_specs=pl.BlockSpec((1, s, d), head),
        out_shape=jax.ShapeDtypeStruct((b, s, h * d), BF16),
        scratch_shapes=[pltpu.VMEM((s, d), F32), pltpu.VMEM((s, d), F32)],
        compiler_params=_cparams(("parallel", "parallel"), VMEM_LIMIT),
        name="hgrn2",
    )(q, z, z, v, g, cz, cz, cv, lb, ng)


def _dft_tables(seq):
    rows = seq // GRID_W
    n = F_GROUP_DIM
    kc = np.outer(np.arange(n), np.arange(n)) % n
    ang = 2.0 * np.pi * kc / n
    norm = 1.0 / math.sqrt(rows * GRID_W * n)
    ch = np.concatenate([np.cos(ang), -np.sin(ang)], axis=1) * norm
    t = np.arange(seq)
    r, w = t // GRID_W, t % GRID_W
    m = (np.outer(r, r) * GRID_W + np.outer(w, w) * rows) % (rows * GRID_W)
    ang_t = 2.0 * np.pi * m / (rows * GRID_W)
    tok = np.concatenate([np.cos(ang_t), np.sin(ang_t)], axis=1)
    return jnp.asarray(ch, BF16), jnp.asarray(tok, BF16)


def _fourier_kernel(p_ref, ch_ref, tok_ref, y_ref, xs_ref, *, seq):
    for gi in range(F_GROUPS):
        cols = slice(gi * F_GROUP_DIM, (gi + 1) * F_GROUP_DIM)
        x1 = _dot(p_ref[0, :, cols], ch_ref[...]).astype(BF16)
        xs_ref[0:seq, cols] = x1[:, :F_GROUP_DIM]
        xs_ref[seq:2 * seq, cols] = x1[:, F_GROUP_DIM:]
    y_ref[0] = _dot(tok_ref[...], xs_ref[...]).astype(y_ref.dtype)


def _fourier(pf):
    b, s, wdt = pf.shape
    ch, tok = _dft_tables(s)
    kern = functools.partial(_fourier_kernel, seq=s)
    return pl.pallas_call(
        kern,
        grid=(b,),
        in_specs=[pl.BlockSpec((1, s, wdt), lambda i: (i, 0, 0)),
                  pl.BlockSpec(ch.shape, lambda i: (0, 0)),
                  pl.BlockSpec(tok.shape, lambda i: (0, 0), pipeline_mode=pl.Buffered(1))],
        out_specs=pl.BlockSpec((1, s, wdt), lambda i: (i, 0, 0)),
        out_shape=jax.ShapeDtypeStruct((b, s, wdt), BF16),
        scratch_shapes=[pltpu.VMEM((2 * s, wdt), BF16)],
        compiler_params=_cparams(("parallel",), VMEM_LIMIT),
        name="fourier",
    )(pf, ch, tok)


def _layer_norm(t, g, b):
    mu = jnp.mean(t, axis=-1, keepdims=True)
    tc = t - mu
    var = jnp.mean(tc * tc, axis=-1, keepdims=True)
    return tc * lax.rsqrt(var + LN_EPS) * g + b


def _merge_kernel(yf_ref, yh_ref, gt_ref, x_ref, g1_ref, sc2_ref, sh2_ref, wfo_ref, who_ref, wo_ref,
                  lg_ref, lbias_ref, wr_ref, br_ref, x1_ref, idx_ref, wt_ref):
    d = x_ref.shape[-1]
    gf = _sigmoid(gt_ref[0, :, :d].astype(F32))
    gh = _sigmoid(gt_ref[0, :, d:].astype(F32))
    m = gf * _dot(yf_ref[0], wfo_ref[...]) + gh * _dot(yh_ref[0], who_ref[...])
    mix = _dot(m.astype(BF16), wo_ref[...])
    x1 = _layer_norm(DEEPNORM_ALPHA * x_ref[0] + g1_ref[0] * mix, lg_ref[...], lbias_ref[...])
    x1_ref[0] = x1
    u2 = x1 * (1.0 + sc2_ref[0]) + sh2_ref[0]
    logits = jnp.dot(u2, wr_ref[...], preferred_element_type=F32,
                     precision=lax.Precision.HIGHEST) + br_ref[...]
    lane = lax.broadcasted_iota(jnp.int32, logits.shape, 1)
    work = jnp.where(lane < N_EXPERTS, logits, -jnp.inf)
    vals, idxs = [], []
    for _ in range(TOP_K):
        mx = jnp.max(work, axis=-1, keepdims=True)
        sel = jnp.min(jnp.where(work == mx, lane, LANES), axis=-1, keepdims=True)
        vals.append(mx)
        idxs.append(sel)
        work = jnp.where(lane == sel, -jnp.inf, work)
    exps = [jnp.exp(vv - vals[0]) for vv in vals]
    inv = 1.0 / (exps[0] + exps[1] + exps[2] + exps[3])
    for kk in range(TOP_K):
        idx_ref[0, :, kk:kk + 1] = idxs[kk]
        wt_ref[0, :, kk:kk + 1] = exps[kk] * inv


def _merge(yf, yh, gates, x, g1, sc2, sh2, wfo, who, wo, lg, lbias, wr, br, tm):
    b, s, d = x.shape
    tile = lambda i, j: (i, j, 0)
    per_b = lambda i, j: (i, 0, 0)
    const = lambda i, j: (0, 0)
    return pl.pallas_call(
        _merge_kernel,
        grid=(b, s // tm),
        in_specs=[pl.BlockSpec((1, tm, yf.shape[-1]), tile),
                  pl.BlockSpec((1, tm, yh.shape[-1]), tile),
                  pl.BlockSpec((1, tm, gates.shape[-1]), tile),
                  pl.BlockSpec((1, tm, d), tile),
                  pl.BlockSpec((1, 1, d), per_b),
                  pl.BlockSpec((1, 1, d), per_b),
                  pl.BlockSpec((1, 1, d), per_b),
                  pl.BlockSpec(wfo.shape, const),
                  pl.BlockSpec(who.shape, const),
                  pl.BlockSpec(wo.shape, const),
                  pl.BlockSpec((1, d), const),
                  pl.BlockSpec((1, d), const),
                  pl.BlockSpec(wr.shape, const),
                  pl.BlockSpec(br.shape, const)],
        out_specs=[pl.BlockSpec((1, tm, d), tile),
                   pl.BlockSpec((1, tm, TOP_K), tile),
                   pl.BlockSpec((1, tm, TOP_K), tile)],
        out_shape=[jax.ShapeDtypeStruct((b, s, d), F32),
                   jax.ShapeDtypeStruct((b, s, TOP_K), jnp.int32),
                   jax.ShapeDtypeStruct((b, s, TOP_K), F32)],
        compiler_params=_cparams(("parallel", "parallel"), VMEM_LIMIT),
        name="merge_ln_router",
    )(yf, yh, gates, x, g1, sc2, sh2, wfo, who, wo, lg, lbias, wr, br)


def _lane_cumsum_exclusive(x):
    lane = lax.broadcasted_iota(jnp.int32, x.shape, 1)
    inc = x
    sh = 1
    while sh < LANES:
        inc = inc + jnp.where(lane >= sh, pltpu.roll(inc, sh, 1), 0)
        sh *= 2
    return inc - x


def _route_kernel(idx_ref, dest_ref, cnt_ref, start_ref, run_ref, *, block_rows):
    phase = pl.program_id(0)
    t = pl.program_id(1)
    tt = idx_ref.shape[0]
    lane = lax.broadcasted_iota(jnp.int32, (tt, LANES), 1)
    onehots = [jnp.where(lane == idx_ref[:, kk:kk + 1], 1.0, 0.0) for kk in range(TOP_K)]
    colsums = [jnp.sum(oh, axis=0, keepdims=True) for oh in onehots]

    @pl.when(jnp.logical_and(phase == 0, t == 0))
    def _():
        run_ref[...] = jnp.zeros_like(run_ref)

    @pl.when(phase == 0)
    def _():
        run_ref[...] += colsums[0] + colsums[1] + colsums[2] + colsums[3]

    @pl.when(jnp.logical_and(phase == 1, t == 0))
    def _():
        cnt = run_ref[...].astype(jnp.int32)
        padded = (cnt + (block_rows - 1)) // block_rows * block_rows
        cnt_ref[...] = cnt
        start_ref[...] = _lane_cumsum_exclusive(padded)
        run_ref[...] = jnp.zeros_like(run_ref)

    @pl.when(phase == 1)
    def _():
        r = lax.broadcasted_iota(jnp.int32, (tt, tt), 0)
        c = lax.broadcasted_iota(jnp.int32, (tt, tt), 1)
        strict = jnp.where(r > c, 1.0, 0.0).astype(BF16)
        base = run_ref[...] + start_ref[...].astype(F32)
        for kk in range(TOP_K):
            prefix = _dot(strict, onehots[kk].astype(BF16))
            rank = jnp.sum(onehots[kk] * (prefix + base), axis=-1, keepdims=True)
            dest_ref[:, kk:kk + 1] = rank.astype(jnp.int32)
            base = base + colsums[kk]
        run_ref[...] = base - start_ref[...].astype(F32)


def _route(idx, block_rows, tt):
    n = idx.shape[0]
    kern = functools.partial(_route_kernel, block_rows=block_rows)
    return pl.pallas_call(
        kern,
        grid=(2, n // tt),
        in_specs=[pl.BlockSpec((tt, TOP_K), lambda p, t: (t, 0))],
        out_specs=[pl.BlockSpec((tt, TOP_K), lambda p, t: (t * p, 0)),
                   pl.BlockSpec((1, LANES), lambda p, t: (0, 0)),
                   pl.BlockSpec((1, LANES), lambda p, t: (0, 0))],
        out_shape=[jax.ShapeDtypeStruct((n, TOP_K), jnp.int32),
                   jax.ShapeDtypeStruct((1, LANES), jnp.int32),
                   jax.ShapeDtypeStruct((1, LANES), jnp.int32)],
        scratch_shapes=[pltpu.VMEM((1, LANES), F32)],
        compiler_params=_cparams(("arbitrary", "arbitrary")),
        name="route_ranks",
    )(idx)


def _dispatch_kernel(dest_ref, x1_ref, sc_ref, sh_ref, zeros_ref, h_ref, u_ref, sem):
    del zeros_ref
    tt = x1_ref.shape[0]
    u_ref[...] = x1_ref[...] * (1.0 + sc_ref[0]) + sh_ref[0]

    def row_copy(i, kk):
        d = dest_ref[0, 0, i * TOP_K + kk]
        return pltpu.make_async_copy(u_ref.at[pl.ds(i, 1), :], h_ref.at[pl.ds(d, 1), :], sem)

    def issue(i, _):
        for kk in range(TOP_K):
            row_copy(i, kk).start()
        return 0

    lax.fori_loop(0, tt, issue, 0)

    def drain(i, _):
        for kk in range(TOP_K):
            row_copy(i, kk).wait()
        return 0

    lax.fori_loop(0, tt, drain, 0)


def _dispatch(dest_flat, x1, sc2, sh2, n_rows, tt):
    n, d = x1.shape
    b = sc2.shape[0]
    tiles_per_b = (n // b) // tt
    zeros = jnp.zeros((n_rows, d), F32)
    per_b = lambda t: (t // tiles_per_b, 0, 0)
    return pl.pallas_call(
        _dispatch_kernel,
        grid=(n // tt,),
        in_specs=[pl.BlockSpec((1, 1, tt * TOP_K), lambda t: (t, 0, 0), memory_space=pltpu.SMEM),
                  pl.BlockSpec((tt, d), lambda t: (t, 0)),
                  pl.BlockSpec((1, 1, d), per_b),
                  pl.BlockSpec((1, 1, d), per_b),
                  pl.BlockSpec(memory_space=pl.ANY)],
        out_specs=pl.BlockSpec(memory_space=pl.ANY),
        out_shape=jax.ShapeDtypeStruct((n_rows, d), F32),
        scratch_shapes=[pltpu.VMEM((tt, d), F32), pltpu.SemaphoreType.DMA(())],
        input_output_aliases={4: 0},
        compiler_params=_cparams(("arbitrary",)),
        name="moe_dispatch",
    )(dest_flat, x1, sc2, sh2, zeros)


def _expert_kernel(be_ref, nu_ref, h_ref, wgu_ref, bgu_ref, wdn_ref, bdn_ref, o_ref):
    del be_ref
    d = h_ref.shape[-1]

    @pl.when(pl.program_id(0) < nu_ref[0])
    def _():
        gu = _dot(h_ref[...].astype(BF16), wgu_ref[0]) + bgu_ref[0]
        gate = jnp.minimum(gu[:, :d], SWIGLU_LIMIT)
        up = jnp.clip(gu[:, d:], -SWIGLU_LIMIT, SWIGLU_LIMIT)
        act = (up + 1.0) * gate * _sigmoid(SWIGLU_ALPHA * gate)
        o_ref[...] = _dot(act.astype(BF16), wdn_ref[0]) + bdn_ref[0]


def _experts(h_sorted, block_expert, n_used, wgu, bgu, wdn, bdn):
    n_rows, d = h_sorted.shape
    bm = MOE_ROWS
    n_blocks = n_rows // bm
    e = wgu.shape[0]
    blk = lambda j, be, nu: (jnp.minimum(j, nu[0] - 1), 0)
    exp = lambda j, be, nu: (be[jnp.minimum(j, nu[0] - 1)], 0, 0)
    grid_spec = pltpu.PrefetchScalarGridSpec(
        num_scalar_prefetch=2,
        grid=(n_blocks,),
        in_specs=[pl.BlockSpec((bm, d), blk),
                  pl.BlockSpec((1, d, 2 * d), exp),
                  pl.BlockSpec((1, 1, 2 * d), exp),
                  pl.BlockSpec((1, d, d), exp),
                  pl.BlockSpec((1, 1, d), exp)],
        out_specs=pl.BlockSpec((bm, d), blk),
    )
    return pl.pallas_call(
        _expert_kernel,
        grid_spec=grid_spec,
        out_shape=jax.ShapeDtypeStruct((n_rows, d), F32),
        compiler_params=_cparams(("arbitrary",), VMEM_LIMIT),
        name="moe_experts",
    )(block_expert, n_used, h_sorted, wgu, bgu.reshape(e, 1, 2 * d), wdn, bdn.reshape(e, 1, d))


def _combine_kernel(dest_ref, wt_ref, x1_ref, g2_ref, lg_ref, lbias_ref, o_hbm, y_ref, buf_ref, sem):
    tt = x1_ref.shape[0]

    def row_copy(i, kk):
        d = dest_ref[0, 0, i * TOP_K + kk]
        return pltpu.make_async_copy(o_hbm.at[pl.ds(d, 1), :], buf_ref.at[kk, pl.ds(i, 1), :], sem)

    def issue(i, _):
        for kk in range(TOP_K):
            row_copy(i, kk).start()
        return 0

    lax.fori_loop(0, tt, issue, 0)

    def drain(i, _):
        for kk in range(TOP_K):
            row_copy(i, kk).wait()
        return 0

    lax.fori_loop(0, tt, drain, 0)

    ff = wt_ref[:, 0:1] * buf_ref[0]
    for kk in range(1, TOP_K):
        ff = ff + wt_ref[:, kk:kk + 1] * buf_ref[kk]
    y_ref[...] = _layer_norm(DEEPNORM_ALPHA * x1_ref[...] + g2_ref[0] * ff, lg_ref[...], lbias_ref[...])


def _combine(dest_flat, wt, x1, g2, lg, lbias, o_sorted, tt):
    n, d = x1.shape
    b = g2.shape[0]
    tiles_per_b = (n // b) // tt
    const = lambda t: (0, 0)
    return pl.pallas_call(
        _combine_kernel,
        grid=(n // tt,),
        in_specs=[pl.BlockSpec((1, 1, tt * TOP_K), lambda t: (t, 0, 0), memory_space=pltpu.SMEM),
                  pl.BlockSpec((tt, TOP_K), lambda t: (t, 0)),
                  pl.BlockSpec((tt, d), lambda t: (t, 0)),
                  pl.BlockSpec((1, 1, d), lambda t: (t // tiles_per_b, 0, 0)),
                  pl.BlockSpec((1, d), const),
                  pl.BlockSpec((1, d), const),
                  pl.BlockSpec(memory_space=pl.ANY)],
        out_specs=pl.BlockSpec((tt, d), lambda t: (t, 0)),
        out_shape=jax.ShapeDtypeStruct((n, d), F32),
        scratch_shapes=[pltpu.VMEM((TOP_K, tt, d), F32), pltpu.SemaphoreType.DMA(())],
        compiler_params=_cparams(("arbitrary",), VMEM_LIMIT),
        name="moe_combine",
    )(dest_flat, wt, x1, g2, lg, lbias, o_sorted)


def kernel(x, c, ctx, c_ctx, w_ada, b_ada, w_in, lb_raw, hg_norm_g, w_four_out, w_hg_out, w_o, ln1_g, ln1_b,
           w_router, b_router, w_gate_up, b_gate_up, w_down, b_down, ln2_g, ln2_b):
    b, s, d = x.shape
    lc = ctx.shape[1]
    n = b * s
    f_w = F_GROUPS * F_GROUP_DIM
    hk = HG_HEADS * HG_D
    assert w_ada.shape[0] == DEPTH and s % GRID_W == 0 and s % 512 == 0 and lc % CHUNK == 0

    pad = (-(b + 1)) % 8
    c_rows = jnp.concatenate([c, c_ctx[None, :], jnp.zeros((pad, d), F32)], axis=0)
    mod = _ada(c_rows, w_ada[0], b_ada[0]).reshape(b + 1 + pad, N_MOD, 1, d)
    shift1, scale1, gate1, shift2, scale2, gate2 = (mod[:b, i] for i in range(N_MOD))
    cshift1 = jnp.broadcast_to(mod[b, 0], (b, 1, d))
    cscale1 = jnp.broadcast_to(mod[b, 1], (b, 1, d))

    lower_bounds = jnp.cumsum(jax.nn.softmax(lb_raw.astype(F32), axis=0), axis=0)[0]

    w_in_b = w_in[0].astype(BF16)
    o_f, o_q, o_z, o_v, o_g, o_gt = f_w, f_w + hk, f_w + 3 * hk, f_w + 4 * hk, f_w + 5 * hk, f_w + 5 * hk + 2 * d
    pf, pq, pz, pv, pg, gates = _inproj(
        x, scale1, shift1, w_in_b,
        [(0, f_w, BF16), (o_f, hk, BF16), (o_q, 2 * hk, F32), (o_z, hk, BF16), (o_v, hk, BF16), (o_g, 2 * d, BF16)],
        tm=512)
    w_ctx = w_in_b[:, o_q:o_v]
    cz, cv = _inproj(ctx, cscale1, cshift1, w_ctx, [(0, 2 * hk, F32), (2 * hk, hk, BF16)], tm=lc)
    y_h = _hgrn(pq, pz, pv, pg, cz, cv, lower_bounds, hg_norm_g[0].reshape(1, hk))
    y_f = _fourier(pf)

    wr = jnp.zeros((d, LANES), F32).at[:, :N_EXPERTS].set(w_router[0])
    br = jnp.zeros((1, LANES), F32).at[0, :N_EXPERTS].set(b_router[0])
    x1, idx, wt = _merge(y_f, y_h, gates, x, gate1, scale2, shift2,
                         w_four_out[0].astype(BF16), w_hg_out[0].astype(BF16), w_o[0].astype(BF16),
                         ln1_g[0].reshape(1, d), ln1_b[0].reshape(1, d), wr, br, tm=512)

    idx2 = idx.reshape(n, TOP_K)
    dest, counts, starts = _route(idx2, MOE_ROWS, 512)
    n_blocks = -(-(n * TOP_K) // MOE_ROWS) + N_EXPERTS
    n_rows = n_blocks * MOE_ROWS
    cnt = counts[0, :N_EXPERTS]
    pad_ends = starts[0, :N_EXPERTS] + (cnt + MOE_ROWS - 1) // MOE_ROWS * MOE_ROWS
    block_expert = jnp.minimum(
        jnp.searchsorted(pad_ends, jnp.arange(n_blocks, dtype=jnp.int32) * MOE_ROWS, side='right'),
        N_EXPERTS - 1).astype(jnp.int32)
    n_used = (pad_ends[-1] // MOE_ROWS).astype(jnp.int32).reshape(1)

    tt = 256
    dest_flat = dest.reshape(n // tt, 1, tt * TOP_K)
    x1_2d = x1.reshape(n, d)
    h_sorted = _dispatch(dest_flat, x1_2d, scale2, shift2, n_rows, tt)
    o_sorted = _experts(h_sorted, block_expert, n_used, w_gate_up[0].astype(BF16), b_gate_up[0],
                        w_down[0].astype(BF16), b_down[0])
    out = _combine(dest_flat, wt.reshape(n, TOP_K), x1_2d, gate2, ln2_g[0].reshape(1, d), ln2_b[0].reshape(1, d),
                   o_sorted, tt)
    return out.reshape(b, s, d)
```

```python
import functools
import math

import jax
import jax.numpy as jnp
import numpy as np
from jax import lax
from jax.experimental import pallas as pl
from jax.experimental.pallas import tpu as pltpu

F32 = jnp.float32
BF16 = jnp.bfloat16

GRID_W = 64
N_MOD = 6
F_GROUPS = 4
F_GROUP_DIM = 128
HG_HEADS = 4
HG_D = 128
CHUNK = 64
N_EXPERTS = 32
TOP_K = 4
SWIGLU_LIMIT = 7.0
SWIGLU_ALPHA = 1.702
LN_EPS = 1e-5
RMS_EPS = 1e-6
DEPTH = 1
DEEPNORM_ALPHA = (2.0 * DEPTH) ** 0.25

LANES = 128
SUBLANES = 8
MOE_ROWS = 512
VMEM_LIMIT = 56 * 1024 * 1024


def _cparams(sem, vmem=None):
    return pltpu.CompilerParams(dimension_semantics=sem, vmem_limit_bytes=vmem)


def _dot(a, b):
    return jnp.dot(a, b, preferred_element_type=F32)


def _sigmoid(x):
    return 1.0 / (1.0 + jnp.exp(-x))


def _ada_kernel(c_ref, w_ref, b_ref, o_ref):
    c = c_ref[...]
    a = c * _sigmoid(c)
    o_ref[...] = jnp.dot(a, w_ref[...], preferred_element_type=F32,
                         precision=lax.Precision.HIGHEST) + b_ref[...]


def _ada(c_rows, w, b):
    r, d = c_rows.shape
    n = w.shape[1]
    tn = 512
    return pl.pallas_call(
        _ada_kernel,
        grid=(n // tn,),
        in_specs=[pl.BlockSpec((r, d), lambda j: (0, 0)),
                  pl.BlockSpec((d, tn), lambda j: (0, j)),
                  pl.BlockSpec((1, tn), lambda j: (0, j))],
        out_specs=pl.BlockSpec((r, tn), lambda j: (0, j)),
        out_shape=jax.ShapeDtypeStruct((r, n), F32),
        compiler_params=_cparams(("arbitrary",)),
        name="ada_mod",
    )(c_rows, w, b.reshape(1, n))


def _inproj_kernel(x_ref, sc_ref, sh_ref, w_ref, *o_refs, col_chunks):
    u = (x_ref[0] * (1.0 + sc_ref[0]) + sh_ref[0]).astype(BF16)
    for o_ref, chunks in zip(o_refs, col_chunks):
        for (w0, o0, n) in chunks:
            o_ref[0, :, o0:o0 + n] = _dot(u, w_ref[:, w0:w0 + n]).astype(o_ref.dtype)


def _inproj(x, scale, shift, w_bf16, outs, tm):
    b, s, d = x.shape
    n_w = w_bf16.shape[1]
    col_chunks = []
    for (c0, width, _) in outs:
        step = min(width, 512)
        col_chunks.append(tuple((c0 + o, o, step) for o in range(0, width, step)))
    kern = functools.partial(_inproj_kernel, col_chunks=tuple(col_chunks))
    return pl.pallas_call(
        kern,
        grid=(b, s // tm),
        in_specs=[pl.BlockSpec((1, tm, d), lambda i, j: (i, j, 0)),
                  pl.BlockSpec((1, 1, d), lambda i, j: (i, 0, 0)),
                  pl.BlockSpec((1, 1, d), lambda i, j: (i, 0, 0)),
                  pl.BlockSpec((d, n_w), lambda i, j: (0, 0))],
        out_specs=[pl.BlockSpec((1, tm, width), lambda i, j: (i, j, 0)) for (_, width, _) in outs],
        out_shape=[jax.ShapeDtypeStruct((b, s, width), dt) for (_, width, dt) in outs],
        compiler_params=_cparams(("parallel", "parallel"), VMEM_LIMIT),
        name="in_proj",
    )(x, scale, shift, w_bf16)


def _tri_cumsum(tri, lf):
    hi = lf.astype(BF16)
    r1 = lf - hi.astype(F32)
    mid = r1.astype(BF16)
    lo = (r1 - mid.astype(F32)).astype(BF16)
    return _dot(tri, hi) + _dot(tri, mid) + _dot(tri, lo)


HG_TILE = 256


def _gla_tile(z, v, lb, state_t, tri, forward, q=None):
    r = z.shape[0]
    nc = r // CHUNK
    f = lb + (1.0 - lb) * _sigmoid(z)
    k = 1.0 - f
    cum = _tri_cumsum(tri, jnp.log(f))
    k_inv = k * jnp.exp(-cum)
    tot_row = CHUNK - 1 if forward else 0
    decays, upds = [], []
    for ci in range(nc):
        rows = slice(ci * CHUNK, (ci + 1) * CHUNK)
        dec = jnp.exp(cum[ci * CHUNK + tot_row:ci * CHUNK + tot_row + 1, :])
        k_end = (k_inv[rows] * dec).astype(BF16)
        decays.append(dec)
        upds.append(lax.dot_general(v[rows], k_end, (((0,), (0,)), ((), ())), preferred_element_type=F32))
    entering = [None] * nc
    for ci in (range(nc) if forward else reversed(range(nc))):
        entering[ci] = state_t
        state_t = state_t * decays[ci] + upds[ci]
    if q is None:
        return None, state_t
    q_dec = (q * jnp.exp(cum)).astype(BF16)
    scores = lax.dot_general(q_dec, k_inv.astype(BF16), (((1,), (1,)), ((), ())), preferred_element_type=F32)
    scores = jnp.where(tri > 0, scores, 0.0).astype(BF16)
    inter = [lax.dot_general(q_dec[ci * CHUNK:(ci + 1) * CHUNK], entering[ci].astype(BF16),
                             (((1,), (1,)), ((), ())), preferred_element_type=F32) for ci in range(nc)]
    return _dot(scores, v) + jnp.concatenate(inter, axis=0), state_t


def _hgrn_kernel(q_ref, zf_ref, zb_ref, v_ref, g_ref, czf_ref, czb_ref, cv_ref, lb_ref, ng_ref, trif_ref, trib_ref,
                 y_ref, of_ref, ob_ref, *, seq, ctx_len):
    r = HG_TILE
    n_lat = seq // r
    n_ctx = ctx_len // r
    lb_f = lb_ref[0:1, :]
    lb_b = lb_ref[1:2, :]

    def ctx_body(i, carry):
        s_f, s_b = carry
        sl_f = pl.ds(pl.multiple_of(i * r, r), r)
        sl_b = pl.ds(pl.multiple_of((n_ctx - 1 - i) * r, r), r)
        _, s_f = _gla_tile(czf_ref[0, sl_f, :], cv_ref[0, sl_f, :], lb_f, s_f, trif_ref[...], True)
        _, s_b = _gla_tile(czb_ref[0, sl_b, :], cv_ref[0, sl_b, :], lb_b, s_b, trib_ref[...], False)
        return s_f, s_b

    zero = jnp.zeros((HG_D, HG_D), F32)
    carry = lax.fori_loop(0, n_ctx, ctx_body, (zero, zero))

    def lat_body(i, carry):
        s_f, s_b = carry
        sl_f = pl.ds(pl.multiple_of(i * r, r), r)
        sl_b = pl.ds(pl.multiple_of((n_lat - 1 - i) * r, r), r)
        qf = q_ref[0, sl_f, :].astype(F32)
        qb = q_ref[0, sl_b, :].astype(F32)
        qf = qf * _sigmoid(qf)
        qb = qb * _sigmoid(qb)
        o_f, s_f = _gla_tile(zf_ref[0, sl_f, :], v_ref[0, sl_f, :], lb_f, s_f, trif_ref[...], True, qf)
        o_b, s_b = _gla_tile(zb_ref[0, sl_b, :], v_ref[0, sl_b, :], lb_b, s_b, trib_ref[...], False, qb)
        of_ref[sl_f, :] = o_f
        ob_ref[sl_b, :] = o_b
        return s_f, s_b

    lax.fori_loop(0, n_lat, lat_body, carry)

    ng = ng_ref[...]

    def out_body(i, _):
        sl = pl.ds(pl.multiple_of(i * r, r), r)
        o = of_ref[sl, :] + ob_ref[sl, :]
        o = o * lax.rsqrt(jnp.mean(o * o, axis=-1, keepdims=True) + RMS_EPS) * ng
        g = g_ref[0, sl, :].astype(F32)
        y_ref[0, sl, :] = (o * (g * _sigmoid(g))).astype(y_ref.dtype)
        return 0

    lax.fori_loop(0, n_lat, out_body, 0)


def _chunk_tri(r, forward):
    i = np.arange(r)
    same = (i[:, None] // CHUNK) == (i[None, :] // CHUNK)
    order = (i[:, None] >= i[None, :]) if forward else (i[:, None] <= i[None, :])
    return jnp.asarray(same & order, BF16)


def _hgrn(q, z, v, g, cz, cv, lb, ng):
    b, s, _ = q.shape
    lc = cz.shape[1]
    h = HG_HEADS
    d = HG_D
    r = HG_TILE
    kern = functools.partial(_hgrn_kernel, seq=s, ctx_len=lc)
    head = lambda i, j: (i, 0, j)
    const = lambda i, j: (0, 0)
    return pl.pallas_call(
        kern,
        grid=(b, h),
        in_specs=[pl.BlockSpec((1, s, d), head),
                  pl.BlockSpec((1, s, d), head),
                  pl.BlockSpec((1, s, d), lambda i, j: (i, 0, j + h)),
                  pl.BlockSpec((1, s, d), head),
                  pl.BlockSpec((1, s, d), head),
                  pl.BlockSpec((1, lc, d), head),
                  pl.BlockSpec((1, lc, d), lambda i, j: (i, 0, j + h)),
                  pl.BlockSpec((1, lc, d), head),
                  pl.BlockSpec((2, d), lambda i, j: (0, j)),
                  pl.BlockSpec((1, d), lambda i, j: (0, j)),
                  pl.BlockSpec((r, r), const),
                  pl.BlockSpec((r, r), const)],
        out_specs=pl.BlockSpec((1, s, d), head),
        out_shape=jax.ShapeDtypeStruct((b, s, h * d), BF16),
        scratch_shapes=[pltpu.VMEM((s, d), F32), pltpu.VMEM((s, d), F32)],
        compiler_params=_cparams(("parallel", "parallel"), VMEM_LIMIT),
        name="hgrn2",
    )(q, z, z, v, g, cz, cz, cv, lb, ng, _chunk_tri(r, True), _chunk_tri(r, False))


def _dft_tables(seq):
    rows = seq // GRID_W
    n = F_GROUP_DIM
    kc = np.outer(np.arange(n), np.arange(n)) % n
    ang = 2.0 * np.pi * kc / n
    norm = 1.0 / math.sqrt(rows * GRID_W * n)
    ch = np.concatenate([np.cos(ang), -np.sin(ang)], axis=1) * norm
    t = np.arange(seq)
    r, w = t // GRID_W, t % GRID_W
    m = (np.outer(r, r) * GRID_W + np.outer(w, w) * rows) % (rows * GRID_W)
    ang_t = 2.0 * np.pi * m / (rows * GRID_W)
    tok = np.concatenate([np.cos(ang_t), np.sin(ang_t)], axis=1)
    return jnp.asarray(ch, BF16), jnp.asarray(tok, BF16)


def _fourier_kernel(p_ref, ch_ref, tok_ref, y_ref, xs_ref, *, seq):
    for gi in range(F_GROUPS):
        cols = slice(gi * F_GROUP_DIM, (gi + 1) * F_GROUP_DIM)
        x1 = _dot(p_ref[0, :, cols], ch_ref[...]).astype(BF16)
        xs_ref[0:seq, cols] = x1[:, :F_GROUP_DIM]
        xs_ref[seq:2 * seq, cols] = x1[:, F_GROUP_DIM:]
    y_ref[0] = _dot(tok_ref[...], xs_ref[...]).astype(y_ref.dtype)


def _fourier(pf):
    b, s, wdt = pf.shape
    ch, tok = _dft_tables(s)
    kern = functools.partial(_fourier_kernel, seq=s)
    return pl.pallas_call(
        kern,
        grid=(b,),
        in_specs=[pl.BlockSpec((1, s, wdt), lambda i: (i, 0, 0)),
                  pl.BlockSpec(ch.shape, lambda i: (0, 0)),
                  pl.BlockSpec(tok.shape, lambda i: (0, 0), pipeline_mode=pl.Buffered(1))],
        out_specs=pl.BlockSpec((1, s, wdt), lambda i: (i, 0, 0)),
        out_shape=jax.ShapeDtypeStruct((b, s, wdt), BF16),
        scratch_shapes=[pltpu.VMEM((2 * s, wdt), BF16)],
        compiler_params=_cparams(("parallel",), VMEM_LIMIT),
        name="fourier",
    )(pf, ch, tok)


def _layer_norm(t, g, b):
    mu = jnp.mean(t, axis=-1, keepdims=True)
    tc = t - mu
    var = jnp.mean(tc * tc, axis=-1, keepdims=True)
    return tc * lax.rsqrt(var + LN_EPS) * g + b


def _merge_kernel(yf_ref, yh_ref, gt_ref, x_ref, g1_ref, sc2_ref, sh2_ref, wfo_ref, who_ref, wo_ref,
                  lg_ref, lbias_ref, wr_ref, br_ref, x1_ref, idx_ref, wt_ref):
    d = x_ref.shape[-1]
    gf = _sigmoid(gt_ref[0, :, :d].astype(F32))
    gh = _sigmoid(gt_ref[0, :, d:].astype(F32))
    m = gf * _dot(yf_ref[0], wfo_ref[...]) + gh * _dot(yh_ref[0], who_ref[...])
    mix = _dot(m.astype(BF16), wo_ref[...])
    x1 = _layer_norm(DEEPNORM_ALPHA * x_ref[0] + g1_ref[0] * mix, lg_ref[...], lbias_ref[...])
    x1_ref[0] = x1
    u2 = x1 * (1.0 + sc2_ref[0]) + sh2_ref[0]
    logits = jnp.dot(u2, wr_ref[...], preferred_element_type=F32,
                     precision=lax.Precision.HIGHEST) + br_ref[...]
    lane = lax.broadcasted_iota(jnp.int32, logits.shape, 1)
    work = jnp.where(lane < N_EXPERTS, logits, -jnp.inf)
    vals, idxs = [], []
    for _ in range(TOP_K):
        mx = jnp.max(work, axis=-1, keepdims=True)
        sel = jnp.min(jnp.where(work == mx, lane, LANES), axis=-1, keepdims=True)
        vals.append(mx)
        idxs.append(sel)
        work = jnp.where(lane == sel, -jnp.inf, work)
    exps = [jnp.exp(vv - vals[0]) for vv in vals]
    inv = 1.0 / (exps[0] + exps[1] + exps[2] + exps[3])
    for kk in range(TOP_K):
        idx_ref[0, :, kk:kk + 1] = idxs[kk]
        wt_ref[0, :, kk:kk + 1] = exps[kk] * inv


def _merge(yf, yh, gates, x, g1, sc2, sh2, wfo, who, wo, lg, lbias, wr, br, tm):
    b, s, d = x.shape
    tile = lambda i, j: (i, j, 0)
    per_b = lambda i, j: (i, 0, 0)
    const = lambda i, j: (0, 0)
    return pl.pallas_call(
        _merge_kernel,
        grid=(b, s // tm),
        in_specs=[pl.BlockSpec((1, tm, yf.shape[-1]), tile),
                  pl.BlockSpec((1, tm, yh.shape[-1]), tile),
                  pl.BlockSpec((1, tm, gates.shape[-1]), tile),
                  pl.BlockSpec((1, tm, d), tile),
                  pl.BlockSpec((1, 1, d), per_b),
                  pl.BlockSpec((1, 1, d), per_b),
                  pl.BlockSpec((1, 1, d), per_b),
                  pl.BlockSpec(wfo.shape, const),
                  pl.BlockSpec(who.shape, const),
                  pl.BlockSpec(wo.shape, const),
                  pl.BlockSpec((1, d), const),
                  pl.BlockSpec((1, d), const),
                  pl.BlockSpec(wr.shape, const),
                  pl.BlockSpec(br.shape, const)],
        out_specs=[pl.BlockSpec((1, tm, d), tile),
                   pl.BlockSpec((1, tm, TOP_K), tile),
                   pl.BlockSpec((1, tm, TOP_K), tile)],
        out_shape=[jax.ShapeDtypeStruct((b, s, d), F32),
                   jax.ShapeDtypeStruct((b, s, TOP_K), jnp.int32),
                   jax.ShapeDtypeStruct((b, s, TOP_K), F32)],
        compiler_params=_cparams(("parallel", "parallel"), VMEM_LIMIT),
        name="merge_ln_router",
    )(yf, yh, gates, x, g1, sc2, sh2, wfo, who, wo, lg, lbias, wr, br)


def _lane_cumsum_exclusive(x):
    lane = lax.broadcasted_iota(jnp.int32, x.shape, 1)
    inc = x
    sh = 1
    while sh < LANES:
        inc = inc + jnp.where(lane >= sh, pltpu.roll(inc, sh, 1), 0)
        sh *= 2
    return inc - x


def _route_kernel(idx_ref, dest_ref, cnt_ref, start_ref, run_ref, *, block_rows):
    phase = pl.program_id(0)
    t = pl.program_id(1)
    tt = idx_ref.shape[0]
    lane = lax.broadcasted_iota(jnp.int32, (tt, LANES), 1)
    onehots = [jnp.where(lane == idx_ref[:, kk:kk + 1], 1.0, 0.0) for kk in range(TOP_K)]
    colsums = [jnp.sum(oh, axis=0, keepdims=True) for oh in onehots]

    @pl.when(jnp.logical_and(phase == 0, t == 0))
    def _():
        run_ref[...] = jnp.zeros_like(run_ref)

    @pl.when(phase == 0)
    def _():
        run_ref[...] += colsums[0] + colsums[1] + colsums[2] + colsums[3]

    @pl.when(jnp.logical_and(phase == 1, t == 0))
    def _():
        cnt = run_ref[...].astype(jnp.int32)
        padded = (cnt + (block_rows - 1)) // block_rows * block_rows
        cnt_ref[...] = cnt
        start_ref[...] = _lane_cumsum_exclusive(padded)
        run_ref[...] = jnp.zeros_like(run_ref)

    @pl.when(phase == 1)
    def _():
        r = lax.broadcasted_iota(jnp.int32, (tt, tt), 0)
        c = lax.broadcasted_iota(jnp.int32, (tt, tt), 1)
        strict = jnp.where(r > c, 1.0, 0.0).astype(BF16)
        base = run_ref[...] + start_ref[...].astype(F32)
        for kk in range(TOP_K):
            prefix = _dot(strict, onehots[kk].astype(BF16))
            rank = jnp.sum(onehots[kk] * (prefix + base), axis=-1, keepdims=True)
            dest_ref[:, kk:kk + 1] = rank.astype(jnp.int32)
            base = base + colsums[kk]
        run_ref[...] = base - start_ref[...].astype(F32)


def _route(idx, block_rows, tt):
    n = idx.shape[0]
    kern = functools.partial(_route_kernel, block_rows=block_rows)
    return pl.pallas_call(
        kern,
        grid=(2, n // tt),
        in_specs=[pl.BlockSpec((tt, TOP_K), lambda p, t: (t, 0))],
        out_specs=[pl.BlockSpec((tt, TOP_K), lambda p, t: (t * p, 0)),
                   pl.BlockSpec((1, LANES), lambda p, t: (0, 0)),
                   pl.BlockSpec((1, LANES), lambda p, t: (0, 0))],
        out_shape=[jax.ShapeDtypeStruct((n, TOP_K), jnp.int32),
                   jax.ShapeDtypeStruct((1, LANES), jnp.int32),
                   jax.ShapeDtypeStruct((1, LANES), jnp.int32)],
        scratch_shapes=[pltpu.VMEM((1, LANES), F32)],
        compiler_params=_cparams(("arbitrary", "arbitrary")),
        name="route_ranks",
    )(idx)


def _dispatch_kernel(cnt_ref, start_ref, dest_ref, x1_ref, sc_ref, sh_ref, h_ref, u_ref, zero_ref, sem, *, block_rows):
    tt = x1_ref.shape[0]
    u_ref[...] = x1_ref[...] * (1.0 + sc_ref[0]) + sh_ref[0]

    def row_copy(i, kk):
        d = dest_ref[0, 0, i * TOP_K + kk]
        return pltpu.make_async_copy(u_ref.at[pl.ds(i, 1), :], h_ref.at[pl.ds(d, 1), :], sem)

    def issue(i, _):
        for kk in range(TOP_K):
            row_copy(i, kk).start()
        return 0

    lax.fori_loop(0, tt, issue, 0)

    def drain(i, _):
        for kk in range(TOP_K):
            row_copy(i, kk).wait()
        return 0

    lax.fori_loop(0, tt, drain, 0)

    @pl.when(pl.program_id(0) == pl.num_programs(0) - 1)
    def _():
        zero_ref[...] = jnp.zeros_like(zero_ref)

        def pad_expert(e, _):
            cnt = cnt_ref[e]
            rem = (block_rows - (cnt & (block_rows - 1))) & (block_rows - 1)
            pos = start_ref[e] + cnt
            head = rem & (SUBLANES - 1)
            for i in range(SUBLANES - 1):
                @pl.when(i < head)
                def _(i=i):
                    cp = pltpu.make_async_copy(zero_ref.at[pl.ds(0, 1), :], h_ref.at[pl.ds(pos + i, 1), :], sem)
                    cp.start()
                    cp.wait()

            pos = pos + head
            size = block_rows // 2
            while size >= SUBLANES:
                take = rem & size

                @pl.when(take != 0)
                def _(pos=pos, size=size):
                    dst = h_ref.at[pl.ds(pl.multiple_of(pos, SUBLANES), size), :]
                    cp = pltpu.make_async_copy(zero_ref.at[pl.ds(0, size), :], dst, sem)
                    cp.start()
                    cp.wait()

                pos = pos + take
                size //= 2
            return 0

        lax.fori_loop(0, N_EXPERTS, pad_expert, 0)


def _dispatch(counts, starts, dest_flat, x1, sc2, sh2, n_rows, tt):
    n, d = x1.shape
    b = sc2.shape[0]
    tiles_per_b = (n // b) // tt
    per_b = lambda t, c, s: (t // tiles_per_b, 0, 0)
    grid_spec = pltpu.PrefetchScalarGridSpec(
        num_scalar_prefetch=2,
        grid=(n // tt,),
        in_specs=[pl.BlockSpec((1, 1, tt * TOP_K), lambda t, c, s: (t, 0, 0), memory_space=pltpu.SMEM),
                  pl.BlockSpec((tt, d), lambda t, c, s: (t, 0)),
                  pl.BlockSpec((1, 1, d), per_b),
                  pl.BlockSpec((1, 1, d), per_b)],
        out_specs=pl.BlockSpec(memory_space=pl.ANY),
        scratch_shapes=[pltpu.VMEM((tt, d), F32), pltpu.VMEM((MOE_ROWS // 2, d), F32), pltpu.SemaphoreType.DMA(())],
    )
    return pl.pallas_call(
        functools.partial(_dispatch_kernel, block_rows=MOE_ROWS),
        grid_spec=grid_spec,
        out_shape=jax.ShapeDtypeStruct((n_rows, d), F32),
        compiler_params=_cparams(("arbitrary",)),
        name="moe_dispatch",
    )(counts, starts, dest_flat, x1, sc2, sh2)


def _expert_kernel(be_ref, nu_ref, h_ref, wgu_ref, bgu_ref, wdn_ref, bdn_ref, o_ref):
    del be_ref
    d = h_ref.shape[-1]

    @pl.when(pl.program_id(0) < nu_ref[0])
    def _():
        gu = _dot(h_ref[...].astype(BF16), wgu_ref[0]) + bgu_ref[0]
        gate = jnp.minimum(gu[:, :d], SWIGLU_LIMIT)
        up = jnp.clip(gu[:, d:], -SWIGLU_LIMIT, SWIGLU_LIMIT)
        act = (up + 1.0) * gate * _sigmoid(SWIGLU_ALPHA * gate)
        o_ref[...] = _dot(act.astype(BF16), wdn_ref[0]) + bdn_ref[0]


def _experts(h_sorted, block_expert, n_used, wgu, bgu, wdn, bdn):
    n_rows, d = h_sorted.shape
    bm = MOE_ROWS
    n_blocks = n_rows // bm
    e = wgu.shape[0]
    blk = lambda j, be, nu: (jnp.minimum(j, nu[0] - 1), 0)
    exp = lambda j, be, nu: (be[jnp.minimum(j, nu[0] - 1)], 0, 0)
    grid_spec = pltpu.PrefetchScalarGridSpec(
        num_scalar_prefetch=2,
        grid=(n_blocks,),
        in_specs=[pl.BlockSpec((bm, d), blk),
                  pl.BlockSpec((1, d, 2 * d), exp),
                  pl.BlockSpec((1, 1, 2 * d), exp),
                  pl.BlockSpec((1, d, d), exp),
                  pl.BlockSpec((1, 1, d), exp)],
        out_specs=pl.BlockSpec((bm, d), blk),
    )
    return pl.pallas_call(
        _expert_kernel,
        grid_spec=grid_spec,
        out_shape=jax.ShapeDtypeStruct((n_rows, d), F32),
        compiler_params=_cparams(("arbitrary",), VMEM_LIMIT),
        name="moe_experts",
    )(block_expert, n_used, h_sorted, wgu, bgu.reshape(e, 1, 2 * d), wdn, bdn.reshape(e, 1, d))


def _combine_kernel(dest_ref, wt_ref, x1_ref, g2_ref, lg_ref, lbias_ref, o_hbm, y_ref, buf_ref, sem):
    tt = x1_ref.shape[0]

    def row_copy(i, kk):
        d = dest_ref[0, 0, i * TOP_K + kk]
        return pltpu.make_async_copy(o_hbm.at[pl.ds(d, 1), :], buf_ref.at[kk, pl.ds(i, 1), :], sem)

    def issue(i, _):
        for kk in range(TOP_K):
            row_copy(i, kk).start()
        return 0

    lax.fori_loop(0, tt, issue, 0)

    def drain(i, _):
        for kk in range(TOP_K):
            row_copy(i, kk).wait()
        return 0

    lax.fori_loop(0, tt, drain, 0)

    ff = wt_ref[:, 0:1] * buf_ref[0]
    for kk in range(1, TOP_K):
        ff = ff + wt_ref[:, kk:kk + 1] * buf_ref[kk]
    y_ref[...] = _layer_norm(DEEPNORM_ALPHA * x1_ref[...] + g2_ref[0] * ff, lg_ref[...], lbias_ref[...])


def _combine(dest_flat, wt, x1, g2, lg, lbias, o_sorted, tt):
    n, d = x1.shape
    b = g2.shape[0]
    tiles_per_b = (n // b) // tt
    const = lambda t: (0, 0)
    return pl.pallas_call(
        _combine_kernel,
        grid=(n // tt,),
        in_specs=[pl.BlockSpec((1, 1, tt * TOP_K), lambda t: (t, 0, 0), memory_space=pltpu.SMEM),
                  pl.BlockSpec((tt, TOP_K), lambda t: (t, 0)),
                  pl.BlockSpec((tt, d), lambda t: (t, 0)),
                  pl.BlockSpec((1, 1, d), lambda t: (t // tiles_per_b, 0, 0)),
                  pl.BlockSpec((1, d), const),
                  pl.BlockSpec((1, d), const),
                  pl.BlockSpec(memory_space=pl.ANY)],
        out_specs=pl.BlockSpec((tt, d), lambda t: (t, 0)),
        out_shape=jax.ShapeDtypeStruct((n, d), F32),
        scratch_shapes=[pltpu.VMEM((TOP_K, tt, d), F32), pltpu.SemaphoreType.DMA(())],
        compiler_params=_cparams(("arbitrary",), VMEM_LIMIT),
        name="moe_combine",
    )(dest_flat, wt, x1, g2, lg, lbias, o_sorted)


def kernel(x, c, ctx, c_ctx, w_ada, b_ada, w_in, lb_raw, hg_norm_g, w_four_out, w_hg_out, w_o, ln1_g, ln1_b,
           w_router, b_router, w_gate_up, b_gate_up, w_down, b_down, ln2_g, ln2_b):
    b, s, d = x.shape
    lc = ctx.shape[1]
    n = b * s
    f_w = F_GROUPS * F_GROUP_DIM
    hk = HG_HEADS * HG_D
    assert w_ada.shape[0] == DEPTH and s % GRID_W == 0 and s % 512 == 0 and lc % HG_TILE == 0

    pad = (-(b + 1)) % 8
    c_rows = jnp.concatenate([c, c_ctx[None, :], jnp.zeros((pad, d), F32)], axis=0)
    mod = _ada(c_rows, w_ada[0], b_ada[0]).reshape(b + 1 + pad, N_MOD, 1, d)
    shift1, scale1, gate1, shift2, scale2, gate2 = (mod[:b, i] for i in range(N_MOD))
    cshift1 = jnp.broadcast_to(mod[b, 0], (b, 1, d))
    cscale1 = jnp.broadcast_to(mod[b, 1], (b, 1, d))

    lower_bounds = jnp.cumsum(jax.nn.softmax(lb_raw.astype(F32), axis=0), axis=0)[0]

    w_in_b = w_in[0].astype(BF16)
    o_f, o_q, o_z, o_v, o_g, o_gt = f_w, f_w + hk, f_w + 3 * hk, f_w + 4 * hk, f_w + 5 * hk, f_w + 5 * hk + 2 * d
    pf, pq, pz, pv, pg, gates = _inproj(
        x, scale1, shift1, w_in_b,
        [(0, f_w, BF16), (o_f, hk, BF16), (o_q, 2 * hk, F32), (o_z, hk, BF16), (o_v, hk, BF16), (o_g, 2 * d, BF16)],
        tm=512)
    w_ctx = w_in_b[:, o_q:o_v]
    cz, cv = _inproj(ctx, cscale1, cshift1, w_ctx, [(0, 2 * hk, F32), (2 * hk, hk, BF16)], tm=lc)
    y_h = _hgrn(pq, pz, pv, pg, cz, cv, lower_bounds, hg_norm_g[0].reshape(1, hk))
    y_f = _fourier(pf)

    wr = jnp.zeros((d, LANES), F32).at[:, :N_EXPERTS].set(w_router[0])
    br = jnp.zeros((1, LANES), F32).at[0, :N_EXPERTS].set(b_router[0])
    x1, idx, wt = _merge(y_f, y_h, gates, x, gate1, scale2, shift2,
                         w_four_out[0].astype(BF16), w_hg_out[0].astype(BF16), w_o[0].astype(BF16),
                         ln1_g[0].reshape(1, d), ln1_b[0].reshape(1, d), wr, br, tm=512)

    idx2 = idx.reshape(n, TOP_K)
    dest, counts, starts = _route(idx2, MOE_ROWS, 512)
    n_blocks = -(-(n * TOP_K) // MOE_ROWS) + N_EXPERTS
    n_rows = n_blocks * MOE_ROWS
    cnt = counts[0, :N_EXPERTS]
    st = starts[0, :N_EXPERTS]
    pad_ends = st + (cnt + MOE_ROWS - 1) // MOE_ROWS * MOE_ROWS
    block_first_row = jnp.arange(n_blocks, dtype=jnp.int32) * MOE_ROWS
    block_expert = jnp.minimum(jnp.sum((pad_ends[None, :] <= block_first_row[:, None]).astype(jnp.int32), axis=1),
                               N_EXPERTS - 1)
    n_used = (pad_ends[-1] // MOE_ROWS).astype(jnp.int32).reshape(1)

    tt = 256
    dest_flat = dest.reshape(n // tt, 1, tt * TOP_K)
    x1_2d = x1.reshape(n, d)
    h_sorted = _dispatch(cnt, st, dest_flat, x1_2d, scale2, shift2, n_rows, tt)
    o_sorted = _experts(h_sorted, block_expert, n_used, w_gate_up[0].astype(BF16), b_gate_up[0],
                        w_down[0].astype(BF16), b_down[0])
    out = _combine(dest_flat, wt.reshape(n, TOP_K), x1_2d, gate2, ln2_g[0].reshape(1, d), ln2_b[0].reshape(1, d),
                   o_sorted, tt)
    return out.reshape(b, s, d)
```

```python
import functools
import math

import jax
import jax.numpy as jnp
import numpy as np
from jax import lax
from jax.experimental import pallas as pl
from jax.experimental.pallas import tpu as pltpu

F32 = jnp.float32
BF16 = jnp.bfloat16

GRID_W = 64
N_MOD = 6
F_GROUPS = 4
F_GROUP_DIM = 128
HG_HEADS = 4
HG_D = 128
CHUNK = 64
N_EXPERTS = 32
TOP_K = 4
SWIGLU_LIMIT = 7.0
SWIGLU_ALPHA = 1.702
LN_EPS = 1e-5
RMS_EPS = 1e-6
DEPTH = 1
DEEPNORM_ALPHA = (2.0 * DEPTH) ** 0.25

LANES = 128
SUBLANES = 8
MOE_ROWS = 512
VMEM_LIMIT = 56 * 1024 * 1024


def _cparams(sem, vmem=None):
    return pltpu.CompilerParams(dimension_semantics=sem, vmem_limit_bytes=vmem)


def _dot(a, b):
    return jnp.dot(a, b, preferred_element_type=F32)


def _sigmoid(x):
    return 1.0 / (1.0 + jnp.exp(-x))


def _ada_kernel(c_ref, w_ref, b_ref, o_ref):
    c = c_ref[...]
    a = c * _sigmoid(c)
    o_ref[...] = jnp.dot(a, w_ref[...], preferred_element_type=F32,
                         precision=lax.Precision.HIGHEST) + b_ref[...]


def _ada(c_rows, w, b):
    r, d = c_rows.shape
    n = w.shape[1]
    tn = 512
    return pl.pallas_call(
        _ada_kernel,
        grid=(n // tn,),
        in_specs=[pl.BlockSpec((r, d), lambda j: (0, 0)),
                  pl.BlockSpec((d, tn), lambda j: (0, j)),
                  pl.BlockSpec((1, tn), lambda j: (0, j))],
        out_specs=pl.BlockSpec((r, tn), lambda j: (0, j)),
        out_shape=jax.ShapeDtypeStruct((r, n), F32),
        compiler_params=_cparams(("arbitrary",)),
        name="ada_mod",
    )(c_rows, w, b.reshape(1, n))


def _inproj_kernel(x_ref, sc_ref, sh_ref, lb_ref, w_ref, *o_refs, plan):
    u = (x_ref[0] * (1.0 + sc_ref[0]) + sh_ref[0]).astype(BF16)
    refs = iter(o_refs)
    for kind, chunks in plan:
        if kind == "forget":
            k_ref, hi_ref, lo_ref = next(refs), next(refs), next(refs)
            for (w0, o0, n) in chunks:
                lb = lb_ref[:, o0:o0 + n]
                f = lb + (1.0 - lb) * _sigmoid(_dot(u, w_ref[:, w0:w0 + n]))
                lf = jnp.log(f)
                hi = lf.astype(BF16)
                k_ref[0, :, o0:o0 + n] = (1.0 - f).astype(BF16)
                hi_ref[0, :, o0:o0 + n] = hi
                lo_ref[0, :, o0:o0 + n] = (lf - hi.astype(F32)).astype(BF16)
        else:
            o_ref = next(refs)
            for (w0, o0, n) in chunks:
                p = _dot(u, w_ref[:, w0:w0 + n])
                if kind == "silu":
                    p = p * _sigmoid(p)
                o_ref[0, :, o0:o0 + n] = p.astype(o_ref.dtype)


def _inproj(x, scale, shift, lb_row, w_bf16, outs, tm):
    b, s, d = x.shape
    n_w = w_bf16.shape[1]
    plan, widths = [], []
    for (c0, width, kind) in outs:
        step = min(width, 512)
        plan.append((kind, tuple((c0 + o, o, step) for o in range(0, width, step))))
        widths += [width] * (3 if kind == "forget" else 1)
    kern = functools.partial(_inproj_kernel, plan=tuple(plan))
    return pl.pallas_call(
        kern,
        grid=(b, s // tm),
        in_specs=[pl.BlockSpec((1, tm, d), lambda i, j: (i, j, 0)),
                  pl.BlockSpec((1, 1, d), lambda i, j: (i, 0, 0)),
                  pl.BlockSpec((1, 1, d), lambda i, j: (i, 0, 0)),
                  pl.BlockSpec(lb_row.shape, lambda i, j: (0, 0)),
                  pl.BlockSpec((d, n_w), lambda i, j: (0, 0))],
        out_specs=[pl.BlockSpec((1, tm, width), lambda i, j: (i, j, 0)) for width in widths],
        out_shape=[jax.ShapeDtypeStruct((b, s, width), BF16) for width in widths],
        compiler_params=_cparams(("parallel", "parallel"), VMEM_LIMIT),
        name="in_proj",
    )(x, scale, shift, lb_row, w_bf16)


HG_TILE = 256
HG_GROUP = 2


def _gla_group(jobs, states):
    states = list(states)
    nj = len(jobs)
    r, dk = jobs[0][1].shape
    dv = jobs[0][4].shape[1]
    nc = r // CHUNK
    zero = jnp.zeros((CHUNK, dk), BF16)
    contract_last = (((1,), (1,)), ((), ()))
    contract_rows = (((0,), (0,)), ((), ()))

    cums = []
    for (_, hi, lo, _, _, _, tri, _, _) in jobs:
        both = _dot(tri, jnp.concatenate([hi, lo], axis=1))
        cums.append(both[:, :dk] + both[:, dk:])

    k_inv_b, decays, k_end_blk, q_dec = [], [], [], []
    for cum, (_, _, _, k, _, q, _, _, forward) in zip(cums, jobs):
        k_inv = k.astype(F32) * jnp.exp(-cum)
        k_inv_b.append(k_inv.astype(BF16))
        tot_row = CHUNK - 1 if forward else 0
        dec_j, rows_j = [], []
        for ci in range(nc):
            dec = jnp.exp(cum[ci * CHUNK + tot_row:ci * CHUNK + tot_row + 1, :])
            k_end = (k_inv[ci * CHUNK:(ci + 1) * CHUNK] * dec).astype(BF16)
            dec_j.append(dec)
            rows_j.append(jnp.concatenate([k_end if cj == ci else zero for cj in range(nc)], axis=1))
        decays.append(dec_j)
        k_end_blk.append(jnp.concatenate(rows_j, axis=0))
        q_dec.append(None if q is None else (q.astype(F32) * jnp.exp(cum)).astype(BF16))

    upd_all = [lax.dot_general(job[4], blk, contract_rows, preferred_element_type=F32)
               for job, blk in zip(jobs, k_end_blk)]
    scores = [None if qd is None else lax.dot_general(qd, kb, contract_last, preferred_element_type=F32)
              for qd, kb in zip(q_dec, k_inv_b)]

    s_all = []
    for ji in range(nj):
        chain, forward = jobs[ji][0], jobs[ji][8]
        st = states[chain]
        entering = [None] * nc
        for ci in (range(nc) if forward else reversed(range(nc))):
            entering[ci] = st
            st = st * decays[ji][ci] + upd_all[ji][:, ci * dk:(ci + 1) * dk]
        states[chain] = st
        s_all.append(None if q_dec[ji] is None
                     else jnp.concatenate([e.astype(BF16) for e in entering], axis=0))

    outs = []
    for ji in range(nj):
        if q_dec[ji] is None:
            outs.append(None)
            continue
        sc = jnp.where(jobs[ji][7] > 0, scores[ji], 0.0).astype(BF16)
        intra = _dot(sc, jobs[ji][4])
        inter = lax.dot_general(q_dec[ji], s_all[ji], contract_last, preferred_element_type=F32)
        inter = jnp.concatenate([inter[ci * CHUNK:(ci + 1) * CHUNK, ci * dv:(ci + 1) * dv] for ci in range(nc)],
                                axis=0)
        outs.append(intra + inter)
    return outs, states


def _hgrn_kernel(q_ref, kf_ref, kb_ref, hif_ref, hib_ref, lof_ref, lob_ref, v_ref, g_ref,
                 ckf_ref, ckb_ref, chif_ref, chib_ref, clof_ref, clob_ref, cv_ref, ng_ref, trif_ref, trib_ref,
                 keepf_ref, keepb_ref, y_ref, of_ref, ob_ref, *, seq, ctx_len):
    r = HG_TILE
    n_lat = seq // r
    n_ctx = ctx_len // r
    group = HG_GROUP

    def ctx_body(i, carry):
        sl_f = pl.ds(pl.multiple_of(i * r, r), r)
        sl_b = pl.ds(pl.multiple_of((n_ctx - 1 - i) * r, r), r)
        jobs = [(0, chif_ref[0, sl_f, :], clof_ref[0, sl_f, :], ckf_ref[0, sl_f, :], cv_ref[0, sl_f, :], None,
                 trif_ref[...], None, True),
                (1, chib_ref[0, sl_b, :], clob_ref[0, sl_b, :], ckb_ref[0, sl_b, :], cv_ref[0, sl_b, :], None,
                 trib_ref[...], None, False)]
        _, states = _gla_group(jobs, carry)
        return tuple(states)

    zero = jnp.zeros((HG_D, HG_D), F32)
    carry = lax.fori_loop(0, n_ctx, ctx_body, (zero, zero))

    def lat_body(i, carry):
        jobs, slices = [], []
        for t in range(group):
            sl_f = pl.ds(pl.multiple_of((i * group + t) * r, r), r)
            sl_b = pl.ds(pl.multiple_of((n_lat - 1 - (i * group + t)) * r, r), r)
            jobs.append((0, hif_ref[0, sl_f, :], lof_ref[0, sl_f, :], kf_ref[0, sl_f, :], v_ref[0, sl_f, :],
                         q_ref[0, sl_f, :], trif_ref[...], keepf_ref[...], True))
            jobs.append((1, hib_ref[0, sl_b, :], lob_ref[0, sl_b, :], kb_ref[0, sl_b, :], v_ref[0, sl_b, :],
                         q_ref[0, sl_b, :], trib_ref[...], keepb_ref[...], False))
            slices += [(of_ref, sl_f), (ob_ref, sl_b)]
        outs, states = _gla_group(jobs, carry)
        for (ref, sl), o in zip(slices, outs):
            ref[sl, :] = o
        return tuple(states)

    lax.fori_loop(0, n_lat // group, lat_body, carry)

    ng = ng_ref[...]

    def out_body(i, _):
        sl = pl.ds(pl.multiple_of(i * r, r), r)
        o = of_ref[sl, :] + ob_ref[sl, :]
        o = o * lax.rsqrt(jnp.mean(o * o, axis=-1, keepdims=True) + RMS_EPS) * ng
        g = g_ref[0, sl, :].astype(F32)
        y_ref[0, sl, :] = (o * (g * _sigmoid(g))).astype(y_ref.dtype)
        return 0

    lax.fori_loop(0, n_lat, out_body, 0)


def _chunk_tri(r, forward, dtype):
    i = np.arange(r)
    same = (i[:, None] // CHUNK) == (i[None, :] // CHUNK)
    order = (i[:, None] >= i[None, :]) if forward else (i[:, None] <= i[None, :])
    return jnp.asarray(same & order, dtype)


def _hgrn(q, k, hi, lo, v, g, ck, chi, clo, cv, ng):
    b, s, _ = q.shape
    lc = ck.shape[1]
    h = HG_HEADS
    d = HG_D
    r = HG_TILE
    kern = functools.partial(_hgrn_kernel, seq=s, ctx_len=lc)
    fwd = lambda i, j: (i, 0, j)
    bwd = lambda i, j: (i, 0, j + h)
    const = lambda i, j: (0, 0)
    lat = lambda im: pl.BlockSpec((1, s, d), im)
    cx = lambda im: pl.BlockSpec((1, lc, d), im)
    return pl.pallas_call(
        kern,
        grid=(b, h),
        in_specs=[lat(fwd), lat(fwd), lat(bwd), lat(fwd), lat(bwd), lat(fwd), lat(bwd), lat(fwd), lat(fwd),
                  cx(fwd), cx(bwd), cx(fwd), cx(bwd), cx(fwd), cx(bwd), cx(fwd),
                  pl.BlockSpec((1, d), lambda i, j: (0, j)),
                  pl.BlockSpec((r, r), const),
                  pl.BlockSpec((r, r), const),
                  pl.BlockSpec((r, r), const),
                  pl.BlockSpec((r, r), const)],
        out_specs=pl.BlockSpec((1, s, d), fwd),
        out_shape=jax.ShapeDtypeStruct((b, s, h * d), BF16),
        scratch_shapes=[pltpu.VMEM((s, d), F32), pltpu.VMEM((s, d), F32)],
        compiler_params=_cparams(("parallel", "parallel"), VMEM_LIMIT),
        name="hgrn2",
    )(q, k, k, hi, hi, lo, lo, v, g, ck, ck, chi, chi, clo, clo, cv, ng,
      _chunk_tri(r, True, BF16), _chunk_tri(r, False, BF16), _chunk_tri(r, True, F32), _chunk_tri(r, False, F32))


def _dft_tables(seq):
    rows = seq // GRID_W
    n = F_GROUP_DIM
    kc = np.outer(np.arange(n), np.arange(n)) % n
    ang = 2.0 * np.pi * kc / n
    norm = 1.0 / math.sqrt(rows * GRID_W * n)
    ch = np.concatenate([np.cos(ang), -np.sin(ang)], axis=1) * norm
    t = np.arange(seq)
    r, w = t // GRID_W, t % GRID_W
    m = (np.outer(r, r) * GRID_W + np.outer(w, w) * rows) % (rows * GRID_W)
    ang_t = 2.0 * np.pi * m / (rows * GRID_W)
    tok = np.concatenate([np.cos(ang_t), np.sin(ang_t)], axis=1)
    return jnp.asarray(ch, BF16), jnp.asarray(tok, BF16)


def _fourier_kernel(p_ref, ch_ref, tok_ref, y_ref, xs_ref, *, seq):
    for gi in range(F_GROUPS):
        cols = slice(gi * F_GROUP_DIM, (gi + 1) * F_GROUP_DIM)
        x1 = _dot(p_ref[0, :, cols], ch_ref[...]).astype(BF16)
        xs_ref[0:seq, cols] = x1[:, :F_GROUP_DIM]
        xs_ref[seq:2 * seq, cols] = x1[:, F_GROUP_DIM:]
    y_ref[0] = _dot(tok_ref[...], xs_ref[...]).astype(y_ref.dtype)


def _fourier(pf):
    b, s, wdt = pf.shape
    ch, tok = _dft_tables(s)
    kern = functools.partial(_fourier_kernel, seq=s)
    return pl.pallas_call(
        kern,
        grid=(b,),
        in_specs=[pl.BlockSpec((1, s, wdt), lambda i: (i, 0, 0)),
                  pl.BlockSpec(ch.shape, lambda i: (0, 0)),
                  pl.BlockSpec(tok.shape, lambda i: (0, 0), pipeline_mode=pl.Buffered(1))],
        out_specs=pl.BlockSpec((1, s, wdt), lambda i: (i, 0, 0)),
        out_shape=jax.ShapeDtypeStruct((b, s, wdt), BF16),
        scratch_shapes=[pltpu.VMEM((2 * s, wdt), BF16)],
        compiler_params=_cparams(("parallel",), VMEM_LIMIT),
        name="fourier",
    )(pf, ch, tok)


def _layer_norm(t, g, b):
    mu = jnp.mean(t, axis=-1, keepdims=True)
    tc = t - mu
    var = jnp.mean(tc * tc, axis=-1, keepdims=True)
    return tc * lax.rsqrt(var + LN_EPS) * g + b


def _merge_kernel(yf_ref, yh_ref, gt_ref, x_ref, g1_ref, sc2_ref, sh2_ref, wfo_ref, who_ref, wo_ref,
                  lg_ref, lbias_ref, wr_ref, br_ref, x1_ref, idx_ref, wt_ref):
    d = x_ref.shape[-1]
    gf = _sigmoid(gt_ref[0, :, :d].astype(F32))
    gh = _sigmoid(gt_ref[0, :, d:].astype(F32))
    m = gf * _dot(yf_ref[0], wfo_ref[...]) + gh * _dot(yh_ref[0], who_ref[...])
    mix = _dot(m.astype(BF16), wo_ref[...])
    x1 = _layer_norm(DEEPNORM_ALPHA * x_ref[0] + g1_ref[0] * mix, lg_ref[...], lbias_ref[...])
    x1_ref[0] = x1
    u2 = x1 * (1.0 + sc2_ref[0]) + sh2_ref[0]
    logits = jnp.dot(u2, wr_ref[...], preferred_element_type=F32,
                     precision=lax.Precision.HIGHEST) + br_ref[...]
    lane = lax.broadcasted_iota(jnp.int32, logits.shape, 1)
    work = jnp.where(lane < N_EXPERTS, logits, -jnp.inf)
    vals, idxs = [], []
    for _ in range(TOP_K):
        mx = jnp.max(work, axis=-1, keepdims=True)
        sel = jnp.min(jnp.where(work == mx, lane, LANES), axis=-1, keepdims=True)
        vals.append(mx)
        idxs.append(sel)
        work = jnp.where(lane == sel, -jnp.inf, work)
    exps = [jnp.exp(vv - vals[0]) for vv in vals]
    inv = 1.0 / (exps[0] + exps[1] + exps[2] + exps[3])
    for kk in range(TOP_K):
        idx_ref[0, :, kk:kk + 1] = idxs[kk]
        wt_ref[0, :, kk:kk + 1] = exps[kk] * inv


def _merge(yf, yh, gates, x, g1, sc2, sh2, wfo, who, wo, lg, lbias, wr, br, tm):
    b, s, d = x.shape
    tile = lambda i, j: (i, j, 0)
    per_b = lambda i, j: (i, 0, 0)
    const = lambda i, j: (0, 0)
    return pl.pallas_call(
        _merge_kernel,
        grid=(b, s // tm),
        in_specs=[pl.BlockSpec((1, tm, yf.shape[-1]), tile),
                  pl.BlockSpec((1, tm, yh.shape[-1]), tile),
                  pl.BlockSpec((1, tm, gates.shape[-1]), tile),
                  pl.BlockSpec((1, tm, d), tile),
                  pl.BlockSpec((1, 1, d), per_b),
                  pl.BlockSpec((1, 1, d), per_b),
                  pl.BlockSpec((1, 1, d), per_b),
                  pl.BlockSpec(wfo.shape, const),
                  pl.BlockSpec(who.shape, const),
                  pl.BlockSpec(wo.shape, const),
                  pl.BlockSpec((1, d), const),
                  pl.BlockSpec((1, d), const),
                  pl.BlockSpec(wr.shape, const),
                  pl.BlockSpec(br.shape, const)],
        out_specs=[pl.BlockSpec((1, tm, d), tile),
                   pl.BlockSpec((1, tm, TOP_K), tile),
                   pl.BlockSpec((1, tm, TOP_K), tile)],
        out_shape=[jax.ShapeDtypeStruct((b, s, d), F32),
                   jax.ShapeDtypeStruct((b, s, TOP_K), jnp.int32),
                   jax.ShapeDtypeStruct((b, s, TOP_K), F32)],
        compiler_params=_cparams(("parallel", "parallel"), VMEM_LIMIT),
        name="merge_ln_router",
    )(yf, yh, gates, x, g1, sc2, sh2, wfo, who, wo, lg, lbias, wr, br)


def _lane_cumsum_exclusive(x):
    lane = lax.broadcasted_iota(jnp.int32, x.shape, 1)
    inc = x
    sh = 1
    while sh < LANES:
        inc = inc + jnp.where(lane >= sh, pltpu.roll(inc, sh, 1), 0)
        sh *= 2
    return inc - x


def _route_kernel(idx_ref, dest_ref, cnt_ref, start_ref, run_ref, *, block_rows):
    phase = pl.program_id(0)
    t = pl.program_id(1)
    tt = idx_ref.shape[0]
    lane = lax.broadcasted_iota(jnp.int32, (tt, LANES), 1)
    onehots = [jnp.where(lane == idx_ref[:, kk:kk + 1], 1.0, 0.0) for kk in range(TOP_K)]
    colsums = [jnp.sum(oh, axis=0, keepdims=True) for oh in onehots]

    @pl.when(jnp.logical_and(phase == 0, t == 0))
    def _():
        run_ref[...] = jnp.zeros_like(run_ref)

    @pl.when(phase == 0)
    def _():
        run_ref[...] += colsums[0] + colsums[1] + colsums[2] + colsums[3]

    @pl.when(jnp.logical_and(phase == 1, t == 0))
    def _():
        cnt = run_ref[...].astype(jnp.int32)
        padded = (cnt + (block_rows - 1)) // block_rows * block_rows
        cnt_ref[...] = cnt
        start_ref[...] = _lane_cumsum_exclusive(padded)
        run_ref[...] = jnp.zeros_like(run_ref)

    @pl.when(phase == 1)
    def _():
        r = lax.broadcasted_iota(jnp.int32, (tt, tt), 0)
        c = lax.broadcasted_iota(jnp.int32, (tt, tt), 1)
        strict = jnp.where(r > c, 1.0, 0.0).astype(BF16)
        base = run_ref[...] + start_ref[...].astype(F32)
        for kk in range(TOP_K):
            prefix = _dot(strict, onehots[kk].astype(BF16))
            rank = jnp.sum(onehots[kk] * (prefix + base), axis=-1, keepdims=True)
            dest_ref[:, kk:kk + 1] = rank.astype(jnp.int32)
            base = base + colsums[kk]
        run_ref[...] = base - start_ref[...].astype(F32)


def _route(idx, block_rows, tt):
    n = idx.shape[0]
    kern = functools.partial(_route_kernel, block_rows=block_rows)
    return pl.pallas_call(
        kern,
        grid=(2, n // tt),
        in_specs=[pl.BlockSpec((tt, TOP_K), lambda p, t: (t, 0))],
        out_specs=[pl.BlockSpec((tt, TOP_K), lambda p, t: (t * p, 0)),
                   pl.BlockSpec((1, LANES), lambda p, t: (0, 0)),
                   pl.BlockSpec((1, LANES), lambda p, t: (0, 0))],
        out_shape=[jax.ShapeDtypeStruct((n, TOP_K), jnp.int32),
                   jax.ShapeDtypeStruct((1, LANES), jnp.int32),
                   jax.ShapeDtypeStruct((1, LANES), jnp.int32)],
        scratch_shapes=[pltpu.VMEM((1, LANES), F32)],
        compiler_params=_cparams(("arbitrary", "arbitrary")),
        name="route_ranks",
    )(idx)


def _dispatch_kernel(cnt_ref, start_ref, dest_ref, x1_ref, sc_ref, sh_ref, h_ref, u_ref, zero_ref, sem, *, block_rows):
    tt = x1_ref.shape[0]
    u_ref[...] = x1_ref[...] * (1.0 + sc_ref[0]) + sh_ref[0]

    def row_copy(i, kk):
        d = dest_ref[0, 0, i * TOP_K + kk]
        return pltpu.make_async_copy(u_ref.at[pl.ds(i, 1), :], h_ref.at[pl.ds(d, 1), :], sem)

    def issue(i, _):
        for kk in range(TOP_K):
            row_copy(i, kk).start()
        return 0

    lax.fori_loop(0, tt, issue, 0)

    def drain(i, _):
        for kk in range(TOP_K):
            row_copy(i, kk).wait()
        return 0

    lax.fori_loop(0, tt, drain, 0)

    @pl.when(pl.program_id(0) == pl.num_programs(0) - 1)
    def _():
        zero_ref[...] = jnp.zeros_like(zero_ref)

        def pad_expert(e, _):
            cnt = cnt_ref[e]
            rem = (block_rows - (cnt & (block_rows - 1))) & (block_rows - 1)
            pos = start_ref[e] + cnt
            head = rem & (SUBLANES - 1)
            for i in range(SUBLANES - 1):
                @pl.when(i < head)
                def _(i=i):
                    cp = pltpu.make_async_copy(zero_ref.at[pl.ds(0, 1), :], h_ref.at[pl.ds(pos + i, 1), :], sem)
                    cp.start()
                    cp.wait()

            pos = pos + head
            size = block_rows // 2
            while size >= SUBLANES:
                take = rem & size

                @pl.when(take != 0)
                def _(pos=pos, size=size):
                    dst = h_ref.at[pl.ds(pl.multiple_of(pos, SUBLANES), size), :]
                    cp = pltpu.make_async_copy(zero_ref.at[pl.ds(0, size), :], dst, sem)
                    cp.start()
                    cp.wait()

                pos = pos + take
                size //= 2
            return 0

        lax.fori_loop(0, N_EXPERTS, pad_expert, 0)


def _dispatch(counts, starts, dest_flat, x1, sc2, sh2, n_rows, tt):
    n, d = x1.shape
    b = sc2.shape[0]
    tiles_per_b = (n // b) // tt
    per_b = lambda t, c, s: (t // tiles_per_b, 0, 0)
    grid_spec = pltpu.PrefetchScalarGridSpec(
        num_scalar_prefetch=2,
        grid=(n // tt,),
        in_specs=[pl.BlockSpec((1, 1, tt * TOP_K), lambda t, c, s: (t, 0, 0), memory_space=pltpu.SMEM),
                  pl.BlockSpec((tt, d), lambda t, c, s: (t, 0)),
                  pl.BlockSpec((1, 1, d), per_b),
                  pl.BlockSpec((1, 1, d), per_b)],
        out_specs=pl.BlockSpec(memory_space=pl.ANY),
        scratch_shapes=[pltpu.VMEM((tt, d), F32), pltpu.VMEM((MOE_ROWS // 2, d), F32), pltpu.SemaphoreType.DMA(())],
    )
    return pl.pallas_call(
        functools.partial(_dispatch_kernel, block_rows=MOE_ROWS),
        grid_spec=grid_spec,
        out_shape=jax.ShapeDtypeStruct((n_rows, d), F32),
        compiler_params=_cparams(("arbitrary",)),
        name="moe_dispatch",
    )(counts, starts, dest_flat, x1, sc2, sh2)


def _expert_kernel(be_ref, nu_ref, h_ref, wgu_ref, bgu_ref, wdn_ref, bdn_ref, o_ref):
    del be_ref
    d = h_ref.shape[-1]

    @pl.when(pl.program_id(0) < nu_ref[0])
    def _():
        gu = _dot(h_ref[...].astype(BF16), wgu_ref[0]) + bgu_ref[0]
        gate = jnp.minimum(gu[:, :d], SWIGLU_LIMIT)
        up = jnp.clip(gu[:, d:], -SWIGLU_LIMIT, SWIGLU_LIMIT)
        act = (up + 1.0) * gate * _sigmoid(SWIGLU_ALPHA * gate)
        o_ref[...] = _dot(act.astype(BF16), wdn_ref[0]) + bdn_ref[0]


def _experts(h_sorted, block_expert, n_used, wgu, bgu, wdn, bdn):
    n_rows, d = h_sorted.shape
    bm = MOE_ROWS
    n_blocks = n_rows // bm
    e = wgu.shape[0]
    blk = lambda j, be, nu: (jnp.minimum(j, nu[0] - 1), 0)
    exp = lambda j, be, nu: (be[jnp.minimum(j, nu[0] - 1)], 0, 0)
    grid_spec = pltpu.PrefetchScalarGridSpec(
        num_scalar_prefetch=2,
        grid=(n_blocks,),
        in_specs=[pl.BlockSpec((bm, d), blk),
                  pl.BlockSpec((1, d, 2 * d), exp),
                  pl.BlockSpec((1, 1, 2 * d), exp),
                  pl.BlockSpec((1, d, d), exp),
                  pl.BlockSpec((1, 1, d), exp)],
        out_specs=pl.BlockSpec((bm, d), blk),
    )
    return pl.pallas_call(
        _expert_kernel,
        grid_spec=grid_spec,
        out_shape=jax.ShapeDtypeStruct((n_rows, d), F32),
        compiler_params=_cparams(("arbitrary",), VMEM_LIMIT),
        name="moe_experts",
    )(block_expert, n_used, h_sorted, wgu, bgu.reshape(e, 1, 2 * d), wdn, bdn.reshape(e, 1, d))


def _combine_kernel(dest_ref, wt_ref, x1_ref, g2_ref, lg_ref, lbias_ref, o_hbm, y_ref, buf_ref, sem):
    tt = x1_ref.shape[0]

    def row_copy(i, kk):
        d = dest_ref[0, 0, i * TOP_K + kk]
        return pltpu.make_async_copy(o_hbm.at[pl.ds(d, 1), :], buf_ref.at[kk, pl.ds(i, 1), :], sem)

    def issue(i, _):
        for kk in range(TOP_K):
            row_copy(i, kk).start()
        return 0

    lax.fori_loop(0, tt, issue, 0)

    def drain(i, _):
        for kk in range(TOP_K):
            row_copy(i, kk).wait()
        return 0

    lax.fori_loop(0, tt, drain, 0)

    ff = wt_ref[:, 0:1] * buf_ref[0]
    for kk in range(1, TOP_K):
        ff = ff + wt_ref[:, kk:kk + 1] * buf_ref[kk]
    y_ref[...] = _layer_norm(DEEPNORM_ALPHA * x1_ref[...] + g2_ref[0] * ff, lg_ref[...], lbias_ref[...])


def _combine(dest_flat, wt, x1, g2, lg, lbias, o_sorted, tt):
    n, d = x1.shape
    b = g2.shape[0]
    tiles_per_b = (n // b) // tt
    const = lambda t: (0, 0)
    return pl.pallas_call(
        _combine_kernel,
        grid=(n // tt,),
        in_specs=[pl.BlockSpec((1, 1, tt * TOP_K), lambda t: (t, 0, 0), memory_space=pltpu.SMEM),
                  pl.BlockSpec((tt, TOP_K), lambda t: (t, 0)),
                  pl.BlockSpec((tt, d), lambda t: (t, 0)),
                  pl.BlockSpec((1, 1, d), lambda t: (t // tiles_per_b, 0, 0)),
                  pl.BlockSpec((1, d), const),
                  pl.BlockSpec((1, d), const),
                  pl.BlockSpec(memory_space=pl.ANY)],
        out_specs=pl.BlockSpec((tt, d), lambda t: (t, 0)),
        out_shape=jax.ShapeDtypeStruct((n, d), F32),
        scratch_shapes=[pltpu.VMEM((TOP_K, tt, d), F32), pltpu.SemaphoreType.DMA(())],
        compiler_params=_cparams(("arbitrary",), VMEM_LIMIT),
        name="moe_combine",
    )(dest_flat, wt, x1, g2, lg, lbias, o_sorted)


def kernel(x, c, ctx, c_ctx, w_ada, b_ada, w_in, lb_raw, hg_norm_g, w_four_out, w_hg_out, w_o, ln1_g, ln1_b,
           w_router, b_router, w_gate_up, b_gate_up, w_down, b_down, ln2_g, ln2_b):
    b, s, d = x.shape
    lc = ctx.shape[1]
    n = b * s
    f_w = F_GROUPS * F_GROUP_DIM
    hk = HG_HEADS * HG_D
    assert w_ada.shape[0] == DEPTH and s % GRID_W == 0 and s % (HG_TILE * HG_GROUP) == 0 and s % 512 == 0 and lc % HG_TILE == 0

    pad = (-(b + 1)) % 8
    c_rows = jnp.concatenate([c, c_ctx[None, :], jnp.zeros((pad, d), F32)], axis=0)
    mod = _ada(c_rows, w_ada[0], b_ada[0]).reshape(b + 1 + pad, N_MOD, 1, d)
    shift1, scale1, gate1, shift2, scale2, gate2 = (mod[:b, i] for i in range(N_MOD))
    cshift1 = jnp.broadcast_to(mod[b, 0], (b, 1, d))
    cscale1 = jnp.broadcast_to(mod[b, 1], (b, 1, d))

    lower_bounds = jnp.cumsum(jax.nn.softmax(lb_raw.astype(F32), axis=0), axis=0)[0]

    w_in_b = w_in[0].astype(BF16)
    o_f, o_q, o_z, o_v, o_g, o_gt = f_w, f_w + hk, f_w + 3 * hk, f_w + 4 * hk, f_w + 5 * hk, f_w + 5 * hk + 2 * d
    lb_row = lower_bounds.reshape(1, 2 * hk)
    pf, pq, pk, phi, plo, pv, pg, gates = _inproj(
        x, scale1, shift1, lb_row, w_in_b,
        [(0, f_w, "cast"), (o_f, hk, "silu"), (o_q, 2 * hk, "forget"), (o_z, hk, "cast"), (o_v, hk, "cast"),
         (o_g, 2 * d, "cast")], tm=512)
    w_ctx = w_in_b[:, o_q:o_v]
    ck, chi, clo, cv = _inproj(ctx, cscale1, cshift1, lb_row, w_ctx, [(0, 2 * hk, "forget"), (2 * hk, hk, "cast")], tm=lc)

    y_h = _hgrn(pq, pk, phi, plo, pv, pg, ck, chi, clo, cv, hg_norm_g[0].reshape(1, hk))
    y_f = _fourier(pf)

    wr = jnp.zeros((d, LANES), F32).at[:, :N_EXPERTS].set(w_router[0])
    br = jnp.zeros((1, LANES), F32).at[0, :N_EXPERTS].set(b_router[0])
    x1, idx, wt = _merge(y_f, y_h, gates, x, gate1, scale2, shift2,
                         w_four_out[0].astype(BF16), w_hg_out[0].astype(BF16), w_o[0].astype(BF16),
                         ln1_g[0].reshape(1, d), ln1_b[0].reshape(1, d), wr, br, tm=512)

    idx2 = idx.reshape(n, TOP_K)
    dest, counts, starts = _route(idx2, MOE_ROWS, 512)
    n_blocks = -(-(n * TOP_K) // MOE_ROWS) + N_EXPERTS
    n_rows = n_blocks * MOE_ROWS
    cnt = counts[0, :N_EXPERTS]
    st = starts[0, :N_EXPERTS]
    pad_ends = st + (cnt + MOE_ROWS - 1) // MOE_ROWS * MOE_ROWS
    block_first_row = jnp.arange(n_blocks, dtype=jnp.int32) * MOE_ROWS
    block_expert = jnp.minimum(jnp.sum((pad_ends[None, :] <= block_first_row[:, None]).astype(jnp.int32), axis=1),
                               N_EXPERTS - 1)
    n_used = (pad_ends[-1] // MOE_ROWS).astype(jnp.int32).reshape(1)

    tt = 256
    dest_flat = dest.reshape(n // tt, 1, tt * TOP_K)
    x1_2d = x1.reshape(n, d)
    h_sorted = _dispatch(cnt, st, dest_flat, x1_2d, scale2, shift2, n_rows, tt)
    o_sorted = _experts(h_sorted, block_expert, n_used, w_gate_up[0].astype(BF16), b_gate_up[0],
                        w_down[0].astype(BF16), b_down[0])
    out = _combine(dest_flat, wt.reshape(n, TOP_K), x1_2d, gate2, ln2_g[0].reshape(1, d), ln2_b[0].reshape(1, d),
                   o_sorted, tt)
    return out.reshape(b, s, d)
```

```python
import functools
import math

import jax
import jax.numpy as jnp
import numpy as np
from jax import lax
from jax.experimental import pallas as pl
from jax.experimental.pallas import tpu as pltpu

F32 = jnp.float32
BF16 = jnp.bfloat16

GRID_W = 64
N_MOD = 6
F_GROUPS = 4
F_GROUP_DIM = 128
HG_HEADS = 4
HG_D = 128
CHUNK = 64
N_EXPERTS = 32
TOP_K = 4
SWIGLU_LIMIT = 7.0
SWIGLU_ALPHA = 1.702
LN_EPS = 1e-5
RMS_EPS = 1e-6
DEPTH = 1
DEEPNORM_ALPHA = (2.0 * DEPTH) ** 0.25

LANES = 128
SUBLANES = 8
MOE_ROWS = 512
VMEM_LIMIT = 56 * 1024 * 1024


def _cparams(sem, vmem=None):
    return pltpu.CompilerParams(dimension_semantics=sem, vmem_limit_bytes=vmem)


def _dot(a, b):
    return jnp.dot(a, b, preferred_element_type=F32)


def _sigmoid(x):
    return 1.0 / (1.0 + jnp.exp(-x))


def _ada_kernel(c_ref, w_ref, b_ref, o_ref):
    c = c_ref[...]
    a = c * _sigmoid(c)
    o_ref[...] = jnp.dot(a, w_ref[...], preferred_element_type=F32,
                         precision=lax.Precision.HIGHEST) + b_ref[...]


def _ada(c_rows, w, b):
    r, d = c_rows.shape
    n = w.shape[1]
    tn = 512
    return pl.pallas_call(
        _ada_kernel,
        grid=(n // tn,),
        in_specs=[pl.BlockSpec((r, d), lambda j: (0, 0)),
                  pl.BlockSpec((d, tn), lambda j: (0, j)),
                  pl.BlockSpec((1, tn), lambda j: (0, j))],
        out_specs=pl.BlockSpec((r, tn), lambda j: (0, j)),
        out_shape=jax.ShapeDtypeStruct((r, n), F32),
        compiler_params=_cparams(("arbitrary",)),
        name="ada_mod",
    )(c_rows, w, b.reshape(1, n))


def _inproj_kernel(x_ref, sc_ref, sh_ref, lb_ref, w_ref, *o_refs, plan):
    u = (x_ref[0] * (1.0 + sc_ref[0]) + sh_ref[0]).astype(BF16)
    refs = iter(o_refs)
    for kind, chunks in plan:
        if kind == "forget":
            k_ref, hi_ref, lo_ref = next(refs), next(refs), next(refs)
            for (w0, o0, n) in chunks:
                lb = lb_ref[:, o0:o0 + n]
                f = lb + (1.0 - lb) * _sigmoid(_dot(u, w_ref[:, w0:w0 + n]))
                lf = jnp.log(f)
                hi = lf.astype(BF16)
                k_ref[0, :, o0:o0 + n] = (1.0 - f).astype(BF16)
                hi_ref[0, :, o0:o0 + n] = hi
                lo_ref[0, :, o0:o0 + n] = (lf - hi.astype(F32)).astype(BF16)
        else:
            o_ref = next(refs)
            for (w0, o0, n) in chunks:
                p = _dot(u, w_ref[:, w0:w0 + n])
                if kind == "silu":
                    p = p * _sigmoid(p)
                o_ref[0, :, o0:o0 + n] = p.astype(o_ref.dtype)


def _inproj(x, scale, shift, lb_row, w_bf16, outs, tm):
    b, s, d = x.shape
    n_w = w_bf16.shape[1]
    plan, widths = [], []
    for (c0, width, kind) in outs:
        step = min(width, 512)
        plan.append((kind, tuple((c0 + o, o, step) for o in range(0, width, step))))
        widths += [width] * (3 if kind == "forget" else 1)
    kern = functools.partial(_inproj_kernel, plan=tuple(plan))
    return pl.pallas_call(
        kern,
        grid=(b, s // tm),
        in_specs=[pl.BlockSpec((1, tm, d), lambda i, j: (i, j, 0)),
                  pl.BlockSpec((1, 1, d), lambda i, j: (i, 0, 0)),
                  pl.BlockSpec((1, 1, d), lambda i, j: (i, 0, 0)),
                  pl.BlockSpec(lb_row.shape, lambda i, j: (0, 0)),
                  pl.BlockSpec((d, n_w), lambda i, j: (0, 0))],
        out_specs=[pl.BlockSpec((1, tm, width), lambda i, j: (i, j, 0)) for width in widths],
        out_shape=[jax.ShapeDtypeStruct((b, s, width), BF16) for width in widths],
        compiler_params=_cparams(("parallel", "parallel"), VMEM_LIMIT),
        name="in_proj",
    )(x, scale, shift, lb_row, w_bf16)


HG_TILE = 256
HG_GROUP = 2


def _gla_group(jobs, states):
    states = list(states)
    nj = len(jobs)
    r, dk = jobs[0][1].shape
    dv = jobs[0][4].shape[1]
    nc = r // CHUNK
    zero = jnp.zeros((CHUNK, dk), BF16)
    contract_last = (((1,), (1,)), ((), ()))
    contract_rows = (((0,), (0,)), ((), ()))

    cums = []
    for (_, hi, lo, _, _, _, tri, _, _) in jobs:
        both = _dot(tri, jnp.concatenate([hi, lo], axis=1))
        cums.append(both[:, :dk] + both[:, dk:])

    k_inv_b, decays, k_end_blk, q_dec = [], [], [], []
    for cum, (_, _, _, k, _, q, _, _, forward) in zip(cums, jobs):
        k_inv = k.astype(F32) * jnp.exp(-cum)
        k_inv_b.append(k_inv.astype(BF16))
        tot_row = CHUNK - 1 if forward else 0
        dec_j, rows_j = [], []
        for ci in range(nc):
            dec = jnp.exp(cum[ci * CHUNK + tot_row:ci * CHUNK + tot_row + 1, :])
            k_end = (k_inv[ci * CHUNK:(ci + 1) * CHUNK] * dec).astype(BF16)
            dec_j.append(dec)
            rows_j.append(jnp.concatenate([k_end if cj == ci else zero for cj in range(nc)], axis=1))
        decays.append(dec_j)
        k_end_blk.append(jnp.concatenate(rows_j, axis=0))
        q_dec.append(None if q is None else (q.astype(F32) * jnp.exp(cum)).astype(BF16))

    upd_all = [lax.dot_general(job[4], blk, contract_rows, preferred_element_type=F32)
               for job, blk in zip(jobs, k_end_blk)]
    scores = [None if qd is None else lax.dot_general(qd, kb, contract_last, preferred_element_type=F32)
              for qd, kb in zip(q_dec, k_inv_b)]

    s_all = []
    for ji in range(nj):
        chain, forward = jobs[ji][0], jobs[ji][8]
        st = states[chain]
        entering = [None] * nc
        for ci in (range(nc) if forward else reversed(range(nc))):
            entering[ci] = st
            st = st * decays[ji][ci] + upd_all[ji][:, ci * dk:(ci + 1) * dk]
        states[chain] = st
        s_all.append(None if q_dec[ji] is None
                     else jnp.concatenate([e.astype(BF16) for e in entering], axis=0))

    outs = []
    for ji in range(nj):
        if q_dec[ji] is None:
            outs.append(None)
            continue
        sc = jnp.where(jobs[ji][7] > 0, scores[ji], 0.0).astype(BF16)
        intra = _dot(sc, jobs[ji][4])
        inter = lax.dot_general(q_dec[ji], s_all[ji], contract_last, preferred_element_type=F32)
        inter = jnp.concatenate([inter[ci * CHUNK:(ci + 1) * CHUNK, ci * dv:(ci + 1) * dv] for ci in range(nc)],
                                axis=0)
        outs.append(intra + inter)
    return outs, states


def _hgrn_kernel(q_ref, kf_ref, kb_ref, hif_ref, hib_ref, lof_ref, lob_ref, v_ref, g_ref,
                 ckf_ref, ckb_ref, chif_ref, chib_ref, clof_ref, clob_ref, cv_ref, ng_ref, trif_ref, trib_ref,
                 keepf_ref, keepb_ref, y_ref, of_ref, ob_ref, *, seq, ctx_len):
    r = HG_TILE
    n_lat = seq // r
    n_ctx = ctx_len // r
    group = HG_GROUP

    def ctx_body(i, carry):
        sl_f = pl.ds(pl.multiple_of(i * r, r), r)
        sl_b = pl.ds(pl.multiple_of((n_ctx - 1 - i) * r, r), r)
        jobs = [(0, chif_ref[0, sl_f, :], clof_ref[0, sl_f, :], ckf_ref[0, sl_f, :], cv_ref[0, sl_f, :], None,
                 trif_ref[...], None, True),
                (1, chib_ref[0, sl_b, :], clob_ref[0, sl_b, :], ckb_ref[0, sl_b, :], cv_ref[0, sl_b, :], None,
                 trib_ref[...], None, False)]
        _, states = _gla_group(jobs, carry)
        return tuple(states)

    zero = jnp.zeros((HG_D, HG_D), F32)
    carry = lax.fori_loop(0, n_ctx, ctx_body, (zero, zero))

    def lat_body(i, carry):
        jobs, slices = [], []
        for t in range(group):
            sl_f = pl.ds(pl.multiple_of((i * group + t) * r, r), r)
            sl_b = pl.ds(pl.multiple_of((n_lat - 1 - (i * group + t)) * r, r), r)
            jobs.append((0, hif_ref[0, sl_f, :], lof_ref[0, sl_f, :], kf_ref[0, sl_f, :], v_ref[0, sl_f, :],
                         q_ref[0, sl_f, :], trif_ref[...], keepf_ref[...], True))
            jobs.append((1, hib_ref[0, sl_b, :], lob_ref[0, sl_b, :], kb_ref[0, sl_b, :], v_ref[0, sl_b, :],
                         q_ref[0, sl_b, :], trib_ref[...], keepb_ref[...], False))
            slices += [(of_ref, sl_f), (ob_ref, sl_b)]
        outs, states = _gla_group(jobs, carry)
        for (ref, sl), o in zip(slices, outs):
            ref[sl, :] = o
        return tuple(states)

    lax.fori_loop(0, n_lat // group, lat_body, carry)

    ng = ng_ref[...]

    def out_body(i, _):
        sl = pl.ds(pl.multiple_of(i * r, r), r)
        o = of_ref[sl, :] + ob_ref[sl, :]
        o = o * lax.rsqrt(jnp.mean(o * o, axis=-1, keepdims=True) + RMS_EPS) * ng
        g = g_ref[0, sl, :].astype(F32)
        y_ref[0, sl, :] = (o * (g * _sigmoid(g))).astype(y_ref.dtype)
        return 0

    lax.fori_loop(0, n_lat, out_body, 0)


def _chunk_tri(r, forward, dtype):
    i = np.arange(r)
    same = (i[:, None] // CHUNK) == (i[None, :] // CHUNK)
    order = (i[:, None] >= i[None, :]) if forward else (i[:, None] <= i[None, :])
    return jnp.asarray(same & order, dtype)


def _hgrn(q, k, hi, lo, v, g, ck, chi, clo, cv, ng):
    b, s, _ = q.shape
    lc = ck.shape[1]
    h = HG_HEADS
    d = HG_D
    r = HG_TILE
    kern = functools.partial(_hgrn_kernel, seq=s, ctx_len=lc)
    fwd = lambda i, j: (i, 0, j)
    bwd = lambda i, j: (i, 0, j + h)
    const = lambda i, j: (0, 0)
    lat = lambda im: pl.BlockSpec((1, s, d), im)
    cx = lambda im: pl.BlockSpec((1, lc, d), im)
    return pl.pallas_call(
        kern,
        grid=(b, h),
        in_specs=[lat(fwd), lat(fwd), lat(bwd), lat(fwd), lat(bwd), lat(fwd), lat(bwd), lat(fwd), lat(fwd),
                  cx(fwd), cx(bwd), cx(fwd), cx(bwd), cx(fwd), cx(bwd), cx(fwd),
                  pl.BlockSpec((1, d), lambda i, j: (0, j)),
                  pl.BlockSpec((r, r), const),
                  pl.BlockSpec((r, r), const),
                  pl.BlockSpec((r, r), const),
                  pl.BlockSpec((r, r), const)],
        out_specs=pl.BlockSpec((1, s, d), fwd),
        out_shape=jax.ShapeDtypeStruct((b, s, h * d), BF16),
        scratch_shapes=[pltpu.VMEM((s, d), F32), pltpu.VMEM((s, d), F32)],
        compiler_params=_cparams(("parallel", "parallel"), VMEM_LIMIT),
        name="hgrn2",
    )(q, k, k, hi, hi, lo, lo, v, g, ck, ck, chi, chi, clo, clo, cv, ng,
      _chunk_tri(r, True, BF16), _chunk_tri(r, False, BF16), _chunk_tri(r, True, F32), _chunk_tri(r, False, F32))


def _dft_tables(seq):
    rows = seq // GRID_W
    n = F_GROUP_DIM
    kc = np.outer(np.arange(n), np.arange(n)) % n
    ang = 2.0 * np.pi * kc / n
    norm = 1.0 / math.sqrt(rows * GRID_W * n)
    ch = np.concatenate([np.cos(ang), -np.sin(ang)], axis=1) * norm
    t = np.arange(seq)
    r, w = t // GRID_W, t % GRID_W
    m = (np.outer(r, r) * GRID_W + np.outer(w, w) * rows) % (rows * GRID_W)
    ang_t = 2.0 * np.pi * m / (rows * GRID_W)
    tok = np.concatenate([np.cos(ang_t), np.sin(ang_t)], axis=1)
    return jnp.asarray(ch, BF16), jnp.asarray(tok, BF16)


def _fourier_kernel(p_ref, ch_ref, tok_ref, y_ref, xs_ref, *, seq):
    for gi in range(F_GROUPS):
        cols = slice(gi * F_GROUP_DIM, (gi + 1) * F_GROUP_DIM)
        x1 = _dot(p_ref[0, :, cols], ch_ref[...]).astype(BF16)
        xs_ref[0:seq, cols] = x1[:, :F_GROUP_DIM]
        xs_ref[seq:2 * seq, cols] = x1[:, F_GROUP_DIM:]
    y_ref[0] = _dot(tok_ref[...], xs_ref[...]).astype(y_ref.dtype)


def _fourier(pf):
    b, s, wdt = pf.shape
    ch, tok = _dft_tables(s)
    kern = functools.partial(_fourier_kernel, seq=s)
    return pl.pallas_call(
        kern,
        grid=(b,),
        in_specs=[pl.BlockSpec((1, s, wdt), lambda i: (i, 0, 0)),
                  pl.BlockSpec(ch.shape, lambda i: (0, 0)),
                  pl.BlockSpec(tok.shape, lambda i: (0, 0), pipeline_mode=pl.Buffered(1))],
        out_specs=pl.BlockSpec((1, s, wdt), lambda i: (i, 0, 0)),
        out_shape=jax.ShapeDtypeStruct((b, s, wdt), BF16),
        scratch_shapes=[pltpu.VMEM((2 * s, wdt), BF16)],
        compiler_params=_cparams(("parallel",), VMEM_LIMIT),
        name="fourier",
    )(pf, ch, tok)


def _layer_norm(t, g, b):
    mu = jnp.mean(t, axis=-1, keepdims=True)
    tc = t - mu
    var = jnp.mean(tc * tc, axis=-1, keepdims=True)
    return tc * lax.rsqrt(var + LN_EPS) * g + b


def _merge_kernel(yf_ref, yh_ref, gt_ref, x_ref, g1_ref, sc2_ref, sh2_ref, wfo_ref, who_ref, wo_ref,
                  lg_ref, lbias_ref, wr_ref, br_ref, x1_ref, idx_ref, wt_ref):
    d = x_ref.shape[-1]
    gf = _sigmoid(gt_ref[0, :, :d].astype(F32))
    gh = _sigmoid(gt_ref[0, :, d:].astype(F32))
    m = gf * _dot(yf_ref[0], wfo_ref[...]) + gh * _dot(yh_ref[0], who_ref[...])
    mix = _dot(m.astype(BF16), wo_ref[...])
    x1 = _layer_norm(DEEPNORM_ALPHA * x_ref[0] + g1_ref[0] * mix, lg_ref[...], lbias_ref[...])
    x1_ref[0] = x1
    u2 = x1 * (1.0 + sc2_ref[0]) + sh2_ref[0]
    logits = jnp.dot(u2, wr_ref[...], preferred_element_type=F32,
                     precision=lax.Precision.HIGHEST) + br_ref[...]
    lane = lax.broadcasted_iota(jnp.int32, logits.shape, 1)
    work = jnp.where(lane < N_EXPERTS, logits, -jnp.inf)
    vals, idxs = [], []
    for _ in range(TOP_K):
        mx = jnp.max(work, axis=-1, keepdims=True)
        sel = jnp.min(jnp.where(work == mx, lane, LANES), axis=-1, keepdims=True)
        vals.append(mx)
        idxs.append(sel)
        work = jnp.where(lane == sel, -jnp.inf, work)
    exps = [jnp.exp(vv - vals[0]) for vv in vals]
    inv = 1.0 / (exps[0] + exps[1] + exps[2] + exps[3])
    for kk in range(TOP_K):
        idx_ref[0, :, kk:kk + 1] = idxs[kk]
        wt_ref[0, :, kk:kk + 1] = exps[kk] * inv


def _merge(yf, yh, gates, x, g1, sc2, sh2, wfo, who, wo, lg, lbias, wr, br, tm):
    b, s, d = x.shape
    tile = lambda i, j: (i, j, 0)
    per_b = lambda i, j: (i, 0, 0)
    const = lambda i, j: (0, 0)
    return pl.pallas_call(
        _merge_kernel,
        grid=(b, s // tm),
        in_specs=[pl.BlockSpec((1, tm, yf.shape[-1]), tile),
                  pl.BlockSpec((1, tm, yh.shape[-1]), tile),
                  pl.BlockSpec((1, tm, gates.shape[-1]), tile),
                  pl.BlockSpec((1, tm, d), tile),
                  pl.BlockSpec((1, 1, d), per_b),
                  pl.BlockSpec((1, 1, d), per_b),
                  pl.BlockSpec((1, 1, d), per_b),
                  pl.BlockSpec(wfo.shape, const),
                  pl.BlockSpec(who.shape, const),
                  pl.BlockSpec(wo.shape, const),
                  pl.BlockSpec((1, d), const),
                  pl.BlockSpec((1, d), const),
                  pl.BlockSpec(wr.shape, const),
                  pl.BlockSpec(br.shape, const)],
        out_specs=[pl.BlockSpec((1, tm, d), tile),
                   pl.BlockSpec((1, tm, TOP_K), tile),
                   pl.BlockSpec((1, tm, TOP_K), tile)],
        out_shape=[jax.ShapeDtypeStruct((b, s, d), F32),
                   jax.ShapeDtypeStruct((b, s, TOP_K), jnp.int32),
                   jax.ShapeDtypeStruct((b, s, TOP_K), F32)],
        compiler_params=_cparams(("parallel", "parallel"), VMEM_LIMIT),
        name="merge_ln_router",
    )(yf, yh, gates, x, g1, sc2, sh2, wfo, who, wo, lg, lbias, wr, br)


def _lane_cumsum_exclusive(x):
    lane = lax.broadcasted_iota(jnp.int32, x.shape, 1)
    inc = x
    sh = 1
    while sh < LANES:
        inc = inc + jnp.where(lane >= sh, pltpu.roll(inc, sh, 1), 0)
        sh *= 2
    return inc - x


def _route_kernel(idx_ref, dest_ref, cnt_ref, start_ref, run_ref, *, block_rows):
    phase = pl.program_id(0)
    t = pl.program_id(1)
    tt = idx_ref.shape[0]
    lane = lax.broadcasted_iota(jnp.int32, (tt, LANES), 1)
    onehots = [jnp.where(lane == idx_ref[:, kk:kk + 1], 1.0, 0.0) for kk in range(TOP_K)]
    colsums = [jnp.sum(oh, axis=0, keepdims=True) for oh in onehots]

    @pl.when(jnp.logical_and(phase == 0, t == 0))
    def _():
        run_ref[...] = jnp.zeros_like(run_ref)

    @pl.when(phase == 0)
    def _():
        run_ref[...] += colsums[0] + colsums[1] + colsums[2] + colsums[3]

    @pl.when(jnp.logical_and(phase == 1, t == 0))
    def _():
        cnt = run_ref[...].astype(jnp.int32)
        padded = (cnt + (block_rows - 1)) // block_rows * block_rows
        cnt_ref[...] = cnt
        start_ref[...] = _lane_cumsum_exclusive(padded)
        run_ref[...] = jnp.zeros_like(run_ref)

    @pl.when(phase == 1)
    def _():
        r = lax.broadcasted_iota(jnp.int32, (tt, tt), 0)
        c = lax.broadcasted_iota(jnp.int32, (tt, tt), 1)
        strict = jnp.where(r > c, 1.0, 0.0).astype(BF16)
        base = run_ref[...] + start_ref[...].astype(F32)
        for kk in range(TOP_K):
            prefix = _dot(strict, onehots[kk].astype(BF16))
            rank = jnp.sum(onehots[kk] * (prefix + base), axis=-1, keepdims=True)
            dest_ref[:, kk:kk + 1] = rank.astype(jnp.int32)
            base = base + colsums[kk]
        run_ref[...] = base - start_ref[...].astype(F32)


def _route(idx, block_rows, tt):
    n = idx.shape[0]
    kern = functools.partial(_route_kernel, block_rows=block_rows)
    return pl.pallas_call(
        kern,
        grid=(2, n // tt),
        in_specs=[pl.BlockSpec((tt, TOP_K), lambda p, t: (t, 0))],
        out_specs=[pl.BlockSpec((tt, TOP_K), lambda p, t: (t * p, 0)),
                   pl.BlockSpec((1, LANES), lambda p, t: (0, 0)),
                   pl.BlockSpec((1, LANES), lambda p, t: (0, 0))],
        out_shape=[jax.ShapeDtypeStruct((n, TOP_K), jnp.int32),
                   jax.ShapeDtypeStruct((1, LANES), jnp.int32),
                   jax.ShapeDtypeStruct((1, LANES), jnp.int32)],
        scratch_shapes=[pltpu.VMEM((1, LANES), F32)],
        compiler_params=_cparams(("arbitrary", "arbitrary")),
        name="route_ranks",
    )(idx)


def _to_row_tiles(ref, base, val):
    rows = val.shape[0]
    for s in range(SUBLANES):
        ref[pl.ds(base + s, rows, stride=SUBLANES), :] = val[:, s * LANES:(s + 1) * LANES]


def _from_row_tiles(ref, base, rows):
    return jnp.concatenate([ref[pl.ds(base + s, rows, stride=SUBLANES), :] for s in range(SUBLANES)], axis=1)


def _row_tile(ref, r):
    return ref.at[pl.ds(pl.multiple_of(r * SUBLANES, SUBLANES), SUBLANES), :]


def _row_tiles(ref, r, n):
    return ref.at[pl.ds(pl.multiple_of(r * SUBLANES, SUBLANES), n * SUBLANES), :]


def _dispatch_kernel(cnt_ref, start_ref, dest_ref, x1_ref, sc_ref, sh_ref, h_ref, u_ref, zero_ref, sems, *, block_rows):
    t = pl.program_id(0)
    nt = pl.num_programs(0)
    tt = x1_ref.shape[0]
    slot = t % 2
    ubase = pl.multiple_of(slot * (tt * SUBLANES), tt * SUBLANES)
    _to_row_tiles(u_ref, ubase, x1_ref[...] * (1.0 + sc_ref[0]) + sh_ref[0])

    def issue(i, _):
        for kk in range(TOP_K):
            d = dest_ref[0, 0, i * TOP_K + kk]
            pltpu.make_async_copy(_row_tile(u_ref, slot * tt + i), _row_tile(h_ref, d), sems.at[slot]).start()
        return 0

    lax.fori_loop(0, tt, issue, 0)

    def drain(sl):
        for _ in range(TOP_K):
            pltpu.make_async_copy(_row_tiles(u_ref, sl * tt, tt), _row_tiles(h_ref, 0, tt), sems.at[sl]).wait()

    @pl.when(t > 0)
    def _():
        drain(1 - slot)

    @pl.when(t == nt - 1)
    def _():
        drain(slot)
        zero_ref[...] = jnp.zeros_like(zero_ref)

        def pad_expert(e, _):
            cnt = cnt_ref[e]
            rem = (block_rows - (cnt & (block_rows - 1))) & (block_rows - 1)
            pos = start_ref[e] + cnt
            size = block_rows // 2
            while size >= 1:
                take = rem & size

                @pl.when(take != 0)
                def _(pos=pos, size=size):
                    cp = pltpu.make_async_copy(_row_tiles(zero_ref, 0, size), _row_tiles(h_ref, pos, size), sems.at[2])
                    cp.start()
                    cp.wait()

                pos = pos + take
                size //= 2
            return 0

        lax.fori_loop(0, N_EXPERTS, pad_expert, 0)


def _dispatch(counts, starts, dest_flat, x1, sc2, sh2, n_rows, tt):
    n, d = x1.shape
    b = sc2.shape[0]
    tiles_per_b = (n // b) // tt
    per_b = lambda t, c, s: (t // tiles_per_b, 0, 0)
    grid_spec = pltpu.PrefetchScalarGridSpec(
        num_scalar_prefetch=2,
        grid=(n // tt,),
        in_specs=[pl.BlockSpec((1, 1, tt * TOP_K), lambda t, c, s: (t, 0, 0), memory_space=pltpu.SMEM),
                  pl.BlockSpec((tt, d), lambda t, c, s: (t, 0)),
                  pl.BlockSpec((1, 1, d), per_b),
                  pl.BlockSpec((1, 1, d), per_b)],
        out_specs=pl.BlockSpec(memory_space=pl.ANY),
        scratch_shapes=[pltpu.VMEM((2 * tt * SUBLANES, LANES), F32), pltpu.VMEM((MOE_ROWS // 2 * SUBLANES, LANES), F32),
                        pltpu.SemaphoreType.DMA((3,))],
    )
    return pl.pallas_call(
        functools.partial(_dispatch_kernel, block_rows=MOE_ROWS),
        grid_spec=grid_spec,
        out_shape=jax.ShapeDtypeStruct((n_rows * SUBLANES, LANES), F32),
        compiler_params=_cparams(("arbitrary",)),
        name="moe_dispatch",
    )(counts, starts, dest_flat, x1, sc2, sh2)


def _expert_kernel(be_ref, nu_ref, h_ref, wgu_ref, bgu_ref, wdn_ref, bdn_ref, o_ref):
    del be_ref
    d = wdn_ref.shape[-1]
    bm = h_ref.shape[0] // SUBLANES

    @pl.when(pl.program_id(0) < nu_ref[0])
    def _():
        gu = _dot(_from_row_tiles(h_ref, 0, bm).astype(BF16), wgu_ref[0]) + bgu_ref[0]
        gate = jnp.minimum(gu[:, :d], SWIGLU_LIMIT)
        up = jnp.clip(gu[:, d:], -SWIGLU_LIMIT, SWIGLU_LIMIT)
        act = (up + 1.0) * gate * _sigmoid(SWIGLU_ALPHA * gate)
        _to_row_tiles(o_ref, 0, _dot(act.astype(BF16), wdn_ref[0]) + bdn_ref[0])


def _experts(h_sorted, block_expert, n_used, wgu, bgu, wdn, bdn):
    n_rows = h_sorted.shape[0] // SUBLANES
    e, d, _ = wdn.shape
    bm = MOE_ROWS
    n_blocks = n_rows // bm
    blk = lambda j, be, nu: (jnp.minimum(j, nu[0] - 1), 0)
    exp = lambda j, be, nu: (be[jnp.minimum(j, nu[0] - 1)], 0, 0)
    grid_spec = pltpu.PrefetchScalarGridSpec(
        num_scalar_prefetch=2,
        grid=(n_blocks,),
        in_specs=[pl.BlockSpec((bm * SUBLANES, LANES), blk),
                  pl.BlockSpec((1, d, 2 * d), exp),
                  pl.BlockSpec((1, 1, 2 * d), exp),
                  pl.BlockSpec((1, d, d), exp),
                  pl.BlockSpec((1, 1, d), exp)],
        out_specs=pl.BlockSpec((bm * SUBLANES, LANES), blk),
    )
    return pl.pallas_call(
        _expert_kernel,
        grid_spec=grid_spec,
        out_shape=jax.ShapeDtypeStruct((n_rows * SUBLANES, LANES), F32),
        compiler_params=_cparams(("arbitrary",), VMEM_LIMIT),
        name="moe_experts",
    )(block_expert, n_used, h_sorted, wgu, bgu.reshape(e, 1, 2 * d), wdn, bdn.reshape(e, 1, d))


def _combine_kernel(dest_ref, dest_next_ref, wt_ref, x1_ref, g2_ref, lg_ref, lbias_ref, o_hbm, y_ref, buf_ref, sems):
    t = pl.program_id(0)
    nt = pl.num_programs(0)
    tt = x1_ref.shape[0]
    slot = t % 2

    def issue(d_ref, sl):
        def body(i, _):
            for kk in range(TOP_K):
                d = d_ref[0, 0, i * TOP_K + kk]
                pltpu.make_async_copy(_row_tile(o_hbm, d), _row_tile(buf_ref, (sl * TOP_K + kk) * tt + i),
                                      sems.at[sl]).start()
            return 0

        lax.fori_loop(0, tt, body, 0)

    @pl.when(t == 0)
    def _():
        issue(dest_ref, 0)

    @pl.when(t + 1 < nt)
    def _():
        issue(dest_next_ref, 1 - slot)

    for kk in range(TOP_K):
        pltpu.make_async_copy(_row_tiles(o_hbm, 0, tt), _row_tiles(buf_ref, (slot * TOP_K + kk) * tt, tt),
                              sems.at[slot]).wait()

    ff = None
    for kk in range(TOP_K):
        base = pl.multiple_of((slot * TOP_K + kk) * (tt * SUBLANES), tt * SUBLANES)
        term = wt_ref[:, kk:kk + 1] * _from_row_tiles(buf_ref, base, tt)
        ff = term if ff is None else ff + term
    y_ref[...] = _layer_norm(DEEPNORM_ALPHA * x1_ref[...] + g2_ref[0] * ff, lg_ref[...], lbias_ref[...])


def _combine(dest_flat, wt, x1, g2, lg, lbias, o_sorted, tt):
    n, d = x1.shape
    b = g2.shape[0]
    nt = n // tt
    tiles_per_b = (n // b) // tt
    const = lambda t: (0, 0)
    dest_spec = lambda im: pl.BlockSpec((1, 1, tt * TOP_K), im, memory_space=pltpu.SMEM)
    return pl.pallas_call(
        _combine_kernel,
        grid=(nt,),
        in_specs=[dest_spec(lambda t: (t, 0, 0)),
                  dest_spec(lambda t: (jnp.minimum(t + 1, nt - 1), 0, 0)),
                  pl.BlockSpec((tt, TOP_K), lambda t: (t, 0)),
                  pl.BlockSpec((tt, d), lambda t: (t, 0)),
                  pl.BlockSpec((1, 1, d), lambda t: (t // tiles_per_b, 0, 0)),
                  pl.BlockSpec((1, d), const),
                  pl.BlockSpec((1, d), const),
                  pl.BlockSpec(memory_space=pl.ANY)],
        out_specs=pl.BlockSpec((tt, d), lambda t: (t, 0)),
        out_shape=jax.ShapeDtypeStruct((n, d), F32),
        scratch_shapes=[pltpu.VMEM((2 * TOP_K * tt * SUBLANES, LANES), F32), pltpu.SemaphoreType.DMA((2,))],
        compiler_params=_cparams(("arbitrary",), VMEM_LIMIT),
        name="moe_combine",
    )(dest_flat, dest_flat, wt, x1, g2, lg, lbias, o_sorted)


def kernel(x, c, ctx, c_ctx, w_ada, b_ada, w_in, lb_raw, hg_norm_g, w_four_out, w_hg_out, w_o, ln1_g, ln1_b,
           w_router, b_router, w_gate_up, b_gate_up, w_down, b_down, ln2_g, ln2_b):
    b, s, d = x.shape
    lc = ctx.shape[1]
    n = b * s
    f_w = F_GROUPS * F_GROUP_DIM
    hk = HG_HEADS * HG_D
    assert w_ada.shape[0] == DEPTH and s % GRID_W == 0 and s % (HG_TILE * HG_GROUP) == 0 and s % 512 == 0 and lc % HG_TILE == 0

    pad = (-(b + 1)) % 8
    c_rows = jnp.concatenate([c, c_ctx[None, :], jnp.zeros((pad, d), F32)], axis=0)
    mod = _ada(c_rows, w_ada[0], b_ada[0]).reshape(b + 1 + pad, N_MOD, 1, d)
    shift1, scale1, gate1, shift2, scale2, gate2 = (mod[:b, i] for i in range(N_MOD))
    cshift1 = jnp.broadcast_to(mod[b, 0], (b, 1, d))
    cscale1 = jnp.broadcast_to(mod[b, 1], (b, 1, d))

    lower_bounds = jnp.cumsum(jax.nn.softmax(lb_raw.astype(F32), axis=0), axis=0)[0]

    w_in_b = w_in[0].astype(BF16)
    o_f, o_q, o_z, o_v, o_g, o_gt = f_w, f_w + hk, f_w + 3 * hk, f_w + 4 * hk, f_w + 5 * hk, f_w + 5 * hk + 2 * d
    lb_row = lower_bounds.reshape(1, 2 * hk)
    pf, pq, pk, phi, plo, pv, pg, gates = _inproj(
        x, scale1, shift1, lb_row, w_in_b,
        [(0, f_w, "cast"), (o_f, hk, "silu"), (o_q, 2 * hk, "forget"), (o_z, hk, "cast"), (o_v, hk, "cast"),
         (o_g, 2 * d, "cast")], tm=512)
    w_ctx = w_in_b[:, o_q:o_v]
    ck, chi, clo, cv = _inproj(ctx, cscale1, cshift1, lb_row, w_ctx, [(0, 2 * hk, "forget"), (2 * hk, hk, "cast")], tm=lc)

    y_h = _hgrn(pq, pk, phi, plo, pv, pg, ck, chi, clo, cv, hg_norm_g[0].reshape(1, hk))
    y_f = _fourier(pf)

    wr = jnp.zeros((d, LANES), F32).at[:, :N_EXPERTS].set(w_router[0])
    br = jnp.zeros((1, LANES), F32).at[0, :N_EXPERTS].set(b_router[0])
    x1, idx, wt = _merge(y_f, y_h, gates, x, gate1, scale2, shift2,
                         w_four_out[0].astype(BF16), w_hg_out[0].astype(BF16), w_o[0].astype(BF16),
                         ln1_g[0].reshape(1, d), ln1_b[0].reshape(1, d), wr, br, tm=512)

    idx2 = idx.reshape(n, TOP_K)
    dest, counts, starts = _route(idx2, MOE_ROWS, 512)
    n_blocks = -(-(n * TOP_K) // MOE_ROWS) + N_EXPERTS
    n_rows = n_blocks * MOE_ROWS
    cnt = counts[0, :N_EXPERTS]
    st = starts[0, :N_EXPERTS]
    pad_ends = st + (cnt + MOE_ROWS - 1) // MOE_ROWS * MOE_ROWS
    block_first_row = jnp.arange(n_blocks, dtype=jnp.int32) * MOE_ROWS
    block_expert = jnp.minimum(jnp.sum((pad_ends[None, :] <= block_first_row[:, None]).astype(jnp.int32), axis=1),
                               N_EXPERTS - 1)
    n_used = (pad_ends[-1] // MOE_ROWS).astype(jnp.int32).reshape(1)

    tt = 256
    dest_flat = dest.reshape(n // tt, 1, tt * TOP_K)
    x1_2d = x1.reshape(n, d)
    h_sorted = _dispatch(cnt, st, dest_flat, x1_2d, scale2, shift2, n_rows, tt)
    o_sorted = _experts(h_sorted, block_expert, n_used, w_gate_up[0].astype(BF16), b_gate_up[0],
                        w_down[0].astype(BF16), b_down[0])
    out = _combine(dest_flat, wt.reshape(n, TOP_K), x1_2d, gate2, ln2_g[0].reshape(1, d), ln2_b[0].reshape(1, d),
                   o_sorted, tt)
    return out.reshape(b, s, d)
```

```python
import functools
import math

import jax
import jax.numpy as jnp
import numpy as np
from jax import lax
from jax.experimental import pallas as pl
from jax.experimental.pallas import tpu as pltpu

F32 = jnp.float32
BF16 = jnp.bfloat16

GRID_W = 64
N_MOD = 6
F_GROUPS = 4
F_GROUP_DIM = 128
HG_HEADS = 4
HG_D = 128
CHUNK = 64
N_EXPERTS = 32
TOP_K = 4
SWIGLU_LIMIT = 7.0
SWIGLU_ALPHA = 1.702
LN_EPS = 1e-5
RMS_EPS = 1e-6
DEPTH = 1
DEEPNORM_ALPHA = (2.0 * DEPTH) ** 0.25

LANES = 128
SUBLANES = 8
MOE_ROWS = 512
VMEM_LIMIT = 56 * 1024 * 1024


def _cparams(sem, vmem=None):
    return pltpu.CompilerParams(dimension_semantics=sem, vmem_limit_bytes=vmem)


def _dot(a, b):
    return jnp.dot(a, b, preferred_element_type=F32)


def _sigmoid(x):
    return 1.0 / (1.0 + jnp.exp(-x))


def _ada_kernel(c_ref, w_ref, b_ref, o_ref):
    c = c_ref[...]
    a = c * _sigmoid(c)
    o_ref[...] = jnp.dot(a, w_ref[...], preferred_element_type=F32,
                         precision=lax.Precision.HIGHEST) + b_ref[...]


def _ada(c_rows, w, b):
    r, d = c_rows.shape
    n = w.shape[1]
    tn = 512
    return pl.pallas_call(
        _ada_kernel,
        grid=(n // tn,),
        in_specs=[pl.BlockSpec((r, d), lambda j: (0, 0)),
                  pl.BlockSpec((d, tn), lambda j: (0, j)),
                  pl.BlockSpec((1, tn), lambda j: (0, j))],
        out_specs=pl.BlockSpec((r, tn), lambda j: (0, j)),
        out_shape=jax.ShapeDtypeStruct((r, n), F32),
        compiler_params=_cparams(("arbitrary",)),
        name="ada_mod",
    )(c_rows, w, b.reshape(1, n))


def _inproj_kernel(x_ref, sc_ref, sh_ref, lb_ref, w_ref, *o_refs, plan):
    u = (x_ref[0] * (1.0 + sc_ref[0]) + sh_ref[0]).astype(BF16)
    refs = iter(o_refs)
    for kind, chunks in plan:
        if kind == "forget":
            k_ref, hi_ref, lo_ref = next(refs), next(refs), next(refs)
            for (w0, o0, n) in chunks:
                lb = lb_ref[:, o0:o0 + n]
                f = lb + (1.0 - lb) * _sigmoid(_dot(u, w_ref[:, w0:w0 + n]))
                lf = jnp.log(f)
                hi = lf.astype(BF16)
                k_ref[0, :, o0:o0 + n] = (1.0 - f).astype(BF16)
                hi_ref[0, :, o0:o0 + n] = hi
                lo_ref[0, :, o0:o0 + n] = (lf - hi.astype(F32)).astype(BF16)
        else:
            o_ref = next(refs)
            for (w0, o0, n) in chunks:
                p = _dot(u, w_ref[:, w0:w0 + n])
                if kind == "silu":
                    p = p * _sigmoid(p)
                o_ref[0, :, o0:o0 + n] = p.astype(o_ref.dtype)


def _inproj(x, scale, shift, lb_row, w_bf16, outs, tm):
    b, s, d = x.shape
    n_w = w_bf16.shape[1]
    plan, widths = [], []
    for (c0, width, kind) in outs:
        step = min(width, 512)
        plan.append((kind, tuple((c0 + o, o, step) for o in range(0, width, step))))
        widths += [width] * (3 if kind == "forget" else 1)
    kern = functools.partial(_inproj_kernel, plan=tuple(plan))
    return pl.pallas_call(
        kern,
        grid=(b, s // tm),
        in_specs=[pl.BlockSpec((1, tm, d), lambda i, j: (i, j, 0)),
                  pl.BlockSpec((1, 1, d), lambda i, j: (i, 0, 0)),
                  pl.BlockSpec((1, 1, d), lambda i, j: (i, 0, 0)),
                  pl.BlockSpec(lb_row.shape, lambda i, j: (0, 0)),
                  pl.BlockSpec((d, n_w), lambda i, j: (0, 0))],
        out_specs=[pl.BlockSpec((1, tm, width), lambda i, j: (i, j, 0)) for width in widths],
        out_shape=[jax.ShapeDtypeStruct((b, s, width), BF16) for width in widths],
        compiler_params=_cparams(("parallel", "parallel"), VMEM_LIMIT),
        name="in_proj",
    )(x, scale, shift, lb_row, w_bf16)


HG_TILE = 256
HG_GROUP = 4


def _gla_group(jobs, states):
    states = list(states)
    nj = len(jobs)
    r, dk = jobs[0][1].shape
    dv = jobs[0][4].shape[1]
    nc = r // CHUNK
    zero = jnp.zeros((CHUNK, dk), BF16)
    contract_last = (((1,), (1,)), ((), ()))
    contract_rows = (((0,), (0,)), ((), ()))

    cums = []
    for (_, hi, lo, _, _, _, tri, _, _) in jobs:
        both = _dot(tri, jnp.concatenate([hi, lo], axis=1))
        cums.append(both[:, :dk] + both[:, dk:])

    k_inv_b, decays, k_end_blk, q_dec = [], [], [], []
    for cum, (_, _, _, k, _, q, _, _, forward) in zip(cums, jobs):
        k_inv = k.astype(F32) * jnp.exp(-cum)
        k_inv_b.append(k_inv.astype(BF16))
        tot_row = CHUNK - 1 if forward else 0
        dec_j, rows_j = [], []
        for ci in range(nc):
            dec = jnp.exp(cum[ci * CHUNK + tot_row:ci * CHUNK + tot_row + 1, :])
            k_end = (k_inv[ci * CHUNK:(ci + 1) * CHUNK] * dec).astype(BF16)
            dec_j.append(dec)
            rows_j.append(jnp.concatenate([k_end if cj == ci else zero for cj in range(nc)], axis=1))
        decays.append(dec_j)
        k_end_blk.append(jnp.concatenate(rows_j, axis=0))
        q_dec.append(None if q is None else (q.astype(F32) * jnp.exp(cum)).astype(BF16))

    upd_all = [lax.dot_general(job[4], blk, contract_rows, preferred_element_type=F32)
               for job, blk in zip(jobs, k_end_blk)]
    scores = [None if qd is None else lax.dot_general(qd, kb, contract_last, preferred_element_type=F32)
              for qd, kb in zip(q_dec, k_inv_b)]

    s_all = []
    for ji in range(nj):
        chain, forward = jobs[ji][0], jobs[ji][8]
        st = states[chain]
        entering = [None] * nc
        for ci in (range(nc) if forward else reversed(range(nc))):
            entering[ci] = st
            st = st * decays[ji][ci] + upd_all[ji][:, ci * dk:(ci + 1) * dk]
        states[chain] = st
        s_all.append(None if q_dec[ji] is None
                     else [jnp.concatenate([entering[ci].astype(BF16), entering[ci + 1].astype(BF16)], axis=0)
                           for ci in range(0, nc, 2)])

    outs = []
    for ji in range(nj):
        if q_dec[ji] is None:
            outs.append(None)
            continue
        sc = jnp.where(jobs[ji][7] > 0, scores[ji], 0.0).astype(BF16)
        intra = _dot(sc, jobs[ji][4])
        inter = []
        for pi, s_pair in enumerate(s_all[ji]):
            rows = slice(2 * pi * CHUNK, (2 * pi + 2) * CHUNK)
            both = lax.dot_general(q_dec[ji][rows], s_pair, contract_last, preferred_element_type=F32)
            inter += [both[:CHUNK, :dv], both[CHUNK:, dv:]]
        outs.append(intra + jnp.concatenate(inter, axis=0))
    return outs, states


def _hgrn_kernel(q_ref, kf_ref, kb_ref, hif_ref, hib_ref, lof_ref, lob_ref, v_ref, g_ref,
                 ckf_ref, ckb_ref, chif_ref, chib_ref, clof_ref, clob_ref, cv_ref, ng_ref, trif_ref, trib_ref,
                 keepf_ref, keepb_ref, y_ref, of_ref, ob_ref, *, seq, ctx_len):
    r = HG_TILE
    n_lat = seq // r
    n_ctx = ctx_len // r
    group = HG_GROUP

    def ctx_body(i, carry):
        sl_f = pl.ds(pl.multiple_of(i * r, r), r)
        sl_b = pl.ds(pl.multiple_of((n_ctx - 1 - i) * r, r), r)
        jobs = [(0, chif_ref[0, sl_f, :], clof_ref[0, sl_f, :], ckf_ref[0, sl_f, :], cv_ref[0, sl_f, :], None,
                 trif_ref[...], None, True),
                (1, chib_ref[0, sl_b, :], clob_ref[0, sl_b, :], ckb_ref[0, sl_b, :], cv_ref[0, sl_b, :], None,
                 trib_ref[...], None, False)]
        _, states = _gla_group(jobs, carry)
        return tuple(states)

    zero = jnp.zeros((HG_D, HG_D), F32)
    carry = lax.fori_loop(0, n_ctx, ctx_body, (zero, zero))

    def lat_body(i, carry):
        jobs, slices = [], []
        for t in range(group):
            sl_f = pl.ds(pl.multiple_of((i * group + t) * r, r), r)
            sl_b = pl.ds(pl.multiple_of((n_lat - 1 - (i * group + t)) * r, r), r)
            jobs.append((0, hif_ref[0, sl_f, :], lof_ref[0, sl_f, :], kf_ref[0, sl_f, :], v_ref[0, sl_f, :],
                         q_ref[0, sl_f, :], trif_ref[...], keepf_ref[...], True))
            jobs.append((1, hib_ref[0, sl_b, :], lob_ref[0, sl_b, :], kb_ref[0, sl_b, :], v_ref[0, sl_b, :],
                         q_ref[0, sl_b, :], trib_ref[...], keepb_ref[...], False))
            slices += [(of_ref, sl_f), (ob_ref, sl_b)]
        outs, states = _gla_group(jobs, carry)
        for (ref, sl), o in zip(slices, outs):
            ref[sl, :] = o
        return tuple(states)

    lax.fori_loop(0, n_lat // group, lat_body, carry)

    ng = ng_ref[...]

    def out_body(i, _):
        sl = pl.ds(pl.multiple_of(i * r, r), r)
        o = of_ref[sl, :] + ob_ref[sl, :]
        o = o * lax.rsqrt(jnp.mean(o * o, axis=-1, keepdims=True) + RMS_EPS) * ng
        g = g_ref[0, sl, :].astype(F32)
        y_ref[0, sl, :] = (o * (g * _sigmoid(g))).astype(y_ref.dtype)
        return 0

    lax.fori_loop(0, n_lat, out_body, 0)


def _chunk_tri(r, forward, dtype):
    i = np.arange(r)
    same = (i[:, None] // CHUNK) == (i[None, :] // CHUNK)
    order = (i[:, None] >= i[None, :]) if forward else (i[:, None] <= i[None, :])
    return jnp.asarray(same & order, dtype)


def _hgrn(q, k, hi, lo, v, g, ck, chi, clo, cv, ng):
    b, s, _ = q.shape
    lc = ck.shape[1]
    h = HG_HEADS
    d = HG_D
    r = HG_TILE
    kern = functools.partial(_hgrn_kernel, seq=s, ctx_len=lc)
    fwd = lambda i, j: (i, 0, j)
    bwd = lambda i, j: (i, 0, j + h)
    const = lambda i, j: (0, 0)
    lat = lambda im: pl.BlockSpec((1, s, d), im)
    cx = lambda im: pl.BlockSpec((1, lc, d), im)
    return pl.pallas_call(
        kern,
        grid=(b, h),
        in_specs=[lat(fwd), lat(fwd), lat(bwd), lat(fwd), lat(bwd), lat(fwd), lat(bwd), lat(fwd), lat(fwd),
                  cx(fwd), cx(bwd), cx(fwd), cx(bwd), cx(fwd), cx(bwd), cx(fwd),
                  pl.BlockSpec((1, d), lambda i, j: (0, j)),
                  pl.BlockSpec((r, r), const),
                  pl.BlockSpec((r, r), const),
                  pl.BlockSpec((r, r), const),
                  pl.BlockSpec((r, r), const)],
        out_specs=pl.BlockSpec((1, s, d), fwd),
        out_shape=jax.ShapeDtypeStruct((b, s, h * d), BF16),
        scratch_shapes=[pltpu.VMEM((s, d), F32), pltpu.VMEM((s, d), F32)],
        compiler_params=_cparams(("parallel", "parallel"), VMEM_LIMIT),
        name="hgrn2",
    )(q, k, k, hi, hi, lo, lo, v, g, ck, ck, chi, chi, clo, clo, cv, ng,
      _chunk_tri(r, True, BF16), _chunk_tri(r, False, BF16), _chunk_tri(r, True, F32), _chunk_tri(r, False, F32))


def _dft_tables(seq):
    rows = seq // GRID_W
    n = F_GROUP_DIM
    kc = np.outer(np.arange(n), np.arange(n)) % n
    ang = 2.0 * np.pi * kc / n
    norm = 1.0 / math.sqrt(rows * GRID_W * n)
    ch = np.concatenate([np.cos(ang), -np.sin(ang)], axis=1) * norm
    t = np.arange(seq)
    r, w = t // GRID_W, t % GRID_W
    m = (np.outer(r, r) * GRID_W + np.outer(w, w) * rows) % (rows * GRID_W)
    ang_t = 2.0 * np.pi * m / (rows * GRID_W)
    tok = np.concatenate([np.cos(ang_t), np.sin(ang_t)], axis=1)
    return jnp.asarray(ch, BF16), jnp.asarray(tok, BF16)


def _fourier_kernel(p_ref, ch_ref, tok_ref, y_ref, xs_ref, *, seq):
    for gi in range(F_GROUPS):
        cols = slice(gi * F_GROUP_DIM, (gi + 1) * F_GROUP_DIM)
        x1 = _dot(p_ref[0, :, cols], ch_ref[...]).astype(BF16)
        xs_ref[0:seq, cols] = x1[:, :F_GROUP_DIM]
        xs_ref[seq:2 * seq, cols] = x1[:, F_GROUP_DIM:]
    y_ref[0] = _dot(tok_ref[...], xs_ref[...]).astype(y_ref.dtype)


def _fourier(pf):
    b, s, wdt = pf.shape
    ch, tok = _dft_tables(s)
    kern = functools.partial(_fourier_kernel, seq=s)
    return pl.pallas_call(
        kern,
        grid=(b,),
        in_specs=[pl.BlockSpec((1, s, wdt), lambda i: (i, 0, 0)),
                  pl.BlockSpec(ch.shape, lambda i: (0, 0)),
                  pl.BlockSpec(tok.shape, lambda i: (0, 0), pipeline_mode=pl.Buffered(1))],
        out_specs=pl.BlockSpec((1, s, wdt), lambda i: (i, 0, 0)),
        out_shape=jax.ShapeDtypeStruct((b, s, wdt), BF16),
        scratch_shapes=[pltpu.VMEM((2 * s, wdt), BF16)],
        compiler_params=_cparams(("parallel",), VMEM_LIMIT),
        name="fourier",
    )(pf, ch, tok)


def _layer_norm(t, g, b):
    mu = jnp.mean(t, axis=-1, keepdims=True)
    tc = t - mu
    var = jnp.mean(tc * tc, axis=-1, keepdims=True)
    return tc * lax.rsqrt(var + LN_EPS) * g + b


def _merge_kernel(yf_ref, yh_ref, gt_ref, x_ref, g1_ref, sc2_ref, sh2_ref, wfo_ref, who_ref, wo_ref,
                  lg_ref, lbias_ref, wr_ref, br_ref, x1_ref, idx_ref, wt_ref):
    d = x_ref.shape[-1]
    gf = _sigmoid(gt_ref[0, :, :d].astype(F32))
    gh = _sigmoid(gt_ref[0, :, d:].astype(F32))
    m = gf * _dot(yf_ref[0], wfo_ref[...]) + gh * _dot(yh_ref[0], who_ref[...])
    mix = _dot(m.astype(BF16), wo_ref[...])
    x1 = _layer_norm(DEEPNORM_ALPHA * x_ref[0] + g1_ref[0] * mix, lg_ref[...], lbias_ref[...])
    x1_ref[0] = x1
    u2 = x1 * (1.0 + sc2_ref[0]) + sh2_ref[0]
    logits = _dot(u2.astype(BF16), wr_ref[...]) + br_ref[...]
    lane = lax.broadcasted_iota(jnp.int32, logits.shape, 1)
    work = jnp.where(lane < N_EXPERTS, logits, -jnp.inf)
    vals, idxs = [], []
    for _ in range(TOP_K):
        mx = jnp.max(work, axis=-1, keepdims=True)
        sel = jnp.min(jnp.where(work == mx, lane, LANES), axis=-1, keepdims=True)
        vals.append(mx)
        idxs.append(sel)
        work = jnp.where(lane == sel, -jnp.inf, work)
    exps = [jnp.exp(vv - vals[0]) for vv in vals]
    inv = 1.0 / (exps[0] + exps[1] + exps[2] + exps[3])
    for kk in range(TOP_K):
        idx_ref[0, :, kk:kk + 1] = idxs[kk]
        wt_ref[0, :, kk:kk + 1] = exps[kk] * inv


def _merge(yf, yh, gates, x, g1, sc2, sh2, wfo, who, wo, lg, lbias, wr, br, tm):
    b, s, d = x.shape
    tile = lambda i, j: (i, j, 0)
    per_b = lambda i, j: (i, 0, 0)
    const = lambda i, j: (0, 0)
    return pl.pallas_call(
        _merge_kernel,
        grid=(b, s // tm),
        in_specs=[pl.BlockSpec((1, tm, yf.shape[-1]), tile),
                  pl.BlockSpec((1, tm, yh.shape[-1]), tile),
                  pl.BlockSpec((1, tm, gates.shape[-1]), tile),
                  pl.BlockSpec((1, tm, d), tile),
                  pl.BlockSpec((1, 1, d), per_b),
                  pl.BlockSpec((1, 1, d), per_b),
                  pl.BlockSpec((1, 1, d), per_b),
                  pl.BlockSpec(wfo.shape, const),
                  pl.BlockSpec(who.shape, const),
                  pl.BlockSpec(wo.shape, const),
                  pl.BlockSpec((1, d), const),
                  pl.BlockSpec((1, d), const),
                  pl.BlockSpec(wr.shape, const),
                  pl.BlockSpec(br.shape, const)],
        out_specs=[pl.BlockSpec((1, tm, d), tile),
                   pl.BlockSpec((1, tm, TOP_K), tile),
                   pl.BlockSpec((1, tm, TOP_K), tile)],
        out_shape=[jax.ShapeDtypeStruct((b, s, d), F32),
                   jax.ShapeDtypeStruct((b, s, TOP_K), jnp.int32),
                   jax.ShapeDtypeStruct((b, s, TOP_K), F32)],
        compiler_params=_cparams(("parallel", "parallel"), VMEM_LIMIT),
        name="merge_ln_router",
    )(yf, yh, gates, x, g1, sc2, sh2, wfo, who, wo, lg, lbias, wr, br)


def _lane_cumsum_exclusive(x):
    lane = lax.broadcasted_iota(jnp.int32, x.shape, 1)
    inc = x
    sh = 1
    while sh < LANES:
        inc = inc + jnp.where(lane >= sh, pltpu.roll(inc, sh, 1), 0)
        sh *= 2
    return inc - x


def _route_kernel(idx_ref, dest_ref, cnt_ref, start_ref, run_ref, *, block_rows):
    phase = pl.program_id(0)
    t = pl.program_id(1)
    tt = idx_ref.shape[0]
    lane = lax.broadcasted_iota(jnp.int32, (tt, LANES), 1)
    onehots = [jnp.where(lane == idx_ref[:, kk:kk + 1], 1.0, 0.0) for kk in range(TOP_K)]
    colsums = [jnp.sum(oh, axis=0, keepdims=True) for oh in onehots]

    @pl.when(jnp.logical_and(phase == 0, t == 0))
    def _():
        run_ref[...] = jnp.zeros_like(run_ref)

    @pl.when(phase == 0)
    def _():
        run_ref[...] += colsums[0] + colsums[1] + colsums[2] + colsums[3]

    @pl.when(jnp.logical_and(phase == 1, t == 0))
    def _():
        cnt = run_ref[...].astype(jnp.int32)
        padded = (cnt + (block_rows - 1)) // block_rows * block_rows
        cnt_ref[...] = cnt
        start_ref[...] = _lane_cumsum_exclusive(padded)
        run_ref[...] = jnp.zeros_like(run_ref)

    @pl.when(phase == 1)
    def _():
        r = lax.broadcasted_iota(jnp.int32, (tt, tt), 0)
        c = lax.broadcasted_iota(jnp.int32, (tt, tt), 1)
        strict = jnp.where(r > c, 1.0, 0.0).astype(BF16)
        base = run_ref[...] + start_ref[...].astype(F32)
        for kk in range(TOP_K):
            prefix = _dot(strict, onehots[kk].astype(BF16))
            rank = jnp.sum(onehots[kk] * (prefix + base), axis=-1, keepdims=True)
            dest_ref[:, kk:kk + 1] = rank.astype(jnp.int32)
            base = base + colsums[kk]
        run_ref[...] = base - start_ref[...].astype(F32)


def _route(idx, block_rows, tt):
    n = idx.shape[0]
    kern = functools.partial(_route_kernel, block_rows=block_rows)
    return pl.pallas_call(
        kern,
        grid=(2, n // tt),
        in_specs=[pl.BlockSpec((tt, TOP_K), lambda p, t: (t, 0))],
        out_specs=[pl.BlockSpec((tt, TOP_K), lambda p, t: (t * p, 0)),
                   pl.BlockSpec((1, LANES), lambda p, t: (0, 0)),
                   pl.BlockSpec((1, LANES), lambda p, t: (0, 0))],
        out_shape=[jax.ShapeDtypeStruct((n, TOP_K), jnp.int32),
                   jax.ShapeDtypeStruct((1, LANES), jnp.int32),
                   jax.ShapeDtypeStruct((1, LANES), jnp.int32)],
        scratch_shapes=[pltpu.VMEM((1, LANES), F32)],
        compiler_params=_cparams(("arbitrary", "arbitrary")),
        name="route_ranks",
    )(idx)


def _to_row_tiles(ref, base, val):
    rows = val.shape[0]
    for s in range(SUBLANES):
        ref[pl.ds(base + s, rows, stride=SUBLANES), :] = val[:, s * LANES:(s + 1) * LANES]


def _from_row_tiles(ref, base, rows):
    return jnp.concatenate([ref[pl.ds(base + s, rows, stride=SUBLANES), :] for s in range(SUBLANES)], axis=1)


def _row_tile(ref, r):
    return ref.at[pl.ds(pl.multiple_of(r * SUBLANES, SUBLANES), SUBLANES), :]


def _row_tiles(ref, r, n):
    return ref.at[pl.ds(pl.multiple_of(r * SUBLANES, SUBLANES), n * SUBLANES), :]


def _dispatch_kernel(cnt_ref, start_ref, dest_ref, x1_ref, sc_ref, sh_ref, h_ref, u_ref, zero_ref, sems, *, block_rows):
    t = pl.program_id(0)
    nt = pl.num_programs(0)
    tt = x1_ref.shape[0]
    slot = t % 2
    ubase = pl.multiple_of(slot * (tt * SUBLANES), tt * SUBLANES)
    _to_row_tiles(u_ref, ubase, x1_ref[...] * (1.0 + sc_ref[0]) + sh_ref[0])

    def issue(i, _):
        for kk in range(TOP_K):
            d = dest_ref[0, 0, i * TOP_K + kk]
            pltpu.make_async_copy(_row_tile(u_ref, slot * tt + i), _row_tile(h_ref, d), sems.at[slot]).start()
        return 0

    lax.fori_loop(0, tt, issue, 0)

    def drain(sl):
        for _ in range(TOP_K):
            pltpu.make_async_copy(_row_tiles(u_ref, sl * tt, tt), _row_tiles(h_ref, 0, tt), sems.at[sl]).wait()

    @pl.when(t > 0)
    def _():
        drain(1 - slot)

    @pl.when(t == nt - 1)
    def _():
        drain(slot)
        zero_ref[...] = jnp.zeros_like(zero_ref)

        def pad_expert(e, _):
            cnt = cnt_ref[e]
            rem = (block_rows - (cnt & (block_rows - 1))) & (block_rows - 1)
            pos = start_ref[e] + cnt
            size = block_rows // 2
            while size >= 1:
                take = rem & size

                @pl.when(take != 0)
                def _(pos=pos, size=size):
                    cp = pltpu.make_async_copy(_row_tiles(zero_ref, 0, size), _row_tiles(h_ref, pos, size), sems.at[2])
                    cp.start()
                    cp.wait()

                pos = pos + take
                size //= 2
            return 0

        lax.fori_loop(0, N_EXPERTS, pad_expert, 0)


def _dispatch(counts, starts, dest_flat, x1, sc2, sh2, n_rows, tt):
    n, d = x1.shape
    b = sc2.shape[0]
    tiles_per_b = (n // b) // tt
    per_b = lambda t, c, s: (t // tiles_per_b, 0, 0)
    grid_spec = pltpu.PrefetchScalarGridSpec(
        num_scalar_prefetch=2,
        grid=(n // tt,),
        in_specs=[pl.BlockSpec((1, 1, tt * TOP_K), lambda t, c, s: (t, 0, 0), memory_space=pltpu.SMEM),
                  pl.BlockSpec((tt, d), lambda t, c, s: (t, 0)),
                  pl.BlockSpec((1, 1, d), per_b),
                  pl.BlockSpec((1, 1, d), per_b)],
        out_specs=pl.BlockSpec(memory_space=pl.ANY),
        scratch_shapes=[pltpu.VMEM((2 * tt * SUBLANES, LANES), F32), pltpu.VMEM((MOE_ROWS // 2 * SUBLANES, LANES), F32),
                        pltpu.SemaphoreType.DMA((3,))],
    )
    return pl.pallas_call(
        functools.partial(_dispatch_kernel, block_rows=MOE_ROWS),
        grid_spec=grid_spec,
        out_shape=jax.ShapeDtypeStruct((n_rows * SUBLANES, LANES), F32),
        compiler_params=_cparams(("arbitrary",)),
        name="moe_dispatch",
    )(counts, starts, dest_flat, x1, sc2, sh2)


def _expert_kernel(be_ref, nu_ref, h_ref, wgu_ref, bgu_ref, wdn_ref, bdn_ref, o_ref, wgu_b_ref, wdn_b_ref):
    j = pl.program_id(0)
    d = wdn_ref.shape[-1]
    bm = h_ref.shape[0] // SUBLANES

    @pl.when(j < nu_ref[0])
    def _():
        @pl.when(jnp.logical_or(j == 0, be_ref[j] != be_ref[jnp.maximum(j - 1, 0)]))
        def _():
            wgu_b_ref[...] = wgu_ref[0].astype(BF16)
            wdn_b_ref[...] = wdn_ref[0].astype(BF16)

        gu = _dot(_from_row_tiles(h_ref, 0, bm).astype(BF16), wgu_b_ref[...]) + bgu_ref[0]
        gate = jnp.minimum(gu[:, :d], SWIGLU_LIMIT)
        up = jnp.clip(gu[:, d:], -SWIGLU_LIMIT, SWIGLU_LIMIT)
        act = (up + 1.0) * gate * _sigmoid(SWIGLU_ALPHA * gate)
        _to_row_tiles(o_ref, 0, _dot(act.astype(BF16), wdn_b_ref[...]) + bdn_ref[0])


def _experts(h_sorted, block_expert, n_used, wgu, bgu, wdn, bdn):
    n_rows = h_sorted.shape[0] // SUBLANES
    e, d, _ = wdn.shape
    bm = MOE_ROWS
    n_blocks = n_rows // bm
    blk = lambda j, be, nu: (jnp.minimum(j, nu[0] - 1), 0)
    exp = lambda j, be, nu: (be[jnp.minimum(j, nu[0] - 1)], 0, 0)
    grid_spec = pltpu.PrefetchScalarGridSpec(
        num_scalar_prefetch=2,
        grid=(n_blocks,),
        in_specs=[pl.BlockSpec((bm * SUBLANES, LANES), blk),
                  pl.BlockSpec((1, d, 2 * d), exp),
                  pl.BlockSpec((1, 1, 2 * d), exp),
                  pl.BlockSpec((1, d, d), exp),
                  pl.BlockSpec((1, 1, d), exp)],
        out_specs=pl.BlockSpec((bm * SUBLANES, LANES), blk),
        scratch_shapes=[pltpu.VMEM((d, 2 * d), BF16), pltpu.VMEM((d, d), BF16)],
    )
    return pl.pallas_call(
        _expert_kernel,
        grid_spec=grid_spec,
        out_shape=jax.ShapeDtypeStruct((n_rows * SUBLANES, LANES), F32),
        compiler_params=_cparams(("arbitrary",), VMEM_LIMIT),
        name="moe_experts",
    )(block_expert, n_used, h_sorted, wgu, bgu.reshape(e, 1, 2 * d), wdn, bdn.reshape(e, 1, d))


def _combine_kernel(dest_ref, dest_next_ref, wt_ref, x1_ref, g2_ref, lg_ref, lbias_ref, o_hbm, y_ref, buf_ref, sems):
    t = pl.program_id(0)
    nt = pl.num_programs(0)
    tt = x1_ref.shape[0]
    slot = t % 2

    def issue(d_ref, sl):
        def body(i, _):
            for kk in range(TOP_K):
                d = d_ref[0, 0, i * TOP_K + kk]
                pltpu.make_async_copy(_row_tile(o_hbm, d), _row_tile(buf_ref, (sl * TOP_K + kk) * tt + i),
                                      sems.at[sl]).start()
            return 0

        lax.fori_loop(0, tt, body, 0)

    @pl.when(t == 0)
    def _():
        issue(dest_ref, 0)

    @pl.when(t + 1 < nt)
    def _():
        issue(dest_next_ref, 1 - slot)

    for kk in range(TOP_K):
        pltpu.make_async_copy(_row_tiles(o_hbm, 0, tt), _row_tiles(buf_ref, (slot * TOP_K + kk) * tt, tt),
                              sems.at[slot]).wait()

    ff = None
    for kk in range(TOP_K):
        base = pl.multiple_of((slot * TOP_K + kk) * (tt * SUBLANES), tt * SUBLANES)
        term = wt_ref[:, kk:kk + 1] * _from_row_tiles(buf_ref, base, tt)
        ff = term if ff is None else ff + term
    y_ref[...] = _layer_norm(DEEPNORM_ALPHA * x1_ref[...] + g2_ref[0] * ff, lg_ref[...], lbias_ref[...])


def _combine(dest_flat, wt, x1, g2, lg, lbias, o_sorted, tt):
    n, d = x1.shape
    b = g2.shape[0]
    nt = n // tt
    tiles_per_b = (n // b) // tt
    const = lambda t: (0, 0)
    dest_spec = lambda im: pl.BlockSpec((1, 1, tt * TOP_K), im, memory_space=pltpu.SMEM)
    return pl.pallas_call(
        _combine_kernel,
        grid=(nt,),
        in_specs=[dest_spec(lambda t: (t, 0, 0)),
                  dest_spec(lambda t: (jnp.minimum(t + 1, nt - 1), 0, 0)),
                  pl.BlockSpec((tt, TOP_K), lambda t: (t, 0)),
                  pl.BlockSpec((tt, d), lambda t: (t, 0)),
                  pl.BlockSpec((1, 1, d), lambda t: (t // tiles_per_b, 0, 0)),
                  pl.BlockSpec((1, d), const),
                  pl.BlockSpec((1, d), const),
                  pl.BlockSpec(memory_space=pl.ANY)],
        out_specs=pl.BlockSpec((tt, d), lambda t: (t, 0)),
        out_shape=jax.ShapeDtypeStruct((n, d), F32),
        scratch_shapes=[pltpu.VMEM((2 * TOP_K * tt * SUBLANES, LANES), F32), pltpu.SemaphoreType.DMA((2,))],
        compiler_params=_cparams(("arbitrary",), VMEM_LIMIT),
        name="moe_combine",
    )(dest_flat, dest_flat, wt, x1, g2, lg, lbias, o_sorted)


def kernel(x, c, ctx, c_ctx, w_ada, b_ada, w_in, lb_raw, hg_norm_g, w_four_out, w_hg_out, w_o, ln1_g, ln1_b,
           w_router, b_router, w_gate_up, b_gate_up, w_down, b_down, ln2_g, ln2_b):
    b, s, d = x.shape
    lc = ctx.shape[1]
    n = b * s
    f_w = F_GROUPS * F_GROUP_DIM
    hk = HG_HEADS * HG_D
    assert w_ada.shape[0] == DEPTH and s % GRID_W == 0 and s % (HG_TILE * HG_GROUP) == 0 and s % 512 == 0 and lc % HG_TILE == 0

    pad = (-(b + 1)) % 8
    c_rows = jnp.concatenate([c, c_ctx[None, :], jnp.zeros((pad, d), F32)], axis=0)
    mod = _ada(c_rows, w_ada[0], b_ada[0]).reshape(b + 1 + pad, N_MOD, 1, d)
    shift1, scale1, gate1, shift2, scale2, gate2 = (mod[:b, i] for i in range(N_MOD))
    cshift1 = jnp.broadcast_to(mod[b, 0], (b, 1, d))
    cscale1 = jnp.broadcast_to(mod[b, 1], (b, 1, d))

    lower_bounds = jnp.cumsum(jax.nn.softmax(lb_raw.astype(F32), axis=0), axis=0)[0]

    w_in_b = w_in[0].astype(BF16)
    o_f, o_q, o_z, o_v, o_g, o_gt = f_w, f_w + hk, f_w + 3 * hk, f_w + 4 * hk, f_w + 5 * hk, f_w + 5 * hk + 2 * d
    lb_row = lower_bounds.reshape(1, 2 * hk)
    pf, pq, pk, phi, plo, pv, pg, gates = _inproj(
        x, scale1, shift1, lb_row, w_in_b,
        [(0, f_w, "cast"), (o_f, hk, "silu"), (o_q, 2 * hk, "forget"), (o_z, hk, "cast"), (o_v, hk, "cast"),
         (o_g, 2 * d, "cast")], tm=512)
    w_ctx = w_in_b[:, o_q:o_v]
    ck, chi, clo, cv = _inproj(ctx, cscale1, cshift1, lb_row, w_ctx, [(0, 2 * hk, "forget"), (2 * hk, hk, "cast")], tm=lc)

    y_h = _hgrn(pq, pk, phi, plo, pv, pg, ck, chi, clo, cv, hg_norm_g[0].reshape(1, hk))
    y_f = _fourier(pf)

    wr = jnp.zeros((d, LANES), BF16).at[:, :N_EXPERTS].set(w_router[0].astype(BF16))
    br = jnp.zeros((1, LANES), F32).at[0, :N_EXPERTS].set(b_router[0])
    x1, idx, wt = _merge(y_f, y_h, gates, x, gate1, scale2, shift2,
                         w_four_out[0].astype(BF16), w_hg_out[0].astype(BF16), w_o[0].astype(BF16),
                         ln1_g[0].reshape(1, d), ln1_b[0].reshape(1, d), wr, br, tm=512)

    idx2 = idx.reshape(n, TOP_K)
    dest, counts, starts = _route(idx2, MOE_ROWS, 512)
    n_blocks = -(-(n * TOP_K) // MOE_ROWS) + N_EXPERTS
    n_rows = n_blocks * MOE_ROWS
    cnt = counts[0, :N_EXPERTS]
    st = starts[0, :N_EXPERTS]
    pad_ends = st + (cnt + MOE_ROWS - 1) // MOE_ROWS * MOE_ROWS
    block_first_row = jnp.arange(n_blocks, dtype=jnp.int32) * MOE_ROWS
    block_expert = jnp.minimum(jnp.sum((pad_ends[None, :] <= block_first_row[:, None]).astype(jnp.int32), axis=1),
                               N_EXPERTS - 1)
    n_used = (pad_ends[-1] // MOE_ROWS).astype(jnp.int32).reshape(1)

    tt = 256
    dest_flat = dest.reshape(n // tt, 1, tt * TOP_K)
    x1_2d = x1.reshape(n, d)
    h_sorted = _dispatch(cnt, st, dest_flat, x1_2d, scale2, shift2, n_rows, tt)
    o_sorted = _experts(h_sorted, block_expert, n_used, w_gate_up[0], b_gate_up[0], w_down[0], b_down[0])
    out = _combine(dest_flat, wt.reshape(n, TOP_K), x1_2d, gate2, ln2_g[0].reshape(1, d), ln2_b[0].reshape(1, d),
                   o_sorted, tt)
    return out.reshape(b, s, d)
```

```python
import functools
import math

import jax
import jax.numpy as jnp
import numpy as np
from jax import lax
from jax.experimental import pallas as pl
from jax.experimental.pallas import tpu as pltpu

F32 = jnp.float32
BF16 = jnp.bfloat16

GRID_W = 64
N_MOD = 6
F_GROUPS = 4
F_GROUP_DIM = 128
HG_HEADS = 4
HG_D = 128
CHUNK = 64
N_EXPERTS = 32
TOP_K = 4
SWIGLU_LIMIT = 7.0
SWIGLU_ALPHA = 1.702
LN_EPS = 1e-5
RMS_EPS = 1e-6
DEPTH = 1
DEEPNORM_ALPHA = (2.0 * DEPTH) ** 0.25

LANES = 128
SUBLANES = 8
MOE_ROWS = 512
DMA_QUEUES = 2
VMEM_LIMIT = 56 * 1024 * 1024


def _cparams(sem, vmem=None):
    return pltpu.CompilerParams(dimension_semantics=sem, vmem_limit_bytes=vmem)


def _dot(a, b):
    return jnp.dot(a, b, preferred_element_type=F32)


def _sigmoid(x):
    return 1.0 / (1.0 + jnp.exp(-x))


def _ada_kernel(c_ref, w_ref, b_ref, o_ref):
    c = c_ref[...]
    a = c * _sigmoid(c)
    o_ref[...] = jnp.dot(a, w_ref[...], preferred_element_type=F32,
                         precision=lax.Precision.HIGHEST) + b_ref[...]


def _ada(c_rows, w, b):
    r, d = c_rows.shape
    n = w.shape[1]
    tn = 512
    return pl.pallas_call(
        _ada_kernel,
        grid=(n // tn,),
        in_specs=[pl.BlockSpec((r, d), lambda j: (0, 0)),
                  pl.BlockSpec((d, tn), lambda j: (0, j)),
                  pl.BlockSpec((1, tn), lambda j: (0, j))],
        out_specs=pl.BlockSpec((r, tn), lambda j: (0, j)),
        out_shape=jax.ShapeDtypeStruct((r, n), F32),
        compiler_params=_cparams(("arbitrary",)),
        name="ada_mod",
    )(c_rows, w, b.reshape(1, n))


def _inproj_kernel(x_ref, sc_ref, sh_ref, lb_ref, w_ref, *o_refs, plan):
    u = (x_ref[0] * (1.0 + sc_ref[0]) + sh_ref[0]).astype(BF16)
    refs = iter(o_refs)
    for kind, chunks in plan:
        if kind == "forget":
            k_ref, hi_ref, lo_ref = next(refs), next(refs), next(refs)
            for (w0, o0, n) in chunks:
                lb = lb_ref[:, o0:o0 + n]
                f = lb + (1.0 - lb) * _sigmoid(_dot(u, w_ref[:, w0:w0 + n]))
                lf = jnp.log(f)
                hi = lf.astype(BF16)
                k_ref[0, :, o0:o0 + n] = (1.0 - f).astype(BF16)
                hi_ref[0, :, o0:o0 + n] = hi
                lo_ref[0, :, o0:o0 + n] = (lf - hi.astype(F32)).astype(BF16)
        else:
            o_ref = next(refs)
            for (w0, o0, n) in chunks:
                p = _dot(u, w_ref[:, w0:w0 + n])
                if kind == "silu":
                    p = p * _sigmoid(p)
                o_ref[0, :, o0:o0 + n] = p.astype(o_ref.dtype)


def _inproj(x, scale, shift, lb_row, w_bf16, outs, tm):
    b, s, d = x.shape
    n_w = w_bf16.shape[1]
    plan, widths = [], []
    for (c0, width, kind) in outs:
        step = min(width, 512)
        plan.append((kind, tuple((c0 + o, o, step) for o in range(0, width, step))))
        widths += [width] * (3 if kind == "forget" else 1)
    kern = functools.partial(_inproj_kernel, plan=tuple(plan))
    return pl.pallas_call(
        kern,
        grid=(b, s // tm),
        in_specs=[pl.BlockSpec((1, tm, d), lambda i, j: (i, j, 0)),
                  pl.BlockSpec((1, 1, d), lambda i, j: (i, 0, 0)),
                  pl.BlockSpec((1, 1, d), lambda i, j: (i, 0, 0)),
                  pl.BlockSpec(lb_row.shape, lambda i, j: (0, 0)),
                  pl.BlockSpec((d, n_w), lambda i, j: (0, 0))],
        out_specs=[pl.BlockSpec((1, tm, width), lambda i, j: (i, j, 0)) for width in widths],
        out_shape=[jax.ShapeDtypeStruct((b, s, width), BF16) for width in widths],
        compiler_params=_cparams(("parallel", "parallel"), VMEM_LIMIT),
        name="in_proj",
    )(x, scale, shift, lb_row, w_bf16)


HG_TILE = 256
HG_GROUP = 4


def _gla_group(jobs, states):
    states = list(states)
    nj = len(jobs)
    r, dk = jobs[0][1].shape
    dv = jobs[0][4].shape[1]
    nc = r // CHUNK
    zero = jnp.zeros((CHUNK, dk), BF16)
    contract_last = (((1,), (1,)), ((), ()))
    contract_rows = (((0,), (0,)), ((), ()))

    cums = []
    for (_, hi, lo, _, _, _, tri, _, _) in jobs:
        both = _dot(tri, jnp.concatenate([hi, lo], axis=1))
        cums.append(both[:, :dk] + both[:, dk:])

    k_inv_b, decays, k_end_blk, q_dec = [], [], [], []
    for cum, (_, _, _, k, _, q, _, _, forward) in zip(cums, jobs):
        k_inv = k.astype(F32) * jnp.exp(-cum)
        k_inv_b.append(k_inv.astype(BF16))
        tot_row = CHUNK - 1 if forward else 0
        dec_j, rows_j = [], []
        for ci in range(nc):
            dec = jnp.exp(cum[ci * CHUNK + tot_row:ci * CHUNK + tot_row + 1, :])
            k_end = (k_inv[ci * CHUNK:(ci + 1) * CHUNK] * dec).astype(BF16)
            dec_j.append(dec)
            rows_j.append(jnp.concatenate([k_end if cj == ci else zero for cj in range(nc)], axis=1))
        decays.append(dec_j)
        k_end_blk.append(jnp.concatenate(rows_j, axis=0))
        q_dec.append(None if q is None else (q.astype(F32) * jnp.exp(cum)).astype(BF16))

    upd_all = [lax.dot_general(job[4], blk, contract_rows, preferred_element_type=F32)
               for job, blk in zip(jobs, k_end_blk)]
    scores = [None if qd is None else lax.dot_general(qd, kb, contract_last, preferred_element_type=F32)
              for qd, kb in zip(q_dec, k_inv_b)]

    s_all = []
    for ji in range(nj):
        chain, forward = jobs[ji][0], jobs[ji][8]
        st = states[chain]
        entering = [None] * nc
        for ci in (range(nc) if forward else reversed(range(nc))):
            entering[ci] = st
            st = st * decays[ji][ci] + upd_all[ji][:, ci * dk:(ci + 1) * dk]
        states[chain] = st
        s_all.append(None if q_dec[ji] is None
                     else [jnp.concatenate([entering[ci].astype(BF16), entering[ci + 1].astype(BF16)], axis=0)
                           for ci in range(0, nc, 2)])

    outs = []
    for ji in range(nj):
        if q_dec[ji] is None:
            outs.append(None)
            continue
        sc = jnp.where(jobs[ji][7] > 0, scores[ji], 0.0).astype(BF16)
        intra = _dot(sc, jobs[ji][4])
        inter = []
        for pi, s_pair in enumerate(s_all[ji]):
            rows = slice(2 * pi * CHUNK, (2 * pi + 2) * CHUNK)
            both = lax.dot_general(q_dec[ji][rows], s_pair, contract_last, preferred_element_type=F32)
            inter += [both[:CHUNK, :dv], both[CHUNK:, dv:]]
        outs.append(intra + jnp.concatenate(inter, axis=0))
    return outs, states


def _hgrn_kernel(q_ref, kf_ref, kb_ref, hif_ref, hib_ref, lof_ref, lob_ref, v_ref, g_ref,
                 ckf_ref, ckb_ref, chif_ref, chib_ref, clof_ref, clob_ref, cv_ref, ng_ref, trif_ref, trib_ref,
                 keepf_ref, keepb_ref, y_ref, of_ref, ob_ref, *, seq, ctx_len):
    r = HG_TILE
    n_lat = seq // r
    n_ctx = ctx_len // r
    group = HG_GROUP

    def ctx_body(i, carry):
        sl_f = pl.ds(pl.multiple_of(i * r, r), r)
        sl_b = pl.ds(pl.multiple_of((n_ctx - 1 - i) * r, r), r)
        jobs = [(0, chif_ref[0, sl_f, :], clof_ref[0, sl_f, :], ckf_ref[0, sl_f, :], cv_ref[0, sl_f, :], None,
                 trif_ref[...], None, True),
                (1, chib_ref[0, sl_b, :], clob_ref[0, sl_b, :], ckb_ref[0, sl_b, :], cv_ref[0, sl_b, :], None,
                 trib_ref[...], None, False)]
        _, states = _gla_group(jobs, carry)
        return tuple(states)

    zero = jnp.zeros((HG_D, HG_D), F32)
    carry = lax.fori_loop(0, n_ctx, ctx_body, (zero, zero))

    def lat_body(i, carry):
        jobs, slices = [], []
        for t in range(group):
            sl_f = pl.ds(pl.multiple_of((i * group + t) * r, r), r)
            sl_b = pl.ds(pl.multiple_of((n_lat - 1 - (i * group + t)) * r, r), r)
            jobs.append((0, hif_ref[0, sl_f, :], lof_ref[0, sl_f, :], kf_ref[0, sl_f, :], v_ref[0, sl_f, :],
                         q_ref[0, sl_f, :], trif_ref[...], keepf_ref[...], True))
            jobs.append((1, hib_ref[0, sl_b, :], lob_ref[0, sl_b, :], kb_ref[0, sl_b, :], v_ref[0, sl_b, :],
                         q_ref[0, sl_b, :], trib_ref[...], keepb_ref[...], False))
            slices += [(of_ref, sl_f), (ob_ref, sl_b)]
        outs, states = _gla_group(jobs, carry)
        for (ref, sl), o in zip(slices, outs):
            ref[sl, :] = o
        return tuple(states)

    lax.fori_loop(0, n_lat // group, lat_body, carry)

    ng = ng_ref[...]

    def out_body(i, _):
        sl = pl.ds(pl.multiple_of(i * r, r), r)
        o = of_ref[sl, :] + ob_ref[sl, :]
        o = o * lax.rsqrt(jnp.mean(o * o, axis=-1, keepdims=True) + RMS_EPS) * ng
        g = g_ref[0, sl, :].astype(F32)
        y_ref[0, sl, :] = (o * (g * _sigmoid(g))).astype(y_ref.dtype)
        return 0

    lax.fori_loop(0, n_lat, out_body, 0)


def _chunk_tri(r, forward, dtype):
    i = np.arange(r)
    same = (i[:, None] // CHUNK) == (i[None, :] // CHUNK)
    order = (i[:, None] >= i[None, :]) if forward else (i[:, None] <= i[None, :])
    return jnp.asarray(same & order, dtype)


def _hgrn(q, k, hi, lo, v, g, ck, chi, clo, cv, ng):
    b, s, _ = q.shape
    lc = ck.shape[1]
    h = HG_HEADS
    d = HG_D
    r = HG_TILE
    kern = functools.partial(_hgrn_kernel, seq=s, ctx_len=lc)
    fwd = lambda i, j: (i, 0, j)
    bwd = lambda i, j: (i, 0, j + h)
    const = lambda i, j: (0, 0)
    lat = lambda im: pl.BlockSpec((1, s, d), im)
    cx = lambda im: pl.BlockSpec((1, lc, d), im)
    return pl.pallas_call(
        kern,
        grid=(b, h),
        in_specs=[lat(fwd), lat(fwd), lat(bwd), lat(fwd), lat(bwd), lat(fwd), lat(bwd), lat(fwd), lat(fwd),
                  cx(fwd), cx(bwd), cx(fwd), cx(bwd), cx(fwd), cx(bwd), cx(fwd),
                  pl.BlockSpec((1, d), lambda i, j: (0, j)),
                  pl.BlockSpec((r, r), const),
                  pl.BlockSpec((r, r), const),
                  pl.BlockSpec((r, r), const),
                  pl.BlockSpec((r, r), const)],
        out_specs=pl.BlockSpec((1, s, d), fwd),
        out_shape=jax.ShapeDtypeStruct((b, s, h * d), BF16),
        scratch_shapes=[pltpu.VMEM((s, d), F32), pltpu.VMEM((s, d), F32)],
        compiler_params=_cparams(("parallel", "parallel"), VMEM_LIMIT),
        name="hgrn2",
    )(q, k, k, hi, hi, lo, lo, v, g, ck, ck, chi, chi, clo, clo, cv, ng,
      _chunk_tri(r, True, BF16), _chunk_tri(r, False, BF16), _chunk_tri(r, True, F32), _chunk_tri(r, False, F32))


def _dft_tables(seq):
    rows = seq // GRID_W
    n = F_GROUP_DIM
    kc = np.outer(np.arange(n), np.arange(n)) % n
    ang = 2.0 * np.pi * kc / n
    norm = 1.0 / math.sqrt(rows * GRID_W * n)
    ch = np.concatenate([np.cos(ang), -np.sin(ang)], axis=1) * norm
    t = np.arange(seq)
    r, w = t // GRID_W, t % GRID_W
    m = (np.outer(r, r) * GRID_W + np.outer(w, w) * rows) % (rows * GRID_W)
    ang_t = 2.0 * np.pi * m / (rows * GRID_W)
    tok = np.concatenate([np.cos(ang_t), np.sin(ang_t)], axis=1)
    return jnp.asarray(ch, BF16), jnp.asarray(tok, BF16)


def _fourier_kernel(p_ref, ch_ref, tok_ref, y_ref, xs_ref, *, seq):
    for gi in range(F_GROUPS):
        cols = slice(gi * F_GROUP_DIM, (gi + 1) * F_GROUP_DIM)
        x1 = _dot(p_ref[0, :, cols], ch_ref[...]).astype(BF16)
        xs_ref[0:seq, cols] = x1[:, :F_GROUP_DIM]
        xs_ref[seq:2 * seq, cols] = x1[:, F_GROUP_DIM:]
    y_ref[0] = _dot(tok_ref[...], xs_ref[...]).astype(y_ref.dtype)


def _fourier(pf):
    b, s, wdt = pf.shape
    ch, tok = _dft_tables(s)
    kern = functools.partial(_fourier_kernel, seq=s)
    return pl.pallas_call(
        kern,
        grid=(b,),
        in_specs=[pl.BlockSpec((1, s, wdt), lambda i: (i, 0, 0)),
                  pl.BlockSpec(ch.shape, lambda i: (0, 0)),
                  pl.BlockSpec(tok.shape, lambda i: (0, 0), pipeline_mode=pl.Buffered(1))],
        out_specs=pl.BlockSpec((1, s, wdt), lambda i: (i, 0, 0)),
        out_shape=jax.ShapeDtypeStruct((b, s, wdt), BF16),
        scratch_shapes=[pltpu.VMEM((2 * s, wdt), BF16)],
        compiler_params=_cparams(("parallel",), VMEM_LIMIT),
        name="fourier",
    )(pf, ch, tok)


def _layer_norm(t, g, b):
    mu = jnp.mean(t, axis=-1, keepdims=True)
    tc = t - mu
    var = jnp.mean(tc * tc, axis=-1, keepdims=True)
    return tc * lax.rsqrt(var + LN_EPS) * g + b


def _merge_kernel(yf_ref, yh_ref, gt_ref, x_ref, g1_ref, sc2_ref, sh2_ref, wfo_ref, who_ref, wo_ref,
                  lg_ref, lbias_ref, wr_ref, br_ref, x1_ref, idx_ref, wt_ref):
    d = x_ref.shape[-1]
    gf = _sigmoid(gt_ref[0, :, :d].astype(F32))
    gh = _sigmoid(gt_ref[0, :, d:].astype(F32))
    m = gf * _dot(yf_ref[0], wfo_ref[...]) + gh * _dot(yh_ref[0], who_ref[...])
    mix = _dot(m.astype(BF16), wo_ref[...])
    x1 = _layer_norm(DEEPNORM_ALPHA * x_ref[0] + g1_ref[0] * mix, lg_ref[...], lbias_ref[...])
    x1_ref[0] = x1
    u2 = x1 * (1.0 + sc2_ref[0]) + sh2_ref[0]
    logits = _dot(u2.astype(BF16), wr_ref[...]) + br_ref[...]
    lane = lax.broadcasted_iota(jnp.int32, logits.shape, 1)
    work = jnp.where(lane < N_EXPERTS, logits, -jnp.inf)
    vals, idxs = [], []
    for _ in range(TOP_K):
        mx = jnp.max(work, axis=-1, keepdims=True)
        sel = jnp.min(jnp.where(work == mx, lane, LANES), axis=-1, keepdims=True)
        vals.append(mx)
        idxs.append(sel)
        work = jnp.where(lane == sel, -jnp.inf, work)
    exps = [jnp.exp(vv - vals[0]) for vv in vals]
    inv = 1.0 / (exps[0] + exps[1] + exps[2] + exps[3])
    for kk in range(TOP_K):
        idx_ref[0, :, kk:kk + 1] = idxs[kk]
        wt_ref[0, :, kk:kk + 1] = exps[kk] * inv


def _merge(yf, yh, gates, x, g1, sc2, sh2, wfo, who, wo, lg, lbias, wr, br, tm):
    b, s, d = x.shape
    tile = lambda i, j: (i, j, 0)
    per_b = lambda i, j: (i, 0, 0)
    const = lambda i, j: (0, 0)
    return pl.pallas_call(
        _merge_kernel,
        grid=(b, s // tm),
        in_specs=[pl.BlockSpec((1, tm, yf.shape[-1]), tile),
                  pl.BlockSpec((1, tm, yh.shape[-1]), tile),
                  pl.BlockSpec((1, tm, gates.shape[-1]), tile),
                  pl.BlockSpec((1, tm, d), tile),
                  pl.BlockSpec((1, 1, d), per_b),
                  pl.BlockSpec((1, 1, d), per_b),
                  pl.BlockSpec((1, 1, d), per_b),
                  pl.BlockSpec(wfo.shape, const),
                  pl.BlockSpec(who.shape, const),
                  pl.BlockSpec(wo.shape, const),
                  pl.BlockSpec((1, d), const),
                  pl.BlockSpec((1, d), const),
                  pl.BlockSpec(wr.shape, const),
                  pl.BlockSpec(br.shape, const)],
        out_specs=[pl.BlockSpec((1, tm, d), tile),
                   pl.BlockSpec((1, tm, TOP_K), tile),
                   pl.BlockSpec((1, tm, TOP_K), tile)],
        out_shape=[jax.ShapeDtypeStruct((b, s, d), F32),
                   jax.ShapeDtypeStruct((b, s, TOP_K), jnp.int32),
                   jax.ShapeDtypeStruct((b, s, TOP_K), F32)],
        compiler_params=_cparams(("parallel", "parallel"), VMEM_LIMIT),
        name="merge_ln_router",
    )(yf, yh, gates, x, g1, sc2, sh2, wfo, who, wo, lg, lbias, wr, br)


def _lane_cumsum_exclusive(x):
    lane = lax.broadcasted_iota(jnp.int32, x.shape, 1)
    inc = x
    sh = 1
    while sh < LANES:
        inc = inc + jnp.where(lane >= sh, pltpu.roll(inc, sh, 1), 0)
        sh *= 2
    return inc - x


def _route_kernel(idx_ref, strict_ref, dest_ref, cnt_ref, start_ref, run_ref, *, block_rows):
    phase = pl.program_id(0)
    t = pl.program_id(1)
    tt = idx_ref.shape[0]
    lane = lax.broadcasted_iota(jnp.int32, (tt, LANES), 1)
    onehots = [jnp.where(lane == idx_ref[:, kk:kk + 1], 1.0, 0.0) for kk in range(TOP_K)]
    colsums = [jnp.sum(oh, axis=0, keepdims=True) for oh in onehots]

    @pl.when(jnp.logical_and(phase == 0, t == 0))
    def _():
        run_ref[...] = jnp.zeros_like(run_ref)

    @pl.when(phase == 0)
    def _():
        run_ref[...] += colsums[0] + colsums[1] + colsums[2] + colsums[3]

    @pl.when(jnp.logical_and(phase == 1, t == 0))
    def _():
        cnt = run_ref[...].astype(jnp.int32)
        padded = (cnt + (block_rows - 1)) // block_rows * block_rows
        cnt_ref[...] = cnt
        start_ref[...] = _lane_cumsum_exclusive(padded)
        run_ref[...] = jnp.zeros_like(run_ref)

    @pl.when(phase == 1)
    def _():
        prefixes = _dot(strict_ref[...], jnp.concatenate([oh.astype(BF16) for oh in onehots], axis=1))
        base = run_ref[...] + start_ref[...].astype(F32)
        for kk in range(TOP_K):
            prefix = prefixes[:, kk * LANES:(kk + 1) * LANES]
            rank = jnp.sum(onehots[kk] * (prefix + base), axis=-1, keepdims=True)
            dest_ref[:, kk:kk + 1] = rank.astype(jnp.int32)
            base = base + colsums[kk]
        run_ref[...] = base - start_ref[...].astype(F32)


def _route(idx, block_rows, tt):
    n = idx.shape[0]
    kern = functools.partial(_route_kernel, block_rows=block_rows)
    strict = jnp.asarray(np.tril(np.ones((tt, tt)), -1), BF16)
    return pl.pallas_call(
        kern,
        grid=(2, n // tt),
        in_specs=[pl.BlockSpec((tt, TOP_K), lambda p, t: (t, 0)),
                  pl.BlockSpec((tt, tt), lambda p, t: (0, 0))],
        out_specs=[pl.BlockSpec((tt, TOP_K), lambda p, t: (t * p, 0)),
                   pl.BlockSpec((1, LANES), lambda p, t: (0, 0)),
                   pl.BlockSpec((1, LANES), lambda p, t: (0, 0))],
        out_shape=[jax.ShapeDtypeStruct((n, TOP_K), jnp.int32),
                   jax.ShapeDtypeStruct((1, LANES), jnp.int32),
                   jax.ShapeDtypeStruct((1, LANES), jnp.int32)],
        scratch_shapes=[pltpu.VMEM((1, LANES), F32)],
        compiler_params=_cparams(("arbitrary", "arbitrary")),
        name="route_ranks",
    )(idx, strict)


def _to_row_tiles(ref, base, val):
    rows = val.shape[0]
    for s in range(SUBLANES):
        ref[pl.ds(base + s, rows, stride=SUBLANES), :] = val[:, s * LANES:(s + 1) * LANES]


def _from_row_tiles(ref, base, rows):
    return jnp.concatenate([ref[pl.ds(base + s, rows, stride=SUBLANES), :] for s in range(SUBLANES)], axis=1)


def _row_tile(ref, r):
    return ref.at[pl.ds(pl.multiple_of(r * SUBLANES, SUBLANES), SUBLANES), :]


def _row_tiles(ref, r, n):
    return ref.at[pl.ds(pl.multiple_of(r * SUBLANES, SUBLANES), n * SUBLANES), :]


def _dispatch_kernel(cnt_ref, start_ref, dest_ref, x1_ref, sc_ref, sh_ref, h_ref, u_ref, zero_ref, sems, *, block_rows):
    t = pl.program_id(0)
    nt = pl.num_programs(0)
    tt = x1_ref.shape[0]
    slot = t % 2
    ubase = pl.multiple_of(slot * (tt * SUBLANES), tt * SUBLANES)
    _to_row_tiles(u_ref, ubase, x1_ref[...] * (1.0 + sc_ref[0]) + sh_ref[0])

    def issue(i, _):
        for kk in range(TOP_K):
            d = dest_ref[0, 0, i * TOP_K + kk]
            pltpu.make_async_copy(_row_tile(u_ref, slot * tt + i), _row_tile(h_ref, d),
                                  sems.at[slot]).start(priority=kk % DMA_QUEUES)
        return 0

    lax.fori_loop(0, tt, issue, 0)

    def drain(sl):
        for _ in range(TOP_K):
            pltpu.make_async_copy(_row_tiles(u_ref, sl * tt, tt), _row_tiles(h_ref, 0, tt), sems.at[sl]).wait()

    @pl.when(t > 0)
    def _():
        drain(1 - slot)

    @pl.when(t == nt - 1)
    def _():
        drain(slot)
        zero_ref[...] = jnp.zeros_like(zero_ref)

        def pad_expert(e, _):
            cnt = cnt_ref[e]
            rem = (block_rows - (cnt & (block_rows - 1))) & (block_rows - 1)
            pos = start_ref[e] + cnt
            size = block_rows // 2
            while size >= 1:
                take = rem & size

                @pl.when(take != 0)
                def _(pos=pos, size=size):
                    cp = pltpu.make_async_copy(_row_tiles(zero_ref, 0, size), _row_tiles(h_ref, pos, size), sems.at[2])
                    cp.start()
                    cp.wait()

                pos = pos + take
                size //= 2
            return 0

        lax.fori_loop(0, N_EXPERTS, pad_expert, 0)


def _dispatch(counts, starts, dest_flat, x1, sc2, sh2, n_rows, tt):
    n, d = x1.shape
    b = sc2.shape[0]
    tiles_per_b = (n // b) // tt
    per_b = lambda t, c, s: (t // tiles_per_b, 0, 0)
    grid_spec = pltpu.PrefetchScalarGridSpec(
        num_scalar_prefetch=2,
        grid=(n // tt,),
        in_specs=[pl.BlockSpec((1, 1, tt * TOP_K), lambda t, c, s: (t, 0, 0), memory_space=pltpu.SMEM),
                  pl.BlockSpec((tt, d), lambda t, c, s: (t, 0)),
                  pl.BlockSpec((1, 1, d), per_b),
                  pl.BlockSpec((1, 1, d), per_b)],
        out_specs=pl.BlockSpec(memory_space=pl.ANY),
        scratch_shapes=[pltpu.VMEM((2 * tt * SUBLANES, LANES), F32), pltpu.VMEM((MOE_ROWS // 2 * SUBLANES, LANES), F32),
                        pltpu.SemaphoreType.DMA((3,))],
    )
    return pl.pallas_call(
        functools.partial(_dispatch_kernel, block_rows=MOE_ROWS),
        grid_spec=grid_spec,
        out_shape=jax.ShapeDtypeStruct((n_rows * SUBLANES, LANES), F32),
        compiler_params=_cparams(("arbitrary",)),
        name="moe_dispatch",
    )(counts, starts, dest_flat, x1, sc2, sh2)


def _expert_kernel(be_ref, nu_ref, h_ref, wgu_ref, bgu_ref, wdn_ref, bdn_ref, o_ref, wgu_b_ref, wdn_b_ref):
    j = pl.program_id(0)
    d = wdn_ref.shape[-1]
    bm = h_ref.shape[0] // SUBLANES

    @pl.when(j < nu_ref[0])
    def _():
        @pl.when(jnp.logical_or(j == 0, be_ref[j] != be_ref[jnp.maximum(j - 1, 0)]))
        def _():
            wgu_b_ref[...] = wgu_ref[0].astype(BF16)
            wdn_b_ref[...] = wdn_ref[0].astype(BF16)

        gu = _dot(_from_row_tiles(h_ref, 0, bm).astype(BF16), wgu_b_ref[...]) + bgu_ref[0]
        gate = jnp.minimum(gu[:, :d], SWIGLU_LIMIT)
        up = jnp.clip(gu[:, d:], -SWIGLU_LIMIT, SWIGLU_LIMIT)
        act = (up + 1.0) * gate * _sigmoid(SWIGLU_ALPHA * gate)
        _to_row_tiles(o_ref, 0, _dot(act.astype(BF16), wdn_b_ref[...]) + bdn_ref[0])


def _experts(h_sorted, block_expert, n_used, wgu, bgu, wdn, bdn):
    n_rows = h_sorted.shape[0] // SUBLANES
    e, d, _ = wdn.shape
    bm = MOE_ROWS
    n_blocks = n_rows // bm
    blk = lambda j, be, nu: (jnp.minimum(j, nu[0] - 1), 0)
    exp = lambda j, be, nu: (be[jnp.minimum(j, nu[0] - 1)], 0, 0)
    grid_spec = pltpu.PrefetchScalarGridSpec(
        num_scalar_prefetch=2,
        grid=(n_blocks,),
        in_specs=[pl.BlockSpec((bm * SUBLANES, LANES), blk),
                  pl.BlockSpec((1, d, 2 * d), exp),
                  pl.BlockSpec((1, 1, 2 * d), exp),
                  pl.BlockSpec((1, d, d), exp),
                  pl.BlockSpec((1, 1, d), exp)],
        out_specs=pl.BlockSpec((bm * SUBLANES, LANES), blk),
        scratch_shapes=[pltpu.VMEM((d, 2 * d), BF16), pltpu.VMEM((d, d), BF16)],
    )
    return pl.pallas_call(
        _expert_kernel,
        grid_spec=grid_spec,
        out_shape=jax.ShapeDtypeStruct((n_rows * SUBLANES, LANES), F32),
        compiler_params=_cparams(("arbitrary",), VMEM_LIMIT),
        name="moe_experts",
    )(block_expert, n_used, h_sorted, wgu, bgu.reshape(e, 1, 2 * d), wdn, bdn.reshape(e, 1, d))


def _combine_kernel(dest_ref, dest_next_ref, wt_ref, x1_ref, g2_ref, lg_ref, lbias_ref, o_hbm, y_ref, buf_ref, sems):
    t = pl.program_id(0)
    nt = pl.num_programs(0)
    tt = x1_ref.shape[0]
    slot = t % 2

    def issue(d_ref, sl):
        def body(i, _):
            for kk in range(TOP_K):
                d = d_ref[0, 0, i * TOP_K + kk]
                pltpu.make_async_copy(_row_tile(o_hbm, d), _row_tile(buf_ref, (sl * TOP_K + kk) * tt + i),
                                      sems.at[sl]).start(priority=kk % DMA_QUEUES)
            return 0

        lax.fori_loop(0, tt, body, 0)

    @pl.when(t == 0)
    def _():
        issue(dest_ref, 0)

    @pl.when(t + 1 < nt)
    def _():
        issue(dest_next_ref, 1 - slot)

    for kk in range(TOP_K):
        pltpu.make_async_copy(_row_tiles(o_hbm, 0, tt), _row_tiles(buf_ref, (slot * TOP_K + kk) * tt, tt),
                              sems.at[slot]).wait()

    ff = None
    for kk in range(TOP_K):
        base = pl.multiple_of((slot * TOP_K + kk) * (tt * SUBLANES), tt * SUBLANES)
        term = wt_ref[:, kk:kk + 1] * _from_row_tiles(buf_ref, base, tt)
        ff = term if ff is None else ff + term
    y_ref[...] = _layer_norm(DEEPNORM_ALPHA * x1_ref[...] + g2_ref[0] * ff, lg_ref[...], lbias_ref[...])


def _combine(dest_flat, wt, x1, g2, lg, lbias, o_sorted, tt):
    n, d = x1.shape
    b = g2.shape[0]
    nt = n // tt
    tiles_per_b = (n // b) // tt
    const = lambda t: (0, 0)
    dest_spec = lambda im: pl.BlockSpec((1, 1, tt * TOP_K), im, memory_space=pltpu.SMEM)
    return pl.pallas_call(
        _combine_kernel,
        grid=(nt,),
        in_specs=[dest_spec(lambda t: (t, 0, 0)),
                  dest_spec(lambda t: (jnp.minimum(t + 1, nt - 1), 0, 0)),
                  pl.BlockSpec((tt, TOP_K), lambda t: (t, 0)),
                  pl.BlockSpec((tt, d), lambda t: (t, 0)),
                  pl.BlockSpec((1, 1, d), lambda t: (t // tiles_per_b, 0, 0)),
                  pl.BlockSpec((1, d), const),
                  pl.BlockSpec((1, d), const),
                  pl.BlockSpec(memory_space=pl.ANY)],
        out_specs=pl.BlockSpec((tt, d), lambda t: (t, 0)),
        out_shape=jax.ShapeDtypeStruct((n, d), F32),
        scratch_shapes=[pltpu.VMEM((2 * TOP_K * tt * SUBLANES, LANES), F32), pltpu.SemaphoreType.DMA((2,))],
        compiler_params=_cparams(("arbitrary",), VMEM_LIMIT),
        name="moe_combine",
    )(dest_flat, dest_flat, wt, x1, g2, lg, lbias, o_sorted)


def kernel(x, c, ctx, c_ctx, w_ada, b_ada, w_in, lb_raw, hg_norm_g, w_four_out, w_hg_out, w_o, ln1_g, ln1_b,
           w_router, b_router, w_gate_up, b_gate_up, w_down, b_down, ln2_g, ln2_b):
    b, s, d = x.shape
    lc = ctx.shape[1]
    n = b * s
    f_w = F_GROUPS * F_GROUP_DIM
    hk = HG_HEADS * HG_D
    assert w_ada.shape[0] == DEPTH and s % GRID_W == 0 and s % (HG_TILE * HG_GROUP) == 0 and s % 512 == 0 and lc % HG_TILE == 0

    pad = (-(b + 1)) % 8
    c_rows = jnp.concatenate([c, c_ctx[None, :], jnp.zeros((pad, d), F32)], axis=0)
    mod = _ada(c_rows, w_ada[0], b_ada[0]).reshape(b + 1 + pad, N_MOD, 1, d)
    shift1, scale1, gate1, shift2, scale2, gate2 = (mod[:b, i] for i in range(N_MOD))
    cshift1 = jnp.broadcast_to(mod[b, 0], (b, 1, d))
    cscale1 = jnp.broadcast_to(mod[b, 1], (b, 1, d))

    lower_bounds = jnp.cumsum(jax.nn.softmax(lb_raw.astype(F32), axis=0), axis=0)[0]

    w_in_b = w_in[0].astype(BF16)
    o_f, o_q, o_z, o_v, o_g, o_gt = f_w, f_w + hk, f_w + 3 * hk, f_w + 4 * hk, f_w + 5 * hk, f_w + 5 * hk + 2 * d
    lb_row = lower_bounds.reshape(1, 2 * hk)
    pf, pq, pk, phi, plo, pv, pg, gates = _inproj(
        x, scale1, shift1, lb_row, w_in_b,
        [(0, f_w, "cast"), (o_f, hk, "silu"), (o_q, 2 * hk, "forget"), (o_z, hk, "cast"), (o_v, hk, "cast"),
         (o_g, 2 * d, "cast")], tm=512)
    w_ctx = w_in_b[:, o_q:o_v]
    ck, chi, clo, cv = _inproj(ctx, cscale1, cshift1, lb_row, w_ctx, [(0, 2 * hk, "forget"), (2 * hk, hk, "cast")], tm=lc)

    y_h = _hgrn(pq, pk, phi, plo, pv, pg, ck, chi, clo, cv, hg_norm_g[0].reshape(1, hk))
    y_f = _fourier(pf)

    wr = jnp.zeros((d, LANES), BF16).at[:, :N_EXPERTS].set(w_router[0].astype(BF16))
    br = jnp.zeros((1, LANES), F32).at[0, :N_EXPERTS].set(b_router[0])
    x1, idx, wt = _merge(y_f, y_h, gates, x, gate1, scale2, shift2,
                         w_four_out[0].astype(BF16), w_hg_out[0].astype(BF16), w_o[0].astype(BF16),
                         ln1_g[0].reshape(1, d), ln1_b[0].reshape(1, d), wr, br, tm=512)

    idx2 = idx.reshape(n, TOP_K)
    dest, counts, starts = _route(idx2, MOE_ROWS, 512)
    n_blocks = -(-(n * TOP_K) // MOE_ROWS) + N_EXPERTS
    n_rows = n_blocks * MOE_ROWS
    cnt = counts[0, :N_EXPERTS]
    st = starts[0, :N_EXPERTS]
    pad_ends = st + (cnt + MOE_ROWS - 1) // MOE_ROWS * MOE_ROWS
    block_first_row = jnp.arange(n_blocks, dtype=jnp.int32) * MOE_ROWS
    block_expert = jnp.minimum(jnp.sum((pad_ends[None, :] <= block_first_row[:, None]).astype(jnp.int32), axis=1),
                               N_EXPERTS - 1)
    n_used = (pad_ends[-1] // MOE_ROWS).astype(jnp.int32).reshape(1)

    tt = 256
    dest_flat = dest.reshape(n // tt, 1, tt * TOP_K)
    x1_2d = x1.reshape(n, d)
    h_sorted = _dispatch(cnt, st, dest_flat, x1_2d, scale2, shift2, n_rows, tt)
    o_sorted = _experts(h_sorted, block_expert, n_used, w_gate_up[0], b_gate_up[0], w_down[0], b_down[0])
    out = _combine(dest_flat, wt.reshape(n, TOP_K), x1_2d, gate2, ln2_g[0].reshape(1, d), ln2_b[0].reshape(1, d),
                   o_sorted, tt)
    return out.reshape(b, s, d)
```

```python
import functools
import math

import jax
import jax.numpy as jnp
import numpy as np
from jax import lax
from jax.experimental import pallas as pl
from jax.experimental.pallas import tpu as pltpu

F32 = jnp.float32
BF16 = jnp.bfloat16

GRID_W = 64
N_MOD = 6
F_GROUPS = 4
F_GROUP_DIM = 128
HG_HEADS = 4
HG_D = 128
CHUNK = 64
N_EXPERTS = 32
TOP_K = 4
SWIGLU_LIMIT = 7.0
SWIGLU_ALPHA = 1.702
LN_EPS = 1e-5
RMS_EPS = 1e-6
DEPTH = 1
DEEPNORM_ALPHA = (2.0 * DEPTH) ** 0.25

LANES = 128
SUBLANES = 8
MOE_ROWS = 512
COMBINE_ROWS = 128
ISSUE_UNROLL = 4
DMA_QUEUES = 2
VMEM_LIMIT = 56 * 1024 * 1024


def _cparams(sem, vmem=None):
    return pltpu.CompilerParams(dimension_semantics=sem, vmem_limit_bytes=vmem)


def _dot(a, b):
    return jnp.dot(a, b, preferred_element_type=F32)


def _sigmoid(x):
    return 1.0 / (1.0 + jnp.exp(-x))


def _ada_kernel(c_ref, w_ref, b_ref, o_ref):
    c = c_ref[...]
    a = c * _sigmoid(c)
    o_ref[...] = jnp.dot(a, w_ref[...], preferred_element_type=F32,
                         precision=lax.Precision.HIGHEST) + b_ref[...]


def _ada(c_rows, w, b):
    r, d = c_rows.shape
    n = w.shape[1]
    tn = 512
    return pl.pallas_call(
        _ada_kernel,
        grid=(n // tn,),
        in_specs=[pl.BlockSpec((r, d), lambda j: (0, 0)),
                  pl.BlockSpec((d, tn), lambda j: (0, j)),
                  pl.BlockSpec((1, tn), lambda j: (0, j))],
        out_specs=pl.BlockSpec((r, tn), lambda j: (0, j)),
        out_shape=jax.ShapeDtypeStruct((r, n), F32),
        compiler_params=_cparams(("arbitrary",)),
        name="ada_mod",
    )(c_rows, w, b.reshape(1, n))


def _inproj_kernel(x_ref, sc_ref, sh_ref, lb_ref, w_ref, *o_refs, plan):
    u = (x_ref[0] * (1.0 + sc_ref[0]) + sh_ref[0]).astype(BF16)
    refs = iter(o_refs)
    for kind, chunks in plan:
        if kind == "forget":
            k_ref, hi_ref, lo_ref = next(refs), next(refs), next(refs)
            for (w0, o0, n) in chunks:
                lb = lb_ref[:, o0:o0 + n]
                f = lb + (1.0 - lb) * _sigmoid(_dot(u, w_ref[:, w0:w0 + n]))
                lf = jnp.log(f)
                hi = lf.astype(BF16)
                k_ref[0, :, o0:o0 + n] = (1.0 - f).astype(BF16)
                hi_ref[0, :, o0:o0 + n] = hi
                lo_ref[0, :, o0:o0 + n] = (lf - hi.astype(F32)).astype(BF16)
        else:
            o_ref = next(refs)
            for (w0, o0, n) in chunks:
                p = _dot(u, w_ref[:, w0:w0 + n])
                if kind == "silu":
                    p = p * _sigmoid(p)
                o_ref[0, :, o0:o0 + n] = p.astype(o_ref.dtype)


def _inproj(x, scale, shift, lb_row, w_bf16, outs, tm):
    b, s, d = x.shape
    n_w = w_bf16.shape[1]
    plan, widths = [], []
    for (c0, width, kind) in outs:
        step = min(width, 512)
        plan.append((kind, tuple((c0 + o, o, step) for o in range(0, width, step))))
        widths += [width] * (3 if kind == "forget" else 1)
    kern = functools.partial(_inproj_kernel, plan=tuple(plan))
    return pl.pallas_call(
        kern,
        grid=(b, s // tm),
        in_specs=[pl.BlockSpec((1, tm, d), lambda i, j: (i, j, 0)),
                  pl.BlockSpec((1, 1, d), lambda i, j: (i, 0, 0)),
                  pl.BlockSpec((1, 1, d), lambda i, j: (i, 0, 0)),
                  pl.BlockSpec(lb_row.shape, lambda i, j: (0, 0)),
                  pl.BlockSpec((d, n_w), lambda i, j: (0, 0))],
        out_specs=[pl.BlockSpec((1, tm, width), lambda i, j: (i, j, 0)) for width in widths],
        out_shape=[jax.ShapeDtypeStruct((b, s, width), BF16) for width in widths],
        compiler_params=_cparams(("parallel", "parallel"), VMEM_LIMIT),
        name="in_proj",
    )(x, scale, shift, lb_row, w_bf16)


HG_TILE = 256
HG_GROUP = 4


def _gla_group(jobs, states):
    states = list(states)
    nj = len(jobs)
    r, dk = jobs[0][1].shape
    dv = jobs[0][4].shape[1]
    nc = r // CHUNK
    zero = jnp.zeros((CHUNK, dk), BF16)
    contract_last = (((1,), (1,)), ((), ()))
    contract_rows = (((0,), (0,)), ((), ()))

    cums = []
    for (_, hi, lo, _, _, _, tri, _, _) in jobs:
        both = _dot(tri, jnp.concatenate([hi, lo], axis=1))
        cums.append(both[:, :dk] + both[:, dk:])

    k_inv_b, decays, k_end_blk, q_dec = [], [], [], []
    for cum, (_, _, _, k, _, q, _, _, forward) in zip(cums, jobs):
        k_inv = k.astype(F32) * jnp.exp(-cum)
        k_inv_b.append(k_inv.astype(BF16))
        tot_row = CHUNK - 1 if forward else 0
        dec_j, rows_j = [], []
        for ci in range(nc):
            dec = jnp.exp(cum[ci * CHUNK + tot_row:ci * CHUNK + tot_row + 1, :])
            k_end = (k_inv[ci * CHUNK:(ci + 1) * CHUNK] * dec).astype(BF16)
            dec_j.append(dec)
            rows_j.append(jnp.concatenate([k_end if cj == ci else zero for cj in range(nc)], axis=1))
        decays.append(dec_j)
        k_end_blk.append(jnp.concatenate(rows_j, axis=0))
        q_dec.append(None if q is None else (q.astype(F32) * jnp.exp(cum)).astype(BF16))

    upd_all = [lax.dot_general(job[4], blk, contract_rows, preferred_element_type=F32)
               for job, blk in zip(jobs, k_end_blk)]
    scores = [None if qd is None else lax.dot_general(qd, kb, contract_last, preferred_element_type=F32)
              for qd, kb in zip(q_dec, k_inv_b)]

    s_all = []
    for ji in range(nj):
        chain, forward = jobs[ji][0], jobs[ji][8]
        st = states[chain]
        entering = [None] * nc
        for ci in (range(nc) if forward else reversed(range(nc))):
            entering[ci] = st
            st = st * decays[ji][ci] + upd_all[ji][:, ci * dk:(ci + 1) * dk]
        states[chain] = st
        s_all.append(None if q_dec[ji] is None
                     else [jnp.concatenate([entering[ci].astype(BF16), entering[ci + 1].astype(BF16)], axis=0)
                           for ci in range(0, nc, 2)])

    outs = []
    for ji in range(nj):
        if q_dec[ji] is None:
            outs.append(None)
            continue
        sc = jnp.where(jobs[ji][7] > 0, scores[ji], 0.0).astype(BF16)
        intra = _dot(sc, jobs[ji][4])
        inter = []
        for pi, s_pair in enumerate(s_all[ji]):
            rows = slice(2 * pi * CHUNK, (2 * pi + 2) * CHUNK)
            both = lax.dot_general(q_dec[ji][rows], s_pair, contract_last, preferred_element_type=F32)
            inter += [both[:CHUNK, :dv], both[CHUNK:, dv:]]
        outs.append(intra + jnp.concatenate(inter, axis=0))
    return outs, states


def _hgrn_kernel(q_ref, kf_ref, kb_ref, hif_ref, hib_ref, lof_ref, lob_ref, v_ref, g_ref,
                 ckf_ref, ckb_ref, chif_ref, chib_ref, clof_ref, clob_ref, cv_ref, ng_ref, trif_ref, trib_ref,
                 keepf_ref, keepb_ref, y_ref, of_ref, ob_ref, *, seq, ctx_len):
    r = HG_TILE
    n_lat = seq // r
    n_ctx = ctx_len // r
    group = HG_GROUP

    def ctx_body(i, carry):
        sl_f = pl.ds(pl.multiple_of(i * r, r), r)
        sl_b = pl.ds(pl.multiple_of((n_ctx - 1 - i) * r, r), r)
        jobs = [(0, chif_ref[0, sl_f, :], clof_ref[0, sl_f, :], ckf_ref[0, sl_f, :], cv_ref[0, sl_f, :], None,
                 trif_ref[...], None, True),
                (1, chib_ref[0, sl_b, :], clob_ref[0, sl_b, :], ckb_ref[0, sl_b, :], cv_ref[0, sl_b, :], None,
                 trib_ref[...], None, False)]
        _, states = _gla_group(jobs, carry)
        return tuple(states)

    zero = jnp.zeros((HG_D, HG_D), F32)
    carry = lax.fori_loop(0, n_ctx, ctx_body, (zero, zero))

    def lat_body(i, carry):
        jobs, slices = [], []
        for t in range(group):
            sl_f = pl.ds(pl.multiple_of((i * group + t) * r, r), r)
            sl_b = pl.ds(pl.multiple_of((n_lat - 1 - (i * group + t)) * r, r), r)
            jobs.append((0, hif_ref[0, sl_f, :], lof_ref[0, sl_f, :], kf_ref[0, sl_f, :], v_ref[0, sl_f, :],
                         q_ref[0, sl_f, :], trif_ref[...], keepf_ref[...], True))
            jobs.append((1, hib_ref[0, sl_b, :], lob_ref[0, sl_b, :], kb_ref[0, sl_b, :], v_ref[0, sl_b, :],
                         q_ref[0, sl_b, :], trib_ref[...], keepb_ref[...], False))
            slices += [(of_ref, sl_f), (ob_ref, sl_b)]
        outs, states = _gla_group(jobs, carry)
        for (ref, sl), o in zip(slices, outs):
            ref[sl, :] = o
        return tuple(states)

    lax.fori_loop(0, n_lat // group, lat_body, carry)

    ng = ng_ref[...]

    def out_body(i, _):
        sl = pl.ds(pl.multiple_of(i * r, r), r)
        o = of_ref[sl, :] + ob_ref[sl, :]
        o = o * lax.rsqrt(jnp.mean(o * o, axis=-1, keepdims=True) + RMS_EPS) * ng
        g = g_ref[0, sl, :].astype(F32)
        y_ref[0, sl, :] = (o * (g * _sigmoid(g))).astype(y_ref.dtype)
        return 0

    lax.fori_loop(0, n_lat, out_body, 0)


def _chunk_tri(r, forward, dtype):
    i = np.arange(r)
    same = (i[:, None] // CHUNK) == (i[None, :] // CHUNK)
    order = (i[:, None] >= i[None, :]) if forward else (i[:, None] <= i[None, :])
    return jnp.asarray(same & order, dtype)


def _hgrn(q, k, hi, lo, v, g, ck, chi, clo, cv, ng):
    b, s, _ = q.shape
    lc = ck.shape[1]
    h = HG_HEADS
    d = HG_D
    r = HG_TILE
    kern = functools.partial(_hgrn_kernel, seq=s, ctx_len=lc)
    fwd = lambda i, j: (i, 0, j)
    bwd = lambda i, j: (i, 0, j + h)
    const = lambda i, j: (0, 0)
    lat = lambda im: pl.BlockSpec((1, s, d), im)
    cx = lambda im: pl.BlockSpec((1, lc, d), im)
    return pl.pallas_call(
        kern,
        grid=(b, h),
        in_specs=[lat(fwd), lat(fwd), lat(bwd), lat(fwd), lat(bwd), lat(fwd), lat(bwd), lat(fwd), lat(fwd),
                  cx(fwd), cx(bwd), cx(fwd), cx(bwd), cx(fwd), cx(bwd), cx(fwd),
                  pl.BlockSpec((1, d), lambda i, j: (0, j)),
                  pl.BlockSpec((r, r), const),
                  pl.BlockSpec((r, r), const),
                  pl.BlockSpec((r, r), const),
                  pl.BlockSpec((r, r), const)],
        out_specs=pl.BlockSpec((1, s, d), fwd),
        out_shape=jax.ShapeDtypeStruct((b, s, h * d), BF16),
        scratch_shapes=[pltpu.VMEM((s, d), F32), pltpu.VMEM((s, d), F32)],
        compiler_params=_cparams(("parallel", "parallel"), VMEM_LIMIT),
        name="hgrn2",
    )(q, k, k, hi, hi, lo, lo, v, g, ck, ck, chi, chi, clo, clo, cv, ng,
      _chunk_tri(r, True, BF16), _chunk_tri(r, False, BF16), _chunk_tri(r, True, F32), _chunk_tri(r, False, F32))


def _dft_tables(seq):
    rows = seq // GRID_W
    n = F_GROUP_DIM
    kc = np.outer(np.arange(n), np.arange(n)) % n
    ang = 2.0 * np.pi * kc / n
    norm = 1.0 / math.sqrt(rows * GRID_W * n)
    ch = np.concatenate([np.cos(ang), -np.sin(ang)], axis=1) * norm
    t = np.arange(seq)
    r, w = t // GRID_W, t % GRID_W
    m = (np.outer(r, r) * GRID_W + np.outer(w, w) * rows) % (rows * GRID_W)
    ang_t = 2.0 * np.pi * m / (rows * GRID_W)
    tok = np.concatenate([np.cos(ang_t), np.sin(ang_t)], axis=1)
    return jnp.asarray(ch, BF16), jnp.asarray(tok, BF16)


def _fourier_kernel(p_ref, ch_ref, tok_ref, y_ref, xs_ref, *, seq):
    for gi in range(F_GROUPS):
        cols = slice(gi * F_GROUP_DIM, (gi + 1) * F_GROUP_DIM)
        x1 = _dot(p_ref[0, :, cols], ch_ref[...]).astype(BF16)
        xs_ref[0:seq, cols] = x1[:, :F_GROUP_DIM]
        xs_ref[seq:2 * seq, cols] = x1[:, F_GROUP_DIM:]
    y_ref[0] = _dot(tok_ref[...], xs_ref[...]).astype(y_ref.dtype)


def _fourier(pf):
    b, s, wdt = pf.shape
    ch, tok = _dft_tables(s)
    kern = functools.partial(_fourier_kernel, seq=s)
    return pl.pallas_call(
        kern,
        grid=(b,),
        in_specs=[pl.BlockSpec((1, s, wdt), lambda i: (i, 0, 0)),
                  pl.BlockSpec(ch.shape, lambda i: (0, 0)),
                  pl.BlockSpec(tok.shape, lambda i: (0, 0), pipeline_mode=pl.Buffered(1))],
        out_specs=pl.BlockSpec((1, s, wdt), lambda i: (i, 0, 0)),
        out_shape=jax.ShapeDtypeStruct((b, s, wdt), BF16),
        scratch_shapes=[pltpu.VMEM((2 * s, wdt), BF16)],
        compiler_params=_cparams(("parallel",), VMEM_LIMIT),
        name="fourier",
    )(pf, ch, tok)


def _layer_norm(t, g, b):
    mu = jnp.mean(t, axis=-1, keepdims=True)
    tc = t - mu
    var = jnp.mean(tc * tc, axis=-1, keepdims=True)
    return tc * lax.rsqrt(var + LN_EPS) * g + b


def _merge_kernel(yf_ref, yh_ref, gt_ref, x_ref, g1_ref, sc2_ref, sh2_ref, wfo_ref, who_ref, wo_ref,
                  lg_ref, lbias_ref, wr_ref, br_ref, x1_ref, idx_ref, wt_ref):
    d = x_ref.shape[-1]
    gf = _sigmoid(gt_ref[0, :, :d].astype(F32))
    gh = _sigmoid(gt_ref[0, :, d:].astype(F32))
    m = gf * _dot(yf_ref[0], wfo_ref[...]) + gh * _dot(yh_ref[0], who_ref[...])
    mix = _dot(m.astype(BF16), wo_ref[...])
    x1 = _layer_norm(DEEPNORM_ALPHA * x_ref[0] + g1_ref[0] * mix, lg_ref[...], lbias_ref[...])
    x1_ref[0] = x1
    u2 = x1 * (1.0 + sc2_ref[0]) + sh2_ref[0]
    logits = _dot(u2.astype(BF16), wr_ref[...]) + br_ref[...]
    lane = lax.broadcasted_iota(jnp.int32, logits.shape, 1)
    work = jnp.where(lane < N_EXPERTS, logits, -jnp.inf)
    vals, idxs = [], []
    for _ in range(TOP_K):
        mx = jnp.max(work, axis=-1, keepdims=True)
        sel = jnp.min(jnp.where(work == mx, lane, LANES), axis=-1, keepdims=True)
        vals.append(mx)
        idxs.append(sel)
        work = jnp.where(lane == sel, -jnp.inf, work)
    exps = [jnp.exp(vv - vals[0]) for vv in vals]
    inv = 1.0 / (exps[0] + exps[1] + exps[2] + exps[3])
    for kk in range(TOP_K):
        idx_ref[0, :, kk:kk + 1] = idxs[kk]
        wt_ref[0, :, kk:kk + 1] = exps[kk] * inv


def _merge(yf, yh, gates, x, g1, sc2, sh2, wfo, who, wo, lg, lbias, wr, br, tm):
    b, s, d = x.shape
    tile = lambda i, j: (i, j, 0)
    per_b = lambda i, j: (i, 0, 0)
    const = lambda i, j: (0, 0)
    return pl.pallas_call(
        _merge_kernel,
        grid=(b, s // tm),
        in_specs=[pl.BlockSpec((1, tm, yf.shape[-1]), tile),
                  pl.BlockSpec((1, tm, yh.shape[-1]), tile),
                  pl.BlockSpec((1, tm, gates.shape[-1]), tile),
                  pl.BlockSpec((1, tm, d), tile),
                  pl.BlockSpec((1, 1, d), per_b),
                  pl.BlockSpec((1, 1, d), per_b),
                  pl.BlockSpec((1, 1, d), per_b),
                  pl.BlockSpec(wfo.shape, const),
                  pl.BlockSpec(who.shape, const),
                  pl.BlockSpec(wo.shape, const),
                  pl.BlockSpec((1, d), const),
                  pl.BlockSpec((1, d), const),
                  pl.BlockSpec(wr.shape, const),
                  pl.BlockSpec(br.shape, const)],
        out_specs=[pl.BlockSpec((1, tm, d), tile),
                   pl.BlockSpec((1, tm, TOP_K), tile),
                   pl.BlockSpec((1, tm, TOP_K), tile)],
        out_shape=[jax.ShapeDtypeStruct((b, s, d), F32),
                   jax.ShapeDtypeStruct((b, s, TOP_K), jnp.int32),
                   jax.ShapeDtypeStruct((b, s, TOP_K), F32)],
        compiler_params=_cparams(("parallel", "parallel"), VMEM_LIMIT),
        name="merge_ln_router",
    )(yf, yh, gates, x, g1, sc2, sh2, wfo, who, wo, lg, lbias, wr, br)


def _lane_cumsum_exclusive(x):
    lane = lax.broadcasted_iota(jnp.int32, x.shape, 1)
    inc = x
    sh = 1
    while sh < LANES:
        inc = inc + jnp.where(lane >= sh, pltpu.roll(inc, sh, 1), 0)
        sh *= 2
    return inc - x


def _route_kernel(idx_ref, strict_ref, dest_ref, cnt_ref, start_ref, run_ref, *, block_rows):
    phase = pl.program_id(0)
    t = pl.program_id(1)
    tt = idx_ref.shape[0]
    lane = lax.broadcasted_iota(jnp.int32, (tt, LANES), 1)
    onehots = [jnp.where(lane == idx_ref[:, kk:kk + 1], 1.0, 0.0) for kk in range(TOP_K)]
    colsums = [jnp.sum(oh, axis=0, keepdims=True) for oh in onehots]

    @pl.when(jnp.logical_and(phase == 0, t == 0))
    def _():
        run_ref[...] = jnp.zeros_like(run_ref)

    @pl.when(phase == 0)
    def _():
        run_ref[...] += colsums[0] + colsums[1] + colsums[2] + colsums[3]

    @pl.when(jnp.logical_and(phase == 1, t == 0))
    def _():
        cnt = run_ref[...].astype(jnp.int32)
        padded = (cnt + (block_rows - 1)) // block_rows * block_rows
        cnt_ref[...] = cnt
        start_ref[...] = _lane_cumsum_exclusive(padded)
        run_ref[...] = jnp.zeros_like(run_ref)

    @pl.when(phase == 1)
    def _():
        prefixes = _dot(strict_ref[...], jnp.concatenate([oh.astype(BF16) for oh in onehots], axis=1))
        base = run_ref[...] + start_ref[...].astype(F32)
        for kk in range(TOP_K):
            prefix = prefixes[:, kk * LANES:(kk + 1) * LANES]
            rank = jnp.sum(onehots[kk] * (prefix + base), axis=-1, keepdims=True)
            dest_ref[:, kk:kk + 1] = rank.astype(jnp.int32)
            base = base + colsums[kk]
        run_ref[...] = base - start_ref[...].astype(F32)


def _route(idx, block_rows, tt):
    n = idx.shape[0]
    kern = functools.partial(_route_kernel, block_rows=block_rows)
    strict = jnp.asarray(np.tril(np.ones((tt, tt)), -1), BF16)
    return pl.pallas_call(
        kern,
        grid=(2, n // tt),
        in_specs=[pl.BlockSpec((tt, TOP_K), lambda p, t: (t, 0)),
                  pl.BlockSpec((tt, tt), lambda p, t: (0, 0))],
        out_specs=[pl.BlockSpec((tt, TOP_K), lambda p, t: (t * p, 0)),
                   pl.BlockSpec((1, LANES), lambda p, t: (0, 0)),
                   pl.BlockSpec((1, LANES), lambda p, t: (0, 0))],
        out_shape=[jax.ShapeDtypeStruct((n, TOP_K), jnp.int32),
                   jax.ShapeDtypeStruct((1, LANES), jnp.int32),
                   jax.ShapeDtypeStruct((1, LANES), jnp.int32)],
        scratch_shapes=[pltpu.VMEM((1, LANES), F32)],
        compiler_params=_cparams(("arbitrary", "arbitrary")),
        name="route_ranks",
    )(idx, strict)


def _to_row_tiles(ref, base, val):
    rows = val.shape[0]
    for s in range(SUBLANES):
        ref[pl.ds(base + s, rows, stride=SUBLANES), :] = val[:, s * LANES:(s + 1) * LANES]


def _from_row_tiles(ref, base, rows):
    return jnp.concatenate([ref[pl.ds(base + s, rows, stride=SUBLANES), :] for s in range(SUBLANES)], axis=1)


def _row_tile(ref, r):
    return ref.at[pl.ds(pl.multiple_of(r * SUBLANES, SUBLANES), SUBLANES), :]


def _row_tiles(ref, r, n):
    return ref.at[pl.ds(pl.multiple_of(r * SUBLANES, SUBLANES), n * SUBLANES), :]


def _dispatch_kernel(cnt_ref, start_ref, dest_ref, x1_ref, sc_ref, sh_ref, h_ref, u_ref, zero_ref, sems, *, block_rows):
    t = pl.program_id(0)
    nt = pl.num_programs(0)
    tt = x1_ref.shape[0]
    slot = t % 2
    ubase = pl.multiple_of(slot * (tt * SUBLANES), tt * SUBLANES)
    _to_row_tiles(u_ref, ubase, x1_ref[...] * (1.0 + sc_ref[0]) + sh_ref[0])

    def issue(i, _):
        for kk in range(TOP_K):
            d = dest_ref[0, 0, i * TOP_K + kk]
            pltpu.make_async_copy(_row_tile(u_ref, slot * tt + i), _row_tile(h_ref, d),
                                  sems.at[slot]).start(priority=kk % DMA_QUEUES)
        return 0

    lax.fori_loop(0, tt, issue, 0, unroll=ISSUE_UNROLL)

    def drain(sl):
        for _ in range(TOP_K):
            pltpu.make_async_copy(_row_tiles(u_ref, sl * tt, tt), _row_tiles(h_ref, 0, tt), sems.at[sl]).wait()

    @pl.when(t > 0)
    def _():
        drain(1 - slot)

    @pl.when(t == nt - 1)
    def _():
        drain(slot)
        zero_ref[...] = jnp.zeros_like(zero_ref)

        def pad_expert(e, _):
            cnt = cnt_ref[e]
            rem = (block_rows - (cnt & (block_rows - 1))) & (block_rows - 1)
            pos = start_ref[e] + cnt
            size = block_rows // 2
            while size >= 1:
                take = rem & size

                @pl.when(take != 0)
                def _(pos=pos, size=size):
                    cp = pltpu.make_async_copy(_row_tiles(zero_ref, 0, size), _row_tiles(h_ref, pos, size), sems.at[2])
                    cp.start()
                    cp.wait()

                pos = pos + take
                size //= 2
            return 0

        lax.fori_loop(0, N_EXPERTS, pad_expert, 0)


def _dispatch(counts, starts, dest_flat, x1, sc2, sh2, n_rows, tt):
    n, d = x1.shape
    b = sc2.shape[0]
    tiles_per_b = (n // b) // tt
    per_b = lambda t, c, s: (t // tiles_per_b, 0, 0)
    grid_spec = pltpu.PrefetchScalarGridSpec(
        num_scalar_prefetch=2,
        grid=(n // tt,),
        in_specs=[pl.BlockSpec((1, 1, tt * TOP_K), lambda t, c, s: (t, 0, 0), memory_space=pltpu.SMEM),
                  pl.BlockSpec((tt, d), lambda t, c, s: (t, 0)),
                  pl.BlockSpec((1, 1, d), per_b),
                  pl.BlockSpec((1, 1, d), per_b)],
        out_specs=pl.BlockSpec(memory_space=pl.ANY),
        scratch_shapes=[pltpu.VMEM((2 * tt * SUBLANES, LANES), F32), pltpu.VMEM((MOE_ROWS // 2 * SUBLANES, LANES), F32),
                        pltpu.SemaphoreType.DMA((3,))],
    )
    return pl.pallas_call(
        functools.partial(_dispatch_kernel, block_rows=MOE_ROWS),
        grid_spec=grid_spec,
        out_shape=jax.ShapeDtypeStruct((n_rows * SUBLANES, LANES), F32),
        compiler_params=_cparams(("arbitrary",)),
        name="moe_dispatch",
    )(counts, starts, dest_flat, x1, sc2, sh2)


def _expert_kernel(be_ref, nu_ref, h_ref, wgu_ref, bgu_ref, wdn_ref, bdn_ref, o_ref, wgu_b_ref, wdn_b_ref):
    j = pl.program_id(0)
    d = wdn_ref.shape[-1]
    bm = h_ref.shape[0] // SUBLANES

    @pl.when(j < nu_ref[0])
    def _():
        @pl.when(jnp.logical_or(j == 0, be_ref[j] != be_ref[jnp.maximum(j - 1, 0)]))
        def _():
            wgu_b_ref[...] = wgu_ref[0].astype(BF16)
            wdn_b_ref[...] = wdn_ref[0].astype(BF16)

        gu = _dot(_from_row_tiles(h_ref, 0, bm).astype(BF16), wgu_b_ref[...]) + bgu_ref[0]
        gate = jnp.minimum(gu[:, :d], SWIGLU_LIMIT)
        up = jnp.clip(gu[:, d:], -SWIGLU_LIMIT, SWIGLU_LIMIT)
        act = (up + 1.0) * gate * _sigmoid(SWIGLU_ALPHA * gate)
        _to_row_tiles(o_ref, 0, _dot(act.astype(BF16), wdn_b_ref[...]) + bdn_ref[0])


def _experts(h_sorted, block_expert, n_used, wgu, bgu, wdn, bdn):
    n_rows = h_sorted.shape[0] // SUBLANES
    e, d, _ = wdn.shape
    bm = MOE_ROWS
    n_blocks = n_rows // bm
    blk = lambda j, be, nu: (jnp.minimum(j, nu[0] - 1), 0)
    exp = lambda j, be, nu: (be[jnp.minimum(j, nu[0] - 1)], 0, 0)
    grid_spec = pltpu.PrefetchScalarGridSpec(
        num_scalar_prefetch=2,
        grid=(n_blocks,),
        in_specs=[pl.BlockSpec((bm * SUBLANES, LANES), blk),
                  pl.BlockSpec((1, d, 2 * d), exp),
                  pl.BlockSpec((1, 1, 2 * d), exp),
                  pl.BlockSpec((1, d, d), exp),
                  pl.BlockSpec((1, 1, d), exp)],
        out_specs=pl.BlockSpec((bm * SUBLANES, LANES), blk),
        scratch_shapes=[pltpu.VMEM((d, 2 * d), BF16), pltpu.VMEM((d, d), BF16)],
    )
    return pl.pallas_call(
        _expert_kernel,
        grid_spec=grid_spec,
        out_shape=jax.ShapeDtypeStruct((n_rows * SUBLANES, LANES), F32),
        compiler_params=_cparams(("arbitrary",), VMEM_LIMIT),
        name="moe_experts",
    )(block_expert, n_used, h_sorted, wgu, bgu.reshape(e, 1, 2 * d), wdn, bdn.reshape(e, 1, d))


def _combine_kernel(dest_ref, dest_next_ref, wt_ref, x1_ref, g2_ref, lg_ref, lbias_ref, o_hbm, y_ref, buf_ref, sems):
    t = pl.program_id(0)
    nt = pl.num_programs(0)
    tt = x1_ref.shape[0]
    slot = t % 2

    def issue_row(d_ref, sl, i):
        for kk in range(TOP_K):
            d = d_ref[0, 0, i * TOP_K + kk]
            pltpu.make_async_copy(_row_tile(o_hbm, d), _row_tile(buf_ref, (sl * TOP_K + kk) * tt + i),
                                  sems.at[sl]).start(priority=kk % DMA_QUEUES)

    def drain(sl):
        for kk in range(TOP_K):
            pltpu.make_async_copy(_row_tiles(o_hbm, 0, tt), _row_tiles(buf_ref, (sl * TOP_K + kk) * tt, tt),
                                  sems.at[sl]).wait()

    @pl.when(t == 0)
    def _():
        def first(i, _):
            issue_row(dest_ref, 0, i)
            return 0

        lax.fori_loop(0, tt, first, 0, unroll=ISSUE_UNROLL)

    drain(slot)

    def fused(g, _):
        r0 = pl.multiple_of(g * COMBINE_ROWS, COMBINE_ROWS)
        for i in range(COMBINE_ROWS):
            issue_row(dest_next_ref, 1 - slot, r0 + i)
        rows = pl.ds(r0, COMBINE_ROWS)
        ff = None
        for kk in range(TOP_K):
            base = pl.multiple_of(((slot * TOP_K + kk) * tt + r0) * SUBLANES, COMBINE_ROWS * SUBLANES)
            term = wt_ref[rows, kk:kk + 1] * _from_row_tiles(buf_ref, base, COMBINE_ROWS)
            ff = term if ff is None else ff + term
        y_ref[rows, :] = _layer_norm(DEEPNORM_ALPHA * x1_ref[rows, :] + g2_ref[0] * ff, lg_ref[...], lbias_ref[...])
        return 0

    lax.fori_loop(0, tt // COMBINE_ROWS, fused, 0)

    @pl.when(t == nt - 1)
    def _():
        drain(1 - slot)


def _combine(dest_flat, wt, x1, g2, lg, lbias, o_sorted, tt):
    n, d = x1.shape
    b = g2.shape[0]
    nt = n // tt
    tiles_per_b = (n // b) // tt
    const = lambda t: (0, 0)
    dest_spec = lambda im: pl.BlockSpec((1, 1, tt * TOP_K), im, memory_space=pltpu.SMEM)
    return pl.pallas_call(
        _combine_kernel,
        grid=(nt,),
        in_specs=[dest_spec(lambda t: (t, 0, 0)),
                  dest_spec(lambda t: (jnp.minimum(t + 1, nt - 1), 0, 0)),
                  pl.BlockSpec((tt, TOP_K), lambda t: (t, 0)),
                  pl.BlockSpec((tt, d), lambda t: (t, 0)),
                  pl.BlockSpec((1, 1, d), lambda t: (t // tiles_per_b, 0, 0)),
                  pl.BlockSpec((1, d), const),
                  pl.BlockSpec((1, d), const),
                  pl.BlockSpec(memory_space=pl.ANY)],
        out_specs=pl.BlockSpec((tt, d), lambda t: (t, 0)),
        out_shape=jax.ShapeDtypeStruct((n, d), F32),
        scratch_shapes=[pltpu.VMEM((2 * TOP_K * tt * SUBLANES, LANES), F32), pltpu.SemaphoreType.DMA((2,))],
        compiler_params=_cparams(("arbitrary",), VMEM_LIMIT),
        name="moe_combine",
    )(dest_flat, dest_flat, wt, x1, g2, lg, lbias, o_sorted)


def kernel(x, c, ctx, c_ctx, w_ada, b_ada, w_in, lb_raw, hg_norm_g, w_four_out, w_hg_out, w_o, ln1_g, ln1_b,
           w_router, b_router, w_gate_up, b_gate_up, w_down, b_down, ln2_g, ln2_b):
    b, s, d = x.shape
    lc = ctx.shape[1]
    n = b * s
    f_w = F_GROUPS * F_GROUP_DIM
    hk = HG_HEADS * HG_D
    assert w_ada.shape[0] == DEPTH and s % GRID_W == 0 and s % (HG_TILE * HG_GROUP) == 0 and s % 512 == 0 and lc % HG_TILE == 0

    pad = (-(b + 1)) % 8
    c_rows = jnp.concatenate([c, c_ctx[None, :], jnp.zeros((pad, d), F32)], axis=0)
    mod = _ada(c_rows, w_ada[0], b_ada[0]).reshape(b + 1 + pad, N_MOD, 1, d)
    shift1, scale1, gate1, shift2, scale2, gate2 = (mod[:b, i] for i in range(N_MOD))
    cshift1 = jnp.broadcast_to(mod[b, 0], (b, 1, d))
    cscale1 = jnp.broadcast_to(mod[b, 1], (b, 1, d))

    lower_bounds = jnp.cumsum(jax.nn.softmax(lb_raw.astype(F32), axis=0), axis=0)[0]

    w_in_b = w_in[0].astype(BF16)
    o_f, o_q, o_z, o_v, o_g, o_gt = f_w, f_w + hk, f_w + 3 * hk, f_w + 4 * hk, f_w + 5 * hk, f_w + 5 * hk + 2 * d
    lb_row = lower_bounds.reshape(1, 2 * hk)
    pf, pq, pk, phi, plo, pv, pg, gates = _inproj(
        x, scale1, shift1, lb_row, w_in_b,
        [(0, f_w, "cast"), (o_f, hk, "silu"), (o_q, 2 * hk, "forget"), (o_z, hk, "cast"), (o_v, hk, "cast"),
         (o_g, 2 * d, "cast")], tm=512)
    w_ctx = w_in_b[:, o_q:o_v]
    ck, chi, clo, cv = _inproj(ctx, cscale1, cshift1, lb_row, w_ctx, [(0, 2 * hk, "forget"), (2 * hk, hk, "cast")], tm=lc)

    y_h = _hgrn(pq, pk, phi, plo, pv, pg, ck, chi, clo, cv, hg_norm_g[0].reshape(1, hk))
    y_f = _fourier(pf)

    wr = jnp.zeros((d, LANES), BF16).at[:, :N_EXPERTS].set(w_router[0].astype(BF16))
    br = jnp.zeros((1, LANES), F32).at[0, :N_EXPERTS].set(b_router[0])
    x1, idx, wt = _merge(y_f, y_h, gates, x, gate1, scale2, shift2,
                         w_four_out[0].astype(BF16), w_hg_out[0].astype(BF16), w_o[0].astype(BF16),
                         ln1_g[0].reshape(1, d), ln1_b[0].reshape(1, d), wr, br, tm=512)

    idx2 = idx.reshape(n, TOP_K)
    dest, counts, starts = _route(idx2, MOE_ROWS, 512)
    n_blocks = -(-(n * TOP_K) // MOE_ROWS) + N_EXPERTS
    n_rows = n_blocks * MOE_ROWS
    cnt = counts[0, :N_EXPERTS]
    st = starts[0, :N_EXPERTS]
    pad_ends = st + (cnt + MOE_ROWS - 1) // MOE_ROWS * MOE_ROWS
    block_first_row = jnp.arange(n_blocks, dtype=jnp.int32) * MOE_ROWS
    block_expert = jnp.minimum(jnp.sum((pad_ends[None, :] <= block_first_row[:, None]).astype(jnp.int32), axis=1),
                               N_EXPERTS - 1)
    n_used = (pad_ends[-1] // MOE_ROWS).astype(jnp.int32).reshape(1)

    tt = 256
    dest_flat = dest.reshape(n // tt, 1, tt * TOP_K)
    x1_2d = x1.reshape(n, d)
    h_sorted = _dispatch(cnt, st, dest_flat, x1_2d, scale2, shift2, n_rows, tt)
    o_sorted = _experts(h_sorted, block_expert, n_used, w_gate_up[0], b_gate_up[0], w_down[0], b_down[0])
    out = _combine(dest_flat, wt.reshape(n, TOP_K), x1_2d, gate2, ln2_g[0].reshape(1, d), ln2_b[0].reshape(1, d),
                   o_sorted, tt)
    return out.reshape(b, s, d)
```

```python
import functools
import math

import jax
import jax.numpy as jnp
import numpy as np
from jax import lax
from jax.experimental import pallas as pl
from jax.experimental.pallas import tpu as pltpu

F32 = jnp.float32
BF16 = jnp.bfloat16

GRID_W = 64
N_MOD = 6
F_GROUPS = 4
F_GROUP_DIM = 128
HG_HEADS = 4
HG_D = 128
CHUNK = 64
N_EXPERTS = 32
TOP_K = 4
SWIGLU_LIMIT = 7.0
SWIGLU_ALPHA = 1.702
LN_EPS = 1e-5
RMS_EPS = 1e-6
DEPTH = 1
DEEPNORM_ALPHA = (2.0 * DEPTH) ** 0.25

LANES = 128
SUBLANES = 8
MOE_ROWS = 512
COMBINE_ROWS = 128
ISSUE_UNROLL = 4
DMA_QUEUES = 2
VMEM_LIMIT = 56 * 1024 * 1024


def _cparams(sem, vmem=None):
    return pltpu.CompilerParams(dimension_semantics=sem, vmem_limit_bytes=vmem)


def _dot(a, b):
    return jnp.dot(a, b, preferred_element_type=F32)


def _sigmoid(x):
    return 0.5 * jnp.tanh(0.5 * x) + 0.5


def _ada_kernel(c_ref, w_ref, b_ref, o_ref):
    c = c_ref[...]
    a = c * _sigmoid(c)
    o_ref[...] = jnp.dot(a, w_ref[...], preferred_element_type=F32,
                         precision=lax.Precision.HIGHEST) + b_ref[...]


def _ada(c_rows, w, b):
    r, d = c_rows.shape
    n = w.shape[1]
    tn = 512
    return pl.pallas_call(
        _ada_kernel,
        grid=(n // tn,),
        in_specs=[pl.BlockSpec((r, d), lambda j: (0, 0)),
                  pl.BlockSpec((d, tn), lambda j: (0, j)),
                  pl.BlockSpec((1, tn), lambda j: (0, j))],
        out_specs=pl.BlockSpec((r, tn), lambda j: (0, j)),
        out_shape=jax.ShapeDtypeStruct((r, n), F32),
        compiler_params=_cparams(("arbitrary",)),
        name="ada_mod",
    )(c_rows, w, b.reshape(1, n))


def _inproj_kernel(x_ref, sc_ref, sh_ref, lb_ref, w_ref, *o_refs, plan):
    u = (x_ref[0] * (1.0 + sc_ref[0]) + sh_ref[0]).astype(BF16)
    refs = iter(o_refs)
    for kind, chunks in plan:
        if kind == "forget":
            k_ref, hi_ref, lo_ref = next(refs), next(refs), next(refs)
            for (w0, o0, n) in chunks:
                lb = lb_ref[:, o0:o0 + n]
                f = lb + (1.0 - lb) * _sigmoid(_dot(u, w_ref[:, w0:w0 + n]))
                lf = jnp.log(f)
                hi = lf.astype(BF16)
                k_ref[0, :, o0:o0 + n] = (1.0 - f).astype(BF16)
                hi_ref[0, :, o0:o0 + n] = hi
                lo_ref[0, :, o0:o0 + n] = (lf - hi.astype(F32)).astype(BF16)
        else:
            o_ref = next(refs)
            for (w0, o0, n) in chunks:
                p = _dot(u, w_ref[:, w0:w0 + n])
                if kind == "silu":
                    p = p * _sigmoid(p)
                o_ref[0, :, o0:o0 + n] = p.astype(o_ref.dtype)


def _inproj(x, scale, shift, lb_row, w_bf16, outs, tm):
    b, s, d = x.shape
    n_w = w_bf16.shape[1]
    plan, widths = [], []
    for (c0, width, kind) in outs:
        step = min(width, 512)
        plan.append((kind, tuple((c0 + o, o, step) for o in range(0, width, step))))
        widths += [width] * (3 if kind == "forget" else 1)
    kern = functools.partial(_inproj_kernel, plan=tuple(plan))
    return pl.pallas_call(
        kern,
        grid=(b, s // tm),
        in_specs=[pl.BlockSpec((1, tm, d), lambda i, j: (i, j, 0)),
                  pl.BlockSpec((1, 1, d), lambda i, j: (i, 0, 0)),
                  pl.BlockSpec((1, 1, d), lambda i, j: (i, 0, 0)),
                  pl.BlockSpec(lb_row.shape, lambda i, j: (0, 0)),
                  pl.BlockSpec((d, n_w), lambda i, j: (0, 0), pipeline_mode=pl.Buffered(1))],
        out_specs=[pl.BlockSpec((1, tm, width), lambda i, j: (i, j, 0)) for width in widths],
        out_shape=[jax.ShapeDtypeStruct((b, s, width), BF16) for width in widths],
        compiler_params=_cparams(("parallel", "parallel"), VMEM_LIMIT),
        name="in_proj",
    )(x, scale, shift, lb_row, w_bf16)


HG_TILE = 256
HG_GROUP = 4


def _gla_group(jobs, states):
    states = list(states)
    nj = len(jobs)
    r, dk = jobs[0][1].shape
    dv = jobs[0][4].shape[1]
    nc = r // CHUNK
    zero = jnp.zeros((CHUNK, dk), BF16)
    contract_last = (((1,), (1,)), ((), ()))
    contract_rows = (((0,), (0,)), ((), ()))

    cums = []
    for (_, hi, lo, _, _, _, tri, _, _) in jobs:
        both = _dot(tri, jnp.concatenate([hi, lo], axis=1))
        cums.append(both[:, :dk] + both[:, dk:])

    k_inv_b, decays, k_end_blk, q_dec = [], [], [], []
    for cum, (_, _, _, k, _, q, _, _, forward) in zip(cums, jobs):
        k_inv = k.astype(F32) * jnp.exp(-cum)
        k_inv_b.append(k_inv.astype(BF16))
        tot_row = CHUNK - 1 if forward else 0
        dec_j, rows_j = [], []
        for ci in range(nc):
            dec = jnp.exp(cum[ci * CHUNK + tot_row:ci * CHUNK + tot_row + 1, :])
            k_end = (k_inv[ci * CHUNK:(ci + 1) * CHUNK] * dec).astype(BF16)
            dec_j.append(dec)
            rows_j.append(jnp.concatenate([k_end if cj == ci else zero for cj in range(nc)], axis=1))
        decays.append(dec_j)
        k_end_blk.append(jnp.concatenate(rows_j, axis=0))
        q_dec.append(None if q is None else (q.astype(F32) * jnp.exp(cum)).astype(BF16))

    upd_all = [lax.dot_general(job[4], blk, contract_rows, preferred_element_type=F32)
               for job, blk in zip(jobs, k_end_blk)]
    scores = [None if qd is None else lax.dot_general(qd, kb, contract_last, preferred_element_type=F32)
              for qd, kb in zip(q_dec, k_inv_b)]

    s_all = []
    for ji in range(nj):
        chain, forward = jobs[ji][0], jobs[ji][8]
        st = states[chain]
        entering = [None] * nc
        for ci in (range(nc) if forward else reversed(range(nc))):
            entering[ci] = st
            st = st * decays[ji][ci] + upd_all[ji][:, ci * dk:(ci + 1) * dk]
        states[chain] = st
        s_all.append(None if q_dec[ji] is None
                     else [jnp.concatenate([entering[ci].astype(BF16), entering[ci + 1].astype(BF16)], axis=0)
                           for ci in range(0, nc, 2)])

    outs = []
    for ji in range(nj):
        if q_dec[ji] is None:
            outs.append(None)
            continue
        sc = jnp.where(jobs[ji][7] > 0, scores[ji], 0.0).astype(BF16)
        intra = _dot(sc, jobs[ji][4])
        inter = []
        for pi, s_pair in enumerate(s_all[ji]):
            rows = slice(2 * pi * CHUNK, (2 * pi + 2) * CHUNK)
            both = lax.dot_general(q_dec[ji][rows], s_pair, contract_last, preferred_element_type=F32)
            inter += [both[:CHUNK, :dv], both[CHUNK:, dv:]]
        outs.append(intra + jnp.concatenate(inter, axis=0))
    return outs, states


def _hgrn_kernel(q_ref, kf_ref, kb_ref, hif_ref, hib_ref, lof_ref, lob_ref, v_ref, g_ref,
                 ckf_ref, ckb_ref, chif_ref, chib_ref, clof_ref, clob_ref, cv_ref, ng_ref, trif_ref, trib_ref,
                 keepf_ref, keepb_ref, y_ref, of_ref, ob_ref, *, seq, ctx_len):
    r = HG_TILE
    n_lat = seq // r
    n_ctx = ctx_len // r
    group = HG_GROUP

    def ctx_jobs():
        jobs = []
        for i in range(n_ctx):
            sl_f = pl.ds(i * r, r)
            sl_b = pl.ds((n_ctx - 1 - i) * r, r)
            jobs.append((0, chif_ref[0, sl_f, :], clof_ref[0, sl_f, :], ckf_ref[0, sl_f, :], cv_ref[0, sl_f, :], None,
                         trif_ref[...], None, True))
            jobs.append((1, chib_ref[0, sl_b, :], clob_ref[0, sl_b, :], ckb_ref[0, sl_b, :], cv_ref[0, sl_b, :], None,
                         trib_ref[...], None, False))
        return jobs

    def lat_step(i, carry, lead_jobs):
        jobs, slices = list(lead_jobs), [None] * len(lead_jobs)
        for t in range(group):
            sl_f = pl.ds(pl.multiple_of((i * group + t) * r, r), r)
            sl_b = pl.ds(pl.multiple_of((n_lat - 1 - (i * group + t)) * r, r), r)
            jobs.append((0, hif_ref[0, sl_f, :], lof_ref[0, sl_f, :], kf_ref[0, sl_f, :], v_ref[0, sl_f, :],
                         q_ref[0, sl_f, :], trif_ref[...], keepf_ref[...], True))
            jobs.append((1, hib_ref[0, sl_b, :], lob_ref[0, sl_b, :], kb_ref[0, sl_b, :], v_ref[0, sl_b, :],
                         q_ref[0, sl_b, :], trib_ref[...], keepb_ref[...], False))
            slices += [(of_ref, sl_f), (ob_ref, sl_b)]
        outs, states = _gla_group(jobs, carry)
        for dst, o in zip(slices, outs):
            if dst is not None:
                dst[0][dst[1], :] = o
        return tuple(states)

    zero = jnp.zeros((HG_D, HG_D), F32)
    carry = lat_step(0, (zero, zero), ctx_jobs())
    lax.fori_loop(1, n_lat // group, lambda i, c: lat_step(i, c, []), carry)

    ng = ng_ref[...]

    def out_body(i, _):
        sl = pl.ds(pl.multiple_of(i * r, r), r)
        o = of_ref[sl, :] + ob_ref[sl, :]
        o = o * lax.rsqrt(jnp.mean(o * o, axis=-1, keepdims=True) + RMS_EPS) * ng
        g = g_ref[0, sl, :].astype(F32)
        y_ref[0, sl, :] = (o * (g * _sigmoid(g))).astype(y_ref.dtype)
        return 0

    lax.fori_loop(0, n_lat, out_body, 0)


def _chunk_tri(r, forward, dtype):
    i = np.arange(r)
    same = (i[:, None] // CHUNK) == (i[None, :] // CHUNK)
    order = (i[:, None] >= i[None, :]) if forward else (i[:, None] <= i[None, :])
    return jnp.asarray(same & order, dtype)


def _hgrn(q, k, hi, lo, v, g, ck, chi, clo, cv, ng):
    b, s, _ = q.shape
    lc = ck.shape[1]
    h = HG_HEADS
    d = HG_D
    r = HG_TILE
    kern = functools.partial(_hgrn_kernel, seq=s, ctx_len=lc)
    fwd = lambda i, j: (i, 0, j)
    bwd = lambda i, j: (i, 0, j + h)
    const = lambda i, j: (0, 0)
    lat = lambda im: pl.BlockSpec((1, s, d), im)
    cx = lambda im: pl.BlockSpec((1, lc, d), im)
    return pl.pallas_call(
        kern,
        grid=(b, h),
        in_specs=[lat(fwd), lat(fwd), lat(bwd), lat(fwd), lat(bwd), lat(fwd), lat(bwd), lat(fwd), lat(fwd),
                  cx(fwd), cx(bwd), cx(fwd), cx(bwd), cx(fwd), cx(bwd), cx(fwd),
                  pl.BlockSpec((1, d), lambda i, j: (0, j)),
                  pl.BlockSpec((r, r), const),
                  pl.BlockSpec((r, r), const),
                  pl.BlockSpec((r, r), const),
                  pl.BlockSpec((r, r), const)],
        out_specs=pl.BlockSpec((1, s, d), fwd),
        out_shape=jax.ShapeDtypeStruct((b, s, h * d), BF16),
        scratch_shapes=[pltpu.VMEM((s, d), F32), pltpu.VMEM((s, d), F32)],
        compiler_params=_cparams(("parallel", "parallel"), VMEM_LIMIT),
        name="hgrn2",
    )(q, k, k, hi, hi, lo, lo, v, g, ck, ck, chi, chi, clo, clo, cv, ng,
      _chunk_tri(r, True, BF16), _chunk_tri(r, False, BF16), _chunk_tri(r, True, F32), _chunk_tri(r, False, F32))


def _dft_tables(seq):
    rows = seq // GRID_W
    n = F_GROUP_DIM
    kc = np.outer(np.arange(n), np.arange(n)) % n
    ang = 2.0 * np.pi * kc / n
    norm = 1.0 / math.sqrt(rows * GRID_W * n)
    ch = np.concatenate([np.cos(ang), -np.sin(ang)], axis=1) * norm
    t = np.arange(seq)
    r, w = t // GRID_W, t % GRID_W
    m = (np.outer(r, r) * GRID_W + np.outer(w, w) * rows) % (rows * GRID_W)
    ang_t = 2.0 * np.pi * m / (rows * GRID_W)
    tok = np.concatenate([np.cos(ang_t), np.sin(ang_t)], axis=1)
    return jnp.asarray(ch, BF16), jnp.asarray(tok, BF16)


def _fourier_kernel(p_ref, ch_ref, tok_ref, y_ref, xs_ref, *, seq):
    for gi in range(F_GROUPS):
        cols = slice(gi * F_GROUP_DIM, (gi + 1) * F_GROUP_DIM)
        x1 = _dot(p_ref[0, :, cols], ch_ref[...]).astype(BF16)
        xs_ref[0:seq, cols] = x1[:, :F_GROUP_DIM]
        xs_ref[seq:2 * seq, cols] = x1[:, F_GROUP_DIM:]
    y_ref[0] = _dot(tok_ref[...], xs_ref[...]).astype(y_ref.dtype)


def _fourier(pf):
    b, s, wdt = pf.shape
    ch, tok = _dft_tables(s)
    kern = functools.partial(_fourier_kernel, seq=s)
    return pl.pallas_call(
        kern,
        grid=(b,),
        in_specs=[pl.BlockSpec((1, s, wdt), lambda i: (i, 0, 0)),
                  pl.BlockSpec(ch.shape, lambda i: (0, 0)),
                  pl.BlockSpec(tok.shape, lambda i: (0, 0), pipeline_mode=pl.Buffered(1))],
        out_specs=pl.BlockSpec((1, s, wdt), lambda i: (i, 0, 0)),
        out_shape=jax.ShapeDtypeStruct((b, s, wdt), BF16),
        scratch_shapes=[pltpu.VMEM((2 * s, wdt), BF16)],
        compiler_params=_cparams(("parallel",), VMEM_LIMIT),
        name="fourier",
    )(pf, ch, tok)


def _layer_norm(t, g, b):
    mu = jnp.mean(t, axis=-1, keepdims=True)
    tc = t - mu
    var = jnp.mean(tc * tc, axis=-1, keepdims=True)
    return tc * lax.rsqrt(var + LN_EPS) * g + b


def _merge_kernel(yf_ref, yh_ref, gt_ref, x_ref, g1_ref, sc2_ref, sh2_ref, wfo_ref, who_ref, wo_ref,
                  lg_ref, lbias_ref, wr_ref, br_ref, x1_ref, idx_ref, wt_ref):
    d = x_ref.shape[-1]
    gf = _sigmoid(gt_ref[0, :, :d].astype(F32))
    gh = _sigmoid(gt_ref[0, :, d:].astype(F32))
    m = gf * _dot(yf_ref[0], wfo_ref[...]) + gh * _dot(yh_ref[0], who_ref[...])
    mix = _dot(m.astype(BF16), wo_ref[...])
    x1 = _layer_norm(DEEPNORM_ALPHA * x_ref[0] + g1_ref[0] * mix, lg_ref[...], lbias_ref[...])
    x1_ref[0] = x1
    u2 = x1 * (1.0 + sc2_ref[0]) + sh2_ref[0]
    logits = _dot(u2.astype(BF16), wr_ref[...]) + br_ref[...]
    lane = lax.broadcasted_iota(jnp.int32, logits.shape, 1)
    work = jnp.where(lane < N_EXPERTS, logits, -jnp.inf)
    vals, idxs = [], []
    for _ in range(TOP_K):
        mx = jnp.max(work, axis=-1, keepdims=True)
        sel = jnp.argmax(work, axis=-1, keepdims=True).astype(jnp.int32)
        vals.append(mx)
        idxs.append(sel)
        work = jnp.where(lane == sel, -jnp.inf, work)
    exps = [jnp.exp(vv - vals[0]) for vv in vals]
    inv = 1.0 / (exps[0] + exps[1] + exps[2] + exps[3])
    for kk in range(TOP_K):
        idx_ref[0, :, kk:kk + 1] = idxs[kk]
        wt_ref[0, :, kk:kk + 1] = exps[kk] * inv


def _merge(yf, yh, gates, x, g1, sc2, sh2, wfo, who, wo, lg, lbias, wr, br, tm):
    b, s, d = x.shape
    tile = lambda i, j: (i, j, 0)
    per_b = lambda i, j: (i, 0, 0)
    const = lambda i, j: (0, 0)
    return pl.pallas_call(
        _merge_kernel,
        grid=(b, s // tm),
        in_specs=[pl.BlockSpec((1, tm, yf.shape[-1]), tile),
                  pl.BlockSpec((1, tm, yh.shape[-1]), tile),
                  pl.BlockSpec((1, tm, gates.shape[-1]), tile),
                  pl.BlockSpec((1, tm, d), tile),
                  pl.BlockSpec((1, 1, d), per_b),
                  pl.BlockSpec((1, 1, d), per_b),
                  pl.BlockSpec((1, 1, d), per_b),
                  pl.BlockSpec(wfo.shape, const),
                  pl.BlockSpec(who.shape, const),
                  pl.BlockSpec(wo.shape, const),
                  pl.BlockSpec((1, d), const),
                  pl.BlockSpec((1, d), const),
                  pl.BlockSpec(wr.shape, const),
                  pl.BlockSpec(br.shape, const)],
        out_specs=[pl.BlockSpec((1, tm, d), tile),
                   pl.BlockSpec((1, tm, TOP_K), tile),
                   pl.BlockSpec((1, tm, TOP_K), tile)],
        out_shape=[jax.ShapeDtypeStruct((b, s, d), F32),
                   jax.ShapeDtypeStruct((b, s, TOP_K), jnp.int32),
                   jax.ShapeDtypeStruct((b, s, TOP_K), F32)],
        compiler_params=_cparams(("parallel", "parallel"), VMEM_LIMIT),
        name="merge_ln_router",
    )(yf, yh, gates, x, g1, sc2, sh2, wfo, who, wo, lg, lbias, wr, br)


def _lane_cumsum_exclusive(x):
    lane = lax.broadcasted_iota(jnp.int32, x.shape, 1)
    inc = x
    sh = 1
    while sh < LANES:
        inc = inc + jnp.where(lane >= sh, pltpu.roll(inc, sh, 1), 0)
        sh *= 2
    return inc - x


def _route_kernel(idx_ref, strict_ref, dest_ref, cnt_ref, start_ref, run_ref, *, block_rows):
    phase = pl.program_id(0)
    t = pl.program_id(1)
    tt = idx_ref.shape[0]
    lane = lax.broadcasted_iota(jnp.int32, (tt, LANES), 1)
    onehots = [jnp.where(lane == idx_ref[:, kk:kk + 1], 1.0, 0.0) for kk in range(TOP_K)]
    colsums = [jnp.sum(oh, axis=0, keepdims=True) for oh in onehots]

    @pl.when(jnp.logical_and(phase == 0, t == 0))
    def _():
        run_ref[...] = jnp.zeros_like(run_ref)

    @pl.when(phase == 0)
    def _():
        run_ref[...] += colsums[0] + colsums[1] + colsums[2] + colsums[3]

    @pl.when(jnp.logical_and(phase == 1, t == 0))
    def _():
        cnt = run_ref[...].astype(jnp.int32)
        padded = (cnt + (block_rows - 1)) // block_rows * block_rows
        cnt_ref[...] = cnt
        start_ref[...] = _lane_cumsum_exclusive(padded)
        run_ref[...] = jnp.zeros_like(run_ref)

    @pl.when(phase == 1)
    def _():
        prefixes = _dot(strict_ref[...], jnp.concatenate([oh.astype(BF16) for oh in onehots], axis=1))
        base = run_ref[...] + start_ref[...].astype(F32)
        for kk in range(TOP_K):
            prefix = prefixes[:, kk * LANES:(kk + 1) * LANES]
            rank = jnp.sum(onehots[kk] * (prefix + base), axis=-1, keepdims=True)
            dest_ref[:, kk:kk + 1] = rank.astype(jnp.int32)
            base = base + colsums[kk]
        run_ref[...] = base - start_ref[...].astype(F32)


def _route(idx, block_rows, tt):
    n = idx.shape[0]
    kern = functools.partial(_route_kernel, block_rows=block_rows)
    strict = jnp.asarray(np.tril(np.ones((tt, tt)), -1), BF16)
    return pl.pallas_call(
        kern,
        grid=(2, n // tt),
        in_specs=[pl.BlockSpec((tt, TOP_K), lambda p, t: (t, 0)),
                  pl.BlockSpec((tt, tt), lambda p, t: (0, 0))],
        out_specs=[pl.BlockSpec((tt, TOP_K), lambda p, t: (t * p, 0)),
                   pl.BlockSpec((1, LANES), lambda p, t: (0, 0)),
                   pl.BlockSpec((1, LANES), lambda p, t: (0, 0))],
        out_shape=[jax.ShapeDtypeStruct((n, TOP_K), jnp.int32),
                   jax.ShapeDtypeStruct((1, LANES), jnp.int32),
                   jax.ShapeDtypeStruct((1, LANES), jnp.int32)],
        scratch_shapes=[pltpu.VMEM((1, LANES), F32)],
        compiler_params=_cparams(("arbitrary", "arbitrary")),
        name="route_ranks",
    )(idx, strict)


def _to_row_tiles(ref, base, val):
    rows = val.shape[0]
    for s in range(SUBLANES):
        ref[pl.ds(base + s, rows, stride=SUBLANES), :] = val[:, s * LANES:(s + 1) * LANES]


def _from_row_tiles(ref, base, rows):
    return jnp.concatenate([ref[pl.ds(base + s, rows, stride=SUBLANES), :] for s in range(SUBLANES)], axis=1)


def _row_tile(ref, r):
    return ref.at[pl.ds(pl.multiple_of(r * SUBLANES, SUBLANES), SUBLANES), :]


def _row_tiles(ref, r, n):
    return ref.at[pl.ds(pl.multiple_of(r * SUBLANES, SUBLANES), n * SUBLANES), :]


def _dispatch_kernel(cnt_ref, start_ref, dest_ref, x1_ref, sc_ref, sh_ref, h_ref, u_ref, zero_ref, sems, *, block_rows):
    t = pl.program_id(0)
    nt = pl.num_programs(0)
    tt = x1_ref.shape[0]
    slot = t % 2
    ubase = pl.multiple_of(slot * (tt * SUBLANES), tt * SUBLANES)
    _to_row_tiles(u_ref, ubase, x1_ref[...] * (1.0 + sc_ref[0]) + sh_ref[0])

    def issue(i, _):
        for kk in range(TOP_K):
            d = dest_ref[0, 0, i * TOP_K + kk]
            pltpu.make_async_copy(_row_tile(u_ref, slot * tt + i), _row_tile(h_ref, d),
                                  sems.at[slot]).start(priority=kk % DMA_QUEUES)
        return 0

    lax.fori_loop(0, tt, issue, 0, unroll=ISSUE_UNROLL)

    def drain(sl):
        for _ in range(TOP_K):
            pltpu.make_async_copy(_row_tiles(u_ref, sl * tt, tt), _row_tiles(h_ref, 0, tt), sems.at[sl]).wait()

    @pl.when(t > 0)
    def _():
        drain(1 - slot)

    @pl.when(t == nt - 1)
    def _():
        drain(slot)
        zero_ref[...] = jnp.zeros_like(zero_ref)

        def pad_expert(e, _):
            cnt = cnt_ref[e]
            rem = (block_rows - (cnt & (block_rows - 1))) & (block_rows - 1)
            pos = start_ref[e] + cnt
            size = block_rows // 2
            while size >= 1:
                take = rem & size

                @pl.when(take != 0)
                def _(pos=pos, size=size):
                    cp = pltpu.make_async_copy(_row_tiles(zero_ref, 0, size), _row_tiles(h_ref, pos, size), sems.at[2])
                    cp.start()
                    cp.wait()

                pos = pos + take
                size //= 2
            return 0

        lax.fori_loop(0, N_EXPERTS, pad_expert, 0)


def _dispatch(counts, starts, dest_flat, x1, sc2, sh2, n_rows, tt):
    n, d = x1.shape
    b = sc2.shape[0]
    tiles_per_b = (n // b) // tt
    per_b = lambda t, c, s: (t // tiles_per_b, 0, 0)
    grid_spec = pltpu.PrefetchScalarGridSpec(
        num_scalar_prefetch=2,
        grid=(n // tt,),
        in_specs=[pl.BlockSpec((1, 1, tt * TOP_K), lambda t, c, s: (t, 0, 0), memory_space=pltpu.SMEM),
                  pl.BlockSpec((tt, d), lambda t, c, s: (t, 0)),
                  pl.BlockSpec((1, 1, d), per_b),
                  pl.BlockSpec((1, 1, d), per_b)],
        out_specs=pl.BlockSpec(memory_space=pl.ANY),
        scratch_shapes=[pltpu.VMEM((2 * tt * SUBLANES, LANES), F32), pltpu.VMEM((MOE_ROWS // 2 * SUBLANES, LANES), F32),
                        pltpu.SemaphoreType.DMA((3,))],
    )
    return pl.pallas_call(
        functools.partial(_dispatch_kernel, block_rows=MOE_ROWS),
        grid_spec=grid_spec,
        out_shape=jax.ShapeDtypeStruct((n_rows * SUBLANES, LANES), F32),
        compiler_params=_cparams(("arbitrary",)),
        name="moe_dispatch",
    )(counts, starts, dest_flat, x1, sc2, sh2)


def _expert_kernel(be_ref, nu_ref, h_ref, wgu_ref, bgu_ref, wdn_ref, bdn_ref, o_ref, wgu_b_ref, wdn_b_ref):
    j = pl.program_id(0)
    d = wdn_ref.shape[-1]
    bm = h_ref.shape[0] // SUBLANES

    @pl.when(j < nu_ref[0])
    def _():
        @pl.when(jnp.logical_or(j == 0, be_ref[j] != be_ref[jnp.maximum(j - 1, 0)]))
        def _():
            wgu_b_ref[...] = wgu_ref[0].astype(BF16)
            wdn_b_ref[...] = wdn_ref[0].astype(BF16)

        gu = _dot(_from_row_tiles(h_ref, 0, bm).astype(BF16), wgu_b_ref[...]) + bgu_ref[0]
        gate = jnp.minimum(gu[:, :d], SWIGLU_LIMIT)
        up = jnp.clip(gu[:, d:], -SWIGLU_LIMIT, SWIGLU_LIMIT)
        act = (up + 1.0) * gate * _sigmoid(SWIGLU_ALPHA * gate)
        _to_row_tiles(o_ref, 0, _dot(act.astype(BF16), wdn_b_ref[...]) + bdn_ref[0])


def _experts(h_sorted, block_expert, n_used, wgu, bgu, wdn, bdn):
    n_rows = h_sorted.shape[0] // SUBLANES
    e, d, _ = wdn.shape
    bm = MOE_ROWS
    n_blocks = n_rows // bm
    blk = lambda j, be, nu: (jnp.minimum(j, nu[0] - 1), 0)
    exp = lambda j, be, nu: (be[jnp.minimum(j, nu[0] - 1)], 0, 0)
    grid_spec = pltpu.PrefetchScalarGridSpec(
        num_scalar_prefetch=2,
        grid=(n_blocks,),
        in_specs=[pl.BlockSpec((bm * SUBLANES, LANES), blk),
                  pl.BlockSpec((1, d, 2 * d), exp),
                  pl.BlockSpec((1, 1, 2 * d), exp),
                  pl.BlockSpec((1, d, d), exp),
                  pl.BlockSpec((1, 1, d), exp)],
        out_specs=pl.BlockSpec((bm * SUBLANES, LANES), blk),
        scratch_shapes=[pltpu.VMEM((d, 2 * d), BF16), pltpu.VMEM((d, d), BF16)],
    )
    return pl.pallas_call(
        _expert_kernel,
        grid_spec=grid_spec,
        out_shape=jax.ShapeDtypeStruct((n_rows * SUBLANES, LANES), F32),
        compiler_params=_cparams(("arbitrary",), VMEM_LIMIT),
        name="moe_experts",
    )(block_expert, n_used, h_sorted, wgu, bgu.reshape(e, 1, 2 * d), wdn, bdn.reshape(e, 1, d))


def _combine_kernel(dest_ref, dest_next_ref, wt_ref, x1_ref, g2_ref, lg_ref, lbias_ref, o_hbm, y_ref, buf_ref, sems):
    t = pl.program_id(0)
    nt = pl.num_programs(0)
    tt = x1_ref.shape[0]
    slot = t % 2

    def issue_row(d_ref, sl, i):
        for kk in range(TOP_K):
            d = d_ref[0, 0, i * TOP_K + kk]
            pltpu.make_async_copy(_row_tile(o_hbm, d), _row_tile(buf_ref, (sl * TOP_K + kk) * tt + i),
                                  sems.at[sl]).start(priority=kk % DMA_QUEUES)

    def drain(sl):
        for kk in range(TOP_K):
            pltpu.make_async_copy(_row_tiles(o_hbm, 0, tt), _row_tiles(buf_ref, (sl * TOP_K + kk) * tt, tt),
                                  sems.at[sl]).wait()

    @pl.when(t == 0)
    def _():
        def first(i, _):
            issue_row(dest_ref, 0, i)
            return 0

        lax.fori_loop(0, tt, first, 0, unroll=ISSUE_UNROLL)

    drain(slot)

    def fused(g, _):
        r0 = pl.multiple_of(g * COMBINE_ROWS, COMBINE_ROWS)
        for i in range(COMBINE_ROWS):
            issue_row(dest_next_ref, 1 - slot, r0 + i)
        rows = pl.ds(r0, COMBINE_ROWS)
        ff = None
        for kk in range(TOP_K):
            base = pl.multiple_of(((slot * TOP_K + kk) * tt + r0) * SUBLANES, COMBINE_ROWS * SUBLANES)
            term = wt_ref[rows, kk:kk + 1] * _from_row_tiles(buf_ref, base, COMBINE_ROWS)
            ff = term if ff is None else ff + term
        y_ref[rows, :] = _layer_norm(DEEPNORM_ALPHA * x1_ref[rows, :] + g2_ref[0] * ff, lg_ref[...], lbias_ref[...])
        return 0

    lax.fori_loop(0, tt // COMBINE_ROWS, fused, 0)

    @pl.when(t == nt - 1)
    def _():
        drain(1 - slot)


def _combine(dest_flat, wt, x1, g2, lg, lbias, o_sorted, tt):
    n, d = x1.shape
    b = g2.shape[0]
    nt = n // tt
    tiles_per_b = (n // b) // tt
    const = lambda t: (0, 0)
    dest_spec = lambda im: pl.BlockSpec((1, 1, tt * TOP_K), im, memory_space=pltpu.SMEM)
    return pl.pallas_call(
        _combine_kernel,
        grid=(nt,),
        in_specs=[dest_spec(lambda t: (t, 0, 0)),
                  dest_spec(lambda t: (jnp.minimum(t + 1, nt - 1), 0, 0)),
                  pl.BlockSpec((tt, TOP_K), lambda t: (t, 0)),
                  pl.BlockSpec((tt, d), lambda t: (t, 0)),
                  pl.BlockSpec((1, 1, d), lambda t: (t // tiles_per_b, 0, 0)),
                  pl.BlockSpec((1, d), const),
                  pl.BlockSpec((1, d), const),
                  pl.BlockSpec(memory_space=pl.ANY)],
        out_specs=pl.BlockSpec((tt, d), lambda t: (t, 0)),
        out_shape=jax.ShapeDtypeStruct((n, d), F32),
        scratch_shapes=[pltpu.VMEM((2 * TOP_K * tt * SUBLANES, LANES), F32), pltpu.SemaphoreType.DMA((2,))],
        compiler_params=_cparams(("arbitrary",), VMEM_LIMIT),
        name="moe_combine",
    )(dest_flat, dest_flat, wt, x1, g2, lg, lbias, o_sorted)


def kernel(x, c, ctx, c_ctx, w_ada, b_ada, w_in, lb_raw, hg_norm_g, w_four_out, w_hg_out, w_o, ln1_g, ln1_b,
           w_router, b_router, w_gate_up, b_gate_up, w_down, b_down, ln2_g, ln2_b):
    b, s, d = x.shape
    lc = ctx.shape[1]
    n = b * s
    f_w = F_GROUPS * F_GROUP_DIM
    hk = HG_HEADS * HG_D
    assert w_ada.shape[0] == DEPTH and s % GRID_W == 0 and s % (HG_TILE * HG_GROUP) == 0 and s % 512 == 0 and lc % HG_TILE == 0

    pad = (-(b + 1)) % 8
    c_rows = jnp.concatenate([c, c_ctx[None, :], jnp.zeros((pad, d), F32)], axis=0)
    mod = _ada(c_rows, w_ada[0], b_ada[0]).reshape(b + 1 + pad, N_MOD, 1, d)
    shift1, scale1, gate1, shift2, scale2, gate2 = (mod[:b, i] for i in range(N_MOD))
    cshift1 = jnp.broadcast_to(mod[b, 0], (b, 1, d))
    cscale1 = jnp.broadcast_to(mod[b, 1], (b, 1, d))

    lower_bounds = jnp.cumsum(jax.nn.softmax(lb_raw.astype(F32), axis=0), axis=0)[0]

    w_in_b = w_in[0].astype(BF16)
    o_f, o_q, o_z, o_v, o_g, o_gt = f_w, f_w + hk, f_w + 3 * hk, f_w + 4 * hk, f_w + 5 * hk, f_w + 5 * hk + 2 * d
    lb_row = lower_bounds.reshape(1, 2 * hk)
    pf, pq, pk, phi, plo, pv, pg, gates = _inproj(
        x, scale1, shift1, lb_row, w_in_b,
        [(0, f_w, "cast"), (o_f, hk, "silu"), (o_q, 2 * hk, "forget"), (o_z, hk, "cast"), (o_v, hk, "cast"),
         (o_g, 2 * d, "cast")], tm=1024)
    w_ctx = w_in_b[:, o_q:o_v]
    ck, chi, clo, cv = _inproj(ctx, cscale1, cshift1, lb_row, w_ctx, [(0, 2 * hk, "forget"), (2 * hk, hk, "cast")], tm=lc)

    y_h = _hgrn(pq, pk, phi, plo, pv, pg, ck, chi, clo, cv, hg_norm_g[0].reshape(1, hk))
    y_f = _fourier(pf)

    wr = jnp.zeros((d, LANES), BF16).at[:, :N_EXPERTS].set(w_router[0].astype(BF16))
    br = jnp.zeros((1, LANES), F32).at[0, :N_EXPERTS].set(b_router[0])
    x1, idx, wt = _merge(y_f, y_h, gates, x, gate1, scale2, shift2,
                         w_four_out[0].astype(BF16), w_hg_out[0].astype(BF16), w_o[0].astype(BF16),
                         ln1_g[0].reshape(1, d), ln1_b[0].reshape(1, d), wr, br, tm=1024)

    idx2 = idx.reshape(n, TOP_K)
    dest, counts, starts = _route(idx2, MOE_ROWS, 512)
    n_blocks = -(-(n * TOP_K) // MOE_ROWS) + N_EXPERTS
    n_rows = n_blocks * MOE_ROWS
    cnt = counts[0, :N_EXPERTS]
    st = starts[0, :N_EXPERTS]
    pad_ends = st + (cnt + MOE_ROWS - 1) // MOE_ROWS * MOE_ROWS
    block_first_row = jnp.arange(n_blocks, dtype=jnp.int32) * MOE_ROWS
    block_expert = jnp.minimum(jnp.sum((pad_ends[None, :] <= block_first_row[:, None]).astype(jnp.int32), axis=1),
                               N_EXPERTS - 1)
    n_used = (pad_ends[-1] // MOE_ROWS).astype(jnp.int32).reshape(1)

    tt = 256
    dest_flat = dest.reshape(n // tt, 1, tt * TOP_K)
    x1_2d = x1.reshape(n, d)
    h_sorted = _dispatch(cnt, st, dest_flat, x1_2d, scale2, shift2, n_rows, tt)
    o_sorted = _experts(h_sorted, block_expert, n_used, w_gate_up[0], b_gate_up[0], w_down[0], b_down[0])
    out = _combine(dest_flat, wt.reshape(n, TOP_K), x1_2d, gate2, ln2_g[0].reshape(1, d), ln2_b[0].reshape(1, d),
                   o_sorted, tt)
    return out.reshape(b, s, d)
```

```python
import functools
import math

import jax
import jax.numpy as jnp
import numpy as np
from jax import lax
from jax.experimental import pallas as pl
from jax.experimental.pallas import tpu as pltpu

F32 = jnp.float32
BF16 = jnp.bfloat16

GRID_W = 64
N_MOD = 6
F_GROUPS = 4
F_GROUP_DIM = 128
HG_HEADS = 4
HG_D = 128
CHUNK = 64
N_EXPERTS = 32
TOP_K = 4
SWIGLU_LIMIT = 7.0
SWIGLU_ALPHA = 1.702
LN_EPS = 1e-5
RMS_EPS = 1e-6
DEPTH = 1
DEEPNORM_ALPHA = (2.0 * DEPTH) ** 0.25

LANES = 128
SUBLANES = 8
MOE_ROWS = 512
COMBINE_ROWS = 128
ISSUE_UNROLL = 4
DMA_QUEUES = 2
VMEM_LIMIT = 56 * 1024 * 1024


def _cparams(sem, vmem=None):
    return pltpu.CompilerParams(dimension_semantics=sem, vmem_limit_bytes=vmem)


def _dot(a, b):
    return jnp.dot(a, b, preferred_element_type=F32)


def _sigmoid(x):
    return 0.5 * jnp.tanh(0.5 * x) + 0.5


def _ada_kernel(c_ref, w_ref, b_ref, o_ref):
    c = c_ref[...]
    a = c * _sigmoid(c)
    o_ref[...] = jnp.dot(a, w_ref[...], preferred_element_type=F32,
                         precision=lax.Precision.HIGHEST) + b_ref[...]


def _ada(c_rows, w, b):
    r, d = c_rows.shape
    n = w.shape[1]
    tn = 512
    return pl.pallas_call(
        _ada_kernel,
        grid=(n // tn,),
        in_specs=[pl.BlockSpec((r, d), lambda j: (0, 0)),
                  pl.BlockSpec((d, tn), lambda j: (0, j)),
                  pl.BlockSpec((1, tn), lambda j: (0, j))],
        out_specs=pl.BlockSpec((r, tn), lambda j: (0, j)),
        out_shape=jax.ShapeDtypeStruct((r, n), F32),
        compiler_params=_cparams(("arbitrary",)),
        name="ada_mod",
    )(c_rows, w, b.reshape(1, n))


def _inproj_kernel(x_ref, sc_ref, sh_ref, lb_ref, w_ref, *o_refs, plan):
    u = (x_ref[0] * (1.0 + sc_ref[0]) + sh_ref[0]).astype(BF16)
    refs = iter(o_refs)
    for kind, chunks in plan:
        if kind == "forget":
            k_ref, hi_ref, lo_ref = next(refs), next(refs), next(refs)
            for (w0, o0, n) in chunks:
                lb = lb_ref[:, o0:o0 + n]
                f = lb + (1.0 - lb) * _sigmoid(_dot(u, w_ref[:, w0:w0 + n]))
                lf = jnp.log(f)
                hi = lf.astype(BF16)
                k_ref[0, :, o0:o0 + n] = (1.0 - f).astype(BF16)
                hi_ref[0, :, o0:o0 + n] = hi
                lo_ref[0, :, o0:o0 + n] = (lf - hi.astype(F32)).astype(BF16)
        else:
            o_ref = next(refs)
            for (w0, o0, n) in chunks:
                p = _dot(u, w_ref[:, w0:w0 + n])
                if kind == "silu":
                    p = p * _sigmoid(p)
                o_ref[0, :, o0:o0 + n] = p.astype(o_ref.dtype)


def _inproj(x, scale, shift, lb_row, w_bf16, outs, tm):
    b, s, d = x.shape
    n_w = w_bf16.shape[1]
    plan, widths = [], []
    for (c0, width, kind) in outs:
        step = min(width, 512)
        plan.append((kind, tuple((c0 + o, o, step) for o in range(0, width, step))))
        widths += [width] * (3 if kind == "forget" else 1)
    kern = functools.partial(_inproj_kernel, plan=tuple(plan))
    return pl.pallas_call(
        kern,
        grid=(b, s // tm),
        in_specs=[pl.BlockSpec((1, tm, d), lambda i, j: (i, j, 0)),
                  pl.BlockSpec((1, 1, d), lambda i, j: (i, 0, 0)),
                  pl.BlockSpec((1, 1, d), lambda i, j: (i, 0, 0)),
                  pl.BlockSpec(lb_row.shape, lambda i, j: (0, 0)),
                  pl.BlockSpec((d, n_w), lambda i, j: (0, 0), pipeline_mode=pl.Buffered(1))],
        out_specs=[pl.BlockSpec((1, tm, width), lambda i, j: (i, j, 0)) for width in widths],
        out_shape=[jax.ShapeDtypeStruct((b, s, width), BF16) for width in widths],
        compiler_params=_cparams(("parallel", "parallel"), VMEM_LIMIT),
        name="in_proj",
    )(x, scale, shift, lb_row, w_bf16)


HG_TILE = 256
HG_GROUP = 4


def _gla_group(jobs, states):
    states = list(states)
    nj = len(jobs)
    r, dk = jobs[0][1].shape
    dv = jobs[0][4].shape[1]
    nc = r // CHUNK
    zero = jnp.zeros((CHUNK, dk), BF16)
    contract_last = (((1,), (1,)), ((), ()))
    contract_rows = (((0,), (0,)), ((), ()))

    cums = []
    for (_, hi, lo, _, _, _, tri, _, _) in jobs:
        both = _dot(tri, jnp.concatenate([hi, lo], axis=1))
        cums.append(both[:, :dk] + both[:, dk:])

    k_inv_b, decays, k_end_blk, q_dec = [], [], [], []
    for cum, (_, _, _, k, _, q, _, _, forward) in zip(cums, jobs):
        k_inv = k.astype(F32) * jnp.exp(-cum)
        k_inv_b.append(k_inv.astype(BF16))
        tot_row = CHUNK - 1 if forward else 0
        dec_j, rows_j = [], []
        for ci in range(nc):
            dec = jnp.exp(cum[ci * CHUNK + tot_row:ci * CHUNK + tot_row + 1, :])
            k_end = (k_inv[ci * CHUNK:(ci + 1) * CHUNK] * dec).astype(BF16)
            dec_j.append(dec)
            rows_j.append(jnp.concatenate([k_end if cj == ci else zero for cj in range(nc)], axis=1))
        decays.append(dec_j)
        k_end_blk.append(jnp.concatenate(rows_j, axis=0))
        q_dec.append(None if q is None else (q.astype(F32) * jnp.exp(cum)).astype(BF16))

    upd_all = [lax.dot_general(job[4], blk, contract_rows, preferred_element_type=F32)
               for job, blk in zip(jobs, k_end_blk)]
    scores = [None if qd is None else lax.dot_general(qd, kb, contract_last, preferred_element_type=F32)
              for qd, kb in zip(q_dec, k_inv_b)]

    s_all = []
    for ji in range(nj):
        chain, forward = jobs[ji][0], jobs[ji][8]
        st = states[chain]
        entering = [None] * nc
        for ci in (range(nc) if forward else reversed(range(nc))):
            entering[ci] = st
            st = st * decays[ji][ci] + upd_all[ji][:, ci * dk:(ci + 1) * dk]
        states[chain] = st
        s_all.append(None if q_dec[ji] is None
                     else [jnp.concatenate([entering[ci].astype(BF16), entering[ci + 1].astype(BF16)], axis=0)
                           for ci in range(0, nc, 2)])

    outs = []
    for ji in range(nj):
        if q_dec[ji] is None:
            outs.append(None)
            continue
        sc = jnp.where(jobs[ji][7] > 0, scores[ji], 0.0).astype(BF16)
        intra = _dot(sc, jobs[ji][4])
        inter = []
        for pi, s_pair in enumerate(s_all[ji]):
            rows = slice(2 * pi * CHUNK, (2 * pi + 2) * CHUNK)
            both = lax.dot_general(q_dec[ji][rows], s_pair, contract_last, preferred_element_type=F32)
            inter += [both[:CHUNK, :dv], both[CHUNK:, dv:]]
        outs.append(intra + jnp.concatenate(inter, axis=0))
    return outs, states


def _hgrn_kernel(q_ref, kf_ref, kb_ref, hif_ref, hib_ref, lof_ref, lob_ref, v_ref, g_ref,
                 ckf_ref, ckb_ref, chif_ref, chib_ref, clof_ref, clob_ref, cv_ref, ng_ref, trif_ref, trib_ref,
                 keepf_ref, keepb_ref, y_ref, of_ref, ob_ref, *, seq, ctx_len):
    r = HG_TILE
    n_lat = seq // r
    n_ctx = ctx_len // r
    group = HG_GROUP

    def ctx_jobs():
        jobs = []
        for i in range(n_ctx):
            sl_f = pl.ds(i * r, r)
            sl_b = pl.ds((n_ctx - 1 - i) * r, r)
            jobs.append((0, chif_ref[0, sl_f, :], clof_ref[0, sl_f, :], ckf_ref[0, sl_f, :], cv_ref[0, sl_f, :], None,
                         trif_ref[...], None, True))
            jobs.append((1, chib_ref[0, sl_b, :], clob_ref[0, sl_b, :], ckb_ref[0, sl_b, :], cv_ref[0, sl_b, :], None,
                         trib_ref[...], None, False))
        return jobs

    def lat_step(i, carry, lead_jobs):
        jobs, slices = list(lead_jobs), [None] * len(lead_jobs)
        for t in range(group):
            sl_f = pl.ds(pl.multiple_of((i * group + t) * r, r), r)
            sl_b = pl.ds(pl.multiple_of((n_lat - 1 - (i * group + t)) * r, r), r)
            jobs.append((0, hif_ref[0, sl_f, :], lof_ref[0, sl_f, :], kf_ref[0, sl_f, :], v_ref[0, sl_f, :],
                         q_ref[0, sl_f, :], trif_ref[...], keepf_ref[...], True))
            jobs.append((1, hib_ref[0, sl_b, :], lob_ref[0, sl_b, :], kb_ref[0, sl_b, :], v_ref[0, sl_b, :],
                         q_ref[0, sl_b, :], trib_ref[...], keepb_ref[...], False))
            slices += [(of_ref, sl_f), (ob_ref, sl_b)]
        outs, states = _gla_group(jobs, carry)
        for dst, o in zip(slices, outs):
            if dst is not None:
                dst[0][dst[1], :] = o
        return tuple(states)

    zero = jnp.zeros((HG_D, HG_D), F32)
    carry = lat_step(0, (zero, zero), ctx_jobs())
    lax.fori_loop(1, n_lat // group, lambda i, c: lat_step(i, c, []), carry)

    ng = ng_ref[...]

    def out_body(i, _):
        sl = pl.ds(pl.multiple_of(i * r, r), r)
        o = of_ref[sl, :] + ob_ref[sl, :]
        o = o * lax.rsqrt(jnp.mean(o * o, axis=-1, keepdims=True) + RMS_EPS) * ng
        g = g_ref[0, sl, :].astype(F32)
        y_ref[0, sl, :] = (o * (g * _sigmoid(g))).astype(y_ref.dtype)
        return 0

    lax.fori_loop(0, n_lat, out_body, 0)


def _chunk_tri(r, forward, dtype):
    i = np.arange(r)
    same = (i[:, None] // CHUNK) == (i[None, :] // CHUNK)
    order = (i[:, None] >= i[None, :]) if forward else (i[:, None] <= i[None, :])
    return jnp.asarray(same & order, dtype)


def _hgrn(q, k, hi, lo, v, g, ck, chi, clo, cv, ng):
    b, s, _ = q.shape
    lc = ck.shape[1]
    h = HG_HEADS
    d = HG_D
    r = HG_TILE
    kern = functools.partial(_hgrn_kernel, seq=s, ctx_len=lc)
    fwd = lambda i, j: (i, 0, j)
    bwd = lambda i, j: (i, 0, j + h)
    const = lambda i, j: (0, 0)
    lat = lambda im: pl.BlockSpec((1, s, d), im)
    cx = lambda im: pl.BlockSpec((1, lc, d), im)
    return pl.pallas_call(
        kern,
        grid=(b, h),
        in_specs=[lat(fwd), lat(fwd), lat(bwd), lat(fwd), lat(bwd), lat(fwd), lat(bwd), lat(fwd), lat(fwd),
                  cx(fwd), cx(bwd), cx(fwd), cx(bwd), cx(fwd), cx(bwd), cx(fwd),
                  pl.BlockSpec((1, d), lambda i, j: (0, j)),
                  pl.BlockSpec((r, r), const),
                  pl.BlockSpec((r, r), const),
                  pl.BlockSpec((r, r), const),
                  pl.BlockSpec((r, r), const)],
        out_specs=pl.BlockSpec((1, s, d), fwd),
        out_shape=jax.ShapeDtypeStruct((b, s, h * d), BF16),
        scratch_shapes=[pltpu.VMEM((s, d), F32), pltpu.VMEM((s, d), F32)],
        compiler_params=_cparams(("parallel", "parallel"), VMEM_LIMIT),
        name="hgrn2",
    )(q, k, k, hi, hi, lo, lo, v, g, ck, ck, chi, chi, clo, clo, cv, ng,
      _chunk_tri(r, True, BF16), _chunk_tri(r, False, BF16), _chunk_tri(r, True, F32), _chunk_tri(r, False, F32))


def _dft_tables(seq):
    rows = seq // GRID_W
    n = F_GROUP_DIM
    kc = np.outer(np.arange(n), np.arange(n)) % n
    ang = 2.0 * np.pi * kc / n
    norm = 1.0 / math.sqrt(rows * GRID_W * n)
    ch = np.concatenate([np.cos(ang), -np.sin(ang)], axis=1) * norm
    t = np.arange(seq)
    r, w = t // GRID_W, t % GRID_W
    m = (np.outer(r, r) * GRID_W + np.outer(w, w) * rows) % (rows * GRID_W)
    ang_t = 2.0 * np.pi * m / (rows * GRID_W)
    tok = np.concatenate([np.cos(ang_t), np.sin(ang_t)], axis=1)
    return jnp.asarray(ch, BF16), jnp.asarray(tok, BF16)


def _fourier_kernel(p_ref, ch_ref, tok_ref, y_ref, xs_ref, *, seq):
    for gi in range(F_GROUPS):
        cols = slice(gi * F_GROUP_DIM, (gi + 1) * F_GROUP_DIM)
        x1 = _dot(p_ref[0, :, cols], ch_ref[...]).astype(BF16)
        xs_ref[0:seq, cols] = x1[:, :F_GROUP_DIM]
        xs_ref[seq:2 * seq, cols] = x1[:, F_GROUP_DIM:]
    y_ref[0] = _dot(tok_ref[...], xs_ref[...]).astype(y_ref.dtype)


def _fourier(pf):
    b, s, wdt = pf.shape
    ch, tok = _dft_tables(s)
    kern = functools.partial(_fourier_kernel, seq=s)
    return pl.pallas_call(
        kern,
        grid=(b,),
        in_specs=[pl.BlockSpec((1, s, wdt), lambda i: (i, 0, 0)),
                  pl.BlockSpec(ch.shape, lambda i: (0, 0)),
                  pl.BlockSpec(tok.shape, lambda i: (0, 0), pipeline_mode=pl.Buffered(1))],
        out_specs=pl.BlockSpec((1, s, wdt), lambda i: (i, 0, 0)),
        out_shape=jax.ShapeDtypeStruct((b, s, wdt), BF16),
        scratch_shapes=[pltpu.VMEM((2 * s, wdt), BF16)],
        compiler_params=_cparams(("parallel",), VMEM_LIMIT),
        name="fourier",
    )(pf, ch, tok)


def _layer_norm(t, g, b):
    mu = jnp.mean(t, axis=-1, keepdims=True)
    tc = t - mu
    var = jnp.mean(tc * tc, axis=-1, keepdims=True)
    return tc * lax.rsqrt(var + LN_EPS) * g + b


def _merge_kernel(yf_ref, yh_ref, gt_ref, x_ref, g1_ref, sc2_ref, sh2_ref, wfo_ref, who_ref, wo_ref,
                  lg_ref, lbias_ref, wr_ref, br_ref, x1_ref, idx_ref, wt_ref):
    d = x_ref.shape[-1]
    gf = _sigmoid(gt_ref[0, :, :d].astype(F32))
    gh = _sigmoid(gt_ref[0, :, d:].astype(F32))
    m = gf * _dot(yf_ref[0], wfo_ref[...]) + gh * _dot(yh_ref[0], who_ref[...])
    mix = _dot(m.astype(BF16), wo_ref[...])
    x1 = _layer_norm(DEEPNORM_ALPHA * x_ref[0] + g1_ref[0] * mix, lg_ref[...], lbias_ref[...])
    x1_ref[0] = x1
    u2 = x1 * (1.0 + sc2_ref[0]) + sh2_ref[0]
    logits = _dot(u2.astype(BF16), wr_ref[...]) + br_ref[...]
    lane = lax.broadcasted_iota(jnp.int32, logits.shape, 1)
    lane_f = lane.astype(F32)
    work = jnp.where(lane < N_EXPERTS, logits, -jnp.inf)
    vals, idxs = [], []
    for _ in range(TOP_K):
        mx = jnp.max(work, axis=-1, keepdims=True)
        sel = jnp.min(jnp.where(work == mx, lane_f, float(LANES)), axis=-1, keepdims=True).astype(jnp.int32)
        vals.append(mx)
        idxs.append(sel)
        work = jnp.where(lane == sel, -jnp.inf, work)
    exps = [jnp.exp(vv - vals[0]) for vv in vals]
    inv = 1.0 / (exps[0] + exps[1] + exps[2] + exps[3])
    for kk in range(TOP_K):
        idx_ref[0, :, kk:kk + 1] = idxs[kk]
        wt_ref[0, :, kk:kk + 1] = exps[kk] * inv


def _merge(yf, yh, gates, x, g1, sc2, sh2, wfo, who, wo, lg, lbias, wr, br, tm):
    b, s, d = x.shape
    tile = lambda i, j: (i, j, 0)
    per_b = lambda i, j: (i, 0, 0)
    const = lambda i, j: (0, 0)
    return pl.pallas_call(
        _merge_kernel,
        grid=(b, s // tm),
        in_specs=[pl.BlockSpec((1, tm, yf.shape[-1]), tile),
                  pl.BlockSpec((1, tm, yh.shape[-1]), tile),
                  pl.BlockSpec((1, tm, gates.shape[-1]), tile),
                  pl.BlockSpec((1, tm, d), tile),
                  pl.BlockSpec((1, 1, d), per_b),
                  pl.BlockSpec((1, 1, d), per_b),
                  pl.BlockSpec((1, 1, d), per_b),
                  pl.BlockSpec(wfo.shape, const),
                  pl.BlockSpec(who.shape, const),
                  pl.BlockSpec(wo.shape, const),
                  pl.BlockSpec((1, d), const),
                  pl.BlockSpec((1, d), const),
                  pl.BlockSpec(wr.shape, const),
                  pl.BlockSpec(br.shape, const)],
        out_specs=[pl.BlockSpec((1, tm, d), tile),
                   pl.BlockSpec((1, tm, TOP_K), tile),
                   pl.BlockSpec((1, tm, TOP_K), tile)],
        out_shape=[jax.ShapeDtypeStruct((b, s, d), F32),
                   jax.ShapeDtypeStruct((b, s, TOP_K), jnp.int32),
                   jax.ShapeDtypeStruct((b, s, TOP_K), F32)],
        compiler_params=_cparams(("parallel", "parallel"), VMEM_LIMIT),
        name="merge_ln_router",
    )(yf, yh, gates, x, g1, sc2, sh2, wfo, who, wo, lg, lbias, wr, br)


def _route_kernel(idx_ref, before_ref, dest_ref, cnt_ref, start_ref, run_ref, *, block_rows):
    phase = pl.program_id(0)
    t = pl.program_id(1)
    tt = idx_ref.shape[1]
    expert = lax.broadcasted_iota(jnp.int32, (N_EXPERTS, tt), 0)
    onehots = [jnp.where(expert == idx_ref[kk:kk + 1, :], 1.0, 0.0) for kk in range(TOP_K)]
    counts = [jnp.sum(oh, axis=1, keepdims=True) for oh in onehots]

    @pl.when(jnp.logical_and(phase == 0, t == 0))
    def _():
        run_ref[...] = jnp.zeros_like(run_ref)

    @pl.when(phase == 0)
    def _():
        run_ref[...] += counts[0] + counts[1] + counts[2] + counts[3]

    @pl.when(jnp.logical_and(phase == 1, t == 0))
    def _():
        cnt = run_ref[...]
        padded = jnp.floor((cnt + (block_rows - 1)) * (1.0 / block_rows)) * block_rows
        r = lax.broadcasted_iota(jnp.int32, (N_EXPERTS, N_EXPERTS), 0)
        c = lax.broadcasted_iota(jnp.int32, (N_EXPERTS, N_EXPERTS), 1)
        earlier = jnp.where(r > c, 1.0, 0.0)
        start = jnp.dot(earlier, padded, preferred_element_type=F32, precision=lax.Precision.HIGHEST)
        cnt_ref[...] = cnt.astype(jnp.int32)
        start_ref[...] = start.astype(jnp.int32)
        run_ref[...] = jnp.zeros_like(run_ref)

    @pl.when(phase == 1)
    def _():
        prefixes = _dot(jnp.concatenate([oh.astype(BF16) for oh in onehots], axis=0), before_ref[...])
        start = start_ref[:, 0:1].astype(F32)
        base = run_ref[:, 0:1] + start
        for kk in range(TOP_K):
            prefix = prefixes[kk * N_EXPERTS:(kk + 1) * N_EXPERTS, :]
            dest_ref[kk:kk + 1, :] = jnp.sum(onehots[kk] * (prefix + base), axis=0, keepdims=True).astype(jnp.int32)
            base = base + counts[kk]
        run_ref[...] = jnp.broadcast_to(base - start, run_ref.shape)


def _route(idx_t, block_rows, tt):
    n = idx_t.shape[1]
    kern = functools.partial(_route_kernel, block_rows=block_rows)
    before = jnp.asarray(np.triu(np.ones((tt, tt)), 1), BF16)
    stat = pl.BlockSpec((N_EXPERTS, LANES), lambda p, t: (0, 0))
    return pl.pallas_call(
        kern,
        grid=(2, n // tt),
        in_specs=[pl.BlockSpec((TOP_K, tt), lambda p, t: (0, t)),
                  pl.BlockSpec((tt, tt), lambda p, t: (0, 0))],
        out_specs=[pl.BlockSpec((TOP_K, tt), lambda p, t: (0, t * p)), stat, stat],
        out_shape=[jax.ShapeDtypeStruct((TOP_K, n), jnp.int32),
                   jax.ShapeDtypeStruct((N_EXPERTS, LANES), jnp.int32),
                   jax.ShapeDtypeStruct((N_EXPERTS, LANES), jnp.int32)],
        scratch_shapes=[pltpu.VMEM((N_EXPERTS, LANES), F32)],
        compiler_params=_cparams(("arbitrary", "arbitrary")),
        name="route_ranks",
    )(idx_t, before)


def _to_row_tiles(ref, base, val):
    rows = val.shape[0]
    for s in range(SUBLANES):
        ref[pl.ds(base + s, rows, stride=SUBLANES), :] = val[:, s * LANES:(s + 1) * LANES]


def _from_row_tiles(ref, base, rows):
    return jnp.concatenate([ref[pl.ds(base + s, rows, stride=SUBLANES), :] for s in range(SUBLANES)], axis=1)


def _row_tile(ref, r):
    return ref.at[pl.ds(pl.multiple_of(r * SUBLANES, SUBLANES), SUBLANES), :]


def _row_tiles(ref, r, n):
    return ref.at[pl.ds(pl.multiple_of(r * SUBLANES, SUBLANES), n * SUBLANES), :]


def _dispatch_kernel(cnt_ref, start_ref, dest_ref, x1_ref, sc_ref, sh_ref, h_ref, u_ref, zero_ref, sems, *, block_rows):
    t = pl.program_id(0)
    nt = pl.num_programs(0)
    tt = x1_ref.shape[0]
    slot = t % 2
    ubase = pl.multiple_of(slot * (tt * SUBLANES), tt * SUBLANES)
    _to_row_tiles(u_ref, ubase, x1_ref[...] * (1.0 + sc_ref[0]) + sh_ref[0])

    def issue(i, _):
        for kk in range(TOP_K):
            d = dest_ref[0, 0, kk * tt + i]
            pltpu.make_async_copy(_row_tile(u_ref, slot * tt + i), _row_tile(h_ref, d),
                                  sems.at[slot]).start(priority=kk % DMA_QUEUES)
        return 0

    lax.fori_loop(0, tt, issue, 0, unroll=ISSUE_UNROLL)

    def drain(sl):
        for _ in range(TOP_K):
            pltpu.make_async_copy(_row_tiles(u_ref, sl * tt, tt), _row_tiles(h_ref, 0, tt), sems.at[sl]).wait()

    @pl.when(t > 0)
    def _():
        drain(1 - slot)

    @pl.when(t == nt - 1)
    def _():
        drain(slot)
        zero_ref[...] = jnp.zeros_like(zero_ref)

        def pad_expert(e, _):
            cnt = cnt_ref[e]
            rem = (block_rows - (cnt & (block_rows - 1))) & (block_rows - 1)
            pos = start_ref[e] + cnt
            size = block_rows // 2
            while size >= 1:
                take = rem & size

                @pl.when(take != 0)
                def _(pos=pos, size=size):
                    cp = pltpu.make_async_copy(_row_tiles(zero_ref, 0, size), _row_tiles(h_ref, pos, size), sems.at[2])
                    cp.start()
                    cp.wait()

                pos = pos + take
                size //= 2
            return 0

        lax.fori_loop(0, N_EXPERTS, pad_expert, 0)


def _dispatch(counts, starts, dest_flat, x1, sc2, sh2, n_rows, tt):
    n, d = x1.shape
    b = sc2.shape[0]
    tiles_per_b = (n // b) // tt
    per_b = lambda t, c, s: (t // tiles_per_b, 0, 0)
    grid_spec = pltpu.PrefetchScalarGridSpec(
        num_scalar_prefetch=2,
        grid=(n // tt,),
        in_specs=[pl.BlockSpec((1, 1, tt * TOP_K), lambda t, c, s: (t, 0, 0), memory_space=pltpu.SMEM),
                  pl.BlockSpec((tt, d), lambda t, c, s: (t, 0)),
                  pl.BlockSpec((1, 1, d), per_b),
                  pl.BlockSpec((1, 1, d), per_b)],
        out_specs=pl.BlockSpec(memory_space=pl.ANY),
        scratch_shapes=[pltpu.VMEM((2 * tt * SUBLANES, LANES), F32), pltpu.VMEM((MOE_ROWS // 2 * SUBLANES, LANES), F32),
                        pltpu.SemaphoreType.DMA((3,))],
    )
    return pl.pallas_call(
        functools.partial(_dispatch_kernel, block_rows=MOE_ROWS),
        grid_spec=grid_spec,
        out_shape=jax.ShapeDtypeStruct((n_rows * SUBLANES, LANES), F32),
        compiler_params=_cparams(("arbitrary",)),
        name="moe_dispatch",
    )(counts, starts, dest_flat, x1, sc2, sh2)


def _expert_kernel(be_ref, nu_ref, h_ref, wgu_ref, bgu_ref, wdn_ref, bdn_ref, o_ref, wgu_b_ref, wdn_b_ref):
    j = pl.program_id(0)
    d = wdn_ref.shape[-1]
    bm = h_ref.shape[0] // SUBLANES

    @pl.when(j < nu_ref[0])
    def _():
        @pl.when(jnp.logical_or(j == 0, be_ref[j] != be_ref[jnp.maximum(j - 1, 0)]))
        def _():
            wgu_b_ref[...] = wgu_ref[0].astype(BF16)
            wdn_b_ref[...] = wdn_ref[0].astype(BF16)

        gu = _dot(_from_row_tiles(h_ref, 0, bm).astype(BF16), wgu_b_ref[...]) + bgu_ref[0]
        gate = jnp.minimum(gu[:, :d], SWIGLU_LIMIT)
        up = jnp.clip(gu[:, d:], -SWIGLU_LIMIT, SWIGLU_LIMIT)
        act = (up + 1.0) * gate * _sigmoid(SWIGLU_ALPHA * gate)
        _to_row_tiles(o_ref, 0, _dot(act.astype(BF16), wdn_b_ref[...]) + bdn_ref[0])


def _experts(h_sorted, block_expert, n_used, wgu, bgu, wdn, bdn):
    n_rows = h_sorted.shape[0] // SUBLANES
    e, d, _ = wdn.shape
    bm = MOE_ROWS
    n_blocks = n_rows // bm
    blk = lambda j, be, nu: (jnp.minimum(j, nu[0] - 1), 0)
    exp = lambda j, be, nu: (be[jnp.minimum(j, nu[0] - 1)], 0, 0)
    grid_spec = pltpu.PrefetchScalarGridSpec(
        num_scalar_prefetch=2,
        grid=(n_blocks,),
        in_specs=[pl.BlockSpec((bm * SUBLANES, LANES), blk),
                  pl.BlockSpec((1, d, 2 * d), exp),
                  pl.BlockSpec((1, 1, 2 * d), exp),
                  pl.BlockSpec((1, d, d), exp),
                  pl.BlockSpec((1, 1, d), exp)],
        out_specs=pl.BlockSpec((bm * SUBLANES, LANES), blk),
        scratch_shapes=[pltpu.VMEM((d, 2 * d), BF16), pltpu.VMEM((d, d), BF16)],
    )
    return pl.pallas_call(
        _expert_kernel,
        grid_spec=grid_spec,
        out_shape=jax.ShapeDtypeStruct((n_rows * SUBLANES, LANES), F32),
        compiler_params=_cparams(("arbitrary",), VMEM_LIMIT),
        name="moe_experts",
    )(block_expert, n_used, h_sorted, wgu, bgu.reshape(e, 1, 2 * d), wdn, bdn.reshape(e, 1, d))


def _combine_kernel(dest_ref, dest_next_ref, wt_ref, x1_ref, g2_ref, lg_ref, lbias_ref, o_hbm, y_ref, buf_ref, sems):
    t = pl.program_id(0)
    nt = pl.num_programs(0)
    tt = x1_ref.shape[0]
    slot = t % 2

    def issue_row(d_ref, sl, i):
        for kk in range(TOP_K):
            d = d_ref[0, 0, kk * tt + i]
            pltpu.make_async_copy(_row_tile(o_hbm, d), _row_tile(buf_ref, (sl * TOP_K + kk) * tt + i),
                                  sems.at[sl]).start(priority=kk % DMA_QUEUES)

    def drain(sl):
        for kk in range(TOP_K):
            pltpu.make_async_copy(_row_tiles(o_hbm, 0, tt), _row_tiles(buf_ref, (sl * TOP_K + kk) * tt, tt),
                                  sems.at[sl]).wait()

    @pl.when(t == 0)
    def _():
        def first(i, _):
            issue_row(dest_ref, 0, i)
            return 0

        lax.fori_loop(0, tt, first, 0, unroll=ISSUE_UNROLL)

    drain(slot)

    def fused(g, _):
        r0 = pl.multiple_of(g * COMBINE_ROWS, COMBINE_ROWS)
        for i in range(COMBINE_ROWS):
            issue_row(dest_next_ref, 1 - slot, r0 + i)
        rows = pl.ds(r0, COMBINE_ROWS)
        ff = None
        for kk in range(TOP_K):
            base = pl.multiple_of(((slot * TOP_K + kk) * tt + r0) * SUBLANES, COMBINE_ROWS * SUBLANES)
            term = wt_ref[rows, kk:kk + 1] * _from_row_tiles(buf_ref, base, COMBINE_ROWS)
            ff = term if ff is None else ff + term
        y_ref[rows, :] = _layer_norm(DEEPNORM_ALPHA * x1_ref[rows, :] + g2_ref[0] * ff, lg_ref[...], lbias_ref[...])
        return 0

    lax.fori_loop(0, tt // COMBINE_ROWS, fused, 0)

    @pl.when(t == nt - 1)
    def _():
        drain(1 - slot)


def _combine(dest_flat, wt, x1, g2, lg, lbias, o_sorted, tt):
    n, d = x1.shape
    b = g2.shape[0]
    nt = n // tt
    tiles_per_b = (n // b) // tt
    const = lambda t: (0, 0)
    dest_spec = lambda im: pl.BlockSpec((1, 1, tt * TOP_K), im, memory_space=pltpu.SMEM)
    return pl.pallas_call(
        _combine_kernel,
        grid=(nt,),
        in_specs=[dest_spec(lambda t: (t, 0, 0)),
                  dest_spec(lambda t: (jnp.minimum(t + 1, nt - 1), 0, 0)),
                  pl.BlockSpec((tt, TOP_K), lambda t: (t, 0)),
                  pl.BlockSpec((tt, d), lambda t: (t, 0)),
                  pl.BlockSpec((1, 1, d), lambda t: (t // tiles_per_b, 0, 0)),
                  pl.BlockSpec((1, d), const),
                  pl.BlockSpec((1, d), const),
                  pl.BlockSpec(memory_space=pl.ANY)],
        out_specs=pl.BlockSpec((tt, d), lambda t: (t, 0)),
        out_shape=jax.ShapeDtypeStruct((n, d), F32),
        scratch_shapes=[pltpu.VMEM((2 * TOP_K * tt * SUBLANES, LANES), F32), pltpu.SemaphoreType.DMA((2,))],
        compiler_params=_cparams(("arbitrary",), VMEM_LIMIT),
        name="moe_combine",
    )(dest_flat, dest_flat, wt, x1, g2, lg, lbias, o_sorted)


def kernel(x, c, ctx, c_ctx, w_ada, b_ada, w_in, lb_raw, hg_norm_g, w_four_out, w_hg_out, w_o, ln1_g, ln1_b,
           w_router, b_router, w_gate_up, b_gate_up, w_down, b_down, ln2_g, ln2_b):
    b, s, d = x.shape
    lc = ctx.shape[1]
    n = b * s
    f_w = F_GROUPS * F_GROUP_DIM
    hk = HG_HEADS * HG_D
    assert w_ada.shape[0] == DEPTH and s % GRID_W == 0 and s % (HG_TILE * HG_GROUP) == 0 and s % 512 == 0 and lc % HG_TILE == 0

    pad = (-(b + 1)) % 8
    c_rows = jnp.concatenate([c, c_ctx[None, :], jnp.zeros((pad, d), F32)], axis=0)
    mod = _ada(c_rows, w_ada[0], b_ada[0]).reshape(b + 1 + pad, N_MOD, 1, d)
    shift1, scale1, gate1, shift2, scale2, gate2 = (mod[:b, i] for i in range(N_MOD))
    cshift1 = jnp.broadcast_to(mod[b, 0], (b, 1, d))
    cscale1 = jnp.broadcast_to(mod[b, 1], (b, 1, d))

    lower_bounds = jnp.cumsum(jax.nn.softmax(lb_raw.astype(F32), axis=0), axis=0)[0]

    w_in_b = w_in[0].astype(BF16)
    o_f, o_q, o_z, o_v, o_g, o_gt = f_w, f_w + hk, f_w + 3 * hk, f_w + 4 * hk, f_w + 5 * hk, f_w + 5 * hk + 2 * d
    lb_row = lower_bounds.reshape(1, 2 * hk)
    pf, pq, pk, phi, plo, pv, pg, gates = _inproj(
        x, scale1, shift1, lb_row, w_in_b,
        [(0, f_w, "cast"), (o_f, hk, "silu"), (o_q, 2 * hk, "forget"), (o_z, hk, "cast"), (o_v, hk, "cast"),
         (o_g, 2 * d, "cast")], tm=1024)
    w_ctx = w_in_b[:, o_q:o_v]
    ck, chi, clo, cv = _inproj(ctx, cscale1, cshift1, lb_row, w_ctx, [(0, 2 * hk, "forget"), (2 * hk, hk, "cast")], tm=lc)

    y_h = _hgrn(pq, pk, phi, plo, pv, pg, ck, chi, clo, cv, hg_norm_g[0].reshape(1, hk))
    y_f = _fourier(pf)

    wr = jnp.zeros((d, LANES), BF16).at[:, :N_EXPERTS].set(w_router[0].astype(BF16))
    br = jnp.zeros((1, LANES), F32).at[0, :N_EXPERTS].set(b_router[0])
    x1, idx, wt = _merge(y_f, y_h, gates, x, gate1, scale2, shift2,
                         w_four_out[0].astype(BF16), w_hg_out[0].astype(BF16), w_o[0].astype(BF16),
                         ln1_g[0].reshape(1, d), ln1_b[0].reshape(1, d), wr, br, tm=1024)

    dest_t, counts, starts = _route(idx.reshape(n, TOP_K).T, MOE_ROWS, 1024)
    n_blocks = -(-(n * TOP_K) // MOE_ROWS) + N_EXPERTS
    n_rows = n_blocks * MOE_ROWS
    cnt = counts[:, 0]
    st = starts[:, 0]
    pad_ends = st + (cnt + MOE_ROWS - 1) // MOE_ROWS * MOE_ROWS
    block_first_row = jnp.arange(n_blocks, dtype=jnp.int32) * MOE_ROWS
    block_expert = jnp.minimum(jnp.sum((pad_ends[None, :] <= block_first_row[:, None]).astype(jnp.int32), axis=1),
                               N_EXPERTS - 1)
    n_used = (pad_ends[-1] // MOE_ROWS).astype(jnp.int32).reshape(1)

    tt = 256
    dest_flat = dest_t.reshape(TOP_K, n // tt, tt).transpose(1, 0, 2).reshape(n // tt, 1, TOP_K * tt)
    x1_2d = x1.reshape(n, d)
    h_sorted = _dispatch(cnt, st, dest_flat, x1_2d, scale2, shift2, n_rows, tt)
    o_sorted = _experts(h_sorted, block_expert, n_used, w_gate_up[0], b_gate_up[0], w_down[0], b_down[0])
    out = _combine(dest_flat, wt.reshape(n, TOP_K), x1_2d, gate2, ln2_g[0].reshape(1, d), ln2_b[0].reshape(1, d),
                   o_sorted, tt)
    return out.reshape(b, s, d)
```

```python
import functools
import math

import jax
import jax.numpy as jnp
import numpy as np
from jax import lax
from jax.experimental import pallas as pl
from jax.experimental.pallas import tpu as pltpu

F32 = jnp.float32
BF16 = jnp.bfloat16

GRID_W = 64
N_MOD = 6
F_GROUPS = 4
F_GROUP_DIM = 128
HG_HEADS = 4
HG_D = 128
CHUNK = 64
N_EXPERTS = 32
TOP_K = 4
SWIGLU_LIMIT = 7.0
SWIGLU_ALPHA = 1.702
LN_EPS = 1e-5
RMS_EPS = 1e-6
DEPTH = 1
DEEPNORM_ALPHA = (2.0 * DEPTH) ** 0.25

LANES = 128
SUBLANES = 8
MOE_ROWS = 512
COMBINE_ROWS = 128
ISSUE_UNROLL = 4
DMA_QUEUES = 2
VMEM_LIMIT = 56 * 1024 * 1024


def _cparams(sem, vmem=None):
    return pltpu.CompilerParams(dimension_semantics=sem, vmem_limit_bytes=vmem)


def _dot(a, b):
    return jnp.dot(a, b, preferred_element_type=F32)


def _sigmoid(x):
    return 0.5 * jnp.tanh(0.5 * x) + 0.5


def _ada_kernel(c_ref, w_ref, b_ref, o_ref):
    c = c_ref[...]
    a = c * _sigmoid(c)
    o_ref[...] = jnp.dot(a, w_ref[...], preferred_element_type=F32,
                         precision=lax.Precision.HIGHEST) + b_ref[...]


def _ada(c_rows, w, b):
    r, d = c_rows.shape
    n = w.shape[1]
    tn = 512
    return pl.pallas_call(
        _ada_kernel,
        grid=(n // tn,),
        in_specs=[pl.BlockSpec((r, d), lambda j: (0, 0)),
                  pl.BlockSpec((d, tn), lambda j: (0, j)),
                  pl.BlockSpec((1, tn), lambda j: (0, j))],
        out_specs=pl.BlockSpec((r, tn), lambda j: (0, j)),
        out_shape=jax.ShapeDtypeStruct((r, n), F32),
        compiler_params=_cparams(("arbitrary",)),
        name="ada_mod",
    )(c_rows, w, b.reshape(1, n))


def _inproj_kernel(x_ref, sc_ref, sh_ref, lb_ref, w_ref, *o_refs, plan):
    u = (x_ref[0] * (1.0 + sc_ref[0]) + sh_ref[0]).astype(BF16)
    refs = iter(o_refs)
    for kind, chunks in plan:
        if kind == "forget":
            k_ref, hi_ref, lo_ref = next(refs), next(refs), next(refs)
            for (w0, o0, n) in chunks:
                lb = lb_ref[:, o0:o0 + n]
                f = lb + (1.0 - lb) * _sigmoid(_dot(u, w_ref[:, w0:w0 + n]))
                lf = jnp.log(f)
                hi = lf.astype(BF16)
                k_ref[0, :, o0:o0 + n] = (1.0 - f).astype(BF16)
                hi_ref[0, :, o0:o0 + n] = hi
                lo_ref[0, :, o0:o0 + n] = (lf - hi.astype(F32)).astype(BF16)
        else:
            o_ref = next(refs)
            for (w0, o0, n) in chunks:
                p = _dot(u, w_ref[:, w0:w0 + n])
                if kind == "silu":
                    p = p * _sigmoid(p)
                o_ref[0, :, o0:o0 + n] = p.astype(o_ref.dtype)


def _inproj(x, scale, shift, lb_row, w_bf16, outs, tm):
    b, s, d = x.shape
    n_w = w_bf16.shape[1]
    plan, widths = [], []
    for (c0, width, kind) in outs:
        step = min(width, 512)
        plan.append((kind, tuple((c0 + o, o, step) for o in range(0, width, step))))
        widths += [width] * (3 if kind == "forget" else 1)
    kern = functools.partial(_inproj_kernel, plan=tuple(plan))
    return pl.pallas_call(
        kern,
        grid=(b, s // tm),
        in_specs=[pl.BlockSpec((1, tm, d), lambda i, j: (i, j, 0)),
                  pl.BlockSpec((1, 1, d), lambda i, j: (i, 0, 0)),
                  pl.BlockSpec((1, 1, d), lambda i, j: (i, 0, 0)),
                  pl.BlockSpec(lb_row.shape, lambda i, j: (0, 0)),
                  pl.BlockSpec((d, n_w), lambda i, j: (0, 0), pipeline_mode=pl.Buffered(1))],
        out_specs=[pl.BlockSpec((1, tm, width), lambda i, j: (i, j, 0)) for width in widths],
        out_shape=[jax.ShapeDtypeStruct((b, s, width), BF16) for width in widths],
        compiler_params=_cparams(("parallel", "parallel"), VMEM_LIMIT),
        name="in_proj",
    )(x, scale, shift, lb_row, w_bf16)


HG_TILE = 256
HG_GROUP = 4


def _gla_group(jobs, states):
    states = list(states)
    nj = len(jobs)
    r, dk = jobs[0][1].shape
    dv = jobs[0][4].shape[1]
    nc = r // CHUNK
    zero = jnp.zeros((CHUNK, dk), BF16)
    contract_last = (((1,), (1,)), ((), ()))
    contract_rows = (((0,), (0,)), ((), ()))

    cums = []
    for (_, hi, lo, _, _, _, tri, _, _) in jobs:
        both = _dot(tri, jnp.concatenate([hi, lo], axis=1))
        cums.append(both[:, :dk] + both[:, dk:])

    k_inv_b, decays, k_end_blk, q_dec = [], [], [], []
    for cum, (_, _, _, k, _, q, _, _, forward) in zip(cums, jobs):
        k_inv = k.astype(F32) * jnp.exp(-cum)
        k_inv_b.append(k_inv.astype(BF16))
        tot_row = CHUNK - 1 if forward else 0
        dec_j, rows_j = [], []
        for ci in range(nc):
            dec = jnp.exp(cum[ci * CHUNK + tot_row:ci * CHUNK + tot_row + 1, :])
            k_end = (k_inv[ci * CHUNK:(ci + 1) * CHUNK] * dec).astype(BF16)
            dec_j.append(dec)
            rows_j.append(jnp.concatenate([k_end if cj == ci else zero for cj in range(nc)], axis=1))
        decays.append(dec_j)
        k_end_blk.append(jnp.concatenate(rows_j, axis=0))
        q_dec.append(None if q is None else (q.astype(F32) * jnp.exp(cum)).astype(BF16))

    upd_all = [lax.dot_general(job[4], blk, contract_rows, preferred_element_type=F32)
               for job, blk in zip(jobs, k_end_blk)]
    scores = [None if qd is None else lax.dot_general(qd, kb, contract_last, preferred_element_type=F32)
              for qd, kb in zip(q_dec, k_inv_b)]

    s_all = []
    for ji in range(nj):
        chain, forward = jobs[ji][0], jobs[ji][8]
        st = states[chain]
        entering = [None] * nc
        for ci in (range(nc) if forward else reversed(range(nc))):
            entering[ci] = st
            st = st * decays[ji][ci] + upd_all[ji][:, ci * dk:(ci + 1) * dk]
        states[chain] = st
        s_all.append(None if q_dec[ji] is None
                     else [jnp.concatenate([entering[ci].astype(BF16), entering[ci + 1].astype(BF16)], axis=0)
                           for ci in range(0, nc, 2)])

    outs = []
    for ji in range(nj):
        if q_dec[ji] is None:
            outs.append(None)
            continue
        sc = jnp.where(jobs[ji][7] > 0, scores[ji], 0.0).astype(BF16)
        intra = _dot(sc, jobs[ji][4])
        inter = []
        for pi, s_pair in enumerate(s_all[ji]):
            rows = slice(2 * pi * CHUNK, (2 * pi + 2) * CHUNK)
            both = lax.dot_general(q_dec[ji][rows], s_pair, contract_last, preferred_element_type=F32)
            inter += [both[:CHUNK, :dv], both[CHUNK:, dv:]]
        outs.append(intra + jnp.concatenate(inter, axis=0))
    return outs, states


def _hgrn_kernel(q_ref, kf_ref, kb_ref, hif_ref, hib_ref, lof_ref, lob_ref, v_ref, g_ref,
                 ckf_ref, ckb_ref, chif_ref, chib_ref, clof_ref, clob_ref, cv_ref, ng_ref, trif_ref, trib_ref,
                 keepf_ref, keepb_ref, y_ref, of_ref, ob_ref, *, seq, ctx_len):
    r = HG_TILE
    n_lat = seq // r
    n_ctx = ctx_len // r
    group = HG_GROUP

    def ctx_jobs():
        jobs = []
        for i in range(n_ctx):
            sl_f = pl.ds(i * r, r)
            sl_b = pl.ds((n_ctx - 1 - i) * r, r)
            jobs.append((0, chif_ref[0, sl_f, :], clof_ref[0, sl_f, :], ckf_ref[0, sl_f, :], cv_ref[0, sl_f, :], None,
                         trif_ref[...], None, True))
            jobs.append((1, chib_ref[0, sl_b, :], clob_ref[0, sl_b, :], ckb_ref[0, sl_b, :], cv_ref[0, sl_b, :], None,
                         trib_ref[...], None, False))
        return jobs

    ng = ng_ref[...]

    def readout(tile, o):
        sl = pl.ds(tile * r, r)
        o = o * lax.rsqrt(jnp.mean(o * o, axis=-1, keepdims=True) + RMS_EPS) * ng
        g = g_ref[0, sl, :].astype(F32)
        y_ref[0, sl, :] = (o * (g * _sigmoid(g))).astype(y_ref.dtype)

    def step_of(tile, forward):
        return (tile if forward else n_lat - 1 - tile) // group

    carry = (jnp.zeros((HG_D, HG_D), F32),) * 2
    for i in range(n_lat // group):
        jobs = ctx_jobs() if i == 0 else []
        meta = [None] * len(jobs)
        for t in range(group):
            for forward in (True, False):
                tile = i * group + t if forward else n_lat - 1 - (i * group + t)
                sl = pl.ds(tile * r, r)
                hi_ref, lo_ref, k_ref, tri_ref, keep_ref = ((hif_ref, lof_ref, kf_ref, trif_ref, keepf_ref) if forward
                                                            else (hib_ref, lob_ref, kb_ref, trib_ref, keepb_ref))
                jobs.append((0 if forward else 1, hi_ref[0, sl, :], lo_ref[0, sl, :], k_ref[0, sl, :], v_ref[0, sl, :],
                             q_ref[0, sl, :], tri_ref[...], keep_ref[...], forward))
                meta.append((tile, forward))
        outs, carry = _gla_group(jobs, carry)
        waiting = {}
        for m, o in zip(meta, outs):
            if m is None:
                continue
            tile, forward = m
            mine, other = (of_ref, ob_ref) if forward else (ob_ref, of_ref)
            other_step = step_of(tile, not forward)
            if other_step < i:
                readout(tile, o + other[pl.ds(tile * r, r), :])
            elif other_step > i:
                mine[pl.ds(tile * r, r), :] = o
            elif tile in waiting:
                readout(tile, o + waiting.pop(tile))
            else:
                waiting[tile] = o


def _chunk_tri(r, forward, dtype):
    i = np.arange(r)
    same = (i[:, None] // CHUNK) == (i[None, :] // CHUNK)
    order = (i[:, None] >= i[None, :]) if forward else (i[:, None] <= i[None, :])
    return jnp.asarray(same & order, dtype)


def _hgrn(q, k, hi, lo, v, g, ck, chi, clo, cv, ng):
    b, s, _ = q.shape
    lc = ck.shape[1]
    h = HG_HEADS
    d = HG_D
    r = HG_TILE
    kern = functools.partial(_hgrn_kernel, seq=s, ctx_len=lc)
    fwd = lambda i, j: (i, 0, j)
    bwd = lambda i, j: (i, 0, j + h)
    const = lambda i, j: (0, 0)
    lat = lambda im: pl.BlockSpec((1, s, d), im)
    cx = lambda im: pl.BlockSpec((1, lc, d), im)
    return pl.pallas_call(
        kern,
        grid=(b, h),
        in_specs=[lat(fwd), lat(fwd), lat(bwd), lat(fwd), lat(bwd), lat(fwd), lat(bwd), lat(fwd), lat(fwd),
                  cx(fwd), cx(bwd), cx(fwd), cx(bwd), cx(fwd), cx(bwd), cx(fwd),
                  pl.BlockSpec((1, d), lambda i, j: (0, j)),
                  pl.BlockSpec((r, r), const),
                  pl.BlockSpec((r, r), const),
                  pl.BlockSpec((r, r), const),
                  pl.BlockSpec((r, r), const)],
        out_specs=pl.BlockSpec((1, s, d), fwd),
        out_shape=jax.ShapeDtypeStruct((b, s, h * d), BF16),
        scratch_shapes=[pltpu.VMEM((s, d), F32), pltpu.VMEM((s, d), F32)],
        compiler_params=_cparams(("parallel", "parallel"), VMEM_LIMIT),
        name="hgrn2",
    )(q, k, k, hi, hi, lo, lo, v, g, ck, ck, chi, chi, clo, clo, cv, ng,
      _chunk_tri(r, True, BF16), _chunk_tri(r, False, BF16), _chunk_tri(r, True, F32), _chunk_tri(r, False, F32))


def _dft_tables(seq):
    rows = seq // GRID_W
    n = F_GROUP_DIM
    kc = np.outer(np.arange(n), np.arange(n)) % n
    ang = 2.0 * np.pi * kc / n
    norm = 1.0 / math.sqrt(rows * GRID_W * n)
    ch = np.concatenate([np.cos(ang), -np.sin(ang)], axis=1) * norm
    t = np.arange(seq)
    r, w = t // GRID_W, t % GRID_W
    m = (np.outer(r, r) * GRID_W + np.outer(w, w) * rows) % (rows * GRID_W)
    ang_t = 2.0 * np.pi * m / (rows * GRID_W)
    tok = np.concatenate([np.cos(ang_t), np.sin(ang_t)], axis=1)
    return jnp.asarray(ch, BF16), jnp.asarray(tok, BF16)


def _fourier_kernel(p_ref, ch_ref, tok_ref, y_ref, xs_ref, *, seq):
    for gi in range(F_GROUPS):
        cols = slice(gi * F_GROUP_DIM, (gi + 1) * F_GROUP_DIM)
        x1 = _dot(p_ref[0, :, cols], ch_ref[...]).astype(BF16)
        xs_ref[0:seq, cols] = x1[:, :F_GROUP_DIM]
        xs_ref[seq:2 * seq, cols] = x1[:, F_GROUP_DIM:]
    y_ref[0] = _dot(tok_ref[...], xs_ref[...]).astype(y_ref.dtype)


def _fourier(pf):
    b, s, wdt = pf.shape
    ch, tok = _dft_tables(s)
    kern = functools.partial(_fourier_kernel, seq=s)
    return pl.pallas_call(
        kern,
        grid=(b,),
        in_specs=[pl.BlockSpec((1, s, wdt), lambda i: (i, 0, 0)),
                  pl.BlockSpec(ch.shape, lambda i: (0, 0)),
                  pl.BlockSpec(tok.shape, lambda i: (0, 0), pipeline_mode=pl.Buffered(1))],
        out_specs=pl.BlockSpec((1, s, wdt), lambda i: (i, 0, 0)),
        out_shape=jax.ShapeDtypeStruct((b, s, wdt), BF16),
        scratch_shapes=[pltpu.VMEM((2 * s, wdt), BF16)],
        compiler_params=_cparams(("parallel",), VMEM_LIMIT),
        name="fourier",
    )(pf, ch, tok)


def _layer_norm(t, g, b):
    mu = jnp.mean(t, axis=-1, keepdims=True)
    tc = t - mu
    var = jnp.mean(tc * tc, axis=-1, keepdims=True)
    return tc * lax.rsqrt(var + LN_EPS) * g + b


def _merge_kernel(yf_ref, yh_ref, gt_ref, x_ref, g1_ref, sc2_ref, sh2_ref, wfo_ref, who_ref, wo_ref,
                  lg_ref, lbias_ref, wr_ref, br_ref, x1_ref, idx_ref, wt_ref):
    d = x_ref.shape[-1]
    gf = _sigmoid(gt_ref[0, :, :d].astype(F32))
    gh = _sigmoid(gt_ref[0, :, d:].astype(F32))
    m = gf * _dot(yf_ref[0], wfo_ref[...]) + gh * _dot(yh_ref[0], who_ref[...])
    mix = _dot(m.astype(BF16), wo_ref[...])
    x1 = _layer_norm(DEEPNORM_ALPHA * x_ref[0] + g1_ref[0] * mix, lg_ref[...], lbias_ref[...])
    x1_ref[0] = x1
    u2 = x1 * (1.0 + sc2_ref[0]) + sh2_ref[0]
    logits = lax.dot_general(wr_ref[...], u2.astype(BF16), (((1,), (1,)), ((), ())),
                             preferred_element_type=F32) + br_ref[...]
    expert = lax.broadcasted_iota(jnp.int32, logits.shape, 0)
    expert_f = expert.astype(F32)
    work = logits
    vals, idxs = [], []
    for _ in range(TOP_K):
        mx = jnp.max(work, axis=0, keepdims=True)
        sel = jnp.min(jnp.where(work == mx, expert_f, float(N_EXPERTS)), axis=0, keepdims=True).astype(jnp.int32)
        vals.append(mx)
        idxs.append(sel)
        work = jnp.where(expert == sel, -jnp.inf, work)
    exps = [jnp.exp(vv - vals[0]) for vv in vals]
    inv = 1.0 / (exps[0] + exps[1] + exps[2] + exps[3])
    for kk in range(TOP_K):
        idx_ref[kk:kk + 1, :] = idxs[kk]
        wt_ref[kk:kk + 1, :] = exps[kk] * inv


def _merge(yf, yh, gates, x, g1, sc2, sh2, wfo, who, wo, lg, lbias, wr, br, tm):
    b, s, d = x.shape
    tile = lambda i, j: (i, j, 0)
    per_b = lambda i, j: (i, 0, 0)
    const = lambda i, j: (0, 0)
    slots = lambda i, j: (0, i * (s // tm) + j)
    return pl.pallas_call(
        _merge_kernel,
        grid=(b, s // tm),
        in_specs=[pl.BlockSpec((1, tm, yf.shape[-1]), tile),
                  pl.BlockSpec((1, tm, yh.shape[-1]), tile),
                  pl.BlockSpec((1, tm, gates.shape[-1]), tile),
                  pl.BlockSpec((1, tm, d), tile),
                  pl.BlockSpec((1, 1, d), per_b),
                  pl.BlockSpec((1, 1, d), per_b),
                  pl.BlockSpec((1, 1, d), per_b),
                  pl.BlockSpec(wfo.shape, const),
                  pl.BlockSpec(who.shape, const),
                  pl.BlockSpec(wo.shape, const),
                  pl.BlockSpec((1, d), const),
                  pl.BlockSpec((1, d), const),
                  pl.BlockSpec(wr.shape, const),
                  pl.BlockSpec(br.shape, const)],
        out_specs=[pl.BlockSpec((1, tm, d), tile),
                   pl.BlockSpec((TOP_K, tm), slots),
                   pl.BlockSpec((TOP_K, tm), slots)],
        out_shape=[jax.ShapeDtypeStruct((b, s, d), F32),
                   jax.ShapeDtypeStruct((TOP_K, b * s), jnp.int32),
                   jax.ShapeDtypeStruct((TOP_K, b * s), F32)],
        compiler_params=_cparams(("parallel", "parallel"), VMEM_LIMIT),
        name="merge_ln_router",
    )(yf, yh, gates, x, g1, sc2, sh2, wfo, who, wo, lg, lbias, wr, br)


def _route_kernel(idx_ref, before_ref, dest_ref, cnt_ref, start_ref, run_ref, *, block_rows):
    phase = pl.program_id(0)
    t = pl.program_id(1)
    tt = idx_ref.shape[1]
    expert = lax.broadcasted_iota(jnp.int32, (N_EXPERTS, tt), 0)
    onehots = [jnp.where(expert == idx_ref[kk:kk + 1, :], 1.0, 0.0) for kk in range(TOP_K)]
    counts = [jnp.sum(oh, axis=1, keepdims=True) for oh in onehots]

    @pl.when(jnp.logical_and(phase == 0, t == 0))
    def _():
        run_ref[...] = jnp.zeros_like(run_ref)

    @pl.when(phase == 0)
    def _():
        run_ref[...] += counts[0] + counts[1] + counts[2] + counts[3]

    @pl.when(jnp.logical_and(phase == 1, t == 0))
    def _():
        cnt = run_ref[...]
        padded = jnp.floor((cnt + (block_rows - 1)) * (1.0 / block_rows)) * block_rows
        r = lax.broadcasted_iota(jnp.int32, (N_EXPERTS, N_EXPERTS), 0)
        c = lax.broadcasted_iota(jnp.int32, (N_EXPERTS, N_EXPERTS), 1)
        earlier = jnp.where(r > c, 1.0, 0.0)
        start = jnp.dot(earlier, padded, preferred_element_type=F32, precision=lax.Precision.HIGHEST)
        cnt_ref[...] = cnt.astype(jnp.int32)
        start_ref[...] = start.astype(jnp.int32)
        run_ref[...] = jnp.zeros_like(run_ref)

    @pl.when(phase == 1)
    def _():
        prefixes = _dot(jnp.concatenate([oh.astype(BF16) for oh in onehots], axis=0), before_ref[...])
        start = start_ref[:, 0:1].astype(F32)
        base = run_ref[:, 0:1] + start
        for kk in range(TOP_K):
            prefix = prefixes[kk * N_EXPERTS:(kk + 1) * N_EXPERTS, :]
            dest_ref[kk:kk + 1, :] = jnp.sum(onehots[kk] * (prefix + base), axis=0, keepdims=True).astype(jnp.int32)
            base = base + counts[kk]
        run_ref[...] = jnp.broadcast_to(base - start, run_ref.shape)


def _route(idx_t, block_rows, tt):
    n = idx_t.shape[1]
    kern = functools.partial(_route_kernel, block_rows=block_rows)
    before = jnp.asarray(np.triu(np.ones((tt, tt)), 1), BF16)
    stat = pl.BlockSpec((N_EXPERTS, LANES), lambda p, t: (0, 0))
    return pl.pallas_call(
        kern,
        grid=(2, n // tt),
        in_specs=[pl.BlockSpec((TOP_K, tt), lambda p, t: (0, t)),
                  pl.BlockSpec((tt, tt), lambda p, t: (0, 0))],
        out_specs=[pl.BlockSpec((TOP_K, tt), lambda p, t: (0, t * p)), stat, stat],
        out_shape=[jax.ShapeDtypeStruct((TOP_K, n), jnp.int32),
                   jax.ShapeDtypeStruct((N_EXPERTS, LANES), jnp.int32),
                   jax.ShapeDtypeStruct((N_EXPERTS, LANES), jnp.int32)],
        scratch_shapes=[pltpu.VMEM((N_EXPERTS, LANES), F32)],
        compiler_params=_cparams(("arbitrary", "arbitrary")),
        name="route_ranks",
    )(idx_t, before)


def _to_row_tiles(ref, base, val):
    rows = val.shape[0]
    for s in range(SUBLANES):
        ref[pl.ds(base + s, rows, stride=SUBLANES), :] = val[:, s * LANES:(s + 1) * LANES]


def _from_row_tiles(ref, base, rows):
    return jnp.concatenate([ref[pl.ds(base + s, rows, stride=SUBLANES), :] for s in range(SUBLANES)], axis=1)


def _row_tile(ref, r):
    return ref.at[pl.ds(pl.multiple_of(r * SUBLANES, SUBLANES), SUBLANES), :]


def _row_tiles(ref, r, n):
    return ref.at[pl.ds(pl.multiple_of(r * SUBLANES, SUBLANES), n * SUBLANES), :]


def _dispatch_kernel(cnt_ref, start_ref, dest_ref, x1_ref, sc_ref, sh_ref, h_ref, u_ref, zero_ref, sems, *, block_rows):
    t = pl.program_id(0)
    nt = pl.num_programs(0)
    tt = x1_ref.shape[0]
    slot = t % 2
    ubase = pl.multiple_of(slot * (tt * SUBLANES), tt * SUBLANES)
    _to_row_tiles(u_ref, ubase, x1_ref[...] * (1.0 + sc_ref[0]) + sh_ref[0])

    def issue(i, _):
        for kk in range(TOP_K):
            d = dest_ref[0, 0, kk * tt + i]
            pltpu.make_async_copy(_row_tile(u_ref, slot * tt + i), _row_tile(h_ref, d),
                                  sems.at[slot]).start(priority=kk % DMA_QUEUES)
        return 0

    lax.fori_loop(0, tt, issue, 0, unroll=ISSUE_UNROLL)

    def drain(sl):
        for _ in range(TOP_K):
            pltpu.make_async_copy(_row_tiles(u_ref, sl * tt, tt), _row_tiles(h_ref, 0, tt), sems.at[sl]).wait()

    @pl.when(t > 0)
    def _():
        drain(1 - slot)

    @pl.when(t == nt - 1)
    def _():
        drain(slot)
        zero_ref[...] = jnp.zeros_like(zero_ref)

        def pad_expert(e, _):
            cnt = cnt_ref[e]
            rem = (block_rows - (cnt & (block_rows - 1))) & (block_rows - 1)
            pos = start_ref[e] + cnt
            size = block_rows // 2
            while size >= 1:
                take = rem & size

                @pl.when(take != 0)
                def _(pos=pos, size=size):
                    cp = pltpu.make_async_copy(_row_tiles(zero_ref, 0, size), _row_tiles(h_ref, pos, size), sems.at[2])
                    cp.start()
                    cp.wait()

                pos = pos + take
                size //= 2
            return 0

        lax.fori_loop(0, N_EXPERTS, pad_expert, 0)


def _dispatch(counts, starts, dest_flat, x1, sc2, sh2, n_rows, tt):
    n, d = x1.shape
    b = sc2.shape[0]
    tiles_per_b = (n // b) // tt
    per_b = lambda t, c, s: (t // tiles_per_b, 0, 0)
    grid_spec = pltpu.PrefetchScalarGridSpec(
        num_scalar_prefetch=2,
        grid=(n // tt,),
        in_specs=[pl.BlockSpec((1, 1, tt * TOP_K), lambda t, c, s: (t, 0, 0), memory_space=pltpu.SMEM),
                  pl.BlockSpec((tt, d), lambda t, c, s: (t, 0)),
                  pl.BlockSpec((1, 1, d), per_b),
                  pl.BlockSpec((1, 1, d), per_b)],
        out_specs=pl.BlockSpec(memory_space=pl.ANY),
        scratch_shapes=[pltpu.VMEM((2 * tt * SUBLANES, LANES), F32), pltpu.VMEM((MOE_ROWS // 2 * SUBLANES, LANES), F32),
                        pltpu.SemaphoreType.DMA((3,))],
    )
    return pl.pallas_call(
        functools.partial(_dispatch_kernel, block_rows=MOE_ROWS),
        grid_spec=grid_spec,
        out_shape=jax.ShapeDtypeStruct((n_rows * SUBLANES, LANES), F32),
        compiler_params=_cparams(("arbitrary",)),
        name="moe_dispatch",
    )(counts, starts, dest_flat, x1, sc2, sh2)


def _expert_kernel(be_ref, nu_ref, h_ref, wgu_ref, bgu_ref, wdn_ref, bdn_ref, o_ref, wgu_b_ref, wdn_b_ref):
    j = pl.program_id(0)
    d = wdn_ref.shape[-1]
    bm = h_ref.shape[0] // SUBLANES

    @pl.when(j < nu_ref[0])
    def _():
        @pl.when(jnp.logical_or(j == 0, be_ref[j] != be_ref[jnp.maximum(j - 1, 0)]))
        def _():
            wgu_b_ref[...] = wgu_ref[0].astype(BF16)
            wdn_b_ref[...] = wdn_ref[0].astype(BF16)

        gu = _dot(_from_row_tiles(h_ref, 0, bm).astype(BF16), wgu_b_ref[...]) + bgu_ref[0]
        gate = jnp.minimum(gu[:, :d], SWIGLU_LIMIT)
        up = jnp.clip(gu[:, d:], -SWIGLU_LIMIT, SWIGLU_LIMIT)
        act = (up + 1.0) * gate * _sigmoid(SWIGLU_ALPHA * gate)
        _to_row_tiles(o_ref, 0, _dot(act.astype(BF16), wdn_b_ref[...]) + bdn_ref[0])


def _experts(h_sorted, block_expert, n_used, wgu, bgu, wdn, bdn):
    n_rows = h_sorted.shape[0] // SUBLANES
    e, d, _ = wdn.shape
    bm = MOE_ROWS
    n_blocks = n_rows // bm
    blk = lambda j, be, nu: (jnp.minimum(j, nu[0] - 1), 0)
    exp = lambda j, be, nu: (be[jnp.minimum(j, nu[0] - 1)], 0, 0)
    grid_spec = pltpu.PrefetchScalarGridSpec(
        num_scalar_prefetch=2,
        grid=(n_blocks,),
        in_specs=[pl.BlockSpec((bm * SUBLANES, LANES), blk),
                  pl.BlockSpec((1, d, 2 * d), exp),
                  pl.BlockSpec((1, 1, 2 * d), exp),
                  pl.BlockSpec((1, d, d), exp),
                  pl.BlockSpec((1, 1, d), exp)],
        out_specs=pl.BlockSpec((bm * SUBLANES, LANES), blk),
        scratch_shapes=[pltpu.VMEM((d, 2 * d), BF16), pltpu.VMEM((d, d), BF16)],
    )
    return pl.pallas_call(
        _expert_kernel,
        grid_spec=grid_spec,
        out_shape=jax.ShapeDtypeStruct((n_rows * SUBLANES, LANES), F32),
        compiler_params=_cparams(("arbitrary",), VMEM_LIMIT),
        name="moe_experts",
    )(block_expert, n_used, h_sorted, wgu, bgu.reshape(e, 1, 2 * d), wdn, bdn.reshape(e, 1, d))


def _combine_kernel(dest_ref, dest_next_ref, wt_ref, x1_ref, g2_ref, lg_ref, lbias_ref, o_hbm, y_ref, buf_ref, sems):
    t = pl.program_id(0)
    nt = pl.num_programs(0)
    tt = x1_ref.shape[0]
    slot = t % 2

    def issue_row(d_ref, sl, i):
        for kk in range(TOP_K):
            d = d_ref[0, 0, kk * tt + i]
            pltpu.make_async_copy(_row_tile(o_hbm, d), _row_tile(buf_ref, (sl * TOP_K + kk) * tt + i),
                                  sems.at[sl]).start(priority=kk % DMA_QUEUES)

    def drain(sl):
        for kk in range(TOP_K):
            pltpu.make_async_copy(_row_tiles(o_hbm, 0, tt), _row_tiles(buf_ref, (sl * TOP_K + kk) * tt, tt),
                                  sems.at[sl]).wait()

    @pl.when(t == 0)
    def _():
        def first(i, _):
            issue_row(dest_ref, 0, i)
            return 0

        lax.fori_loop(0, tt, first, 0, unroll=ISSUE_UNROLL)

    drain(slot)

    def fused(g, _):
        r0 = pl.multiple_of(g * COMBINE_ROWS, COMBINE_ROWS)
        for i in range(COMBINE_ROWS):
            issue_row(dest_next_ref, 1 - slot, r0 + i)
        rows = pl.ds(r0, COMBINE_ROWS)
        ff = None
        for kk in range(TOP_K):
            base = pl.multiple_of(((slot * TOP_K + kk) * tt + r0) * SUBLANES, COMBINE_ROWS * SUBLANES)
            term = wt_ref[rows, kk:kk + 1] * _from_row_tiles(buf_ref, base, COMBINE_ROWS)
            ff = term if ff is None else ff + term
        y_ref[rows, :] = _layer_norm(DEEPNORM_ALPHA * x1_ref[rows, :] + g2_ref[0] * ff, lg_ref[...], lbias_ref[...])
        return 0

    lax.fori_loop(0, tt // COMBINE_ROWS, fused, 0)

    @pl.when(t == nt - 1)
    def _():
        drain(1 - slot)


def _combine(dest_flat, wt, x1, g2, lg, lbias, o_sorted, tt):
    n, d = x1.shape
    b = g2.shape[0]
    nt = n // tt
    tiles_per_b = (n // b) // tt
    const = lambda t: (0, 0)
    dest_spec = lambda im: pl.BlockSpec((1, 1, tt * TOP_K), im, memory_space=pltpu.SMEM)
    return pl.pallas_call(
        _combine_kernel,
        grid=(nt,),
        in_specs=[dest_spec(lambda t: (t, 0, 0)),
                  dest_spec(lambda t: (jnp.minimum(t + 1, nt - 1), 0, 0)),
                  pl.BlockSpec((tt, TOP_K), lambda t: (t, 0)),
                  pl.BlockSpec((tt, d), lambda t: (t, 0)),
                  pl.BlockSpec((1, 1, d), lambda t: (t // tiles_per_b, 0, 0)),
                  pl.BlockSpec((1, d), const),
                  pl.BlockSpec((1, d), const),
                  pl.BlockSpec(memory_space=pl.ANY)],
        out_specs=pl.BlockSpec((tt, d), lambda t: (t, 0)),
        out_shape=jax.ShapeDtypeStruct((n, d), F32),
        scratch_shapes=[pltpu.VMEM((2 * TOP_K * tt * SUBLANES, LANES), F32), pltpu.SemaphoreType.DMA((2,))],
        compiler_params=_cparams(("arbitrary",), VMEM_LIMIT),
        name="moe_combine",
    )(dest_flat, dest_flat, wt, x1, g2, lg, lbias, o_sorted)


def kernel(x, c, ctx, c_ctx, w_ada, b_ada, w_in, lb_raw, hg_norm_g, w_four_out, w_hg_out, w_o, ln1_g, ln1_b,
           w_router, b_router, w_gate_up, b_gate_up, w_down, b_down, ln2_g, ln2_b):
    b, s, d = x.shape
    lc = ctx.shape[1]
    n = b * s
    f_w = F_GROUPS * F_GROUP_DIM
    hk = HG_HEADS * HG_D
    assert w_ada.shape[0] == DEPTH and s % GRID_W == 0 and s % (HG_TILE * HG_GROUP) == 0 and s % 512 == 0 and lc % HG_TILE == 0

    pad = (-(b + 1)) % 8
    c_rows = jnp.concatenate([c, c_ctx[None, :], jnp.zeros((pad, d), F32)], axis=0)
    mod = _ada(c_rows, w_ada[0], b_ada[0]).reshape(b + 1 + pad, N_MOD, 1, d)
    shift1, scale1, gate1, shift2, scale2, gate2 = (mod[:b, i] for i in range(N_MOD))
    cshift1 = jnp.broadcast_to(mod[b, 0], (b, 1, d))
    cscale1 = jnp.broadcast_to(mod[b, 1], (b, 1, d))

    lower_bounds = jnp.cumsum(jax.nn.softmax(lb_raw.astype(F32), axis=0), axis=0)[0]

    w_in_b = w_in[0].astype(BF16)
    o_f, o_q, o_z, o_v, o_g, o_gt = f_w, f_w + hk, f_w + 3 * hk, f_w + 4 * hk, f_w + 5 * hk, f_w + 5 * hk + 2 * d
    lb_row = lower_bounds.reshape(1, 2 * hk)
    pf, pq, pk, phi, plo, pv, pg, gates = _inproj(
        x, scale1, shift1, lb_row, w_in_b,
        [(0, f_w, "cast"), (o_f, hk, "silu"), (o_q, 2 * hk, "forget"), (o_z, hk, "cast"), (o_v, hk, "cast"),
         (o_g, 2 * d, "cast")], tm=1024)
    w_ctx = w_in_b[:, o_q:o_v]
    ck, chi, clo, cv = _inproj(ctx, cscale1, cshift1, lb_row, w_ctx, [(0, 2 * hk, "forget"), (2 * hk, hk, "cast")], tm=lc)

    y_h = _hgrn(pq, pk, phi, plo, pv, pg, ck, chi, clo, cv, hg_norm_g[0].reshape(1, hk))
    y_f = _fourier(pf)

    wr = w_router[0].T.astype(BF16)
    br = b_router[0].reshape(N_EXPERTS, 1)
    x1, idx_t, wt_t = _merge(y_f, y_h, gates, x, gate1, scale2, shift2,
                         w_four_out[0].astype(BF16), w_hg_out[0].astype(BF16), w_o[0].astype(BF16),
                         ln1_g[0].reshape(1, d), ln1_b[0].reshape(1, d), wr, br, tm=1024)

    dest_t, counts, starts = _route(idx_t, MOE_ROWS, 1024)
    n_blocks = -(-(n * TOP_K) // MOE_ROWS) + N_EXPERTS
    n_rows = n_blocks * MOE_ROWS
    cnt = counts[:, 0]
    st = starts[:, 0]
    pad_ends = st + (cnt + MOE_ROWS - 1) // MOE_ROWS * MOE_ROWS
    block_first_row = jnp.arange(n_blocks, dtype=jnp.int32) * MOE_ROWS
    block_expert = jnp.minimum(jnp.sum((pad_ends[None, :] <= block_first_row[:, None]).astype(jnp.int32), axis=1),
                               N_EXPERTS - 1)
    n_used = (pad_ends[-1] // MOE_ROWS).astype(jnp.int32).reshape(1)

    tt = 256
    dest_flat = dest_t.reshape(TOP_K, n // tt, tt).transpose(1, 0, 2).reshape(n // tt, 1, TOP_K * tt)
    x1_2d = x1.reshape(n, d)
    h_sorted = _dispatch(cnt, st, dest_flat, x1_2d, scale2, shift2, n_rows, tt)
    o_sorted = _experts(h_sorted, block_expert, n_used, w_gate_up[0], b_gate_up[0], w_down[0], b_down[0])
    out = _combine(dest_flat, wt_t.T, x1_2d, gate2, ln2_g[0].reshape(1, d), ln2_b[0].reshape(1, d),
                   o_sorted, tt)
    return out.reshape(b, s, d)
```

```python
import functools
import math

import jax
import jax.numpy as jnp
import numpy as np
from jax import lax
from jax.experimental import pallas as pl
from jax.experimental.pallas import tpu as pltpu

F32 = jnp.float32
BF16 = jnp.bfloat16

GRID_W = 64
N_MOD = 6
F_GROUPS = 4
F_GROUP_DIM = 128
HG_HEADS = 4
HG_D = 128
CHUNK = 64
N_EXPERTS = 32
TOP_K = 4
SWIGLU_LIMIT = 7.0
SWIGLU_ALPHA = 1.702
LN_EPS = 1e-5
RMS_EPS = 1e-6
DEPTH = 1
DEEPNORM_ALPHA = (2.0 * DEPTH) ** 0.25

LANES = 128
SUBLANES = 8
ROW_TILE = 1024
MOE_TOKENS = 256
MOE_ROWS = 512
DISPATCH_ROWS = 64
COMBINE_ROWS = 128
ISSUE_UNROLL = 4
DMA_QUEUES = 2
VMEM_LIMIT = 56 * 1024 * 1024


def _cparams(sem, vmem=None):
    return pltpu.CompilerParams(dimension_semantics=sem, vmem_limit_bytes=vmem)


def _dot(a, b):
    return jnp.dot(a, b, preferred_element_type=F32)


def _sigmoid(x):
    return 0.5 * jnp.tanh(0.5 * x) + 0.5


def _ada_kernel(c_ref, w_ref, b_ref, o_ref):
    c = c_ref[...]
    a = c * _sigmoid(c)
    o_ref[...] = jnp.dot(a, w_ref[...], preferred_element_type=F32,
                         precision=lax.Precision.HIGHEST) + b_ref[...]


def _ada(c_rows, w, b):
    r, d = c_rows.shape
    n = w.shape[1]
    tn = 512
    return pl.pallas_call(
        _ada_kernel,
        grid=(n // tn,),
        in_specs=[pl.BlockSpec((r, d), lambda j: (0, 0)),
                  pl.BlockSpec((d, tn), lambda j: (0, j)),
                  pl.BlockSpec((1, tn), lambda j: (0, j))],
        out_specs=pl.BlockSpec((r, tn), lambda j: (0, j)),
        out_shape=jax.ShapeDtypeStruct((r, n), F32),
        compiler_params=_cparams(("arbitrary",)),
        name="ada_mod",
    )(c_rows, w, b.reshape(1, n))


def _inproj_kernel(x_ref, sc_ref, sh_ref, lb_ref, w_ref, *o_refs, plan):
    u = (x_ref[0] * (1.0 + sc_ref[0]) + sh_ref[0]).astype(BF16)
    refs = iter(o_refs)
    for kind, chunks in plan:
        if kind == "forget":
            k_ref, hi_ref, lo_ref = next(refs), next(refs), next(refs)
            for (w0, o0, n) in chunks:
                lb = lb_ref[:, o0:o0 + n]
                f = lb + (1.0 - lb) * _sigmoid(_dot(u, w_ref[:, w0:w0 + n]))
                lf = jnp.log(f)
                hi = lf.astype(BF16)
                k_ref[0, :, o0:o0 + n] = (1.0 - f).astype(BF16)
                hi_ref[0, :, o0:o0 + n] = hi
                lo_ref[0, :, o0:o0 + n] = (lf - hi.astype(F32)).astype(BF16)
        else:
            o_ref = next(refs)
            for (w0, o0, n) in chunks:
                p = _dot(u, w_ref[:, w0:w0 + n])
                if kind == "silu":
                    p = p * _sigmoid(p)
                o_ref[0, :, o0:o0 + n] = p.astype(o_ref.dtype)


def _inproj(x, scale, shift, lb_row, w_bf16, outs, tm):
    b, s, d = x.shape
    n_w = w_bf16.shape[1]
    plan, widths = [], []
    for (c0, width, kind) in outs:
        step = min(width, 512)
        plan.append((kind, tuple((c0 + o, o, step) for o in range(0, width, step))))
        widths += [width] * (3 if kind == "forget" else 1)
    kern = functools.partial(_inproj_kernel, plan=tuple(plan))
    return pl.pallas_call(
        kern,
        grid=(b, s // tm),
        in_specs=[pl.BlockSpec((1, tm, d), lambda i, j: (i, j, 0)),
                  pl.BlockSpec((1, 1, d), lambda i, j: (i, 0, 0)),
                  pl.BlockSpec((1, 1, d), lambda i, j: (i, 0, 0)),
                  pl.BlockSpec(lb_row.shape, lambda i, j: (0, 0)),
                  pl.BlockSpec((d, n_w), lambda i, j: (0, 0), pipeline_mode=pl.Buffered(1))],
        out_specs=[pl.BlockSpec((1, tm, width), lambda i, j: (i, j, 0)) for width in widths],
        out_shape=[jax.ShapeDtypeStruct((b, s, width), BF16) for width in widths],
        compiler_params=_cparams(("parallel", "parallel"), VMEM_LIMIT),
        name="in_proj",
    )(x, scale, shift, lb_row, w_bf16)


HG_TILE = 256
HG_GROUP = 4


def _gla_group(jobs, states):
    states = list(states)
    nj = len(jobs)
    r, dk = jobs[0][1].shape
    dv = jobs[0][4].shape[1]
    nc = r // CHUNK
    zero = jnp.zeros((CHUNK, dk), BF16)
    contract_last = (((1,), (1,)), ((), ()))
    contract_rows = (((0,), (0,)), ((), ()))

    cums = []
    for (_, hi, lo, _, _, _, tri, _, _) in jobs:
        both = _dot(tri, jnp.concatenate([hi, lo], axis=1))
        cums.append(both[:, :dk] + both[:, dk:])

    k_inv_b, decays, k_end_blk, q_dec = [], [], [], []
    for cum, (_, _, _, k, _, q, _, _, forward) in zip(cums, jobs):
        k_inv = k.astype(F32) * jnp.exp(-cum)
        k_inv_b.append(k_inv.astype(BF16))
        tot_row = CHUNK - 1 if forward else 0
        dec_j, rows_j = [], []
        for ci in range(nc):
            dec = jnp.exp(cum[ci * CHUNK + tot_row:ci * CHUNK + tot_row + 1, :])
            k_end = (k_inv[ci * CHUNK:(ci + 1) * CHUNK] * dec).astype(BF16)
            dec_j.append(dec)
            rows_j.append(jnp.concatenate([k_end if cj == ci else zero for cj in range(nc)], axis=1))
        decays.append(dec_j)
        k_end_blk.append(jnp.concatenate(rows_j, axis=0))
        q_dec.append(None if q is None else (q.astype(F32) * jnp.exp(cum)).astype(BF16))

    upd_all = [lax.dot_general(job[4], blk, contract_rows, preferred_element_type=F32)
               for job, blk in zip(jobs, k_end_blk)]
    scores = [None if qd is None else lax.dot_general(qd, kb, contract_last, preferred_element_type=F32)
              for qd, kb in zip(q_dec, k_inv_b)]

    s_all = []
    for ji in range(nj):
        chain, forward = jobs[ji][0], jobs[ji][8]
        st = states[chain]
        entering = [None] * nc
        for ci in (range(nc) if forward else reversed(range(nc))):
            entering[ci] = st
            st = st * decays[ji][ci] + upd_all[ji][:, ci * dk:(ci + 1) * dk]
        states[chain] = st
        s_all.append(None if q_dec[ji] is None
                     else [jnp.concatenate([entering[ci].astype(BF16), entering[ci + 1].astype(BF16)], axis=0)
                           for ci in range(0, nc, 2)])

    outs = []
    for ji in range(nj):
        if q_dec[ji] is None:
            outs.append(None)
            continue
        sc = jnp.where(jobs[ji][7] > 0, scores[ji], 0.0).astype(BF16)
        intra = _dot(sc, jobs[ji][4])
        inter = []
        for pi, s_pair in enumerate(s_all[ji]):
            rows = slice(2 * pi * CHUNK, (2 * pi + 2) * CHUNK)
            both = lax.dot_general(q_dec[ji][rows], s_pair, contract_last, preferred_element_type=F32)
            inter += [both[:CHUNK, :dv], both[CHUNK:, dv:]]
        outs.append(intra + jnp.concatenate(inter, axis=0))
    return outs, states


def _hgrn_kernel(q_ref, kf_ref, kb_ref, hif_ref, hib_ref, lof_ref, lob_ref, v_ref, g_ref,
                 ckf_ref, ckb_ref, chif_ref, chib_ref, clof_ref, clob_ref, cv_ref, ng_ref, trif_ref, trib_ref,
                 keepf_ref, keepb_ref, y_ref, of_ref, ob_ref, *, seq, ctx_len):
    r = HG_TILE
    n_lat = seq // r
    n_ctx = ctx_len // r
    group = HG_GROUP

    def ctx_jobs():
        jobs = []
        for i in range(n_ctx):
            sl_f = pl.ds(i * r, r)
            sl_b = pl.ds((n_ctx - 1 - i) * r, r)
            jobs.append((0, chif_ref[0, sl_f, :], clof_ref[0, sl_f, :], ckf_ref[0, sl_f, :], cv_ref[0, sl_f, :], None,
                         trif_ref[...], None, True))
            jobs.append((1, chib_ref[0, sl_b, :], clob_ref[0, sl_b, :], ckb_ref[0, sl_b, :], cv_ref[0, sl_b, :], None,
                         trib_ref[...], None, False))
        return jobs

    ng = ng_ref[...]

    def readout(tile, o):
        sl = pl.ds(tile * r, r)
        o = o * lax.rsqrt(jnp.mean(o * o, axis=-1, keepdims=True) + RMS_EPS) * ng
        g = g_ref[0, sl, :].astype(F32)
        y_ref[0, sl, :] = (o * (g * _sigmoid(g))).astype(y_ref.dtype)

    def step_of(tile, forward):
        return (tile if forward else n_lat - 1 - tile) // group

    carry = (jnp.zeros((HG_D, HG_D), F32),) * 2
    for i in range(n_lat // group):
        jobs = ctx_jobs() if i == 0 else []
        meta = [None] * len(jobs)
        for t in range(group):
            for forward in (True, False):
                tile = i * group + t if forward else n_lat - 1 - (i * group + t)
                sl = pl.ds(tile * r, r)
                hi_ref, lo_ref, k_ref, tri_ref, keep_ref = ((hif_ref, lof_ref, kf_ref, trif_ref, keepf_ref) if forward
                                                            else (hib_ref, lob_ref, kb_ref, trib_ref, keepb_ref))
                jobs.append((0 if forward else 1, hi_ref[0, sl, :], lo_ref[0, sl, :], k_ref[0, sl, :], v_ref[0, sl, :],
                             q_ref[0, sl, :], tri_ref[...], keep_ref[...], forward))
                meta.append((tile, forward))
        outs, carry = _gla_group(jobs, carry)
        waiting = {}
        for m, o in zip(meta, outs):
            if m is None:
                continue
            tile, forward = m
            mine, other = (of_ref, ob_ref) if forward else (ob_ref, of_ref)
            other_step = step_of(tile, not forward)
            if other_step < i:
                readout(tile, o + other[pl.ds(tile * r, r), :])
            elif other_step > i:
                mine[pl.ds(tile * r, r), :] = o
            elif tile in waiting:
                readout(tile, o + waiting.pop(tile))
            else:
                waiting[tile] = o


def _chunk_tri(r, forward, dtype):
    i = np.arange(r)
    same = (i[:, None] // CHUNK) == (i[None, :] // CHUNK)
    order = (i[:, None] >= i[None, :]) if forward else (i[:, None] <= i[None, :])
    return jnp.asarray(same & order, dtype)


def _hgrn(q, k, hi, lo, v, g, ck, chi, clo, cv, ng):
    b, s, _ = q.shape
    lc = ck.shape[1]
    h = HG_HEADS
    d = HG_D
    r = HG_TILE
    kern = functools.partial(_hgrn_kernel, seq=s, ctx_len=lc)
    fwd = lambda i, j: (i, 0, j)
    bwd = lambda i, j: (i, 0, j + h)
    const = lambda i, j: (0, 0)
    lat = lambda im: pl.BlockSpec((1, s, d), im)
    cx = lambda im: pl.BlockSpec((1, lc, d), im)
    return pl.pallas_call(
        kern,
        grid=(b, h),
        in_specs=[lat(fwd), lat(fwd), lat(bwd), lat(fwd), lat(bwd), lat(fwd), lat(bwd), lat(fwd), lat(fwd),
                  cx(fwd), cx(bwd), cx(fwd), cx(bwd), cx(fwd), cx(bwd), cx(fwd),
                  pl.BlockSpec((1, d), lambda i, j: (0, j)),
                  pl.BlockSpec((r, r), const),
                  pl.BlockSpec((r, r), const),
                  pl.BlockSpec((r, r), const),
                  pl.BlockSpec((r, r), const)],
        out_specs=pl.BlockSpec((1, s, d), fwd),
        out_shape=jax.ShapeDtypeStruct((b, s, h * d), BF16),
        scratch_shapes=[pltpu.VMEM((s, d), F32), pltpu.VMEM((s, d), F32)],
        compiler_params=_cparams(("parallel", "parallel"), VMEM_LIMIT),
        name="hgrn2",
    )(q, k, k, hi, hi, lo, lo, v, g, ck, ck, chi, chi, clo, clo, cv, ng,
      _chunk_tri(r, True, BF16), _chunk_tri(r, False, BF16), _chunk_tri(r, True, F32), _chunk_tri(r, False, F32))


def _dft_tables(seq):
    rows = seq // GRID_W
    n = F_GROUP_DIM
    kc = np.outer(np.arange(n), np.arange(n)) % n
    ang = 2.0 * np.pi * kc / n
    norm = 1.0 / math.sqrt(rows * GRID_W * n)
    ch = np.concatenate([np.cos(ang), -np.sin(ang)], axis=1) * norm
    t = np.arange(seq)
    r, w = t // GRID_W, t % GRID_W
    m = (np.outer(r, r) * GRID_W + np.outer(w, w) * rows) % (rows * GRID_W)
    ang_t = 2.0 * np.pi * m / (rows * GRID_W)
    tok = np.concatenate([np.cos(ang_t), np.sin(ang_t)], axis=1)
    return jnp.asarray(ch, BF16), jnp.asarray(tok, BF16)


def _fourier_kernel(p_ref, ch_ref, tok_ref, y_ref, xs_ref, *, seq):
    for gi in range(F_GROUPS):
        cols = slice(gi * F_GROUP_DIM, (gi + 1) * F_GROUP_DIM)
        x1 = _dot(p_ref[0, :, cols], ch_ref[...]).astype(BF16)
        xs_ref[0:seq, cols] = x1[:, :F_GROUP_DIM]
        xs_ref[seq:2 * seq, cols] = x1[:, F_GROUP_DIM:]
    y_ref[0] = _dot(tok_ref[...], xs_ref[...]).astype(y_ref.dtype)


def _fourier(pf):
    b, s, wdt = pf.shape
    ch, tok = _dft_tables(s)
    kern = functools.partial(_fourier_kernel, seq=s)
    return pl.pallas_call(
        kern,
        grid=(b,),
        in_specs=[pl.BlockSpec((1, s, wdt), lambda i: (i, 0, 0)),
                  pl.BlockSpec(ch.shape, lambda i: (0, 0)),
                  pl.BlockSpec(tok.shape, lambda i: (0, 0), pipeline_mode=pl.Buffered(1))],
        out_specs=pl.BlockSpec((1, s, wdt), lambda i: (i, 0, 0)),
        out_shape=jax.ShapeDtypeStruct((b, s, wdt), BF16),
        scratch_shapes=[pltpu.VMEM((2 * s, wdt), BF16)],
        compiler_params=_cparams(("parallel",), VMEM_LIMIT),
        name="fourier",
    )(pf, ch, tok)


def _layer_norm(t, g, b):
    mu = jnp.mean(t, axis=-1, keepdims=True)
    tc = t - mu
    var = jnp.mean(tc * tc, axis=-1, keepdims=True)
    return tc * lax.rsqrt(var + LN_EPS) * g + b


def _merge_kernel(yf_ref, yh_ref, gt_ref, x_ref, g1_ref, sc2_ref, sh2_ref, wfo_ref, who_ref, wo_ref,
                  lg_ref, lbias_ref, wr_ref, br_ref, x1_ref, idx_ref, wt_ref):
    d = x_ref.shape[-1]
    gf = _sigmoid(gt_ref[0, :, :d].astype(F32))
    gh = _sigmoid(gt_ref[0, :, d:].astype(F32))
    m = gf * _dot(yf_ref[0], wfo_ref[...]) + gh * _dot(yh_ref[0], who_ref[...])
    mix = _dot(m.astype(BF16), wo_ref[...])
    x1 = _layer_norm(DEEPNORM_ALPHA * x_ref[0] + g1_ref[0] * mix, lg_ref[...], lbias_ref[...])
    x1_ref[0] = x1
    u2 = x1 * (1.0 + sc2_ref[0]) + sh2_ref[0]
    logits = lax.dot_general(wr_ref[...], u2.astype(BF16), (((1,), (1,)), ((), ())),
                             preferred_element_type=F32) + br_ref[...]
    expert = lax.broadcasted_iota(jnp.int32, logits.shape, 0)
    expert_f = expert.astype(F32)
    work = logits
    vals, idxs = [], []
    for _ in range(TOP_K):
        mx = jnp.max(work, axis=0, keepdims=True)
        sel = jnp.min(jnp.where(work == mx, expert_f, float(N_EXPERTS)), axis=0, keepdims=True).astype(jnp.int32)
        vals.append(mx)
        idxs.append(sel)
        work = jnp.where(expert == sel, -jnp.inf, work)
    exps = [jnp.exp(vv - vals[0]) for vv in vals]
    inv = 1.0 / (exps[0] + exps[1] + exps[2] + exps[3])
    for kk in range(TOP_K):
        idx_ref[kk:kk + 1, :] = idxs[kk]
        wt_ref[kk:kk + 1, :] = exps[kk] * inv


def _merge(yf, yh, gates, x, g1, sc2, sh2, wfo, who, wo, lg, lbias, wr, br, tm):
    b, s, d = x.shape
    tile = lambda i, j: (i, j, 0)
    per_b = lambda i, j: (i, 0, 0)
    const = lambda i, j: (0, 0)
    slots = lambda i, j: (0, i * (s // tm) + j)
    return pl.pallas_call(
        _merge_kernel,
        grid=(b, s // tm),
        in_specs=[pl.BlockSpec((1, tm, yf.shape[-1]), tile),
                  pl.BlockSpec((1, tm, yh.shape[-1]), tile),
                  pl.BlockSpec((1, tm, gates.shape[-1]), tile),
                  pl.BlockSpec((1, tm, d), tile),
                  pl.BlockSpec((1, 1, d), per_b),
                  pl.BlockSpec((1, 1, d), per_b),
                  pl.BlockSpec((1, 1, d), per_b),
                  pl.BlockSpec(wfo.shape, const),
                  pl.BlockSpec(who.shape, const),
                  pl.BlockSpec(wo.shape, const),
                  pl.BlockSpec((1, d), const),
                  pl.BlockSpec((1, d), const),
                  pl.BlockSpec(wr.shape, const),
                  pl.BlockSpec(br.shape, const)],
        out_specs=[pl.BlockSpec((1, tm, d), tile),
                   pl.BlockSpec((TOP_K, tm), slots),
                   pl.BlockSpec((TOP_K, tm), slots)],
        out_shape=[jax.ShapeDtypeStruct((b, s, d), F32),
                   jax.ShapeDtypeStruct((TOP_K, b * s), jnp.int32),
                   jax.ShapeDtypeStruct((TOP_K, b * s), F32)],
        compiler_params=_cparams(("parallel", "parallel"), VMEM_LIMIT),
        name="merge_ln_router",
    )(yf, yh, gates, x, g1, sc2, sh2, wfo, who, wo, lg, lbias, wr, br)


def _route_kernel(idx_ref, before_ref, dest_ref, cnt_ref, start_ref, run_ref, *, block_rows):
    phase = pl.program_id(0)
    t = pl.program_id(1)
    tt = idx_ref.shape[1]
    expert = lax.broadcasted_iota(jnp.int32, (N_EXPERTS, tt), 0)
    onehots = [jnp.where(expert == idx_ref[kk:kk + 1, :], 1.0, 0.0) for kk in range(TOP_K)]
    counts = [jnp.sum(oh, axis=1, keepdims=True) for oh in onehots]

    @pl.when(jnp.logical_and(phase == 0, t == 0))
    def _():
        run_ref[...] = jnp.zeros_like(run_ref)

    @pl.when(phase == 0)
    def _():
        run_ref[...] += counts[0] + counts[1] + counts[2] + counts[3]

    @pl.when(jnp.logical_and(phase == 1, t == 0))
    def _():
        cnt = run_ref[...]
        padded = jnp.floor((cnt + (block_rows - 1)) * (1.0 / block_rows)) * block_rows
        r = lax.broadcasted_iota(jnp.int32, (N_EXPERTS, N_EXPERTS), 0)
        c = lax.broadcasted_iota(jnp.int32, (N_EXPERTS, N_EXPERTS), 1)
        earlier = jnp.where(r > c, 1.0, 0.0)
        start = jnp.dot(earlier, padded, preferred_element_type=F32, precision=lax.Precision.HIGHEST)
        cnt_ref[...] = cnt.astype(jnp.int32)
        start_ref[...] = start.astype(jnp.int32)
        run_ref[...] = jnp.zeros_like(run_ref)

    @pl.when(phase == 1)
    def _():
        prefixes = _dot(jnp.concatenate([oh.astype(BF16) for oh in onehots], axis=0), before_ref[...])
        start = start_ref[:, 0:1].astype(F32)
        base = run_ref[:, 0:1] + start
        for kk in range(TOP_K):
            prefix = prefixes[kk * N_EXPERTS:(kk + 1) * N_EXPERTS, :]
            dest_ref[kk:kk + 1, :] = jnp.sum(onehots[kk] * (prefix + base), axis=0, keepdims=True).astype(jnp.int32)
            base = base + counts[kk]
        run_ref[...] = jnp.broadcast_to(base - start, run_ref.shape)


def _route(idx_t, block_rows, tt):
    n = idx_t.shape[1]
    kern = functools.partial(_route_kernel, block_rows=block_rows)
    before = jnp.asarray(np.triu(np.ones((tt, tt)), 1), BF16)
    stat = pl.BlockSpec((N_EXPERTS, LANES), lambda p, t: (0, 0))
    return pl.pallas_call(
        kern,
        grid=(2, n // tt),
        in_specs=[pl.BlockSpec((TOP_K, tt), lambda p, t: (0, t)),
                  pl.BlockSpec((tt, tt), lambda p, t: (0, 0))],
        out_specs=[pl.BlockSpec((TOP_K, tt), lambda p, t: (0, t * p)), stat, stat],
        out_shape=[jax.ShapeDtypeStruct((TOP_K, n), jnp.int32),
                   jax.ShapeDtypeStruct((N_EXPERTS, LANES), jnp.int32),
                   jax.ShapeDtypeStruct((N_EXPERTS, LANES), jnp.int32)],
        scratch_shapes=[pltpu.VMEM((N_EXPERTS, LANES), F32)],
        compiler_params=_cparams(("arbitrary", "arbitrary")),
        name="route_ranks",
    )(idx_t, before)


def _to_row_tiles(ref, base, val):
    rows = val.shape[0]
    for s in range(SUBLANES):
        ref[pl.ds(base + s, rows, stride=SUBLANES), :] = val[:, s * LANES:(s + 1) * LANES]


def _from_row_tiles(ref, base, rows):
    return jnp.concatenate([ref[pl.ds(base + s, rows, stride=SUBLANES), :] for s in range(SUBLANES)], axis=1)


def _row_tile(ref, r):
    return ref.at[pl.ds(pl.multiple_of(r * SUBLANES, SUBLANES), SUBLANES), :]


def _row_tiles(ref, r, n):
    return ref.at[pl.ds(pl.multiple_of(r * SUBLANES, SUBLANES), n * SUBLANES), :]


def _dispatch_kernel(cnt_ref, start_ref, dest_ref, x1_ref, sc_ref, sh_ref, h_ref, u_ref, zero_ref, sems, *, block_rows):
    t = pl.program_id(0)
    nt = pl.num_programs(0)
    tt = x1_ref.shape[0]
    slot = t % 2
    ubase = pl.multiple_of(slot * (tt * SUBLANES), tt * SUBLANES)

    def convert(c0):
        rows = pl.ds(c0, DISPATCH_ROWS)
        _to_row_tiles(u_ref, ubase + c0 * SUBLANES, x1_ref[rows, :] * (1.0 + sc_ref[0]) + sh_ref[0])

    convert(0)
    for c0 in range(0, tt, DISPATCH_ROWS):
        if c0 + DISPATCH_ROWS < tt:
            convert(c0 + DISPATCH_ROWS)
        for i in range(c0, c0 + DISPATCH_ROWS):
            for kk in range(TOP_K):
                d = dest_ref[0, 0, kk * tt + i]
                pltpu.make_async_copy(_row_tile(u_ref, slot * tt + i), _row_tile(h_ref, d),
                                      sems.at[slot]).start(priority=kk % DMA_QUEUES)

    def drain(sl):
        for _ in range(TOP_K):
            pltpu.make_async_copy(_row_tiles(u_ref, sl * tt, tt), _row_tiles(h_ref, 0, tt), sems.at[sl]).wait()

    @pl.when(t > 0)
    def _():
        drain(1 - slot)

    @pl.when(t == nt - 1)
    def _():
        drain(slot)
        zero_ref[...] = jnp.zeros_like(zero_ref)

        def pad_expert(e, _):
            cnt = cnt_ref[e]
            rem = (block_rows - (cnt & (block_rows - 1))) & (block_rows - 1)
            pos = start_ref[e] + cnt
            size = block_rows // 2
            while size >= 1:
                take = rem & size

                @pl.when(take != 0)
                def _(pos=pos, size=size):
                    cp = pltpu.make_async_copy(_row_tiles(zero_ref, 0, size), _row_tiles(h_ref, pos, size), sems.at[2])
                    cp.start()
                    cp.wait()

                pos = pos + take
                size //= 2
            return 0

        lax.fori_loop(0, N_EXPERTS, pad_expert, 0)


def _dispatch(counts, starts, dest_flat, x1, sc2, sh2, n_rows, tt):
    n, d = x1.shape
    b = sc2.shape[0]
    tiles_per_b = (n // b) // tt
    per_b = lambda t, c, s: (t // tiles_per_b, 0, 0)
    grid_spec = pltpu.PrefetchScalarGridSpec(
        num_scalar_prefetch=2,
        grid=(n // tt,),
        in_specs=[pl.BlockSpec((1, 1, tt * TOP_K), lambda t, c, s: (t, 0, 0), memory_space=pltpu.SMEM),
                  pl.BlockSpec((tt, d), lambda t, c, s: (t, 0)),
                  pl.BlockSpec((1, 1, d), per_b),
                  pl.BlockSpec((1, 1, d), per_b)],
        out_specs=pl.BlockSpec(memory_space=pl.ANY),
        scratch_shapes=[pltpu.VMEM((2 * tt * SUBLANES, LANES), F32), pltpu.VMEM((MOE_ROWS // 2 * SUBLANES, LANES), F32),
                        pltpu.SemaphoreType.DMA((3,))],
    )
    return pl.pallas_call(
        functools.partial(_dispatch_kernel, block_rows=MOE_ROWS),
        grid_spec=grid_spec,
        out_shape=jax.ShapeDtypeStruct((n_rows * SUBLANES, LANES), F32),
        compiler_params=_cparams(("arbitrary",)),
        name="moe_dispatch",
    )(counts, starts, dest_flat, x1, sc2, sh2)


def _expert_kernel(be_ref, nu_ref, h_ref, wgu_ref, bgu_ref, wdn_ref, bdn_ref, o_ref, wgu_b_ref, wdn_b_ref):
    j = pl.program_id(0)
    d = wdn_ref.shape[-1]
    bm = h_ref.shape[0] // SUBLANES

    @pl.when(j < nu_ref[0])
    def _():
        @pl.when(jnp.logical_or(j == 0, be_ref[j] != be_ref[jnp.maximum(j - 1, 0)]))
        def _():
            wgu_b_ref[...] = wgu_ref[0].astype(BF16)
            wdn_b_ref[...] = wdn_ref[0].astype(BF16)

        gu = _dot(_from_row_tiles(h_ref, 0, bm).astype(BF16), wgu_b_ref[...]) + bgu_ref[0]
        gate = jnp.minimum(gu[:, :d], SWIGLU_LIMIT)
        up = jnp.clip(gu[:, d:], -SWIGLU_LIMIT, SWIGLU_LIMIT)
        act = (up + 1.0) * gate * _sigmoid(SWIGLU_ALPHA * gate)
        _to_row_tiles(o_ref, 0, _dot(act.astype(BF16), wdn_b_ref[...]) + bdn_ref[0])


def _experts(h_sorted, block_expert, n_used, wgu, bgu, wdn, bdn):
    n_rows = h_sorted.shape[0] // SUBLANES
    e, d, _ = wdn.shape
    bm = MOE_ROWS
    n_blocks = n_rows // bm
    blk = lambda j, be, nu: (jnp.minimum(j, nu[0] - 1), 0)
    exp = lambda j, be, nu: (be[jnp.minimum(j, nu[0] - 1)], 0, 0)
    grid_spec = pltpu.PrefetchScalarGridSpec(
        num_scalar_prefetch=2,
        grid=(n_blocks,),
        in_specs=[pl.BlockSpec((bm * SUBLANES, LANES), blk),
                  pl.BlockSpec((1, d, 2 * d), exp),
                  pl.BlockSpec((1, 1, 2 * d), exp),
                  pl.BlockSpec((1, d, d), exp),
                  pl.BlockSpec((1, 1, d), exp)],
        out_specs=pl.BlockSpec((bm * SUBLANES, LANES), blk),
        scratch_shapes=[pltpu.VMEM((d, 2 * d), BF16), pltpu.VMEM((d, d), BF16)],
    )
    return pl.pallas_call(
        _expert_kernel,
        grid_spec=grid_spec,
        out_shape=jax.ShapeDtypeStruct((n_rows * SUBLANES, LANES), F32),
        compiler_params=_cparams(("arbitrary",), VMEM_LIMIT),
        name="moe_experts",
    )(block_expert, n_used, h_sorted, wgu, bgu.reshape(e, 1, 2 * d), wdn, bdn.reshape(e, 1, d))


def _combine_kernel(dest_ref, dest_next_ref, wt_ref, x1_ref, g2_ref, lg_ref, lbias_ref, o_hbm, y_ref, buf_ref, sems):
    t = pl.program_id(0)
    nt = pl.num_programs(0)
    tt = x1_ref.shape[0]
    slot = t % 2

    def issue_row(d_ref, sl, i):
        for kk in range(TOP_K):
            d = d_ref[0, 0, kk * tt + i]
            pltpu.make_async_copy(_row_tile(o_hbm, d), _row_tile(buf_ref, (sl * TOP_K + kk) * tt + i),
                                  sems.at[sl]).start(priority=kk % DMA_QUEUES)

    def drain(sl):
        for kk in range(TOP_K):
            pltpu.make_async_copy(_row_tiles(o_hbm, 0, tt), _row_tiles(buf_ref, (sl * TOP_K + kk) * tt, tt),
                                  sems.at[sl]).wait()

    @pl.when(t == 0)
    def _():
        def first(i, _):
            issue_row(dest_ref, 0, i)
            return 0

        lax.fori_loop(0, tt, first, 0, unroll=ISSUE_UNROLL)

    drain(slot)

    def fused(g, _):
        r0 = pl.multiple_of(g * COMBINE_ROWS, COMBINE_ROWS)
        for i in range(COMBINE_ROWS):
            issue_row(dest_next_ref, 1 - slot, r0 + i)
        rows = pl.ds(r0, COMBINE_ROWS)
        ff = None
        for kk in range(TOP_K):
            base = pl.multiple_of(((slot * TOP_K + kk) * tt + r0) * SUBLANES, COMBINE_ROWS * SUBLANES)
            term = wt_ref[rows, kk:kk + 1] * _from_row_tiles(buf_ref, base, COMBINE_ROWS)
            ff = term if ff is None else ff + term
        y_ref[rows, :] = _layer_norm(DEEPNORM_ALPHA * x1_ref[rows, :] + g2_ref[0] * ff, lg_ref[...], lbias_ref[...])
        return 0

    lax.fori_loop(0, tt // COMBINE_ROWS, fused, 0)

    @pl.when(t == nt - 1)
    def _():
        drain(1 - slot)


def _combine(dest_flat, wt, x1, g2, lg, lbias, o_sorted, tt):
    n, d = x1.shape
    b = g2.shape[0]
    nt = n // tt
    tiles_per_b = (n // b) // tt
    const = lambda t: (0, 0)
    dest_spec = lambda im: pl.BlockSpec((1, 1, tt * TOP_K), im, memory_space=pltpu.SMEM)
    return pl.pallas_call(
        _combine_kernel,
        grid=(nt,),
        in_specs=[dest_spec(lambda t: (t, 0, 0)),
                  dest_spec(lambda t: (jnp.minimum(t + 1, nt - 1), 0, 0)),
                  pl.BlockSpec((tt, TOP_K), lambda t: (t, 0)),
                  pl.BlockSpec((tt, d), lambda t: (t, 0)),
                  pl.BlockSpec((1, 1, d), lambda t: (t // tiles_per_b, 0, 0)),
                  pl.BlockSpec((1, d), const),
                  pl.BlockSpec((1, d), const),
                  pl.BlockSpec(memory_space=pl.ANY)],
        out_specs=pl.BlockSpec((tt, d), lambda t: (t, 0)),
        out_shape=jax.ShapeDtypeStruct((n, d), F32),
        scratch_shapes=[pltpu.VMEM((2 * TOP_K * tt * SUBLANES, LANES), F32), pltpu.SemaphoreType.DMA((2,))],
        compiler_params=_cparams(("arbitrary",), VMEM_LIMIT),
        name="moe_combine",
    )(dest_flat, dest_flat, wt, x1, g2, lg, lbias, o_sorted)


def kernel(x, c, ctx, c_ctx, w_ada, b_ada, w_in, lb_raw, hg_norm_g, w_four_out, w_hg_out, w_o, ln1_g, ln1_b,
           w_router, b_router, w_gate_up, b_gate_up, w_down, b_down, ln2_g, ln2_b):
    b, s, d = x.shape
    lc = ctx.shape[1]
    n = b * s
    f_w = F_GROUPS * F_GROUP_DIM
    hk = HG_HEADS * HG_D
    assert w_ada.shape[0] == DEPTH and s % GRID_W == 0 and s % (HG_TILE * HG_GROUP) == 0 and lc % HG_TILE == 0
    assert s % ROW_TILE == 0 and ROW_TILE % MOE_TOKENS == 0 and MOE_TOKENS % COMBINE_ROWS == 0 == MOE_TOKENS % DISPATCH_ROWS

    pad = (-(b + 1)) % 8
    c_rows = jnp.concatenate([c, c_ctx[None, :], jnp.zeros((pad, d), F32)], axis=0)
    mod = _ada(c_rows, w_ada[0], b_ada[0]).reshape(b + 1 + pad, N_MOD, 1, d)
    shift1, scale1, gate1, shift2, scale2, gate2 = (mod[:b, i] for i in range(N_MOD))
    cshift1 = jnp.broadcast_to(mod[b, 0], (b, 1, d))
    cscale1 = jnp.broadcast_to(mod[b, 1], (b, 1, d))

    lower_bounds = jnp.cumsum(jax.nn.softmax(lb_raw.astype(F32), axis=0), axis=0)[0]

    w_in_b = w_in[0].astype(BF16)
    o_f, o_q, o_z, o_v, o_g, o_gt = f_w, f_w + hk, f_w + 3 * hk, f_w + 4 * hk, f_w + 5 * hk, f_w + 5 * hk + 2 * d
    lb_row = lower_bounds.reshape(1, 2 * hk)
    pf, pq, pk, phi, plo, pv, pg, gates = _inproj(
        x, scale1, shift1, lb_row, w_in_b,
        [(0, f_w, "cast"), (o_f, hk, "silu"), (o_q, 2 * hk, "forget"), (o_z, hk, "cast"), (o_v, hk, "cast"),
         (o_g, 2 * d, "cast")], tm=ROW_TILE)
    w_ctx = w_in_b[:, o_q:o_v]
    ck, chi, clo, cv = _inproj(ctx, cscale1, cshift1, lb_row, w_ctx, [(0, 2 * hk, "forget"), (2 * hk, hk, "cast")], tm=lc)

    y_h = _hgrn(pq, pk, phi, plo, pv, pg, ck, chi, clo, cv, hg_norm_g[0].reshape(1, hk))
    y_f = _fourier(pf)

    wr = w_router[0].T.astype(BF16)
    br = b_router[0].reshape(N_EXPERTS, 1)
    x1, idx_t, wt_t = _merge(y_f, y_h, gates, x, gate1, scale2, shift2,
                         w_four_out[0].astype(BF16), w_hg_out[0].astype(BF16), w_o[0].astype(BF16),
                         ln1_g[0].reshape(1, d), ln1_b[0].reshape(1, d), wr, br, tm=ROW_TILE)

    dest_t, counts, starts = _route(idx_t, MOE_ROWS, ROW_TILE)
    n_blocks = -(-(n * TOP_K) // MOE_ROWS) + N_EXPERTS
    n_rows = n_blocks * MOE_ROWS
    cnt = counts[:, 0]
    st = starts[:, 0]
    pad_ends = st + (cnt + MOE_ROWS - 1) // MOE_ROWS * MOE_ROWS
    block_first_row = jnp.arange(n_blocks, dtype=jnp.int32) * MOE_ROWS
    block_expert = jnp.minimum(jnp.sum((pad_ends[None, :] <= block_first_row[:, None]).astype(jnp.int32), axis=1),
                               N_EXPERTS - 1)
    n_used = (pad_ends[-1] // MOE_ROWS).astype(jnp.int32).reshape(1)

    tt = MOE_TOKENS
    dest_flat = dest_t.reshape(TOP_K, n // tt, tt).transpose(1, 0, 2).reshape(n // tt, 1, TOP_K * tt)
    x1_2d = x1.reshape(n, d)
    h_sorted = _dispatch(cnt, st, dest_flat, x1_2d, scale2, shift2, n_rows, tt)
    o_sorted = _experts(h_sorted, block_expert, n_used, w_gate_up[0], b_gate_up[0], w_down[0], b_down[0])
    out = _combine(dest_flat, wt_t.T, x1_2d, gate2, ln2_g[0].reshape(1, d), ln2_b[0].reshape(1, d),
                   o_sorted, tt)
    return out.reshape(b, s, d)
```

```python
import functools
import math

import jax
import jax.numpy as jnp
import numpy as np
from jax import lax
from jax.experimental import pallas as pl
from jax.experimental.pallas import tpu as pltpu

F32 = jnp.float32
BF16 = jnp.bfloat16

GRID_W = 64
N_MOD = 6
F_GROUPS = 4
F_GROUP_DIM = 128
HG_HEADS = 4
HG_D = 128
CHUNK = 64
N_EXPERTS = 32
TOP_K = 4
SWIGLU_LIMIT = 7.0
SWIGLU_ALPHA = 1.702
LN_EPS = 1e-5
RMS_EPS = 1e-6
DEPTH = 1
DEEPNORM_ALPHA = (2.0 * DEPTH) ** 0.25

LANES = 128
SUBLANES = 8
ROW_TILE = 1024
MOE_TOKENS = 512
MOE_ROWS = 512
COMBINE_ROWS = 128
ISSUE_UNROLL = 4
DMA_QUEUES = 2
VMEM_LIMIT = 56 * 1024 * 1024


def _cparams(sem, vmem=None):
    return pltpu.CompilerParams(dimension_semantics=sem, vmem_limit_bytes=vmem)


def _dot(a, b):
    return jnp.dot(a, b, preferred_element_type=F32)


def _sigmoid(x):
    return 0.5 * jnp.tanh(0.5 * x) + 0.5


def _ada_kernel(c_ref, w_ref, b_ref, o_ref):
    c = c_ref[...]
    a = c * _sigmoid(c)
    o_ref[...] = jnp.dot(a, w_ref[...], preferred_element_type=F32,
                         precision=lax.Precision.HIGHEST) + b_ref[...]


def _ada(c_rows, w, b):
    r, d = c_rows.shape
    n = w.shape[1]
    tn = 512
    return pl.pallas_call(
        _ada_kernel,
        grid=(n // tn,),
        in_specs=[pl.BlockSpec((r, d), lambda j: (0, 0)),
                  pl.BlockSpec((d, tn), lambda j: (0, j)),
                  pl.BlockSpec((1, tn), lambda j: (0, j))],
        out_specs=pl.BlockSpec((r, tn), lambda j: (0, j)),
        out_shape=jax.ShapeDtypeStruct((r, n), F32),
        compiler_params=_cparams(("arbitrary",)),
        name="ada_mod",
    )(c_rows, w, b.reshape(1, n))


def _inproj_kernel(x_ref, sc_ref, sh_ref, lb_ref, w_ref, *o_refs, plan):
    u = (x_ref[0] * (1.0 + sc_ref[0]) + sh_ref[0]).astype(BF16)
    refs = iter(o_refs)
    for kind, chunks in plan:
        if kind == "forget":
            k_ref, hi_ref, lo_ref = next(refs), next(refs), next(refs)
            for (w0, o0, n) in chunks:
                lb = lb_ref[:, o0:o0 + n]
                f = lb + (1.0 - lb) * _sigmoid(_dot(u, w_ref[:, w0:w0 + n]))
                lf = jnp.log(f)
                hi = lf.astype(BF16)
                k_ref[0, :, o0:o0 + n] = (1.0 - f).astype(BF16)
                hi_ref[0, :, o0:o0 + n] = hi
                lo_ref[0, :, o0:o0 + n] = (lf - hi.astype(F32)).astype(BF16)
        else:
            o_ref = next(refs)
            for (w0, o0, n) in chunks:
                p = _dot(u, w_ref[:, w0:w0 + n])
                if kind == "silu":
                    p = p * _sigmoid(p)
                o_ref[0, :, o0:o0 + n] = p.astype(o_ref.dtype)


def _inproj(x, scale, shift, lb_row, w_bf16, outs, tm):
    b, s, d = x.shape
    n_w = w_bf16.shape[1]
    plan, widths = [], []
    for (c0, width, kind) in outs:
        step = min(width, 512)
        plan.append((kind, tuple((c0 + o, o, step) for o in range(0, width, step))))
        widths += [width] * (3 if kind == "forget" else 1)
    kern = functools.partial(_inproj_kernel, plan=tuple(plan))
    return pl.pallas_call(
        kern,
        grid=(b, s // tm),
        in_specs=[pl.BlockSpec((1, tm, d), lambda i, j: (i, j, 0)),
                  pl.BlockSpec((1, 1, d), lambda i, j: (i, 0, 0)),
                  pl.BlockSpec((1, 1, d), lambda i, j: (i, 0, 0)),
                  pl.BlockSpec(lb_row.shape, lambda i, j: (0, 0)),
                  pl.BlockSpec((d, n_w), lambda i, j: (0, 0), pipeline_mode=pl.Buffered(1))],
        out_specs=[pl.BlockSpec((1, tm, width), lambda i, j: (i, j, 0)) for width in widths],
        out_shape=[jax.ShapeDtypeStruct((b, s, width), BF16) for width in widths],
        compiler_params=_cparams(("parallel", "parallel"), VMEM_LIMIT),
        name="in_proj",
    )(x, scale, shift, lb_row, w_bf16)


HG_TILE = 256
HG_GROUP = 4


def _gla_group(jobs, states):
    states = list(states)
    nj = len(jobs)
    r, dk = jobs[0][1].shape
    dv = jobs[0][4].shape[1]
    nc = r // CHUNK
    zero = jnp.zeros((CHUNK, dk), BF16)
    contract_last = (((1,), (1,)), ((), ()))
    contract_rows = (((0,), (0,)), ((), ()))

    cums = []
    for (_, hi, lo, _, _, _, tri, _, _) in jobs:
        both = _dot(tri, jnp.concatenate([hi, lo], axis=1))
        cums.append(both[:, :dk] + both[:, dk:])

    k_inv_b, decays, k_end_blk, q_dec = [], [], [], []
    for cum, (_, _, _, k, _, q, _, _, forward) in zip(cums, jobs):
        k_inv = k.astype(F32) * jnp.exp(-cum)
        k_inv_b.append(k_inv.astype(BF16))
        tot_row = CHUNK - 1 if forward else 0
        dec_j, rows_j = [], []
        for ci in range(nc):
            dec = jnp.exp(cum[ci * CHUNK + tot_row:ci * CHUNK + tot_row + 1, :])
            k_end = (k_inv[ci * CHUNK:(ci + 1) * CHUNK] * dec).astype(BF16)
            dec_j.append(dec)
            rows_j.append(jnp.concatenate([k_end if cj == ci else zero for cj in range(nc)], axis=1))
        decays.append(dec_j)
        k_end_blk.append(jnp.concatenate(rows_j, axis=0))
        q_dec.append(None if q is None else (q.astype(F32) * jnp.exp(cum)).astype(BF16))

    upd_all = [lax.dot_general(job[4], blk, contract_rows, preferred_element_type=F32)
               for job, blk in zip(jobs, k_end_blk)]
    scores = [None if qd is None else lax.dot_general(qd, kb, contract_last, preferred_element_type=F32)
              for qd, kb in zip(q_dec, k_inv_b)]

    s_all = []
    for ji in range(nj):
        chain, forward = jobs[ji][0], jobs[ji][8]
        st = states[chain]
        entering = [None] * nc
        for ci in (range(nc) if forward else reversed(range(nc))):
            entering[ci] = st
            st = st * decays[ji][ci] + upd_all[ji][:, ci * dk:(ci + 1) * dk]
        states[chain] = st
        s_all.append(None if q_dec[ji] is None
                     else [jnp.concatenate([entering[ci].astype(BF16), entering[ci + 1].astype(BF16)], axis=0)
                           for ci in range(0, nc, 2)])

    outs = []
    for ji in range(nj):
        if q_dec[ji] is None:
            outs.append(None)
            continue
        sc = jnp.where(jobs[ji][7] > 0, scores[ji], 0.0).astype(BF16)
        intra = _dot(sc, jobs[ji][4])
        inter = []
        for pi, s_pair in enumerate(s_all[ji]):
            rows = slice(2 * pi * CHUNK, (2 * pi + 2) * CHUNK)
            both = lax.dot_general(q_dec[ji][rows], s_pair, contract_last, preferred_element_type=F32)
            inter += [both[:CHUNK, :dv], both[CHUNK:, dv:]]
        outs.append(intra + jnp.concatenate(inter, axis=0))
    return outs, states


def _hgrn_kernel(q_ref, kf_ref, kb_ref, hif_ref, hib_ref, lof_ref, lob_ref, v_ref, g_ref,
                 ckf_ref, ckb_ref, chif_ref, chib_ref, clof_ref, clob_ref, cv_ref, ng_ref, trif_ref, trib_ref,
                 keepf_ref, keepb_ref, y_ref, of_ref, ob_ref, *, seq, ctx_len):
    r = HG_TILE
    n_lat = seq // r
    n_ctx = ctx_len // r
    group = HG_GROUP

    def ctx_jobs():
        jobs = []
        for i in range(n_ctx):
            sl_f = pl.ds(i * r, r)
            sl_b = pl.ds((n_ctx - 1 - i) * r, r)
            jobs.append((0, chif_ref[0, sl_f, :], clof_ref[0, sl_f, :], ckf_ref[0, sl_f, :], cv_ref[0, sl_f, :], None,
                         trif_ref[...], None, True))
            jobs.append((1, chib_ref[0, sl_b, :], clob_ref[0, sl_b, :], ckb_ref[0, sl_b, :], cv_ref[0, sl_b, :], None,
                         trib_ref[...], None, False))
        return jobs

    ng = ng_ref[...]

    def readout(tile, o):
        sl = pl.ds(tile * r, r)
        o = o * lax.rsqrt(jnp.mean(o * o, axis=-1, keepdims=True) + RMS_EPS) * ng
        g = g_ref[0, sl, :].astype(F32)
        y_ref[0, sl, :] = (o * (g * _sigmoid(g))).astype(y_ref.dtype)

    def step_of(tile, forward):
        return (tile if forward else n_lat - 1 - tile) // group

    carry = (jnp.zeros((HG_D, HG_D), F32),) * 2
    for i in range(n_lat // group):
        jobs = ctx_jobs() if i == 0 else []
        meta = [None] * len(jobs)
        for t in range(group):
            for forward in (True, False):
                tile = i * group + t if forward else n_lat - 1 - (i * group + t)
                sl = pl.ds(tile * r, r)
                hi_ref, lo_ref, k_ref, tri_ref, keep_ref = ((hif_ref, lof_ref, kf_ref, trif_ref, keepf_ref) if forward
                                                            else (hib_ref, lob_ref, kb_ref, trib_ref, keepb_ref))
                jobs.append((0 if forward else 1, hi_ref[0, sl, :], lo_ref[0, sl, :], k_ref[0, sl, :], v_ref[0, sl, :],
                             q_ref[0, sl, :], tri_ref[...], keep_ref[...], forward))
                meta.append((tile, forward))
        outs, carry = _gla_group(jobs, carry)
        waiting = {}
        for m, o in zip(meta, outs):
            if m is None:
                continue
            tile, forward = m
            mine, other = (of_ref, ob_ref) if forward else (ob_ref, of_ref)
            other_step = step_of(tile, not forward)
            if other_step < i:
                readout(tile, o + other[pl.ds(tile * r, r), :])
            elif other_step > i:
                mine[pl.ds(tile * r, r), :] = o
            elif tile in waiting:
                readout(tile, o + waiting.pop(tile))
            else:
                waiting[tile] = o


def _chunk_tri(r, forward, dtype):
    i = np.arange(r)
    same = (i[:, None] // CHUNK) == (i[None, :] // CHUNK)
    order = (i[:, None] >= i[None, :]) if forward else (i[:, None] <= i[None, :])
    return jnp.asarray(same & order, dtype)


def _hgrn(q, k, hi, lo, v, g, ck, chi, clo, cv, ng):
    b, s, _ = q.shape
    lc = ck.shape[1]
    h = HG_HEADS
    d = HG_D
    r = HG_TILE
    kern = functools.partial(_hgrn_kernel, seq=s, ctx_len=lc)
    fwd = lambda i, j: (i, 0, j)
    bwd = lambda i, j: (i, 0, j + h)
    const = lambda i, j: (0, 0)
    lat = lambda im: pl.BlockSpec((1, s, d), im)
    cx = lambda im: pl.BlockSpec((1, lc, d), im)
    return pl.pallas_call(
        kern,
        grid=(b, h),
        in_specs=[lat(fwd), lat(fwd), lat(bwd), lat(fwd), lat(bwd), lat(fwd), lat(bwd), lat(fwd), lat(fwd),
                  cx(fwd), cx(bwd), cx(fwd), cx(bwd), cx(fwd), cx(bwd), cx(fwd),
                  pl.BlockSpec((1, d), lambda i, j: (0, j)),
                  pl.BlockSpec((r, r), const),
                  pl.BlockSpec((r, r), const),
                  pl.BlockSpec((r, r), const),
                  pl.BlockSpec((r, r), const)],
        out_specs=pl.BlockSpec((1, s, d), fwd),
        out_shape=jax.ShapeDtypeStruct((b, s, h * d), BF16),
        scratch_shapes=[pltpu.VMEM((s, d), F32), pltpu.VMEM((s, d), F32)],
        compiler_params=_cparams(("parallel", "parallel"), VMEM_LIMIT),
        name="hgrn2",
    )(q, k, k, hi, hi, lo, lo, v, g, ck, ck, chi, chi, clo, clo, cv, ng,
      _chunk_tri(r, True, BF16), _chunk_tri(r, False, BF16), _chunk_tri(r, True, F32), _chunk_tri(r, False, F32))


def _dft_tables(seq):
    rows = seq // GRID_W
    n = F_GROUP_DIM
    kc = np.outer(np.arange(n), np.arange(n)) % n
    ang = 2.0 * np.pi * kc / n
    norm = 1.0 / math.sqrt(rows * GRID_W * n)
    ch = np.concatenate([np.cos(ang), -np.sin(ang)], axis=1) * norm
    t = np.arange(seq)
    r, w = t // GRID_W, t % GRID_W
    m = (np.outer(r, r) * GRID_W + np.outer(w, w) * rows) % (rows * GRID_W)
    ang_t = 2.0 * np.pi * m / (rows * GRID_W)
    tok = np.concatenate([np.cos(ang_t), np.sin(ang_t)], axis=1)
    return jnp.asarray(ch, BF16), jnp.asarray(tok, BF16)


def _fourier_kernel(p_ref, ch_ref, tok_ref, y_ref, xs_ref, *, seq):
    for gi in range(F_GROUPS):
        cols = slice(gi * F_GROUP_DIM, (gi + 1) * F_GROUP_DIM)
        x1 = _dot(p_ref[0, :, cols], ch_ref[...]).astype(BF16)
        xs_ref[0:seq, cols] = x1[:, :F_GROUP_DIM]
        xs_ref[seq:2 * seq, cols] = x1[:, F_GROUP_DIM:]
    y_ref[0] = _dot(tok_ref[...], xs_ref[...]).astype(y_ref.dtype)


def _fourier(pf):
    b, s, wdt = pf.shape
    ch, tok = _dft_tables(s)
    kern = functools.partial(_fourier_kernel, seq=s)
    return pl.pallas_call(
        kern,
        grid=(b,),
        in_specs=[pl.BlockSpec((1, s, wdt), lambda i: (i, 0, 0)),
                  pl.BlockSpec(ch.shape, lambda i: (0, 0)),
                  pl.BlockSpec(tok.shape, lambda i: (0, 0), pipeline_mode=pl.Buffered(1))],
        out_specs=pl.BlockSpec((1, s, wdt), lambda i: (i, 0, 0)),
        out_shape=jax.ShapeDtypeStruct((b, s, wdt), BF16),
        scratch_shapes=[pltpu.VMEM((2 * s, wdt), BF16)],
        compiler_params=_cparams(("parallel",), VMEM_LIMIT),
        name="fourier",
    )(pf, ch, tok)


def _layer_norm(t, g, b):
    mu = jnp.mean(t, axis=-1, keepdims=True)
    tc = t - mu
    var = jnp.mean(tc * tc, axis=-1, keepdims=True)
    return tc * lax.rsqrt(var + LN_EPS) * g + b


def _merge_kernel(yf_ref, yh_ref, gt_ref, x_ref, g1_ref, sc2_ref, sh2_ref, wfo_ref, who_ref, wo_ref,
                  lg_ref, lbias_ref, wr_ref, br_ref, x1_ref, idx_ref, wt_ref):
    d = x_ref.shape[-1]
    gf = _sigmoid(gt_ref[0, :, :d].astype(F32))
    gh = _sigmoid(gt_ref[0, :, d:].astype(F32))
    m = gf * _dot(yf_ref[0], wfo_ref[...]) + gh * _dot(yh_ref[0], who_ref[...])
    mix = _dot(m.astype(BF16), wo_ref[...])
    x1 = _layer_norm(DEEPNORM_ALPHA * x_ref[0] + g1_ref[0] * mix, lg_ref[...], lbias_ref[...])
    x1_ref[0] = x1
    u2 = x1 * (1.0 + sc2_ref[0]) + sh2_ref[0]
    logits = lax.dot_general(wr_ref[...], u2.astype(BF16), (((1,), (1,)), ((), ())),
                             preferred_element_type=F32) + br_ref[...]
    expert = lax.broadcasted_iota(jnp.int32, logits.shape, 0)
    expert_f = expert.astype(F32)
    work = logits
    vals, idxs = [], []
    for _ in range(TOP_K):
        mx = jnp.max(work, axis=0, keepdims=True)
        sel = jnp.min(jnp.where(work == mx, expert_f, float(N_EXPERTS)), axis=0, keepdims=True).astype(jnp.int32)
        vals.append(mx)
        idxs.append(sel)
        work = jnp.where(expert == sel, -jnp.inf, work)
    exps = [jnp.exp(vv - vals[0]) for vv in vals]
    inv = 1.0 / (exps[0] + exps[1] + exps[2] + exps[3])
    for kk in range(TOP_K):
        idx_ref[kk:kk + 1, :] = idxs[kk]
        wt_ref[kk:kk + 1, :] = exps[kk] * inv


def _merge(yf, yh, gates, x, g1, sc2, sh2, wfo, who, wo, lg, lbias, wr, br, tm):
    b, s, d = x.shape
    tile = lambda i, j: (i, j, 0)
    per_b = lambda i, j: (i, 0, 0)
    const = lambda i, j: (0, 0)
    slots = lambda i, j: (0, i * (s // tm) + j)
    return pl.pallas_call(
        _merge_kernel,
        grid=(b, s // tm),
        in_specs=[pl.BlockSpec((1, tm, yf.shape[-1]), tile),
                  pl.BlockSpec((1, tm, yh.shape[-1]), tile),
                  pl.BlockSpec((1, tm, gates.shape[-1]), tile),
                  pl.BlockSpec((1, tm, d), tile),
                  pl.BlockSpec((1, 1, d), per_b),
                  pl.BlockSpec((1, 1, d), per_b),
                  pl.BlockSpec((1, 1, d), per_b),
                  pl.BlockSpec(wfo.shape, const),
                  pl.BlockSpec(who.shape, const),
                  pl.BlockSpec(wo.shape, const),
                  pl.BlockSpec((1, d), const),
                  pl.BlockSpec((1, d), const),
                  pl.BlockSpec(wr.shape, const),
                  pl.BlockSpec(br.shape, const)],
        out_specs=[pl.BlockSpec((1, tm, d), tile),
                   pl.BlockSpec((TOP_K, tm), slots),
                   pl.BlockSpec((TOP_K, tm), slots)],
        out_shape=[jax.ShapeDtypeStruct((b, s, d), F32),
                   jax.ShapeDtypeStruct((TOP_K, b * s), jnp.int32),
                   jax.ShapeDtypeStruct((TOP_K, b * s), F32)],
        compiler_params=_cparams(("parallel", "parallel"), VMEM_LIMIT),
        name="merge_ln_router",
    )(yf, yh, gates, x, g1, sc2, sh2, wfo, who, wo, lg, lbias, wr, br)


def _route_kernel(idx_ref, before_ref, dest_ref, cnt_ref, start_ref, run_ref, *, block_rows):
    phase = pl.program_id(0)
    t = pl.program_id(1)
    tt = idx_ref.shape[1]
    expert = lax.broadcasted_iota(jnp.int32, (N_EXPERTS, tt), 0)
    onehots = [jnp.where(expert == idx_ref[kk:kk + 1, :], 1.0, 0.0) for kk in range(TOP_K)]
    counts = [jnp.sum(oh, axis=1, keepdims=True) for oh in onehots]

    @pl.when(jnp.logical_and(phase == 0, t == 0))
    def _():
        run_ref[...] = jnp.zeros_like(run_ref)

    @pl.when(phase == 0)
    def _():
        run_ref[...] += counts[0] + counts[1] + counts[2] + counts[3]

    @pl.when(jnp.logical_and(phase == 1, t == 0))
    def _():
        cnt = run_ref[...]
        padded = jnp.floor((cnt + (block_rows - 1)) * (1.0 / block_rows)) * block_rows
        r = lax.broadcasted_iota(jnp.int32, (N_EXPERTS, N_EXPERTS), 0)
        c = lax.broadcasted_iota(jnp.int32, (N_EXPERTS, N_EXPERTS), 1)
        earlier = jnp.where(r > c, 1.0, 0.0)
        start = jnp.dot(earlier, padded, preferred_element_type=F32, precision=lax.Precision.HIGHEST)
        cnt_ref[...] = cnt.astype(jnp.int32)
        start_ref[...] = start.astype(jnp.int32)
        run_ref[...] = jnp.zeros_like(run_ref)

    @pl.when(phase == 1)
    def _():
        prefixes = _dot(jnp.concatenate([oh.astype(BF16) for oh in onehots], axis=0), before_ref[...])
        start = start_ref[:, 0:1].astype(F32)
        base = run_ref[:, 0:1] + start
        for kk in range(TOP_K):
            prefix = prefixes[kk * N_EXPERTS:(kk + 1) * N_EXPERTS, :]
            dest_ref[kk:kk + 1, :] = jnp.sum(onehots[kk] * (prefix + base), axis=0, keepdims=True).astype(jnp.int32)
            base = base + counts[kk]
        run_ref[...] = jnp.broadcast_to(base - start, run_ref.shape)


def _route(idx_t, block_rows, tt):
    n = idx_t.shape[1]
    kern = functools.partial(_route_kernel, block_rows=block_rows)
    before = jnp.asarray(np.triu(np.ones((tt, tt)), 1), BF16)
    stat = pl.BlockSpec((N_EXPERTS, LANES), lambda p, t: (0, 0))
    return pl.pallas_call(
        kern,
        grid=(2, n // tt),
        in_specs=[pl.BlockSpec((TOP_K, tt), lambda p, t: (0, t)),
                  pl.BlockSpec((tt, tt), lambda p, t: (0, 0))],
        out_specs=[pl.BlockSpec((TOP_K, tt), lambda p, t: (0, t * p)), stat, stat],
        out_shape=[jax.ShapeDtypeStruct((TOP_K, n), jnp.int32),
                   jax.ShapeDtypeStruct((N_EXPERTS, LANES), jnp.int32),
                   jax.ShapeDtypeStruct((N_EXPERTS, LANES), jnp.int32)],
        scratch_shapes=[pltpu.VMEM((N_EXPERTS, LANES), F32)],
        compiler_params=_cparams(("arbitrary", "arbitrary")),
        name="route_ranks",
    )(idx_t, before)


def _to_row_tiles(ref, base, val):
    rows = val.shape[0]
    for s in range(SUBLANES):
        ref[pl.ds(base + s, rows, stride=SUBLANES), :] = val[:, s * LANES:(s + 1) * LANES]


def _from_row_tiles(ref, base, rows):
    return jnp.concatenate([ref[pl.ds(base + s, rows, stride=SUBLANES), :] for s in range(SUBLANES)], axis=1)


def _row_tile(ref, r):
    return ref.at[pl.ds(pl.multiple_of(r * SUBLANES, SUBLANES), SUBLANES), :]


def _row_tiles(ref, r, n):
    return ref.at[pl.ds(pl.multiple_of(r * SUBLANES, SUBLANES), n * SUBLANES), :]


def _dispatch_kernel(cnt_ref, start_ref, dest_ref, x1_ref, sc_ref, sh_ref, h_ref, u_ref, zero_ref, sems, *, block_rows):
    t = pl.program_id(0)
    nt = pl.num_programs(0)
    tt = x1_ref.shape[0]
    slot = t % 2
    ubase = pl.multiple_of(slot * (tt * SUBLANES), tt * SUBLANES)

    _to_row_tiles(u_ref, ubase, x1_ref[...] * (1.0 + sc_ref[0]) + sh_ref[0])

    def issue(i, _):
        for kk in range(TOP_K):
            d = dest_ref[0, 0, kk * tt + i]
            pltpu.make_async_copy(_row_tile(u_ref, slot * tt + i), _row_tile(h_ref, d),
                                  sems.at[slot]).start(priority=kk % DMA_QUEUES)
        return 0

    lax.fori_loop(0, tt, issue, 0, unroll=ISSUE_UNROLL)

    def drain(sl):
        for _ in range(TOP_K):
            pltpu.make_async_copy(_row_tiles(u_ref, sl * tt, tt), _row_tiles(h_ref, 0, tt), sems.at[sl]).wait()

    @pl.when(t > 0)
    def _():
        drain(1 - slot)

    @pl.when(t == nt - 1)
    def _():
        drain(slot)
        zero_ref[...] = jnp.zeros_like(zero_ref)

        def pad_expert(e, _):
            cnt = cnt_ref[e]
            rem = (block_rows - (cnt & (block_rows - 1))) & (block_rows - 1)
            pos = start_ref[e] + cnt
            size = block_rows // 2
            while size >= 1:
                take = rem & size

                @pl.when(take != 0)
                def _(pos=pos, size=size):
                    cp = pltpu.make_async_copy(_row_tiles(zero_ref, 0, size), _row_tiles(h_ref, pos, size), sems.at[2])
                    cp.start()
                    cp.wait()

                pos = pos + take
                size //= 2
            return 0

        lax.fori_loop(0, N_EXPERTS, pad_expert, 0)


def _dispatch(counts, starts, dest_flat, x1, sc2, sh2, n_rows, tt):
    n, d = x1.shape
    b = sc2.shape[0]
    tiles_per_b = (n // b) // tt
    per_b = lambda t, c, s: (t // tiles_per_b, 0, 0)
    grid_spec = pltpu.PrefetchScalarGridSpec(
        num_scalar_prefetch=2,
        grid=(n // tt,),
        in_specs=[pl.BlockSpec((1, 1, tt * TOP_K), lambda t, c, s: (t, 0, 0), memory_space=pltpu.SMEM),
                  pl.BlockSpec((tt, d), lambda t, c, s: (t, 0)),
                  pl.BlockSpec((1, 1, d), per_b),
                  pl.BlockSpec((1, 1, d), per_b)],
        out_specs=pl.BlockSpec(memory_space=pl.ANY),
        scratch_shapes=[pltpu.VMEM((2 * tt * SUBLANES, LANES), F32), pltpu.VMEM((MOE_ROWS // 2 * SUBLANES, LANES), F32),
                        pltpu.SemaphoreType.DMA((3,))],
    )
    return pl.pallas_call(
        functools.partial(_dispatch_kernel, block_rows=MOE_ROWS),
        grid_spec=grid_spec,
        out_shape=jax.ShapeDtypeStruct((n_rows * SUBLANES, LANES), F32),
        compiler_params=_cparams(("arbitrary",)),
        name="moe_dispatch",
    )(counts, starts, dest_flat, x1, sc2, sh2)


def _expert_kernel(be_ref, nu_ref, h_ref, wgu_ref, bgu_ref, wdn_ref, bdn_ref, o_ref, wgu_b_ref, wdn_b_ref):
    j = pl.program_id(0)
    d = wdn_ref.shape[-1]
    bm = h_ref.shape[0] // SUBLANES

    @pl.when(j < nu_ref[0])
    def _():
        @pl.when(jnp.logical_or(j == 0, be_ref[j] != be_ref[jnp.maximum(j - 1, 0)]))
        def _():
            wgu_b_ref[...] = wgu_ref[0].astype(BF16)
            wdn_b_ref[...] = wdn_ref[0].astype(BF16)

        gu = _dot(_from_row_tiles(h_ref, 0, bm).astype(BF16), wgu_b_ref[...]) + bgu_ref[0]
        gate = jnp.minimum(gu[:, :d], SWIGLU_LIMIT)
        up = jnp.clip(gu[:, d:], -SWIGLU_LIMIT, SWIGLU_LIMIT)
        act = (up + 1.0) * gate * _sigmoid(SWIGLU_ALPHA * gate)
        _to_row_tiles(o_ref, 0, _dot(act.astype(BF16), wdn_b_ref[...]) + bdn_ref[0])


def _experts(h_sorted, block_expert, n_used, wgu, bgu, wdn, bdn):
    n_rows = h_sorted.shape[0] // SUBLANES
    e, d, _ = wdn.shape
    bm = MOE_ROWS
    n_blocks = n_rows // bm
    blk = lambda j, be, nu: (jnp.minimum(j, nu[0] - 1), 0)
    exp = lambda j, be, nu: (be[jnp.minimum(j, nu[0] - 1)], 0, 0)
    grid_spec = pltpu.PrefetchScalarGridSpec(
        num_scalar_prefetch=2,
        grid=(n_blocks,),
        in_specs=[pl.BlockSpec((bm * SUBLANES, LANES), blk),
                  pl.BlockSpec((1, d, 2 * d), exp),
                  pl.BlockSpec((1, 1, 2 * d), exp),
                  pl.BlockSpec((1, d, d), exp),
                  pl.BlockSpec((1, 1, d), exp)],
        out_specs=pl.BlockSpec((bm * SUBLANES, LANES), blk),
        scratch_shapes=[pltpu.VMEM((d, 2 * d), BF16), pltpu.VMEM((d, d), BF16)],
    )
    return pl.pallas_call(
        _expert_kernel,
        grid_spec=grid_spec,
        out_shape=jax.ShapeDtypeStruct((n_rows * SUBLANES, LANES), F32),
        compiler_params=_cparams(("arbitrary",), VMEM_LIMIT),
        name="moe_experts",
    )(block_expert, n_used, h_sorted, wgu, bgu.reshape(e, 1, 2 * d), wdn, bdn.reshape(e, 1, d))


def _combine_kernel(dest_ref, dest_next_ref, wt_ref, x1_ref, g2_ref, lg_ref, lbias_ref, o_hbm, y_ref, buf_ref, sems):
    t = pl.program_id(0)
    nt = pl.num_programs(0)
    tt = x1_ref.shape[0]
    slot = t % 2

    def issue_row(d_ref, sl, i):
        for kk in range(TOP_K):
            d = d_ref[0, 0, kk * tt + i]
            pltpu.make_async_copy(_row_tile(o_hbm, d), _row_tile(buf_ref, (sl * TOP_K + kk) * tt + i),
                                  sems.at[sl]).start(priority=kk % DMA_QUEUES)

    def drain(sl):
        for kk in range(TOP_K):
            pltpu.make_async_copy(_row_tiles(o_hbm, 0, tt), _row_tiles(buf_ref, (sl * TOP_K + kk) * tt, tt),
                                  sems.at[sl]).wait()

    @pl.when(t == 0)
    def _():
        def first(i, _):
            issue_row(dest_ref, 0, i)
            return 0

        lax.fori_loop(0, tt, first, 0, unroll=ISSUE_UNROLL)

    drain(slot)

    def fused(g, _):
        r0 = pl.multiple_of(g * COMBINE_ROWS, COMBINE_ROWS)
        for i in range(COMBINE_ROWS):
            issue_row(dest_next_ref, 1 - slot, r0 + i)
        rows = pl.ds(r0, COMBINE_ROWS)
        ff = None
        for kk in range(TOP_K):
            base = pl.multiple_of(((slot * TOP_K + kk) * tt + r0) * SUBLANES, COMBINE_ROWS * SUBLANES)
            term = wt_ref[rows, kk:kk + 1] * _from_row_tiles(buf_ref, base, COMBINE_ROWS)
            ff = term if ff is None else ff + term
        y_ref[rows, :] = _layer_norm(DEEPNORM_ALPHA * x1_ref[rows, :] + g2_ref[0] * ff, lg_ref[...], lbias_ref[...])
        return 0

    lax.fori_loop(0, tt // COMBINE_ROWS, fused, 0)

    @pl.when(t == nt - 1)
    def _():
        drain(1 - slot)


def _combine(dest_flat, wt, x1, g2, lg, lbias, o_sorted, tt):
    n, d = x1.shape
    b = g2.shape[0]
    nt = n // tt
    tiles_per_b = (n // b) // tt
    const = lambda t: (0, 0)
    dest_spec = lambda im: pl.BlockSpec((1, 1, tt * TOP_K), im, memory_space=pltpu.SMEM)
    return pl.pallas_call(
        _combine_kernel,
        grid=(nt,),
        in_specs=[dest_spec(lambda t: (t, 0, 0)),
                  dest_spec(lambda t: (jnp.minimum(t + 1, nt - 1), 0, 0)),
                  pl.BlockSpec((tt, TOP_K), lambda t: (t, 0)),
                  pl.BlockSpec((tt, d), lambda t: (t, 0)),
                  pl.BlockSpec((1, 1, d), lambda t: (t // tiles_per_b, 0, 0)),
                  pl.BlockSpec((1, d), const),
                  pl.BlockSpec((1, d), const),
                  pl.BlockSpec(memory_space=pl.ANY)],
        out_specs=pl.BlockSpec((tt, d), lambda t: (t, 0)),
        out_shape=jax.ShapeDtypeStruct((n, d), F32),
        scratch_shapes=[pltpu.VMEM((2 * TOP_K * tt * SUBLANES, LANES), F32), pltpu.SemaphoreType.DMA((2,))],
        compiler_params=_cparams(("arbitrary",), VMEM_LIMIT),
        name="moe_combine",
    )(dest_flat, dest_flat, wt, x1, g2, lg, lbias, o_sorted)


def kernel(x, c, ctx, c_ctx, w_ada, b_ada, w_in, lb_raw, hg_norm_g, w_four_out, w_hg_out, w_o, ln1_g, ln1_b,
           w_router, b_router, w_gate_up, b_gate_up, w_down, b_down, ln2_g, ln2_b):
    b, s, d = x.shape
    lc = ctx.shape[1]
    n = b * s
    f_w = F_GROUPS * F_GROUP_DIM
    hk = HG_HEADS * HG_D
    assert w_ada.shape[0] == DEPTH and s % GRID_W == 0 and s % (HG_TILE * HG_GROUP) == 0 and lc % HG_TILE == 0
    assert s % ROW_TILE == 0 and ROW_TILE % MOE_TOKENS == 0 and MOE_TOKENS % COMBINE_ROWS == 0

    pad = (-(b + 1)) % 8
    c_rows = jnp.concatenate([c, c_ctx[None, :], jnp.zeros((pad, d), F32)], axis=0)
    mod = _ada(c_rows, w_ada[0], b_ada[0]).reshape(b + 1 + pad, N_MOD, 1, d)
    shift1, scale1, gate1, shift2, scale2, gate2 = (mod[:b, i] for i in range(N_MOD))
    cshift1 = jnp.broadcast_to(mod[b, 0], (b, 1, d))
    cscale1 = jnp.broadcast_to(mod[b, 1], (b, 1, d))

    lower_bounds = jnp.cumsum(jax.nn.softmax(lb_raw.astype(F32), axis=0), axis=0)[0]

    w_in_b = w_in[0].astype(BF16)
    o_f, o_q, o_z, o_v, o_g, o_gt = f_w, f_w + hk, f_w + 3 * hk, f_w + 4 * hk, f_w + 5 * hk, f_w + 5 * hk + 2 * d
    lb_row = lower_bounds.reshape(1, 2 * hk)
    pf, pq, pk, phi, plo, pv, pg, gates = _inproj(
        x, scale1, shift1, lb_row, w_in_b,
        [(0, f_w, "cast"), (o_f, hk, "silu"), (o_q, 2 * hk, "forget"), (o_z, hk, "cast"), (o_v, hk, "cast"),
         (o_g, 2 * d, "cast")], tm=ROW_TILE)
    w_ctx = w_in_b[:, o_q:o_v]
    ck, chi, clo, cv = _inproj(ctx, cscale1, cshift1, lb_row, w_ctx, [(0, 2 * hk, "forget"), (2 * hk, hk, "cast")], tm=lc)

    y_h = _hgrn(pq, pk, phi, plo, pv, pg, ck, chi, clo, cv, hg_norm_g[0].reshape(1, hk))
    y_f = _fourier(pf)

    wr = w_router[0].T.astype(BF16)
    br = b_router[0].reshape(N_EXPERTS, 1)
    x1, idx_t, wt_t = _merge(y_f, y_h, gates, x, gate1, scale2, shift2,
                         w_four_out[0].astype(BF16), w_hg_out[0].astype(BF16), w_o[0].astype(BF16),
                         ln1_g[0].reshape(1, d), ln1_b[0].reshape(1, d), wr, br, tm=ROW_TILE)

    dest_t, counts, starts = _route(idx_t, MOE_ROWS, ROW_TILE)
    n_blocks = -(-(n * TOP_K) // MOE_ROWS) + N_EXPERTS
    n_rows = n_blocks * MOE_ROWS
    cnt = counts[:, 0]
    st = starts[:, 0]
    pad_ends = st + (cnt + MOE_ROWS - 1) // MOE_ROWS * MOE_ROWS
    block_first_row = jnp.arange(n_blocks, dtype=jnp.int32) * MOE_ROWS
    block_expert = jnp.minimum(jnp.sum((pad_ends[None, :] <= block_first_row[:, None]).astype(jnp.int32), axis=1),
                               N_EXPERTS - 1)
    n_used = (pad_ends[-1] // MOE_ROWS).astype(jnp.int32).reshape(1)

    tt = MOE_TOKENS
    dest_flat = dest_t.reshape(TOP_K, n // tt, tt).transpose(1, 0, 2).reshape(n // tt, 1, TOP_K * tt)
    x1_2d = x1.reshape(n, d)
    h_sorted = _dispatch(cnt, st, dest_flat, x1_2d, scale2, shift2, n_rows, tt)
    o_sorted = _experts(h_sorted, block_expert, n_used, w_gate_up[0], b_gate_up[0], w_down[0], b_down[0])
    out = _combine(dest_flat, wt_t.T, x1_2d, gate2, ln2_g[0].reshape(1, d), ln2_b[0].reshape(1, d),
                   o_sorted, tt)
    return out.reshape(b, s, d)
```

```python
import functools
import math

import jax
import jax.numpy as jnp
import numpy as np
from jax import lax
from jax.experimental import pallas as pl
from jax.experimental.pallas import tpu as pltpu

F32 = jnp.float32
BF16 = jnp.bfloat16

GRID_W = 64
N_MOD = 6
F_GROUPS = 4
F_GROUP_DIM = 128
HG_HEADS = 4
HG_D = 128
CHUNK = 64
N_EXPERTS = 32
TOP_K = 4
SWIGLU_LIMIT = 7.0
SWIGLU_ALPHA = 1.702
LN_EPS = 1e-5
RMS_EPS = 1e-6
DEPTH = 1
DEEPNORM_ALPHA = (2.0 * DEPTH) ** 0.25

LANES = 128
SUBLANES = 8
ROW_TILE = 1024
MOE_TOKENS = 512
MOE_ROWS = 512
COMBINE_ROWS = 128
ISSUE_UNROLL = 4
DMA_QUEUES = 2
VMEM_LIMIT = 56 * 1024 * 1024


def _cparams(sem, vmem=None):
    return pltpu.CompilerParams(dimension_semantics=sem, vmem_limit_bytes=vmem)


def _dot(a, b):
    return jnp.dot(a, b, preferred_element_type=F32)


def _sigmoid(x):
    return 0.5 * jnp.tanh(0.5 * x) + 0.5


def _ada_kernel(c_ref, w_ref, b_ref, o_ref):
    c = c_ref[...]
    a = c * _sigmoid(c)
    o_ref[...] = jnp.dot(a, w_ref[...], preferred_element_type=F32,
                         precision=lax.Precision.HIGHEST) + b_ref[...]


def _ada(c_rows, w, b):
    r, d = c_rows.shape
    n = w.shape[1]
    tn = 512
    return pl.pallas_call(
        _ada_kernel,
        grid=(n // tn,),
        in_specs=[pl.BlockSpec((r, d), lambda j: (0, 0)),
                  pl.BlockSpec((d, tn), lambda j: (0, j)),
                  pl.BlockSpec((1, tn), lambda j: (0, j))],
        out_specs=pl.BlockSpec((r, tn), lambda j: (0, j)),
        out_shape=jax.ShapeDtypeStruct((r, n), F32),
        compiler_params=_cparams(("arbitrary",)),
        name="ada_mod",
    )(c_rows, w, b.reshape(1, n))


def _inproj_kernel(x_ref, sc_ref, sh_ref, lb_ref, w_ref, *o_refs, plan):
    u = (x_ref[0] * (1.0 + sc_ref[0]) + sh_ref[0]).astype(BF16)
    refs = iter(o_refs)
    for kind, chunks in plan:
        if kind == "forget":
            k_ref, hi_ref, lo_ref = next(refs), next(refs), next(refs)
            for (w0, o0, n) in chunks:
                lb = lb_ref[:, o0:o0 + n]
                f = lb + (1.0 - lb) * _sigmoid(_dot(u, w_ref[:, w0:w0 + n]))
                lf = jnp.log(f)
                hi = lf.astype(BF16)
                k_ref[0, :, o0:o0 + n] = (1.0 - f).astype(BF16)
                hi_ref[0, :, o0:o0 + n] = hi
                lo_ref[0, :, o0:o0 + n] = (lf - hi.astype(F32)).astype(BF16)
        else:
            o_ref = next(refs)
            for (w0, o0, n) in chunks:
                p = _dot(u, w_ref[:, w0:w0 + n])
                if kind == "silu":
                    p = p * _sigmoid(p)
                o_ref[0, :, o0:o0 + n] = p.astype(o_ref.dtype)


def _inproj(x, scale, shift, lb_row, w_bf16, outs, tm):
    b, s, d = x.shape
    n_w = w_bf16.shape[1]
    plan, widths = [], []
    for (c0, width, kind) in outs:
        step = min(width, 512)
        plan.append((kind, tuple((c0 + o, o, step) for o in range(0, width, step))))
        widths += [width] * (3 if kind == "forget" else 1)
    kern = functools.partial(_inproj_kernel, plan=tuple(plan))
    return pl.pallas_call(
        kern,
        grid=(b, s // tm),
        in_specs=[pl.BlockSpec((1, tm, d), lambda i, j: (i, j, 0)),
                  pl.BlockSpec((1, 1, d), lambda i, j: (i, 0, 0)),
                  pl.BlockSpec((1, 1, d), lambda i, j: (i, 0, 0)),
                  pl.BlockSpec(lb_row.shape, lambda i, j: (0, 0)),
                  pl.BlockSpec((d, n_w), lambda i, j: (0, 0), pipeline_mode=pl.Buffered(1))],
        out_specs=[pl.BlockSpec((1, tm, width), lambda i, j: (i, j, 0)) for width in widths],
        out_shape=[jax.ShapeDtypeStruct((b, s, width), BF16) for width in widths],
        compiler_params=_cparams(("parallel", "parallel"), VMEM_LIMIT),
        name="in_proj",
    )(x, scale, shift, lb_row, w_bf16)


HG_TILE = 256
HG_GROUP = 4


def _gla_group(jobs, states):
    states = list(states)
    nj = len(jobs)
    r, dk = jobs[0][1].shape
    dv = jobs[0][4].shape[1]
    nc = r // CHUNK
    zero = jnp.zeros((CHUNK, dk), BF16)
    contract_last = (((1,), (1,)), ((), ()))
    contract_rows = (((0,), (0,)), ((), ()))

    cums = []
    for (_, hi, lo, _, _, _, tri, _, _) in jobs:
        both = _dot(tri, jnp.concatenate([hi, lo], axis=1))
        cums.append(both[:, :dk] + both[:, dk:])

    k_inv_b, decays, k_end_blk, q_dec = [], [], [], []
    for cum, (_, _, _, k, _, q, _, _, forward) in zip(cums, jobs):
        k_inv = k.astype(F32) * jnp.exp(-cum)
        k_inv_b.append(k_inv.astype(BF16))
        tot_row = CHUNK - 1 if forward else 0
        dec_j, rows_j = [], []
        for ci in range(nc):
            dec = jnp.exp(cum[ci * CHUNK + tot_row:ci * CHUNK + tot_row + 1, :])
            k_end = (k_inv[ci * CHUNK:(ci + 1) * CHUNK] * dec).astype(BF16)
            dec_j.append(dec)
            rows_j.append(jnp.concatenate([k_end if cj == ci else zero for cj in range(nc)], axis=1))
        decays.append(dec_j)
        k_end_blk.append(jnp.concatenate(rows_j, axis=0))
        q_dec.append(None if q is None else (q.astype(F32) * jnp.exp(cum)).astype(BF16))

    upd_all = [lax.dot_general(job[4], blk, contract_rows, preferred_element_type=F32)
               for job, blk in zip(jobs, k_end_blk)]
    scores = [None if qd is None else lax.dot_general(qd, kb, contract_last, preferred_element_type=F32)
              for qd, kb in zip(q_dec, k_inv_b)]

    s_all = []
    for ji in range(nj):
        chain, forward = jobs[ji][0], jobs[ji][8]
        st = states[chain]
        entering = [None] * nc
        for ci in (range(nc) if forward else reversed(range(nc))):
            entering[ci] = st
            st = st * decays[ji][ci] + upd_all[ji][:, ci * dk:(ci + 1) * dk]
        states[chain] = st
        s_all.append(None if q_dec[ji] is None
                     else [jnp.concatenate([entering[ci].astype(BF16), entering[ci + 1].astype(BF16)], axis=0)
                           for ci in range(0, nc, 2)])

    outs = []
    for ji in range(nj):
        if q_dec[ji] is None:
            outs.append(None)
            continue
        sc = jnp.where(jobs[ji][7] > 0, scores[ji], 0.0).astype(BF16)
        intra = _dot(sc, jobs[ji][4])
        inter = []
        for pi, s_pair in enumerate(s_all[ji]):
            rows = slice(2 * pi * CHUNK, (2 * pi + 2) * CHUNK)
            both = lax.dot_general(q_dec[ji][rows], s_pair, contract_last, preferred_element_type=F32)
            inter += [both[:CHUNK, :dv], both[CHUNK:, dv:]]
        outs.append(intra + jnp.concatenate(inter, axis=0))
    return outs, states


def _hgrn_kernel(q_ref, kf_ref, kb_ref, hif_ref, hib_ref, lof_ref, lob_ref, v_ref, g_ref,
                 ckf_ref, ckb_ref, chif_ref, chib_ref, clof_ref, clob_ref, cv_ref, ng_ref, trif_ref, trib_ref,
                 keepf_ref, keepb_ref, y_ref, of_ref, ob_ref, *, seq, ctx_len):
    r = HG_TILE
    n_lat = seq // r
    n_ctx = ctx_len // r
    group = HG_GROUP

    def ctx_jobs():
        jobs = []
        for i in range(n_ctx):
            sl_f = pl.ds(i * r, r)
            sl_b = pl.ds((n_ctx - 1 - i) * r, r)
            jobs.append((0, chif_ref[0, sl_f, :], clof_ref[0, sl_f, :], ckf_ref[0, sl_f, :], cv_ref[0, sl_f, :], None,
                         trif_ref[...], None, True))
            jobs.append((1, chib_ref[0, sl_b, :], clob_ref[0, sl_b, :], ckb_ref[0, sl_b, :], cv_ref[0, sl_b, :], None,
                         trib_ref[...], None, False))
        return jobs

    ng = ng_ref[...]

    def readout(tile, o):
        sl = pl.ds(tile * r, r)
        o = o * lax.rsqrt(jnp.mean(o * o, axis=-1, keepdims=True) + RMS_EPS) * ng
        g = g_ref[0, sl, :].astype(F32)
        y_ref[0, sl, :] = (o * (g * _sigmoid(g))).astype(y_ref.dtype)

    def step_of(tile, forward):
        return (tile if forward else n_lat - 1 - tile) // group

    carry = (jnp.zeros((HG_D, HG_D), F32),) * 2
    for i in range(n_lat // group):
        jobs = ctx_jobs() if i == 0 else []
        meta = [None] * len(jobs)
        for t in range(group):
            for forward in (True, False):
                tile = i * group + t if forward else n_lat - 1 - (i * group + t)
                sl = pl.ds(tile * r, r)
                hi_ref, lo_ref, k_ref, tri_ref, keep_ref = ((hif_ref, lof_ref, kf_ref, trif_ref, keepf_ref) if forward
                                                            else (hib_ref, lob_ref, kb_ref, trib_ref, keepb_ref))
                jobs.append((0 if forward else 1, hi_ref[0, sl, :], lo_ref[0, sl, :], k_ref[0, sl, :], v_ref[0, sl, :],
                             q_ref[0, sl, :], tri_ref[...], keep_ref[...], forward))
                meta.append((tile, forward))
        outs, carry = _gla_group(jobs, carry)
        waiting = {}
        for m, o in zip(meta, outs):
            if m is None:
                continue
            tile, forward = m
            mine, other = (of_ref, ob_ref) if forward else (ob_ref, of_ref)
            other_step = step_of(tile, not forward)
            if other_step < i:
                readout(tile, o + other[pl.ds(tile * r, r), :])
            elif other_step > i:
                mine[pl.ds(tile * r, r), :] = o
            elif tile in waiting:
                readout(tile, o + waiting.pop(tile))
            else:
                waiting[tile] = o


def _chunk_tri(r, forward, dtype):
    i = np.arange(r)
    same = (i[:, None] // CHUNK) == (i[None, :] // CHUNK)
    order = (i[:, None] >= i[None, :]) if forward else (i[:, None] <= i[None, :])
    return jnp.asarray(same & order, dtype)


def _hgrn(q, k, hi, lo, v, g, ck, chi, clo, cv, ng):
    b, s, _ = q.shape
    lc = ck.shape[1]
    h = HG_HEADS
    d = HG_D
    r = HG_TILE
    kern = functools.partial(_hgrn_kernel, seq=s, ctx_len=lc)
    fwd = lambda i, j: (i, 0, j)
    bwd = lambda i, j: (i, 0, j + h)
    const = lambda i, j: (0, 0)
    lat = lambda im: pl.BlockSpec((1, s, d), im)
    cx = lambda im: pl.BlockSpec((1, lc, d), im)
    return pl.pallas_call(
        kern,
        grid=(b, h),
        in_specs=[lat(fwd), lat(fwd), lat(bwd), lat(fwd), lat(bwd), lat(fwd), lat(bwd), lat(fwd), lat(fwd),
                  cx(fwd), cx(bwd), cx(fwd), cx(bwd), cx(fwd), cx(bwd), cx(fwd),
                  pl.BlockSpec((1, d), lambda i, j: (0, j)),
                  pl.BlockSpec((r, r), const),
                  pl.BlockSpec((r, r), const),
                  pl.BlockSpec((r, r), const),
                  pl.BlockSpec((r, r), const)],
        out_specs=pl.BlockSpec((1, s, d), fwd),
        out_shape=jax.ShapeDtypeStruct((b, s, h * d), BF16),
        scratch_shapes=[pltpu.VMEM((s, d), F32), pltpu.VMEM((s, d), F32)],
        compiler_params=_cparams(("parallel", "parallel"), VMEM_LIMIT),
        name="hgrn2",
    )(q, k, k, hi, hi, lo, lo, v, g, ck, ck, chi, chi, clo, clo, cv, ng,
      _chunk_tri(r, True, BF16), _chunk_tri(r, False, BF16), _chunk_tri(r, True, F32), _chunk_tri(r, False, F32))


def _dft_tables(seq):
    rows = seq // GRID_W
    n = F_GROUP_DIM
    kc = np.outer(np.arange(n), np.arange(n)) % n
    ang = 2.0 * np.pi * kc / n
    norm = 1.0 / math.sqrt(rows * GRID_W * n)
    ch = np.concatenate([np.cos(ang), -np.sin(ang)], axis=1) * norm
    t = np.arange(seq)
    r, w = t // GRID_W, t % GRID_W
    m = (np.outer(r, r) * GRID_W + np.outer(w, w) * rows) % (rows * GRID_W)
    ang_t = 2.0 * np.pi * m / (rows * GRID_W)
    tok = np.concatenate([np.cos(ang_t), np.sin(ang_t)], axis=1)
    return jnp.asarray(ch, BF16), jnp.asarray(tok, BF16)


def _fourier_kernel(p_ref, ch_ref, tok_ref, y_ref, xs_ref, *, seq):
    for gi in range(F_GROUPS):
        cols = slice(gi * F_GROUP_DIM, (gi + 1) * F_GROUP_DIM)
        x1 = _dot(p_ref[0, :, cols], ch_ref[...]).astype(BF16)
        xs_ref[0:seq, cols] = x1[:, :F_GROUP_DIM]
        xs_ref[seq:2 * seq, cols] = x1[:, F_GROUP_DIM:]
    y_ref[0] = _dot(tok_ref[...], xs_ref[...]).astype(y_ref.dtype)


def _fourier(pf):
    b, s, wdt = pf.shape
    ch, tok = _dft_tables(s)
    kern = functools.partial(_fourier_kernel, seq=s)
    return pl.pallas_call(
        kern,
        grid=(b,),
        in_specs=[pl.BlockSpec((1, s, wdt), lambda i: (i, 0, 0)),
                  pl.BlockSpec(ch.shape, lambda i: (0, 0)),
                  pl.BlockSpec(tok.shape, lambda i: (0, 0), pipeline_mode=pl.Buffered(1))],
        out_specs=pl.BlockSpec((1, s, wdt), lambda i: (i, 0, 0)),
        out_shape=jax.ShapeDtypeStruct((b, s, wdt), BF16),
        scratch_shapes=[pltpu.VMEM((2 * s, wdt), BF16)],
        compiler_params=_cparams(("parallel",), VMEM_LIMIT),
        name="fourier",
    )(pf, ch, tok)


def _layer_norm(t, g, b):
    mu = jnp.mean(t, axis=-1, keepdims=True)
    tc = t - mu
    var = jnp.mean(tc * tc, axis=-1, keepdims=True)
    return tc * lax.rsqrt(var + LN_EPS) * g + b


def _merge_kernel(yf_ref, yh_ref, gt_ref, x_ref, g1_ref, sc2_ref, sh2_ref, wfo_ref, who_ref, wo_ref,
                  lg_ref, lbias_ref, wr_ref, br_ref, x1_ref, idx_ref, wt_ref):
    d = x_ref.shape[-1]
    gf = _sigmoid(gt_ref[0, :, :d].astype(F32))
    gh = _sigmoid(gt_ref[0, :, d:].astype(F32))
    m = gf * _dot(yf_ref[0], wfo_ref[...]) + gh * _dot(yh_ref[0], who_ref[...])
    mix = _dot(m.astype(BF16), wo_ref[...])
    x1 = _layer_norm(DEEPNORM_ALPHA * x_ref[0] + g1_ref[0] * mix, lg_ref[...], lbias_ref[...])
    x1_ref[0] = x1
    u2 = x1 * (1.0 + sc2_ref[0]) + sh2_ref[0]
    logits = lax.dot_general(wr_ref[...], u2.astype(BF16), (((1,), (1,)), ((), ())),
                             preferred_element_type=F32) + br_ref[...]
    expert = lax.broadcasted_iota(jnp.int32, logits.shape, 0)
    expert_f = expert.astype(F32)
    work = logits
    vals, idxs = [], []
    for _ in range(TOP_K):
        mx = jnp.max(work, axis=0, keepdims=True)
        sel = jnp.min(jnp.where(work == mx, expert_f, float(N_EXPERTS)), axis=0, keepdims=True).astype(jnp.int32)
        vals.append(mx)
        idxs.append(sel)
        work = jnp.where(expert == sel, -jnp.inf, work)
    exps = [jnp.exp(vv - vals[0]) for vv in vals]
    inv = 1.0 / (exps[0] + exps[1] + exps[2] + exps[3])
    for kk in range(TOP_K):
        idx_ref[kk:kk + 1, :] = idxs[kk]
        wt_ref[kk:kk + 1, :] = exps[kk] * inv


def _merge(yf, yh, gates, x, g1, sc2, sh2, wfo, who, wo, lg, lbias, wr, br, tm):
    b, s, d = x.shape
    tile = lambda i, j: (i, j, 0)
    per_b = lambda i, j: (i, 0, 0)
    const = lambda i, j: (0, 0)
    slots = lambda i, j: (0, i * (s // tm) + j)
    return pl.pallas_call(
        _merge_kernel,
        grid=(b, s // tm),
        in_specs=[pl.BlockSpec((1, tm, yf.shape[-1]), tile),
                  pl.BlockSpec((1, tm, yh.shape[-1]), tile),
                  pl.BlockSpec((1, tm, gates.shape[-1]), tile),
                  pl.BlockSpec((1, tm, d), tile),
                  pl.BlockSpec((1, 1, d), per_b),
                  pl.BlockSpec((1, 1, d), per_b),
                  pl.BlockSpec((1, 1, d), per_b),
                  pl.BlockSpec(wfo.shape, const),
                  pl.BlockSpec(who.shape, const),
                  pl.BlockSpec(wo.shape, const),
                  pl.BlockSpec((1, d), const),
                  pl.BlockSpec((1, d), const),
                  pl.BlockSpec(wr.shape, const),
                  pl.BlockSpec(br.shape, const)],
        out_specs=[pl.BlockSpec((1, tm, d), tile),
                   pl.BlockSpec((TOP_K, tm), slots),
                   pl.BlockSpec((TOP_K, tm), slots)],
        out_shape=[jax.ShapeDtypeStruct((b, s, d), F32),
                   jax.ShapeDtypeStruct((TOP_K, b * s), jnp.int32),
                   jax.ShapeDtypeStruct((TOP_K, b * s), F32)],
        compiler_params=_cparams(("parallel", "parallel"), VMEM_LIMIT),
        name="merge_ln_router",
    )(yf, yh, gates, x, g1, sc2, sh2, wfo, who, wo, lg, lbias, wr, br)


def _route_kernel(idx_ref, before_ref, dest_ref, cnt_ref, start_ref, run_ref, *, block_rows):
    phase = pl.program_id(0)
    t = pl.program_id(1)
    tt = idx_ref.shape[1]
    expert = lax.broadcasted_iota(jnp.int32, (N_EXPERTS, tt), 0)
    onehots = [jnp.where(expert == idx_ref[kk:kk + 1, :], 1.0, 0.0) for kk in range(TOP_K)]
    counts = [jnp.sum(oh, axis=1, keepdims=True) for oh in onehots]

    @pl.when(jnp.logical_and(phase == 0, t == 0))
    def _():
        run_ref[...] = jnp.zeros_like(run_ref)

    @pl.when(phase == 0)
    def _():
        run_ref[...] += counts[0] + counts[1] + counts[2] + counts[3]

    @pl.when(jnp.logical_and(phase == 1, t == 0))
    def _():
        cnt = run_ref[...]
        padded = jnp.floor((cnt + (block_rows - 1)) * (1.0 / block_rows)) * block_rows
        r = lax.broadcasted_iota(jnp.int32, (N_EXPERTS, N_EXPERTS), 0)
        c = lax.broadcasted_iota(jnp.int32, (N_EXPERTS, N_EXPERTS), 1)
        earlier = jnp.where(r > c, 1.0, 0.0)
        start = jnp.dot(earlier, padded, preferred_element_type=F32, precision=lax.Precision.HIGHEST)
        cnt_ref[...] = cnt.astype(jnp.int32)
        start_ref[...] = start.astype(jnp.int32)
        run_ref[...] = jnp.zeros_like(run_ref)

    @pl.when(phase == 1)
    def _():
        prefixes = _dot(jnp.concatenate([oh.astype(BF16) for oh in onehots], axis=0), before_ref[...])
        start = start_ref[:, 0:1].astype(F32)
        base = run_ref[:, 0:1] + start
        for kk in range(TOP_K):
            prefix = prefixes[kk * N_EXPERTS:(kk + 1) * N_EXPERTS, :]
            dest_ref[kk:kk + 1, :] = jnp.sum(onehots[kk] * (prefix + base), axis=0, keepdims=True).astype(jnp.int32)
            base = base + counts[kk]
        run_ref[...] = jnp.broadcast_to(base - start, run_ref.shape)


def _route(idx_t, block_rows, tt):
    n = idx_t.shape[1]
    kern = functools.partial(_route_kernel, block_rows=block_rows)
    before = jnp.asarray(np.triu(np.ones((tt, tt)), 1), BF16)
    stat = pl.BlockSpec((N_EXPERTS, LANES), lambda p, t: (0, 0))
    return pl.pallas_call(
        kern,
        grid=(2, n // tt),
        in_specs=[pl.BlockSpec((TOP_K, tt), lambda p, t: (0, t)),
                  pl.BlockSpec((tt, tt), lambda p, t: (0, 0))],
        out_specs=[pl.BlockSpec((TOP_K, tt), lambda p, t: (0, t * p)), stat, stat],
        out_shape=[jax.ShapeDtypeStruct((TOP_K, n), jnp.int32),
                   jax.ShapeDtypeStruct((N_EXPERTS, LANES), jnp.int32),
                   jax.ShapeDtypeStruct((N_EXPERTS, LANES), jnp.int32)],
        scratch_shapes=[pltpu.VMEM((N_EXPERTS, LANES), F32)],
        compiler_params=_cparams(("arbitrary", "arbitrary")),
        name="route_ranks",
    )(idx_t, before)


def _to_row_tiles(ref, base, val):
    rows = val.shape[0]
    for s in range(SUBLANES):
        ref[pl.ds(base + s, rows, stride=SUBLANES), :] = val[:, s * LANES:(s + 1) * LANES]


def _from_row_tiles(ref, base, rows):
    return jnp.concatenate([ref[pl.ds(base + s, rows, stride=SUBLANES), :] for s in range(SUBLANES)], axis=1)


def _row_tile(ref, r):
    return ref.at[pl.ds(pl.multiple_of(r * SUBLANES, SUBLANES), SUBLANES), :]


def _row_tiles(ref, r, n):
    return ref.at[pl.ds(pl.multiple_of(r * SUBLANES, SUBLANES), n * SUBLANES), :]


def _dispatch_kernel(cnt_ref, start_ref, dest_ref, x1_ref, sc_ref, sh_ref, h_ref, u_ref, zero_ref, sems, *, block_rows):
    t = pl.program_id(0)
    nt = pl.num_programs(0)
    tt = x1_ref.shape[0]
    slot = t % 2
    ubase = pl.multiple_of(slot * (tt * SUBLANES), tt * SUBLANES)

    _to_row_tiles(u_ref, ubase, x1_ref[...] * (1.0 + sc_ref[0]) + sh_ref[0])

    def issue(i, _):
        for kk in range(TOP_K):
            d = dest_ref[0, 0, kk * tt + i]
            pltpu.make_async_copy(_row_tile(u_ref, slot * tt + i), _row_tile(h_ref, d),
                                  sems.at[slot]).start(priority=kk % DMA_QUEUES)
        return 0

    lax.fori_loop(0, tt, issue, 0, unroll=ISSUE_UNROLL)

    def drain(sl):
        for _ in range(TOP_K):
            pltpu.make_async_copy(_row_tiles(u_ref, sl * tt, tt), _row_tiles(h_ref, 0, tt), sems.at[sl]).wait()

    @pl.when(t > 0)
    def _():
        drain(1 - slot)

    @pl.when(t == nt - 1)
    def _():
        drain(slot)
        zero_ref[...] = jnp.zeros_like(zero_ref)

        def pad_expert(e, _):
            cnt = cnt_ref[e]
            rem = (block_rows - (cnt & (block_rows - 1))) & (block_rows - 1)
            pos = start_ref[e] + cnt
            size = block_rows // 2
            while size >= 1:
                take = rem & size

                @pl.when(take != 0)
                def _(pos=pos, size=size):
                    cp = pltpu.make_async_copy(_row_tiles(zero_ref, 0, size), _row_tiles(h_ref, pos, size), sems.at[2])
                    cp.start()
                    cp.wait()

                pos = pos + take
                size //= 2
            return 0

        lax.fori_loop(0, N_EXPERTS, pad_expert, 0)


def _dispatch(counts, starts, dest_flat, x1, sc2, sh2, n_rows, tt):
    n, d = x1.shape
    b = sc2.shape[0]
    tiles_per_b = (n // b) // tt
    per_b = lambda t, c, s: (t // tiles_per_b, 0, 0)
    grid_spec = pltpu.PrefetchScalarGridSpec(
        num_scalar_prefetch=2,
        grid=(n // tt,),
        in_specs=[pl.BlockSpec((1, 1, tt * TOP_K), lambda t, c, s: (t, 0, 0), memory_space=pltpu.SMEM),
                  pl.BlockSpec((tt, d), lambda t, c, s: (t, 0)),
                  pl.BlockSpec((1, 1, d), per_b),
                  pl.BlockSpec((1, 1, d), per_b)],
        out_specs=pl.BlockSpec(memory_space=pl.ANY),
        scratch_shapes=[pltpu.VMEM((2 * tt * SUBLANES, LANES), F32), pltpu.VMEM((MOE_ROWS // 2 * SUBLANES, LANES), F32),
                        pltpu.SemaphoreType.DMA((3,))],
    )
    return pl.pallas_call(
        functools.partial(_dispatch_kernel, block_rows=MOE_ROWS),
        grid_spec=grid_spec,
        out_shape=jax.ShapeDtypeStruct((n_rows * SUBLANES, LANES), F32),
        compiler_params=_cparams(("arbitrary",)),
        name="moe_dispatch",
    )(counts, starts, dest_flat, x1, sc2, sh2)


def _expert_kernel(be_ref, nu_ref, nxt_ref, h_ref, wgu_hbm, bgu_ref, wdn_hbm, bdn_ref, o_ref,
                   wgu_f_ref, wdn_f_ref, wgu_b_ref, wdn_b_ref, sems):
    j = pl.program_id(0)
    d = wdn_b_ref.shape[-1]
    bm = h_ref.shape[0] // SUBLANES

    def weight_copies(e):
        return (pltpu.make_async_copy(wgu_hbm.at[e], wgu_f_ref, sems.at[0]),
                pltpu.make_async_copy(wdn_hbm.at[e], wdn_f_ref, sems.at[1]))

    @pl.when(j < nu_ref[0])
    def _():
        @pl.when(j == 0)
        def _():
            for cp in weight_copies(be_ref[0]):
                cp.start()

        @pl.when(jnp.logical_or(j == 0, be_ref[j] != be_ref[jnp.maximum(j - 1, 0)]))
        def _():
            for cp in weight_copies(be_ref[j]):
                cp.wait()
            wgu_b_ref[...] = wgu_f_ref[...].astype(BF16)
            wdn_b_ref[...] = wdn_f_ref[...].astype(BF16)

            @pl.when(nxt_ref[j] >= 0)
            def _():
                for cp in weight_copies(nxt_ref[j]):
                    cp.start()

        gu = _dot(_from_row_tiles(h_ref, 0, bm).astype(BF16), wgu_b_ref[...]) + bgu_ref[0]
        gate = jnp.minimum(gu[:, :d], SWIGLU_LIMIT)
        up = jnp.clip(gu[:, d:], -SWIGLU_LIMIT, SWIGLU_LIMIT)
        act = (up + 1.0) * gate * _sigmoid(SWIGLU_ALPHA * gate)
        _to_row_tiles(o_ref, 0, _dot(act.astype(BF16), wdn_b_ref[...]) + bdn_ref[0])


def _experts(h_sorted, block_expert, n_used, next_expert, wgu, bgu, wdn, bdn):
    n_rows = h_sorted.shape[0] // SUBLANES
    e, d, _ = wdn.shape
    bm = MOE_ROWS
    n_blocks = n_rows // bm
    blk = lambda j, be, nu, nx: (jnp.minimum(j, nu[0] - 1), 0)
    exp = lambda j, be, nu, nx: (be[jnp.minimum(j, nu[0] - 1)], 0, 0)
    grid_spec = pltpu.PrefetchScalarGridSpec(
        num_scalar_prefetch=3,
        grid=(n_blocks,),
        in_specs=[pl.BlockSpec((bm * SUBLANES, LANES), blk),
                  pl.BlockSpec(memory_space=pl.ANY),
                  pl.BlockSpec((1, 1, 2 * d), exp),
                  pl.BlockSpec(memory_space=pl.ANY),
                  pl.BlockSpec((1, 1, d), exp)],
        out_specs=pl.BlockSpec((bm * SUBLANES, LANES), blk),
        scratch_shapes=[pltpu.VMEM((d, 2 * d), F32), pltpu.VMEM((d, d), F32),
                        pltpu.VMEM((d, 2 * d), BF16), pltpu.VMEM((d, d), BF16), pltpu.SemaphoreType.DMA((2,))],
    )
    return pl.pallas_call(
        _expert_kernel,
        grid_spec=grid_spec,
        out_shape=jax.ShapeDtypeStruct((n_rows * SUBLANES, LANES), F32),
        compiler_params=_cparams(("arbitrary",), VMEM_LIMIT),
        name="moe_experts",
    )(block_expert, n_used, next_expert, h_sorted, wgu, bgu.reshape(e, 1, 2 * d), wdn, bdn.reshape(e, 1, d))


def _combine_kernel(dest_ref, dest_next_ref, wt_ref, x1_ref, g2_ref, lg_ref, lbias_ref, o_hbm, y_ref, buf_ref, sems):
    t = pl.program_id(0)
    nt = pl.num_programs(0)
    tt = x1_ref.shape[0]
    slot = t % 2

    def issue_row(d_ref, sl, i):
        for kk in range(TOP_K):
            d = d_ref[0, 0, kk * tt + i]
            pltpu.make_async_copy(_row_tile(o_hbm, d), _row_tile(buf_ref, (sl * TOP_K + kk) * tt + i),
                                  sems.at[sl]).start(priority=kk % DMA_QUEUES)

    def drain(sl):
        for kk in range(TOP_K):
            pltpu.make_async_copy(_row_tiles(o_hbm, 0, tt), _row_tiles(buf_ref, (sl * TOP_K + kk) * tt, tt),
                                  sems.at[sl]).wait()

    @pl.when(t == 0)
    def _():
        def first(i, _):
            issue_row(dest_ref, 0, i)
            return 0

        lax.fori_loop(0, tt, first, 0, unroll=ISSUE_UNROLL)

    drain(slot)

    def fused(g, _):
        r0 = pl.multiple_of(g * COMBINE_ROWS, COMBINE_ROWS)
        for i in range(COMBINE_ROWS):
            issue_row(dest_next_ref, 1 - slot, r0 + i)
        rows = pl.ds(r0, COMBINE_ROWS)
        ff = None
        for kk in range(TOP_K):
            base = pl.multiple_of(((slot * TOP_K + kk) * tt + r0) * SUBLANES, COMBINE_ROWS * SUBLANES)
            term = wt_ref[rows, kk:kk + 1] * _from_row_tiles(buf_ref, base, COMBINE_ROWS)
            ff = term if ff is None else ff + term
        y_ref[rows, :] = _layer_norm(DEEPNORM_ALPHA * x1_ref[rows, :] + g2_ref[0] * ff, lg_ref[...], lbias_ref[...])
        return 0

    lax.fori_loop(0, tt // COMBINE_ROWS, fused, 0)

    @pl.when(t == nt - 1)
    def _():
        drain(1 - slot)


def _combine(dest_flat, wt, x1, g2, lg, lbias, o_sorted, tt):
    n, d = x1.shape
    b = g2.shape[0]
    nt = n // tt
    tiles_per_b = (n // b) // tt
    const = lambda t: (0, 0)
    dest_spec = lambda im: pl.BlockSpec((1, 1, tt * TOP_K), im, memory_space=pltpu.SMEM)
    return pl.pallas_call(
        _combine_kernel,
        grid=(nt,),
        in_specs=[dest_spec(lambda t: (t, 0, 0)),
                  dest_spec(lambda t: (jnp.minimum(t + 1, nt - 1), 0, 0)),
                  pl.BlockSpec((tt, TOP_K), lambda t: (t, 0)),
                  pl.BlockSpec((tt, d), lambda t: (t, 0)),
                  pl.BlockSpec((1, 1, d), lambda t: (t // tiles_per_b, 0, 0)),
                  pl.BlockSpec((1, d), const),
                  pl.BlockSpec((1, d), const),
                  pl.BlockSpec(memory_space=pl.ANY)],
        out_specs=pl.BlockSpec((tt, d), lambda t: (t, 0)),
        out_shape=jax.ShapeDtypeStruct((n, d), F32),
        scratch_shapes=[pltpu.VMEM((2 * TOP_K * tt * SUBLANES, LANES), F32), pltpu.SemaphoreType.DMA((2,))],
        compiler_params=_cparams(("arbitrary",), VMEM_LIMIT),
        name="moe_combine",
    )(dest_flat, dest_flat, wt, x1, g2, lg, lbias, o_sorted)


def kernel(x, c, ctx, c_ctx, w_ada, b_ada, w_in, lb_raw, hg_norm_g, w_four_out, w_hg_out, w_o, ln1_g, ln1_b,
           w_router, b_router, w_gate_up, b_gate_up, w_down, b_down, ln2_g, ln2_b):
    b, s, d = x.shape
    lc = ctx.shape[1]
    n = b * s
    f_w = F_GROUPS * F_GROUP_DIM
    hk = HG_HEADS * HG_D
    assert w_ada.shape[0] == DEPTH and s % GRID_W == 0 and s % (HG_TILE * HG_GROUP) == 0 and lc % HG_TILE == 0
    assert s % ROW_TILE == 0 and ROW_TILE % MOE_TOKENS == 0 and MOE_TOKENS % COMBINE_ROWS == 0

    pad = (-(b + 1)) % 8
    c_rows = jnp.concatenate([c, c_ctx[None, :], jnp.zeros((pad, d), F32)], axis=0)
    mod = _ada(c_rows, w_ada[0], b_ada[0]).reshape(b + 1 + pad, N_MOD, 1, d)
    shift1, scale1, gate1, shift2, scale2, gate2 = (mod[:b, i] for i in range(N_MOD))
    cshift1 = jnp.broadcast_to(mod[b, 0], (b, 1, d))
    cscale1 = jnp.broadcast_to(mod[b, 1], (b, 1, d))

    lower_bounds = jnp.cumsum(jax.nn.softmax(lb_raw.astype(F32), axis=0), axis=0)[0]

    w_in_b = w_in[0].astype(BF16)
    o_f, o_q, o_z, o_v, o_g, o_gt = f_w, f_w + hk, f_w + 3 * hk, f_w + 4 * hk, f_w + 5 * hk, f_w + 5 * hk + 2 * d
    lb_row = lower_bounds.reshape(1, 2 * hk)
    pf, pq, pk, phi, plo, pv, pg, gates = _inproj(
        x, scale1, shift1, lb_row, w_in_b,
        [(0, f_w, "cast"), (o_f, hk, "silu"), (o_q, 2 * hk, "forget"), (o_z, hk, "cast"), (o_v, hk, "cast"),
         (o_g, 2 * d, "cast")], tm=ROW_TILE)
    w_ctx = w_in_b[:, o_q:o_v]
    ck, chi, clo, cv = _inproj(ctx, cscale1, cshift1, lb_row, w_ctx, [(0, 2 * hk, "forget"), (2 * hk, hk, "cast")], tm=lc)

    y_h = _hgrn(pq, pk, phi, plo, pv, pg, ck, chi, clo, cv, hg_norm_g[0].reshape(1, hk))
    y_f = _fourier(pf)

    wr = w_router[0].T.astype(BF16)
    br = b_router[0].reshape(N_EXPERTS, 1)
    x1, idx_t, wt_t = _merge(y_f, y_h, gates, x, gate1, scale2, shift2,
                         w_four_out[0].astype(BF16), w_hg_out[0].astype(BF16), w_o[0].astype(BF16),
                         ln1_g[0].reshape(1, d), ln1_b[0].reshape(1, d), wr, br, tm=ROW_TILE)

    dest_t, counts, starts = _route(idx_t, MOE_ROWS, ROW_TILE)
    n_blocks = -(-(n * TOP_K) // MOE_ROWS) + N_EXPERTS
    n_rows = n_blocks * MOE_ROWS
    cnt = counts[:, 0]
    st = starts[:, 0]
    pad_ends = st + (cnt + MOE_ROWS - 1) // MOE_ROWS * MOE_ROWS
    block_first_row = jnp.arange(n_blocks, dtype=jnp.int32) * MOE_ROWS
    block_expert = jnp.minimum(jnp.sum((pad_ends[None, :] <= block_first_row[:, None]).astype(jnp.int32), axis=1),
                               N_EXPERTS - 1)
    n_used = (pad_ends[-1] // MOE_ROWS).astype(jnp.int32).reshape(1)
    group_end_block = pad_ends[block_expert] // MOE_ROWS
    next_expert = jnp.where(group_end_block < n_used[0], block_expert[jnp.minimum(group_end_block, n_blocks - 1)], -1)

    tt = MOE_TOKENS
    dest_flat = dest_t.reshape(TOP_K, n // tt, tt).transpose(1, 0, 2).reshape(n // tt, 1, TOP_K * tt)
    x1_2d = x1.reshape(n, d)
    h_sorted = _dispatch(cnt, st, dest_flat, x1_2d, scale2, shift2, n_rows, tt)
    o_sorted = _experts(h_sorted, block_expert, n_used, next_expert, w_gate_up[0], b_gate_up[0], w_down[0], b_down[0])
    out = _combine(dest_flat, wt_t.T, x1_2d, gate2, ln2_g[0].reshape(1, d), ln2_b[0].reshape(1, d),
                   o_sorted, tt)
    return out.reshape(b, s, d)
```

```python
import functools
import math

import jax
import jax.numpy as jnp
import numpy as np
from jax import lax
from jax.experimental import pallas as pl
from jax.experimental.pallas import tpu as pltpu

F32 = jnp.float32
BF16 = jnp.bfloat16

GRID_W = 64
N_MOD = 6
F_GROUPS = 4
F_GROUP_DIM = 128
HG_HEADS = 4
HG_D = 128
CHUNK = 64
N_EXPERTS = 32
TOP_K = 4
SWIGLU_LIMIT = 7.0
SWIGLU_ALPHA = 1.702
LN_EPS = 1e-5
RMS_EPS = 1e-6
DEPTH = 1
DEEPNORM_ALPHA = (2.0 * DEPTH) ** 0.25

LANES = 128
SUBLANES = 8
ROW_TILE = 1024
MOE_TOKENS = 512
MOE_ROWS = 512
COMBINE_ROWS = 128
ISSUE_UNROLL = 4
DMA_QUEUES = 2
VMEM_LIMIT = 56 * 1024 * 1024


def _cparams(sem, vmem=None):
    return pltpu.CompilerParams(dimension_semantics=sem, vmem_limit_bytes=vmem)


def _dot(a, b):
    return jnp.dot(a, b, preferred_element_type=F32)


def _sigmoid(x):
    return 0.5 * jnp.tanh(0.5 * x) + 0.5


def _ada_kernel(c_ref, w_ref, b_ref, o_ref):
    c = c_ref[...]
    a = c * _sigmoid(c)
    o_ref[...] = jnp.dot(a, w_ref[...], preferred_element_type=F32,
                         precision=lax.Precision.HIGHEST) + b_ref[...]


def _ada(c_rows, w, b):
    r, d = c_rows.shape
    n = w.shape[1]
    tn = 512
    return pl.pallas_call(
        _ada_kernel,
        grid=(n // tn,),
        in_specs=[pl.BlockSpec((r, d), lambda j: (0, 0)),
                  pl.BlockSpec((d, tn), lambda j: (0, j)),
                  pl.BlockSpec((1, tn), lambda j: (0, j))],
        out_specs=pl.BlockSpec((r, tn), lambda j: (0, j)),
        out_shape=jax.ShapeDtypeStruct((r, n), F32),
        compiler_params=_cparams(("arbitrary",)),
        name="ada_mod",
    )(c_rows, w, b.reshape(1, n))


def _inproj_kernel(x_ref, sc_ref, sh_ref, lb_ref, w_ref, *o_refs, plan):
    u = (x_ref[0] * (1.0 + sc_ref[0]) + sh_ref[0]).astype(BF16)
    refs = iter(o_refs)
    for kind, chunks in plan:
        if kind == "forget":
            k_ref, hi_ref, lo_ref = next(refs), next(refs), next(refs)
            for (w0, o0, n) in chunks:
                lb = lb_ref[:, o0:o0 + n]
                f = lb + (1.0 - lb) * _sigmoid(_dot(u, w_ref[:, w0:w0 + n]))
                lf = jnp.log(f)
                hi = lf.astype(BF16)
                k_ref[0, :, o0:o0 + n] = (1.0 - f).astype(BF16)
                hi_ref[0, :, o0:o0 + n] = hi
                lo_ref[0, :, o0:o0 + n] = (lf - hi.astype(F32)).astype(BF16)
        else:
            o_ref = next(refs)
            for (w0, o0, n) in chunks:
                p = _dot(u, w_ref[:, w0:w0 + n])
                if kind == "silu":
                    p = p * _sigmoid(p)
                o_ref[0, :, o0:o0 + n] = p.astype(o_ref.dtype)


def _inproj(x, scale, shift, lb_row, w_bf16, outs, tm):
    b, s, d = x.shape
    n_w = w_bf16.shape[1]
    plan, widths = [], []
    for (c0, width, kind) in outs:
        step = min(width, 512)
        plan.append((kind, tuple((c0 + o, o, step) for o in range(0, width, step))))
        widths += [width] * (3 if kind == "forget" else 1)
    kern = functools.partial(_inproj_kernel, plan=tuple(plan))
    return pl.pallas_call(
        kern,
        grid=(b, s // tm),
        in_specs=[pl.BlockSpec((1, tm, d), lambda i, j: (i, j, 0)),
                  pl.BlockSpec((1, 1, d), lambda i, j: (i, 0, 0)),
                  pl.BlockSpec((1, 1, d), lambda i, j: (i, 0, 0)),
                  pl.BlockSpec(lb_row.shape, lambda i, j: (0, 0)),
                  pl.BlockSpec((d, n_w), lambda i, j: (0, 0), pipeline_mode=pl.Buffered(1))],
        out_specs=[pl.BlockSpec((1, tm, width), lambda i, j: (i, j, 0)) for width in widths],
        out_shape=[jax.ShapeDtypeStruct((b, s, width), BF16) for width in widths],
        compiler_params=_cparams(("parallel", "parallel"), VMEM_LIMIT),
        name="in_proj",
    )(x, scale, shift, lb_row, w_bf16)


HG_TILE = 256
HG_GROUP = 4


def _gla_group(jobs, states):
    states = list(states)
    nj = len(jobs)
    r, dk = jobs[0][1].shape
    dv = jobs[0][4].shape[1]
    nc = r // CHUNK
    zero = jnp.zeros((CHUNK, dk), BF16)
    contract_last = (((1,), (1,)), ((), ()))
    contract_rows = (((0,), (0,)), ((), ()))

    cums = []
    for (_, hi, lo, _, _, _, tri, _, _) in jobs:
        both = _dot(tri, jnp.concatenate([hi, lo], axis=1))
        cums.append(both[:, :dk] + both[:, dk:])

    k_inv_b, decays, k_end_blk, q_dec = [], [], [], []
    for cum, (_, _, _, k, _, q, _, _, forward) in zip(cums, jobs):
        k_inv = k.astype(F32) * jnp.exp(-cum)
        k_inv_b.append(k_inv.astype(BF16))
        tot_row = CHUNK - 1 if forward else 0
        dec_j, rows_j = [], []
        for ci in range(nc):
            dec = jnp.exp(cum[ci * CHUNK + tot_row:ci * CHUNK + tot_row + 1, :])
            k_end = (k_inv[ci * CHUNK:(ci + 1) * CHUNK] * dec).astype(BF16)
            dec_j.append(dec)
            rows_j.append(jnp.concatenate([k_end if cj == ci else zero for cj in range(nc)], axis=1))
        decays.append(dec_j)
        k_end_blk.append(jnp.concatenate(rows_j, axis=0))
        q_dec.append(None if q is None else (q.astype(F32) * jnp.exp(cum)).astype(BF16))

    upd_all = [lax.dot_general(job[4], blk, contract_rows, preferred_element_type=F32)
               for job, blk in zip(jobs, k_end_blk)]
    scores = [None if qd is None else lax.dot_general(qd, kb, contract_last, preferred_element_type=F32)
              for qd, kb in zip(q_dec, k_inv_b)]

    s_all = []
    for ji in range(nj):
        chain, forward = jobs[ji][0], jobs[ji][8]
        st = states[chain]
        entering = [None] * nc
        for ci in (range(nc) if forward else reversed(range(nc))):
            entering[ci] = st
            st = st * decays[ji][ci] + upd_all[ji][:, ci * dk:(ci + 1) * dk]
        states[chain] = st
        s_all.append(None if q_dec[ji] is None
                     else [jnp.concatenate([entering[ci].astype(BF16), entering[ci + 1].astype(BF16)], axis=0)
                           for ci in range(0, nc, 2)])

    outs = []
    for ji in range(nj):
        if q_dec[ji] is None:
            outs.append(None)
            continue
        sc = jnp.where(jobs[ji][7] > 0, scores[ji], 0.0).astype(BF16)
        intra = _dot(sc, jobs[ji][4])
        inter = []
        for pi, s_pair in enumerate(s_all[ji]):
            rows = slice(2 * pi * CHUNK, (2 * pi + 2) * CHUNK)
            both = lax.dot_general(q_dec[ji][rows], s_pair, contract_last, preferred_element_type=F32)
            inter += [both[:CHUNK, :dv], both[CHUNK:, dv:]]
        outs.append(intra + jnp.concatenate(inter, axis=0))
    return outs, states


def _hgrn_kernel(q_ref, kf_ref, kb_ref, hif_ref, hib_ref, lof_ref, lob_ref, v_ref, g_ref,
                 ckf_ref, ckb_ref, chif_ref, chib_ref, clof_ref, clob_ref, cv_ref, ng_ref, trif_ref, trib_ref,
                 keepf_ref, keepb_ref, y_ref, of_ref, ob_ref, *, seq, ctx_len):
    r = HG_TILE
    n_lat = seq // r
    n_ctx = ctx_len // r
    group = HG_GROUP

    def ctx_jobs():
        jobs = []
        for i in range(n_ctx):
            sl_f = pl.ds(i * r, r)
            sl_b = pl.ds((n_ctx - 1 - i) * r, r)
            jobs.append((0, chif_ref[0, sl_f, :], clof_ref[0, sl_f, :], ckf_ref[0, sl_f, :], cv_ref[0, sl_f, :], None,
                         trif_ref[...], None, True))
            jobs.append((1, chib_ref[0, sl_b, :], clob_ref[0, sl_b, :], ckb_ref[0, sl_b, :], cv_ref[0, sl_b, :], None,
                         trib_ref[...], None, False))
        return jobs

    ng = ng_ref[...]

    def readout(tile, o):
        sl = pl.ds(tile * r, r)
        o = o * lax.rsqrt(jnp.mean(o * o, axis=-1, keepdims=True) + RMS_EPS) * ng
        g = g_ref[0, sl, :].astype(F32)
        y_ref[0, sl, :] = (o * (g * _sigmoid(g))).astype(y_ref.dtype)

    def step_of(tile, forward):
        return (tile if forward else n_lat - 1 - tile) // group

    carry = (jnp.zeros((HG_D, HG_D), F32),) * 2
    for i in range(n_lat // group):
        jobs = ctx_jobs() if i == 0 else []
        meta = [None] * len(jobs)
        for t in range(group):
            for forward in (True, False):
                tile = i * group + t if forward else n_lat - 1 - (i * group + t)
                sl = pl.ds(tile * r, r)
                hi_ref, lo_ref, k_ref, tri_ref, keep_ref = ((hif_ref, lof_ref, kf_ref, trif_ref, keepf_ref) if forward
                                                            else (hib_ref, lob_ref, kb_ref, trib_ref, keepb_ref))
                jobs.append((0 if forward else 1, hi_ref[0, sl, :], lo_ref[0, sl, :], k_ref[0, sl, :], v_ref[0, sl, :],
                             q_ref[0, sl, :], tri_ref[...], keep_ref[...], forward))
                meta.append((tile, forward))
        outs, carry = _gla_group(jobs, carry)
        waiting = {}
        for m, o in zip(meta, outs):
            if m is None:
                continue
            tile, forward = m
            mine, other = (of_ref, ob_ref) if forward else (ob_ref, of_ref)
            other_step = step_of(tile, not forward)
            if other_step < i:
                readout(tile, o + other[pl.ds(tile * r, r), :])
            elif other_step > i:
                mine[pl.ds(tile * r, r), :] = o
            elif tile in waiting:
                readout(tile, o + waiting.pop(tile))
            else:
                waiting[tile] = o


def _chunk_tri(r, forward, dtype):
    i = np.arange(r)
    same = (i[:, None] // CHUNK) == (i[None, :] // CHUNK)
    order = (i[:, None] >= i[None, :]) if forward else (i[:, None] <= i[None, :])
    return jnp.asarray(same & order, dtype)


def _hgrn(q, k, hi, lo, v, g, ck, chi, clo, cv, ng):
    b, s, _ = q.shape
    lc = ck.shape[1]
    h = HG_HEADS
    d = HG_D
    r = HG_TILE
    kern = functools.partial(_hgrn_kernel, seq=s, ctx_len=lc)
    fwd = lambda i, j: (i, 0, j)
    bwd = lambda i, j: (i, 0, j + h)
    const = lambda i, j: (0, 0)
    lat = lambda im: pl.BlockSpec((1, s, d), im)
    cx = lambda im: pl.BlockSpec((1, lc, d), im)
    return pl.pallas_call(
        kern,
        grid=(b, h),
        in_specs=[lat(fwd), lat(fwd), lat(bwd), lat(fwd), lat(bwd), lat(fwd), lat(bwd), lat(fwd), lat(fwd),
                  cx(fwd), cx(bwd), cx(fwd), cx(bwd), cx(fwd), cx(bwd), cx(fwd),
                  pl.BlockSpec((1, d), lambda i, j: (0, j)),
                  pl.BlockSpec((r, r), const),
                  pl.BlockSpec((r, r), const),
                  pl.BlockSpec((r, r), const),
                  pl.BlockSpec((r, r), const)],
        out_specs=pl.BlockSpec((1, s, d), fwd),
        out_shape=jax.ShapeDtypeStruct((b, s, h * d), BF16),
        scratch_shapes=[pltpu.VMEM((s, d), F32), pltpu.VMEM((s, d), F32)],
        compiler_params=_cparams(("parallel", "parallel"), VMEM_LIMIT),
        name="hgrn2",
    )(q, k, k, hi, hi, lo, lo, v, g, ck, ck, chi, chi, clo, clo, cv, ng,
      _chunk_tri(r, True, BF16), _chunk_tri(r, False, BF16), _chunk_tri(r, True, F32), _chunk_tri(r, False, F32))


def _dft_tables(seq):
    rows = seq // GRID_W
    n = F_GROUP_DIM
    kc = np.outer(np.arange(n), np.arange(n)) % n
    ang = 2.0 * np.pi * kc / n
    norm = 1.0 / math.sqrt(rows * GRID_W * n)
    ch = np.concatenate([np.cos(ang), -np.sin(ang)], axis=1) * norm
    t = np.arange(seq)
    r, w = t // GRID_W, t % GRID_W
    m = (np.outer(r, r) * GRID_W + np.outer(w, w) * rows) % (rows * GRID_W)
    ang_t = 2.0 * np.pi * m / (rows * GRID_W)
    tok = np.concatenate([np.cos(ang_t), np.sin(ang_t)], axis=1)
    return jnp.asarray(ch, BF16), jnp.asarray(tok, BF16)


def _fourier_kernel(p_ref, ch_ref, tok_ref, y_ref, xs_ref, *, seq):
    for gi in range(F_GROUPS):
        cols = slice(gi * F_GROUP_DIM, (gi + 1) * F_GROUP_DIM)
        x1 = _dot(p_ref[0, :, cols], ch_ref[...]).astype(BF16)
        xs_ref[0:seq, cols] = x1[:, :F_GROUP_DIM]
        xs_ref[seq:2 * seq, cols] = x1[:, F_GROUP_DIM:]
    y_ref[0] = _dot(tok_ref[...], xs_ref[...]).astype(y_ref.dtype)


def _fourier(pf):
    b, s, wdt = pf.shape
    ch, tok = _dft_tables(s)
    kern = functools.partial(_fourier_kernel, seq=s)
    return pl.pallas_call(
        kern,
        grid=(b,),
        in_specs=[pl.BlockSpec((1, s, wdt), lambda i: (i, 0, 0)),
                  pl.BlockSpec(ch.shape, lambda i: (0, 0)),
                  pl.BlockSpec(tok.shape, lambda i: (0, 0), pipeline_mode=pl.Buffered(1))],
        out_specs=pl.BlockSpec((1, s, wdt), lambda i: (i, 0, 0)),
        out_shape=jax.ShapeDtypeStruct((b, s, wdt), BF16),
        scratch_shapes=[pltpu.VMEM((2 * s, wdt), BF16)],
        compiler_params=_cparams(("parallel",), VMEM_LIMIT),
        name="fourier",
    )(pf, ch, tok)


def _layer_norm(t, g, b):
    mu = jnp.mean(t, axis=-1, keepdims=True)
    tc = t - mu
    var = jnp.mean(tc * tc, axis=-1, keepdims=True)
    return tc * lax.rsqrt(var + LN_EPS) * g + b


def _merge_kernel(yf_ref, yh_ref, gt_ref, x_ref, g1_ref, sc2_ref, sh2_ref, wfo_ref, who_ref, wo_ref,
                  lg_ref, lbias_ref, wr_ref, br_ref, x1_ref, idx_ref, wt_ref):
    d = x_ref.shape[-1]
    gf = _sigmoid(gt_ref[0, :, :d].astype(F32))
    gh = _sigmoid(gt_ref[0, :, d:].astype(F32))
    m = gf * _dot(yf_ref[0], wfo_ref[...]) + gh * _dot(yh_ref[0], who_ref[...])
    mix = _dot(m.astype(BF16), wo_ref[...])
    x1 = _layer_norm(DEEPNORM_ALPHA * x_ref[0] + g1_ref[0] * mix, lg_ref[...], lbias_ref[...])
    x1_ref[0] = x1
    u2 = x1 * (1.0 + sc2_ref[0]) + sh2_ref[0]
    logits = lax.dot_general(wr_ref[...], u2.astype(BF16), (((1,), (1,)), ((), ())),
                             preferred_element_type=F32) + br_ref[...]
    expert = lax.broadcasted_iota(jnp.int32, logits.shape, 0)
    expert_f = expert.astype(F32)
    work = logits
    vals, idxs = [], []
    for _ in range(TOP_K):
        mx = jnp.max(work, axis=0, keepdims=True)
        sel = jnp.min(jnp.where(work == mx, expert_f, float(N_EXPERTS)), axis=0, keepdims=True).astype(jnp.int32)
        vals.append(mx)
        idxs.append(sel)
        work = jnp.where(expert == sel, -jnp.inf, work)
    exps = [jnp.exp(vv - vals[0]) for vv in vals]
    inv = 1.0 / (exps[0] + exps[1] + exps[2] + exps[3])
    for kk in range(TOP_K):
        idx_ref[kk:kk + 1, :] = idxs[kk]
        wt_ref[kk:kk + 1, :] = exps[kk] * inv


def _merge(yf, yh, gates, x, g1, sc2, sh2, wfo, who, wo, lg, lbias, wr, br, tm):
    b, s, d = x.shape
    tile = lambda i, j: (i, j, 0)
    per_b = lambda i, j: (i, 0, 0)
    const = lambda i, j: (0, 0)
    slots = lambda i, j: (0, i * (s // tm) + j)
    return pl.pallas_call(
        _merge_kernel,
        grid=(b, s // tm),
        in_specs=[pl.BlockSpec((1, tm, yf.shape[-1]), tile),
                  pl.BlockSpec((1, tm, yh.shape[-1]), tile),
                  pl.BlockSpec((1, tm, gates.shape[-1]), tile),
                  pl.BlockSpec((1, tm, d), tile),
                  pl.BlockSpec((1, 1, d), per_b),
                  pl.BlockSpec((1, 1, d), per_b),
                  pl.BlockSpec((1, 1, d), per_b),
                  pl.BlockSpec(wfo.shape, const),
                  pl.BlockSpec(who.shape, const),
                  pl.BlockSpec(wo.shape, const),
                  pl.BlockSpec((1, d), const),
                  pl.BlockSpec((1, d), const),
                  pl.BlockSpec(wr.shape, const),
                  pl.BlockSpec(br.shape, const)],
        out_specs=[pl.BlockSpec((1, tm, d), tile),
                   pl.BlockSpec((TOP_K, tm), slots),
                   pl.BlockSpec((TOP_K, tm), slots)],
        out_shape=[jax.ShapeDtypeStruct((b, s, d), F32),
                   jax.ShapeDtypeStruct((TOP_K, b * s), jnp.int32),
                   jax.ShapeDtypeStruct((TOP_K, b * s), F32)],
        compiler_params=_cparams(("parallel", "parallel"), VMEM_LIMIT),
        name="merge_ln_router",
    )(yf, yh, gates, x, g1, sc2, sh2, wfo, who, wo, lg, lbias, wr, br)


def _route_kernel(idx_ref, before_ref, dest_ref, cnt_ref, start_ref, run_ref, *, block_rows):
    phase = pl.program_id(0)
    t = pl.program_id(1)
    tt = idx_ref.shape[1]
    expert = lax.broadcasted_iota(jnp.int32, (N_EXPERTS, tt), 0)
    onehots = [jnp.where(expert == idx_ref[kk:kk + 1, :], 1.0, 0.0) for kk in range(TOP_K)]
    counts = [jnp.sum(oh, axis=1, keepdims=True) for oh in onehots]

    @pl.when(jnp.logical_and(phase == 0, t == 0))
    def _():
        run_ref[...] = jnp.zeros_like(run_ref)

    @pl.when(phase == 0)
    def _():
        run_ref[...] += counts[0] + counts[1] + counts[2] + counts[3]

    @pl.when(jnp.logical_and(phase == 1, t == 0))
    def _():
        cnt = run_ref[...]
        padded = jnp.floor((cnt + (block_rows - 1)) * (1.0 / block_rows)) * block_rows
        r = lax.broadcasted_iota(jnp.int32, (N_EXPERTS, N_EXPERTS), 0)
        c = lax.broadcasted_iota(jnp.int32, (N_EXPERTS, N_EXPERTS), 1)
        earlier = jnp.where(r > c, 1.0, 0.0)
        start = jnp.dot(earlier, padded, preferred_element_type=F32, precision=lax.Precision.HIGHEST)
        cnt_ref[...] = cnt.astype(jnp.int32)
        start_ref[...] = start.astype(jnp.int32)
        run_ref[...] = jnp.zeros_like(run_ref)

    @pl.when(phase == 1)
    def _():
        prefixes = _dot(jnp.concatenate([oh.astype(BF16) for oh in onehots], axis=0), before_ref[...])
        start = start_ref[:, 0:1].astype(F32)
        base = run_ref[:, 0:1] + start
        for kk in range(TOP_K):
            prefix = prefixes[kk * N_EXPERTS:(kk + 1) * N_EXPERTS, :]
            dest_ref[kk:kk + 1, :] = jnp.sum(onehots[kk] * (prefix + base), axis=0, keepdims=True).astype(jnp.int32)
            base = base + counts[kk]
        run_ref[...] = jnp.broadcast_to(base - start, run_ref.shape)


def _route(idx_t, block_rows, tt):
    n = idx_t.shape[1]
    kern = functools.partial(_route_kernel, block_rows=block_rows)
    before = jnp.asarray(np.triu(np.ones((tt, tt)), 1), BF16)
    stat = pl.BlockSpec((N_EXPERTS, LANES), lambda p, t: (0, 0))
    return pl.pallas_call(
        kern,
        grid=(2, n // tt),
        in_specs=[pl.BlockSpec((TOP_K, tt), lambda p, t: (0, t)),
                  pl.BlockSpec((tt, tt), lambda p, t: (0, 0))],
        out_specs=[pl.BlockSpec((TOP_K, tt), lambda p, t: (0, t * p)), stat, stat],
        out_shape=[jax.ShapeDtypeStruct((TOP_K, n), jnp.int32),
                   jax.ShapeDtypeStruct((N_EXPERTS, LANES), jnp.int32),
                   jax.ShapeDtypeStruct((N_EXPERTS, LANES), jnp.int32)],
        scratch_shapes=[pltpu.VMEM((N_EXPERTS, LANES), F32)],
        compiler_params=_cparams(("arbitrary", "arbitrary")),
        name="route_ranks",
    )(idx_t, before)


def _to_row_tiles(ref, base, val):
    rows = val.shape[0]
    for s in range(SUBLANES):
        ref[pl.ds(base + s, rows, stride=SUBLANES), :] = val[:, s * LANES:(s + 1) * LANES]


def _from_row_tiles(ref, base, rows):
    return jnp.concatenate([ref[pl.ds(base + s, rows, stride=SUBLANES), :] for s in range(SUBLANES)], axis=1)


def _row_tile(ref, r):
    return ref.at[pl.ds(pl.multiple_of(r * SUBLANES, SUBLANES), SUBLANES), :]


def _row_tiles(ref, r, n):
    return ref.at[pl.ds(pl.multiple_of(r * SUBLANES, SUBLANES), n * SUBLANES), :]


def _dispatch_kernel(cnt_ref, start_ref, dest_ref, x1_ref, sc_ref, sh_ref, h_ref, u_ref, zero_ref, sems, *, block_rows):
    t = pl.program_id(0)
    nt = pl.num_programs(0)
    tt = x1_ref.shape[0]
    slot = t % 2
    ubase = pl.multiple_of(slot * (tt * SUBLANES), tt * SUBLANES)

    _to_row_tiles(u_ref, ubase, x1_ref[...] * (1.0 + sc_ref[0]) + sh_ref[0])

    def issue(i, _):
        for kk in range(TOP_K):
            d = dest_ref[0, 0, kk * tt + i]
            pltpu.make_async_copy(_row_tile(u_ref, slot * tt + i), _row_tile(h_ref, d),
                                  sems.at[slot]).start(priority=kk % DMA_QUEUES)
        return 0

    lax.fori_loop(0, tt, issue, 0, unroll=ISSUE_UNROLL)

    def drain(sl):
        for _ in range(TOP_K):
            pltpu.make_async_copy(_row_tiles(u_ref, sl * tt, tt), _row_tiles(h_ref, 0, tt), sems.at[sl]).wait()

    @pl.when(t > 0)
    def _():
        drain(1 - slot)

    @pl.when(t == nt - 1)
    def _():
        drain(slot)
        zero_ref[...] = jnp.zeros_like(zero_ref)

        def pad_expert(e, _):
            cnt = cnt_ref[e]
            rem = (block_rows - (cnt & (block_rows - 1))) & (block_rows - 1)
            pos = start_ref[e] + cnt
            size = block_rows // 2
            while size >= 1:
                take = rem & size

                @pl.when(take != 0)
                def _(pos=pos, size=size):
                    cp = pltpu.make_async_copy(_row_tiles(zero_ref, 0, size), _row_tiles(h_ref, pos, size), sems.at[2])
                    cp.start()
                    cp.wait()

                pos = pos + take
                size //= 2
            return 0

        lax.fori_loop(0, N_EXPERTS, pad_expert, 0)


def _dispatch(counts, starts, dest_flat, x1, sc2, sh2, n_rows, tt):
    n, d = x1.shape
    b = sc2.shape[0]
    tiles_per_b = (n // b) // tt
    per_b = lambda t, c, s: (t // tiles_per_b, 0, 0)
    grid_spec = pltpu.PrefetchScalarGridSpec(
        num_scalar_prefetch=2,
        grid=(n // tt,),
        in_specs=[pl.BlockSpec((1, 1, tt * TOP_K), lambda t, c, s: (t, 0, 0), memory_space=pltpu.SMEM),
                  pl.BlockSpec((tt, d), lambda t, c, s: (t, 0)),
                  pl.BlockSpec((1, 1, d), per_b),
                  pl.BlockSpec((1, 1, d), per_b)],
        out_specs=pl.BlockSpec(memory_space=pl.ANY),
        scratch_shapes=[pltpu.VMEM((2 * tt * SUBLANES, LANES), F32), pltpu.VMEM((MOE_ROWS // 2 * SUBLANES, LANES), F32),
                        pltpu.SemaphoreType.DMA((3,))],
    )
    return pl.pallas_call(
        functools.partial(_dispatch_kernel, block_rows=MOE_ROWS),
        grid_spec=grid_spec,
        out_shape=jax.ShapeDtypeStruct((n_rows * SUBLANES, LANES), F32),
        compiler_params=_cparams(("arbitrary",)),
        name="moe_dispatch",
    )(counts, starts, dest_flat, x1, sc2, sh2)


def _expert_kernel(be_ref, nu_ref, nxt_ref, h_ref, wgu_hbm, bgu_ref, wdn_hbm, bdn_ref, o_ref,
                   wgu_f_ref, wdn_f_ref, wgu_b_ref, wdn_b_ref, sems):
    j = pl.program_id(0)
    d = wdn_b_ref.shape[-1]
    bm = h_ref.shape[0] // SUBLANES

    def weight_copies(e):
        return (pltpu.make_async_copy(wgu_hbm.at[e], wgu_f_ref, sems.at[0]),
                pltpu.make_async_copy(wdn_hbm.at[e], wdn_f_ref, sems.at[1]))

    @pl.when(j < nu_ref[0])
    def _():
        @pl.when(j == 0)
        def _():
            for cp in weight_copies(be_ref[0]):
                cp.start()

        @pl.when(jnp.logical_or(j == 0, be_ref[j] != be_ref[jnp.maximum(j - 1, 0)]))
        def _():
            for cp in weight_copies(be_ref[j]):
                cp.wait()
            wgu_b_ref[...] = wgu_f_ref[...].astype(BF16)
            wdn_b_ref[...] = wdn_f_ref[...].astype(BF16)

            @pl.when(nxt_ref[j] >= 0)
            def _():
                for cp in weight_copies(nxt_ref[j]):
                    cp.start()

        gu = _dot(_from_row_tiles(h_ref, 0, bm).astype(BF16), wgu_b_ref[...]) + bgu_ref[0]
        gate = jnp.minimum(gu[:, :d], SWIGLU_LIMIT)
        up = jnp.clip(gu[:, d:], -SWIGLU_LIMIT, SWIGLU_LIMIT)
        act = (up + 1.0) * gate * _sigmoid(SWIGLU_ALPHA * gate)
        _to_row_tiles(o_ref, 0, _dot(act.astype(BF16), wdn_b_ref[...]) + bdn_ref[0])


def _experts(h_sorted, block_expert, n_used, next_expert, wgu, bgu, wdn, bdn):
    n_rows = h_sorted.shape[0] // SUBLANES
    e, d, _ = wdn.shape
    bm = MOE_ROWS
    n_blocks = n_rows // bm
    blk = lambda j, be, nu, nx: (jnp.minimum(j, nu[0] - 1), 0)
    exp = lambda j, be, nu, nx: (be[jnp.minimum(j, nu[0] - 1)], 0, 0)
    grid_spec = pltpu.PrefetchScalarGridSpec(
        num_scalar_prefetch=3,
        grid=(n_blocks,),
        in_specs=[pl.BlockSpec((bm * SUBLANES, LANES), blk),
                  pl.BlockSpec(memory_space=pl.ANY),
                  pl.BlockSpec((1, 1, 2 * d), exp),
                  pl.BlockSpec(memory_space=pl.ANY),
                  pl.BlockSpec((1, 1, d), exp)],
        out_specs=pl.BlockSpec((bm * SUBLANES, LANES), blk),
        scratch_shapes=[pltpu.VMEM((d, 2 * d), F32), pltpu.VMEM((d, d), F32),
                        pltpu.VMEM((d, 2 * d), BF16), pltpu.VMEM((d, d), BF16), pltpu.SemaphoreType.DMA((2,))],
    )
    return pl.pallas_call(
        _expert_kernel,
        grid_spec=grid_spec,
        out_shape=jax.ShapeDtypeStruct((n_rows * SUBLANES, LANES), F32),
        compiler_params=_cparams(("arbitrary",), VMEM_LIMIT),
        name="moe_experts",
    )(block_expert, n_used, next_expert, h_sorted, wgu, bgu.reshape(e, 1, 2 * d), wdn, bdn.reshape(e, 1, d))


def _combine_kernel(dest_ref, dest_next_ref, wt_ref, x1_ref, g2_ref, lg_ref, lbias_ref, o_hbm, y_ref, buf0_ref, buf1_ref, sems):
    t = pl.program_id(0)
    nt = pl.num_programs(0)
    tt = x1_ref.shape[0]
    slot = t % 2

    bufs = (buf0_ref, buf1_ref)

    def issue_row(d_ref, sl, i):
        for kk in range(TOP_K):
            d = d_ref[0, 0, kk * tt + i]
            pltpu.make_async_copy(_row_tile(o_hbm, d), _row_tile(bufs[sl], kk * tt + i),
                                  sems.at[sl]).start(priority=kk % DMA_QUEUES)

    def drain(sl):
        for kk in range(TOP_K):
            pltpu.make_async_copy(_row_tiles(o_hbm, 0, tt), _row_tiles(bufs[sl], kk * tt, tt), sems.at[sl]).wait()

    @pl.when(t == 0)
    def _():
        def first(i, _):
            issue_row(dest_ref, 0, i)
            return 0

        lax.fori_loop(0, tt, first, 0, unroll=ISSUE_UNROLL)

    def fused(sl):
        drain(sl)
        for r0 in range(0, tt, COMBINE_ROWS):
            for i in range(r0, r0 + COMBINE_ROWS):
                issue_row(dest_next_ref, 1 - sl, i)
            rows = pl.ds(r0, COMBINE_ROWS)
            ff = None
            for kk in range(TOP_K):
                term = wt_ref[rows, kk:kk + 1] * _from_row_tiles(bufs[sl], (kk * tt + r0) * SUBLANES, COMBINE_ROWS)
                ff = term if ff is None else ff + term
            y_ref[rows, :] = _layer_norm(DEEPNORM_ALPHA * x1_ref[rows, :] + g2_ref[0] * ff, lg_ref[...], lbias_ref[...])

        @pl.when(t == nt - 1)
        def _():
            drain(1 - sl)

    for sl in range(2):
        pl.when(slot == sl)(functools.partial(fused, sl))


def _combine(dest_flat, wt, x1, g2, lg, lbias, o_sorted, tt):
    n, d = x1.shape
    b = g2.shape[0]
    nt = n // tt
    tiles_per_b = (n // b) // tt
    const = lambda t: (0, 0)
    dest_spec = lambda im: pl.BlockSpec((1, 1, tt * TOP_K), im, memory_space=pltpu.SMEM)
    return pl.pallas_call(
        _combine_kernel,
        grid=(nt,),
        in_specs=[dest_spec(lambda t: (t, 0, 0)),
                  dest_spec(lambda t: (jnp.minimum(t + 1, nt - 1), 0, 0)),
                  pl.BlockSpec((tt, TOP_K), lambda t: (t, 0)),
                  pl.BlockSpec((tt, d), lambda t: (t, 0)),
                  pl.BlockSpec((1, 1, d), lambda t: (t // tiles_per_b, 0, 0)),
                  pl.BlockSpec((1, d), const),
                  pl.BlockSpec((1, d), const),
                  pl.BlockSpec(memory_space=pl.ANY)],
        out_specs=pl.BlockSpec((tt, d), lambda t: (t, 0)),
        out_shape=jax.ShapeDtypeStruct((n, d), F32),
        scratch_shapes=[pltpu.VMEM((TOP_K * tt * SUBLANES, LANES), F32)] * 2 + [pltpu.SemaphoreType.DMA((2,))],
        compiler_params=_cparams(("arbitrary",), VMEM_LIMIT),
        name="moe_combine",
    )(dest_flat, dest_flat, wt, x1, g2, lg, lbias, o_sorted)


def kernel(x, c, ctx, c_ctx, w_ada, b_ada, w_in, lb_raw, hg_norm_g, w_four_out, w_hg_out, w_o, ln1_g, ln1_b,
           w_router, b_router, w_gate_up, b_gate_up, w_down, b_down, ln2_g, ln2_b):
    b, s, d = x.shape
    lc = ctx.shape[1]
    n = b * s
    f_w = F_GROUPS * F_GROUP_DIM
    hk = HG_HEADS * HG_D
    assert w_ada.shape[0] == DEPTH and s % GRID_W == 0 and s % (HG_TILE * HG_GROUP) == 0 and lc % HG_TILE == 0
    assert s % ROW_TILE == 0 and ROW_TILE % MOE_TOKENS == 0 and MOE_TOKENS % COMBINE_ROWS == 0

    pad = (-(b + 1)) % 8
    c_rows = jnp.concatenate([c, c_ctx[None, :], jnp.zeros((pad, d), F32)], axis=0)
    mod = _ada(c_rows, w_ada[0], b_ada[0]).reshape(b + 1 + pad, N_MOD, 1, d)
    shift1, scale1, gate1, shift2, scale2, gate2 = (mod[:b, i] for i in range(N_MOD))
    cshift1 = jnp.broadcast_to(mod[b, 0], (b, 1, d))
    cscale1 = jnp.broadcast_to(mod[b, 1], (b, 1, d))

    lower_bounds = jnp.cumsum(jax.nn.softmax(lb_raw.astype(F32), axis=0), axis=0)[0]

    w_in_b = w_in[0].astype(BF16)
    o_f, o_q, o_z, o_v, o_g, o_gt = f_w, f_w + hk, f_w + 3 * hk, f_w + 4 * hk, f_w + 5 * hk, f_w + 5 * hk + 2 * d
    lb_row = lower_bounds.reshape(1, 2 * hk)
    pf, pq, pk, phi, plo, pv, pg, gates = _inproj(
        x, scale1, shift1, lb_row, w_in_b,
        [(0, f_w, "cast"), (o_f, hk, "silu"), (o_q, 2 * hk, "forget"), (o_z, hk, "cast"), (o_v, hk, "cast"),
         (o_g, 2 * d, "cast")], tm=ROW_TILE)
    w_ctx = w_in_b[:, o_q:o_v]
    ck, chi, clo, cv = _inproj(ctx, cscale1, cshift1, lb_row, w_ctx, [(0, 2 * hk, "forget"), (2 * hk, hk, "cast")], tm=lc)

    y_h = _hgrn(pq, pk, phi, plo, pv, pg, ck, chi, clo, cv, hg_norm_g[0].reshape(1, hk))
    y_f = _fourier(pf)

    wr = w_router[0].T.astype(BF16)
    br = b_router[0].reshape(N_EXPERTS, 1)
    x1, idx_t, wt_t = _merge(y_f, y_h, gates, x, gate1, scale2, shift2,
                         w_four_out[0].astype(BF16), w_hg_out[0].astype(BF16), w_o[0].astype(BF16),
                         ln1_g[0].reshape(1, d), ln1_b[0].reshape(1, d), wr, br, tm=ROW_TILE)

    dest_t, counts, starts = _route(idx_t, MOE_ROWS, ROW_TILE)
    n_blocks = -(-(n * TOP_K) // MOE_ROWS) + N_EXPERTS
    n_rows = n_blocks * MOE_ROWS
    cnt = counts[:, 0]
    st = starts[:, 0]
    pad_ends = st + (cnt + MOE_ROWS - 1) // MOE_ROWS * MOE_ROWS
    block_first_row = jnp.arange(n_blocks, dtype=jnp.int32) * MOE_ROWS
    block_expert = jnp.minimum(jnp.sum((pad_ends[None, :] <= block_first_row[:, None]).astype(jnp.int32), axis=1),
                               N_EXPERTS - 1)
    n_used = (pad_ends[-1] // MOE_ROWS).astype(jnp.int32).reshape(1)
    group_end_block = pad_ends[block_expert] // MOE_ROWS
    next_expert = jnp.where(group_end_block < n_used[0], block_expert[jnp.minimum(group_end_block, n_blocks - 1)], -1)

    tt = MOE_TOKENS
    dest_flat = dest_t.reshape(TOP_K, n // tt, tt).transpose(1, 0, 2).reshape(n // tt, 1, TOP_K * tt)
    x1_2d = x1.reshape(n, d)
    h_sorted = _dispatch(cnt, st, dest_flat, x1_2d, scale2, shift2, n_rows, tt)
    o_sorted = _experts(h_sorted, block_expert, n_used, next_expert, w_gate_up[0], b_gate_up[0], w_down[0], b_down[0])
    out = _combine(dest_flat, wt_t.T, x1_2d, gate2, ln2_g[0].reshape(1, d), ln2_b[0].reshape(1, d),
                   o_sorted, tt)
    return out.reshape(b, s, d)
```

```python
import functools
import math

import jax
import jax.numpy as jnp
import numpy as np
from jax import lax
from jax.experimental import pallas as pl
from jax.experimental.pallas import tpu as pltpu

F32 = jnp.float32
BF16 = jnp.bfloat16

GRID_W = 64
N_MOD = 6
F_GROUPS = 4
F_GROUP_DIM = 128
HG_HEADS = 4
HG_D = 128
CHUNK = 64
N_EXPERTS = 32
TOP_K = 4
SWIGLU_LIMIT = 7.0
SWIGLU_ALPHA = 1.702
LN_EPS = 1e-5
RMS_EPS = 1e-6
DEPTH = 1
DEEPNORM_ALPHA = (2.0 * DEPTH) ** 0.25

LANES = 128
SUBLANES = 8
ROW_TILE = 1024
MOE_TOKENS = 512
HOSTED_TOKENS = 256
MOE_ROWS = 512
COMBINE_ROWS = 128
ISSUE_UNROLL = 4
DMA_QUEUES = 2
VMEM_LIMIT = 56 * 1024 * 1024


def _cparams(sem, vmem=None):
    return pltpu.CompilerParams(dimension_semantics=sem, vmem_limit_bytes=vmem)


def _dot(a, b):
    return jnp.dot(a, b, preferred_element_type=F32)


def _sigmoid(x):
    return 0.5 * jnp.tanh(0.5 * x) + 0.5


def _ada_kernel(c_ref, w_ref, b_ref, o_ref):
    c = c_ref[...]
    a = c * _sigmoid(c)
    o_ref[...] = jnp.dot(a, w_ref[...], preferred_element_type=F32,
                         precision=lax.Precision.HIGHEST) + b_ref[...]


def _ada(c_rows, w, b):
    r, d = c_rows.shape
    n = w.shape[1]
    tn = 512
    return pl.pallas_call(
        _ada_kernel,
        grid=(n // tn,),
        in_specs=[pl.BlockSpec((r, d), lambda j: (0, 0)),
                  pl.BlockSpec((d, tn), lambda j: (0, j)),
                  pl.BlockSpec((1, tn), lambda j: (0, j))],
        out_specs=pl.BlockSpec((r, tn), lambda j: (0, j)),
        out_shape=jax.ShapeDtypeStruct((r, n), F32),
        compiler_params=_cparams(("arbitrary",)),
        name="ada_mod",
    )(c_rows, w, b.reshape(1, n))


def _inproj_kernel(x_ref, sc_ref, sh_ref, lb_ref, w_ref, *o_refs, plan):
    u = (x_ref[0] * (1.0 + sc_ref[0]) + sh_ref[0]).astype(BF16)
    refs = iter(o_refs)
    for kind, chunks in plan:
        if kind == "forget":
            k_ref, hi_ref, lo_ref = next(refs), next(refs), next(refs)
            for (w0, o0, n) in chunks:
                lb = lb_ref[:, o0:o0 + n]
                f = lb + (1.0 - lb) * _sigmoid(_dot(u, w_ref[:, w0:w0 + n]))
                lf = jnp.log(f)
                hi = lf.astype(BF16)
                k_ref[0, :, o0:o0 + n] = (1.0 - f).astype(BF16)
                hi_ref[0, :, o0:o0 + n] = hi
                lo_ref[0, :, o0:o0 + n] = (lf - hi.astype(F32)).astype(BF16)
        else:
            o_ref = next(refs)
            for (w0, o0, n) in chunks:
                p = _dot(u, w_ref[:, w0:w0 + n])
                if kind == "silu":
                    p = p * _sigmoid(p)
                o_ref[0, :, o0:o0 + n] = p.astype(o_ref.dtype)


def _inproj(x, scale, shift, lb_row, w_bf16, outs, tm):
    b, s, d = x.shape
    n_w = w_bf16.shape[1]
    plan, widths = [], []
    for (c0, width, kind) in outs:
        step = min(width, 512)
        plan.append((kind, tuple((c0 + o, o, step) for o in range(0, width, step))))
        widths += [width] * (3 if kind == "forget" else 1)
    kern = functools.partial(_inproj_kernel, plan=tuple(plan))
    return pl.pallas_call(
        kern,
        grid=(b, s // tm),
        in_specs=[pl.BlockSpec((1, tm, d), lambda i, j: (i, j, 0)),
                  pl.BlockSpec((1, 1, d), lambda i, j: (i, 0, 0)),
                  pl.BlockSpec((1, 1, d), lambda i, j: (i, 0, 0)),
                  pl.BlockSpec(lb_row.shape, lambda i, j: (0, 0)),
                  pl.BlockSpec((d, n_w), lambda i, j: (0, 0), pipeline_mode=pl.Buffered(1))],
        out_specs=[pl.BlockSpec((1, tm, width), lambda i, j: (i, j, 0)) for width in widths],
        out_shape=[jax.ShapeDtypeStruct((b, s, width), BF16) for width in widths],
        compiler_params=_cparams(("parallel", "parallel"), VMEM_LIMIT),
        name="in_proj",
    )(x, scale, shift, lb_row, w_bf16)


HG_TILE = 256
HG_GROUP = 4


def _gla_group(jobs, states):
    states = list(states)
    nj = len(jobs)
    r, dk = jobs[0][1].shape
    dv = jobs[0][4].shape[1]
    nc = r // CHUNK
    zero = jnp.zeros((CHUNK, dk), BF16)
    contract_last = (((1,), (1,)), ((), ()))
    contract_rows = (((0,), (0,)), ((), ()))

    cums = []
    for (_, hi, lo, _, _, _, tri, _, _) in jobs:
        both = _dot(tri, jnp.concatenate([hi, lo], axis=1))
        cums.append(both[:, :dk] + both[:, dk:])

    k_inv_b, decays, k_end_blk, q_dec = [], [], [], []
    for cum, (_, _, _, k, _, q, _, _, forward) in zip(cums, jobs):
        k_inv = k.astype(F32) * jnp.exp(-cum)
        k_inv_b.append(k_inv.astype(BF16))
        tot_row = CHUNK - 1 if forward else 0
        dec_j, rows_j = [], []
        for ci in range(nc):
            dec = jnp.exp(cum[ci * CHUNK + tot_row:ci * CHUNK + tot_row + 1, :])
            k_end = (k_inv[ci * CHUNK:(ci + 1) * CHUNK] * dec).astype(BF16)
            dec_j.append(dec)
            rows_j.append(jnp.concatenate([k_end if cj == ci else zero for cj in range(nc)], axis=1))
        decays.append(dec_j)
        k_end_blk.append(jnp.concatenate(rows_j, axis=0))
        q_dec.append(None if q is None else (q.astype(F32) * jnp.exp(cum)).astype(BF16))

    upd_all = [lax.dot_general(job[4], blk, contract_rows, preferred_element_type=F32)
               for job, blk in zip(jobs, k_end_blk)]
    scores = [None if qd is None else lax.dot_general(qd, kb, contract_last, preferred_element_type=F32)
              for qd, kb in zip(q_dec, k_inv_b)]

    s_all = []
    for ji in range(nj):
        chain, forward = jobs[ji][0], jobs[ji][8]
        st = states[chain]
        entering = [None] * nc
        for ci in (range(nc) if forward else reversed(range(nc))):
            entering[ci] = st
            st = st * decays[ji][ci] + upd_all[ji][:, ci * dk:(ci + 1) * dk]
        states[chain] = st
        s_all.append(None if q_dec[ji] is None
                     else [jnp.concatenate([entering[ci].astype(BF16), entering[ci + 1].astype(BF16)], axis=0)
                           for ci in range(0, nc, 2)])

    outs = []
    for ji in range(nj):
        if q_dec[ji] is None:
            outs.append(None)
            continue
        sc = jnp.where(jobs[ji][7] > 0, scores[ji], 0.0).astype(BF16)
        intra = _dot(sc, jobs[ji][4])
        inter = []
        for pi, s_pair in enumerate(s_all[ji]):
            rows = slice(2 * pi * CHUNK, (2 * pi + 2) * CHUNK)
            both = lax.dot_general(q_dec[ji][rows], s_pair, contract_last, preferred_element_type=F32)
            inter += [both[:CHUNK, :dv], both[CHUNK:, dv:]]
        outs.append(intra + jnp.concatenate(inter, axis=0))
    return outs, states


def _hgrn_kernel(q_ref, kf_ref, kb_ref, hif_ref, hib_ref, lof_ref, lob_ref, v_ref, g_ref,
                 ckf_ref, ckb_ref, chif_ref, chib_ref, clof_ref, clob_ref, cv_ref, ng_ref, trif_ref, trib_ref,
                 keepf_ref, keepb_ref, y_ref, of_ref, ob_ref, *, seq, ctx_len):
    r = HG_TILE
    n_lat = seq // r
    n_ctx = ctx_len // r
    group = HG_GROUP

    def ctx_jobs():
        jobs = []
        for i in range(n_ctx):
            sl_f = pl.ds(i * r, r)
            sl_b = pl.ds((n_ctx - 1 - i) * r, r)
            jobs.append((0, chif_ref[0, sl_f, :], clof_ref[0, sl_f, :], ckf_ref[0, sl_f, :], cv_ref[0, sl_f, :], None,
                         trif_ref[...], None, True))
            jobs.append((1, chib_ref[0, sl_b, :], clob_ref[0, sl_b, :], ckb_ref[0, sl_b, :], cv_ref[0, sl_b, :], None,
                         trib_ref[...], None, False))
        return jobs

    ng = ng_ref[...]

    def readout(tile, o):
        sl = pl.ds(tile * r, r)
        o = o * lax.rsqrt(jnp.mean(o * o, axis=-1, keepdims=True) + RMS_EPS) * ng
        g = g_ref[0, sl, :].astype(F32)
        y_ref[0, sl, :] = (o * (g * _sigmoid(g))).astype(y_ref.dtype)

    def step_of(tile, forward):
        return (tile if forward else n_lat - 1 - tile) // group

    carry = (jnp.zeros((HG_D, HG_D), F32),) * 2
    for i in range(n_lat // group):
        jobs = ctx_jobs() if i == 0 else []
        meta = [None] * len(jobs)
        for t in range(group):
            for forward in (True, False):
                tile = i * group + t if forward else n_lat - 1 - (i * group + t)
                sl = pl.ds(tile * r, r)
                hi_ref, lo_ref, k_ref, tri_ref, keep_ref = ((hif_ref, lof_ref, kf_ref, trif_ref, keepf_ref) if forward
                                                            else (hib_ref, lob_ref, kb_ref, trib_ref, keepb_ref))
                jobs.append((0 if forward else 1, hi_ref[0, sl, :], lo_ref[0, sl, :], k_ref[0, sl, :], v_ref[0, sl, :],
                             q_ref[0, sl, :], tri_ref[...], keep_ref[...], forward))
                meta.append((tile, forward))
        outs, carry = _gla_group(jobs, carry)
        waiting = {}
        for m, o in zip(meta, outs):
            if m is None:
                continue
            tile, forward = m
            mine, other = (of_ref, ob_ref) if forward else (ob_ref, of_ref)
            other_step = step_of(tile, not forward)
            if other_step < i:
                readout(tile, o + other[pl.ds(tile * r, r), :])
            elif other_step > i:
                mine[pl.ds(tile * r, r), :] = o
            elif tile in waiting:
                readout(tile, o + waiting.pop(tile))
            else:
                waiting[tile] = o


def _chunk_tri(r, forward, dtype):
    i = np.arange(r)
    same = (i[:, None] // CHUNK) == (i[None, :] // CHUNK)
    order = (i[:, None] >= i[None, :]) if forward else (i[:, None] <= i[None, :])
    return jnp.asarray(same & order, dtype)


def _hgrn(q, k, hi, lo, v, g, ck, chi, clo, cv, ng):
    b, s, _ = q.shape
    lc = ck.shape[1]
    h = HG_HEADS
    d = HG_D
    r = HG_TILE
    kern = functools.partial(_hgrn_kernel, seq=s, ctx_len=lc)
    fwd = lambda i, j: (i, 0, j)
    bwd = lambda i, j: (i, 0, j + h)
    const = lambda i, j: (0, 0)
    lat = lambda im: pl.BlockSpec((1, s, d), im)
    cx = lambda im: pl.BlockSpec((1, lc, d), im)
    return pl.pallas_call(
        kern,
        grid=(b, h),
        in_specs=[lat(fwd), lat(fwd), lat(bwd), lat(fwd), lat(bwd), lat(fwd), lat(bwd), lat(fwd), lat(fwd),
                  cx(fwd), cx(bwd), cx(fwd), cx(bwd), cx(fwd), cx(bwd), cx(fwd),
                  pl.BlockSpec((1, d), lambda i, j: (0, j)),
                  pl.BlockSpec((r, r), const),
                  pl.BlockSpec((r, r), const),
                  pl.BlockSpec((r, r), const),
                  pl.BlockSpec((r, r), const)],
        out_specs=pl.BlockSpec((1, s, d), fwd),
        out_shape=jax.ShapeDtypeStruct((b, s, h * d), BF16),
        scratch_shapes=[pltpu.VMEM((s, d), F32), pltpu.VMEM((s, d), F32)],
        compiler_params=_cparams(("parallel", "parallel"), VMEM_LIMIT),
        name="hgrn2",
    )(q, k, k, hi, hi, lo, lo, v, g, ck, ck, chi, chi, clo, clo, cv, ng,
      _chunk_tri(r, True, BF16), _chunk_tri(r, False, BF16), _chunk_tri(r, True, F32), _chunk_tri(r, False, F32))


def _dft_tables(seq):
    rows = seq // GRID_W
    n = F_GROUP_DIM
    kc = np.outer(np.arange(n), np.arange(n)) % n
    ang = 2.0 * np.pi * kc / n
    norm = 1.0 / math.sqrt(rows * GRID_W * n)
    ch = np.concatenate([np.cos(ang), -np.sin(ang)], axis=1) * norm
    t = np.arange(seq)
    r, w = t // GRID_W, t % GRID_W
    m = (np.outer(r, r) * GRID_W + np.outer(w, w) * rows) % (rows * GRID_W)
    ang_t = 2.0 * np.pi * m / (rows * GRID_W)
    tok = np.concatenate([np.cos(ang_t), np.sin(ang_t)], axis=1)
    return jnp.asarray(ch, BF16), jnp.asarray(tok, BF16)


def _fourier_kernel(p_ref, ch_ref, tok_ref, y_ref, xs_ref, *, seq):
    for gi in range(F_GROUPS):
        cols = slice(gi * F_GROUP_DIM, (gi + 1) * F_GROUP_DIM)
        x1 = _dot(p_ref[0, :, cols], ch_ref[...]).astype(BF16)
        xs_ref[0:seq, cols] = x1[:, :F_GROUP_DIM]
        xs_ref[seq:2 * seq, cols] = x1[:, F_GROUP_DIM:]
    y_ref[0] = _dot(tok_ref[...], xs_ref[...]).astype(y_ref.dtype)


def _fourier(pf):
    b, s, wdt = pf.shape
    ch, tok = _dft_tables(s)
    kern = functools.partial(_fourier_kernel, seq=s)
    return pl.pallas_call(
        kern,
        grid=(b,),
        in_specs=[pl.BlockSpec((1, s, wdt), lambda i: (i, 0, 0)),
                  pl.BlockSpec(ch.shape, lambda i: (0, 0)),
                  pl.BlockSpec(tok.shape, lambda i: (0, 0), pipeline_mode=pl.Buffered(1))],
        out_specs=pl.BlockSpec((1, s, wdt), lambda i: (i, 0, 0)),
        out_shape=jax.ShapeDtypeStruct((b, s, wdt), BF16),
        scratch_shapes=[pltpu.VMEM((2 * s, wdt), BF16)],
        compiler_params=_cparams(("parallel",), VMEM_LIMIT),
        name="fourier",
    )(pf, ch, tok)


def _layer_norm(t, g, b):
    mu = jnp.mean(t, axis=-1, keepdims=True)
    tc = t - mu
    var = jnp.mean(tc * tc, axis=-1, keepdims=True)
    return tc * lax.rsqrt(var + LN_EPS) * g + b


def _merge_kernel(yf_ref, yh_ref, gt_ref, x_ref, g1_ref, sc2_ref, sh2_ref, wfo_ref, who_ref, wo_ref,
                  lg_ref, lbias_ref, wr_ref, br_ref, x1_ref, idx_ref, wt_ref):
    d = x_ref.shape[-1]
    gf = _sigmoid(gt_ref[0, :, :d].astype(F32))
    gh = _sigmoid(gt_ref[0, :, d:].astype(F32))
    m = gf * _dot(yf_ref[0], wfo_ref[...]) + gh * _dot(yh_ref[0], who_ref[...])
    mix = _dot(m.astype(BF16), wo_ref[...])
    x1 = _layer_norm(DEEPNORM_ALPHA * x_ref[0] + g1_ref[0] * mix, lg_ref[...], lbias_ref[...])
    x1_ref[0] = x1
    u2 = x1 * (1.0 + sc2_ref[0]) + sh2_ref[0]
    logits = lax.dot_general(wr_ref[...], u2.astype(BF16), (((1,), (1,)), ((), ())),
                             preferred_element_type=F32) + br_ref[...]
    expert = lax.broadcasted_iota(jnp.int32, logits.shape, 0)
    expert_f = expert.astype(F32)
    work = logits
    vals, idxs = [], []
    for _ in range(TOP_K):
        mx = jnp.max(work, axis=0, keepdims=True)
        sel = jnp.min(jnp.where(work == mx, expert_f, float(N_EXPERTS)), axis=0, keepdims=True).astype(jnp.int32)
        vals.append(mx)
        idxs.append(sel)
        work = jnp.where(expert == sel, -jnp.inf, work)
    exps = [jnp.exp(vv - vals[0]) for vv in vals]
    inv = 1.0 / (exps[0] + exps[1] + exps[2] + exps[3])
    for kk in range(TOP_K):
        idx_ref[kk:kk + 1, :] = idxs[kk]
        wt_ref[kk:kk + 1, :] = exps[kk] * inv


def _merge(yf, yh, gates, x, g1, sc2, sh2, wfo, who, wo, lg, lbias, wr, br, tm):
    b, s, d = x.shape
    tile = lambda i, j: (i, j, 0)
    per_b = lambda i, j: (i, 0, 0)
    const = lambda i, j: (0, 0)
    slots = lambda i, j: (0, i * (s // tm) + j)
    return pl.pallas_call(
        _merge_kernel,
        grid=(b, s // tm),
        in_specs=[pl.BlockSpec((1, tm, yf.shape[-1]), tile),
                  pl.BlockSpec((1, tm, yh.shape[-1]), tile),
                  pl.BlockSpec((1, tm, gates.shape[-1]), tile),
                  pl.BlockSpec((1, tm, d), tile),
                  pl.BlockSpec((1, 1, d), per_b),
                  pl.BlockSpec((1, 1, d), per_b),
                  pl.BlockSpec((1, 1, d), per_b),
                  pl.BlockSpec(wfo.shape, const),
                  pl.BlockSpec(who.shape, const),
                  pl.BlockSpec(wo.shape, const),
                  pl.BlockSpec((1, d), const),
                  pl.BlockSpec((1, d), const),
                  pl.BlockSpec(wr.shape, const),
                  pl.BlockSpec(br.shape, const)],
        out_specs=[pl.BlockSpec((1, tm, d), tile),
                   pl.BlockSpec((TOP_K, tm), slots),
                   pl.BlockSpec((TOP_K, tm), slots)],
        out_shape=[jax.ShapeDtypeStruct((b, s, d), F32),
                   jax.ShapeDtypeStruct((TOP_K, b * s), jnp.int32),
                   jax.ShapeDtypeStruct((TOP_K, b * s), F32)],
        compiler_params=_cparams(("parallel", "parallel"), VMEM_LIMIT),
        name="merge_ln_router",
    )(yf, yh, gates, x, g1, sc2, sh2, wfo, who, wo, lg, lbias, wr, br)


def _route_kernel(idx_ref, before_ref, dest_ref, cnt_ref, start_ref, run_ref, *, block_rows):
    phase = pl.program_id(0)
    t = pl.program_id(1)
    tt = idx_ref.shape[1]
    expert = lax.broadcasted_iota(jnp.int32, (N_EXPERTS, tt), 0)
    onehots = [jnp.where(expert == idx_ref[kk:kk + 1, :], 1.0, 0.0) for kk in range(TOP_K)]
    counts = [jnp.sum(oh, axis=1, keepdims=True) for oh in onehots]

    @pl.when(jnp.logical_and(phase == 0, t == 0))
    def _():
        run_ref[...] = jnp.zeros_like(run_ref)

    @pl.when(phase == 0)
    def _():
        run_ref[...] += counts[0] + counts[1] + counts[2] + counts[3]

    @pl.when(jnp.logical_and(phase == 1, t == 0))
    def _():
        cnt = run_ref[...]
        padded = jnp.floor((cnt + (block_rows - 1)) * (1.0 / block_rows)) * block_rows
        r = lax.broadcasted_iota(jnp.int32, (N_EXPERTS, N_EXPERTS), 0)
        c = lax.broadcasted_iota(jnp.int32, (N_EXPERTS, N_EXPERTS), 1)
        earlier = jnp.where(r > c, 1.0, 0.0)
        start = jnp.dot(earlier, padded, preferred_element_type=F32, precision=lax.Precision.HIGHEST)
        cnt_ref[...] = cnt.astype(jnp.int32)
        start_ref[...] = start.astype(jnp.int32)
        run_ref[...] = jnp.zeros_like(run_ref)

    @pl.when(phase == 1)
    def _():
        prefixes = _dot(jnp.concatenate([oh.astype(BF16) for oh in onehots], axis=0), before_ref[...])
        start = start_ref[:, 0:1].astype(F32)
        base = run_ref[:, 0:1] + start
        for kk in range(TOP_K):
            prefix = prefixes[kk * N_EXPERTS:(kk + 1) * N_EXPERTS, :]
            dest_ref[kk:kk + 1, :] = jnp.sum(onehots[kk] * (prefix + base), axis=0, keepdims=True).astype(jnp.int32)
            base = base + counts[kk]
        run_ref[...] = jnp.broadcast_to(base - start, run_ref.shape)


def _route(idx_t, block_rows, tt):
    n = idx_t.shape[1]
    kern = functools.partial(_route_kernel, block_rows=block_rows)
    before = jnp.asarray(np.triu(np.ones((tt, tt)), 1), BF16)
    stat = pl.BlockSpec((N_EXPERTS, LANES), lambda p, t: (0, 0))
    return pl.pallas_call(
        kern,
        grid=(2, n // tt),
        in_specs=[pl.BlockSpec((TOP_K, tt), lambda p, t: (0, t)),
                  pl.BlockSpec((tt, tt), lambda p, t: (0, 0))],
        out_specs=[pl.BlockSpec((TOP_K, tt), lambda p, t: (0, t * p)), stat, stat],
        out_shape=[jax.ShapeDtypeStruct((TOP_K, n), jnp.int32),
                   jax.ShapeDtypeStruct((N_EXPERTS, LANES), jnp.int32),
                   jax.ShapeDtypeStruct((N_EXPERTS, LANES), jnp.int32)],
        scratch_shapes=[pltpu.VMEM((N_EXPERTS, LANES), F32)],
        compiler_params=_cparams(("arbitrary", "arbitrary")),
        name="route_ranks",
    )(idx_t, before)


def _to_row_tiles(ref, base, val):
    rows = val.shape[0]
    for s in range(SUBLANES):
        ref[pl.ds(base + s, rows, stride=SUBLANES), :] = val[:, s * LANES:(s + 1) * LANES]


def _from_row_tiles(ref, base, rows):
    return jnp.concatenate([ref[pl.ds(base + s, rows, stride=SUBLANES), :] for s in range(SUBLANES)], axis=1)


def _row_tile(ref, r):
    return ref.at[pl.ds(pl.multiple_of(r * SUBLANES, SUBLANES), SUBLANES), :]


def _row_tiles(ref, r, n):
    return ref.at[pl.ds(pl.multiple_of(r * SUBLANES, SUBLANES), n * SUBLANES), :]


def _dispatch_kernel(cnt_ref, start_ref, dest_ref, x1_ref, sc_ref, sh_ref, h_ref, u_ref, zero_ref, sems, *, block_rows):
    t = pl.program_id(0)
    nt = pl.num_programs(0)
    tt = x1_ref.shape[0]
    slot = t % 2
    ubase = pl.multiple_of(slot * (tt * SUBLANES), tt * SUBLANES)

    _to_row_tiles(u_ref, ubase, x1_ref[...] * (1.0 + sc_ref[0]) + sh_ref[0])

    def issue(i, _):
        for kk in range(TOP_K):
            d = dest_ref[0, 0, kk * tt + i]
            pltpu.make_async_copy(_row_tile(u_ref, slot * tt + i), _row_tile(h_ref, d),
                                  sems.at[slot]).start(priority=kk % DMA_QUEUES)
        return 0

    lax.fori_loop(0, tt, issue, 0, unroll=ISSUE_UNROLL)

    def drain(sl):
        for _ in range(TOP_K):
            pltpu.make_async_copy(_row_tiles(u_ref, sl * tt, tt), _row_tiles(h_ref, 0, tt), sems.at[sl]).wait()

    @pl.when(t > 0)
    def _():
        drain(1 - slot)

    @pl.when(t == nt - 1)
    def _():
        drain(slot)
        zero_ref[...] = jnp.zeros_like(zero_ref)

        def pad_expert(e, _):
            cnt = cnt_ref[e]
            rem = (block_rows - (cnt & (block_rows - 1))) & (block_rows - 1)
            pos = start_ref[e] + cnt
            size = block_rows // 2
            while size >= 1:
                take = rem & size

                @pl.when(take != 0)
                def _(pos=pos, size=size):
                    cp = pltpu.make_async_copy(_row_tiles(zero_ref, 0, size), _row_tiles(h_ref, pos, size), sems.at[2])
                    cp.start()
                    cp.wait()

                pos = pos + take
                size //= 2
            return 0

        lax.fori_loop(0, N_EXPERTS, pad_expert, 0)


def _dispatch(counts, starts, dest_flat, x1, sc2, sh2, n_rows, tt, tile0):
    n, d = x1.shape
    b = sc2.shape[0]
    tiles_per_b = (n // b) // tt
    per_b = lambda t, c, s: ((t + tile0) // tiles_per_b, 0, 0)
    grid_spec = pltpu.PrefetchScalarGridSpec(
        num_scalar_prefetch=2,
        grid=(dest_flat.shape[0],),
        in_specs=[pl.BlockSpec((1, 1, tt * TOP_K), lambda t, c, s: (t, 0, 0), memory_space=pltpu.SMEM),
                  pl.BlockSpec((tt, d), lambda t, c, s: (t + tile0, 0)),
                  pl.BlockSpec((1, 1, d), per_b),
                  pl.BlockSpec((1, 1, d), per_b)],
        out_specs=pl.BlockSpec(memory_space=pl.ANY),
        scratch_shapes=[pltpu.VMEM((2 * tt * SUBLANES, LANES), F32), pltpu.VMEM((MOE_ROWS // 2 * SUBLANES, LANES), F32),
                        pltpu.SemaphoreType.DMA((3,))],
    )
    return pl.pallas_call(
        functools.partial(_dispatch_kernel, block_rows=MOE_ROWS),
        grid_spec=grid_spec,
        out_shape=jax.ShapeDtypeStruct((n_rows * SUBLANES, LANES), F32),
        compiler_params=_cparams(("arbitrary",)),
        name="moe_dispatch",
    )(counts, starts, dest_flat, x1, sc2, sh2)


def _expert_kernel(be_ref, nu_ref, nxt_ref, cnt_ref, start_ref, h_ref, wgu_hbm, bgu_ref, wdn_hbm, bdn_ref,
                   dest_ref, x1_ref, sc_ref, sh_ref, o_ref, hn_ref,
                   wgu_f_ref, wdn_f_ref, wgu_b_ref, wdn_b_ref, u0_ref, u1_ref, zero_ref, sems, *, n_hosted, block_rows):
    j = pl.program_id(0)
    d = wdn_b_ref.shape[-1]
    bm = h_ref.shape[0] // SUBLANES
    tt = x1_ref.shape[0]
    u_refs = (u0_ref, u1_ref)
    w_sems, u_sems, z_sem = (sems.at[0], sems.at[1]), (sems.at[2], sems.at[3]), sems.at[4]

    def weight_copies(e):
        return (pltpu.make_async_copy(wgu_hbm.at[e], wgu_f_ref, w_sems[0]),
                pltpu.make_async_copy(wdn_hbm.at[e], wdn_f_ref, w_sems[1]))

    def drain_hosted(sl):
        for _ in range(TOP_K):
            pltpu.make_async_copy(_row_tiles(u_refs[sl], 0, tt), _row_tiles(hn_ref, 0, tt), u_sems[sl]).wait()

    def hosted_dispatch(sl):
        _to_row_tiles(u_refs[sl], 0, x1_ref[...] * (1.0 + sc_ref[0]) + sh_ref[0])
        for i in range(tt):
            for kk in range(TOP_K):
                dst = dest_ref[0, 0, kk * tt + i]
                pltpu.make_async_copy(_row_tile(u_refs[sl], i), _row_tile(hn_ref, dst),
                                      u_sems[sl]).start(priority=kk % DMA_QUEUES)

    def pad_rows():
        zero_ref[...] = jnp.zeros_like(zero_ref)

        def pad_expert(e, _):
            cnt = cnt_ref[e]
            rem = (block_rows - (cnt & (block_rows - 1))) & (block_rows - 1)
            pos = start_ref[e] + cnt
            size = block_rows // 2
            while size >= 1:
                take = rem & size

                @pl.when(take != 0)
                def _(pos=pos, size=size):
                    cp = pltpu.make_async_copy(_row_tiles(zero_ref, 0, size), _row_tiles(hn_ref, pos, size), z_sem)
                    cp.start()
                    cp.wait()

                pos = pos + take
                size //= 2
            return 0

        lax.fori_loop(0, N_EXPERTS, pad_expert, 0)

    def block():
        gu = _dot(_from_row_tiles(h_ref, 0, bm).astype(BF16), wgu_b_ref[...]) + bgu_ref[0]
        gate = jnp.minimum(gu[:, :d], SWIGLU_LIMIT)
        up = jnp.clip(gu[:, d:], -SWIGLU_LIMIT, SWIGLU_LIMIT)
        act = (up + 1.0) * gate * _sigmoid(SWIGLU_ALPHA * gate)
        _to_row_tiles(o_ref, 0, _dot(act.astype(BF16), wdn_b_ref[...]) + bdn_ref[0])

    @pl.when(j < nu_ref[0])
    def _():
        @pl.when(j == 0)
        def _():
            for cp in weight_copies(be_ref[0]):
                cp.start()

        @pl.when(jnp.logical_or(j == 0, be_ref[j] != be_ref[jnp.maximum(j - 1, 0)]))
        def _():
            for cp in weight_copies(be_ref[j]):
                cp.wait()
            wgu_b_ref[...] = wgu_f_ref[...].astype(BF16)
            wdn_b_ref[...] = wdn_f_ref[...].astype(BF16)

            @pl.when(nxt_ref[j] >= 0)
            def _():
                for cp in weight_copies(nxt_ref[j]):
                    cp.start()

        for sl in range(2):
            @pl.when(jnp.logical_and(jnp.logical_and(j >= 2, j < n_hosted + 2), j % 2 == sl))
            def _(sl=sl):
                drain_hosted(sl)

            @pl.when(jnp.logical_and(j < n_hosted, j % 2 == sl))
            def _(sl=sl):
                hosted_dispatch(sl)
                block()

        @pl.when(j >= n_hosted)
        def _():
            if n_hosted:
                @pl.when(j == n_hosted + 1)
                def _():
                    pad_rows()

            block()


def _experts(h_sorted, meta, wgu, bgu, wdn, bdn, hosted=None):
    block_expert, n_used, next_expert = meta
    n_rows = h_sorted.shape[0] // SUBLANES
    e, d, _ = wdn.shape
    bm = MOE_ROWS
    n_blocks = n_rows // bm
    tt = HOSTED_TOKENS
    if hosted is None:
        zi = jnp.zeros((N_EXPERTS,), jnp.int32)
        counts, starts, n_hosted, tile0, hn_rows = zi, zi, 0, 0, SUBLANES
        dest_flat = jnp.zeros((1, 1, tt * TOP_K), jnp.int32)
        x1, sc2, sh2 = jnp.zeros((tt, d), F32), jnp.zeros((1, 1, d), F32), jnp.zeros((1, 1, d), F32)
        tiles_per_b = 1
    else:
        counts, starts, dest_flat, x1, sc2, sh2, hn_rows, tile0 = hosted
        n_hosted = dest_flat.shape[0]
        tiles_per_b = (x1.shape[0] // sc2.shape[0]) // tt
        assert n_hosted + 2 <= -(-(n_hosted * tt * TOP_K) // bm)
    last = max(n_hosted - 1, 0)
    blk = lambda j, be, nu, nx, c, st: (jnp.minimum(j, nu[0] - 1), 0)
    exp = lambda j, be, nu, nx, c, st: (be[jnp.minimum(j, nu[0] - 1)], 0, 0)
    tile = lambda j: jnp.minimum(j, last)
    per_b = lambda j, be, nu, nx, c, st: ((tile(j) + tile0) // tiles_per_b, 0, 0)
    grid_spec = pltpu.PrefetchScalarGridSpec(
        num_scalar_prefetch=5,
        grid=(n_blocks,),
        in_specs=[pl.BlockSpec((bm * SUBLANES, LANES), blk),
                  pl.BlockSpec(memory_space=pl.ANY),
                  pl.BlockSpec((1, 1, 2 * d), exp),
                  pl.BlockSpec(memory_space=pl.ANY),
                  pl.BlockSpec((1, 1, d), exp),
                  pl.BlockSpec((1, 1, tt * TOP_K), lambda j, be, nu, nx, c, st: (tile(j), 0, 0), memory_space=pltpu.SMEM),
                  pl.BlockSpec((tt, d), lambda j, be, nu, nx, c, st: (tile(j) + tile0, 0)),
                  pl.BlockSpec((1, 1, d), per_b),
                  pl.BlockSpec((1, 1, d), per_b)],
        out_specs=[pl.BlockSpec((bm * SUBLANES, LANES), blk), pl.BlockSpec(memory_space=pl.ANY)],
        scratch_shapes=[pltpu.VMEM((d, 2 * d), F32), pltpu.VMEM((d, d), F32),
                        pltpu.VMEM((d, 2 * d), BF16), pltpu.VMEM((d, d), BF16),
                        pltpu.VMEM((tt * SUBLANES, LANES), F32), pltpu.VMEM((tt * SUBLANES, LANES), F32),
                        pltpu.VMEM((MOE_ROWS // 2 * SUBLANES, LANES), F32), pltpu.SemaphoreType.DMA((5,))],
    )
    return pl.pallas_call(
        functools.partial(_expert_kernel, n_hosted=n_hosted, block_rows=MOE_ROWS),
        grid_spec=grid_spec,
        out_shape=[jax.ShapeDtypeStruct((n_rows * SUBLANES, LANES), F32),
                   jax.ShapeDtypeStruct((hn_rows * SUBLANES, LANES), F32)],
        compiler_params=_cparams(("arbitrary",), VMEM_LIMIT),
        name="moe_experts",
    )(block_expert, n_used, next_expert, counts, starts, h_sorted, wgu, bgu.reshape(e, 1, 2 * d), wdn,
      bdn.reshape(e, 1, d), dest_flat, x1, sc2, sh2)


def _combine_kernel(dest_ref, dest_next_ref, wt_ref, x1_ref, g2_ref, lg_ref, lbias_ref, o_hbm, y_ref, buf0_ref, buf1_ref, sems):
    t = pl.program_id(0)
    nt = pl.num_programs(0)
    tt = x1_ref.shape[0]
    slot = t % 2

    bufs = (buf0_ref, buf1_ref)

    def issue_row(d_ref, sl, i):
        for kk in range(TOP_K):
            d = d_ref[0, 0, kk * tt + i]
            pltpu.make_async_copy(_row_tile(o_hbm, d), _row_tile(bufs[sl], kk * tt + i),
                                  sems.at[sl]).start(priority=kk % DMA_QUEUES)

    def drain(sl):
        for kk in range(TOP_K):
            pltpu.make_async_copy(_row_tiles(o_hbm, 0, tt), _row_tiles(bufs[sl], kk * tt, tt), sems.at[sl]).wait()

    @pl.when(t == 0)
    def _():
        def first(i, _):
            issue_row(dest_ref, 0, i)
            return 0

        lax.fori_loop(0, tt, first, 0, unroll=ISSUE_UNROLL)

    def fused(sl):
        drain(sl)
        for r0 in range(0, tt, COMBINE_ROWS):
            for i in range(r0, r0 + COMBINE_ROWS):
                issue_row(dest_next_ref, 1 - sl, i)
            rows = pl.ds(r0, COMBINE_ROWS)
            ff = None
            for kk in range(TOP_K):
                term = wt_ref[rows, kk:kk + 1] * _from_row_tiles(bufs[sl], (kk * tt + r0) * SUBLANES, COMBINE_ROWS)
                ff = term if ff is None else ff + term
            y_ref[rows, :] = _layer_norm(DEEPNORM_ALPHA * x1_ref[rows, :] + g2_ref[0] * ff, lg_ref[...], lbias_ref[...])

        @pl.when(t == nt - 1)
        def _():
            drain(1 - sl)

    for sl in range(2):
        pl.when(slot == sl)(functools.partial(fused, sl))


def _combine(dest_flat, wt, x1, g2, lg, lbias, o_sorted, tt, tile0, out=None):
    n, d = x1.shape
    b = g2.shape[0]
    nt = dest_flat.shape[0]
    tiles_per_b = (n // b) // tt
    const = lambda t: (0, 0)
    here = lambda t: (t + tile0, 0)
    dest_spec = lambda im: pl.BlockSpec((1, 1, tt * TOP_K), im, memory_space=pltpu.SMEM)
    in_specs = [dest_spec(lambda t: (t, 0, 0)),
                dest_spec(lambda t: (jnp.minimum(t + 1, nt - 1), 0, 0)),
                pl.BlockSpec((tt, TOP_K), here),
                pl.BlockSpec((tt, d), here),
                pl.BlockSpec((1, 1, d), lambda t: ((t + tile0) // tiles_per_b, 0, 0)),
                pl.BlockSpec((1, d), const),
                pl.BlockSpec((1, d), const),
                pl.BlockSpec(memory_space=pl.ANY)]
    args = [dest_flat, dest_flat, wt, x1, g2, lg, lbias, o_sorted]
    kern, aliases = _combine_kernel, {}
    if out is not None:
        in_specs.append(pl.BlockSpec(memory_space=pl.ANY))
        args.append(out)
        aliases = {len(args) - 1: 0}
        kern = lambda *refs: _combine_kernel(*refs[:8], *refs[9:])
    return pl.pallas_call(
        kern,
        grid=(nt,),
        in_specs=in_specs,
        out_specs=pl.BlockSpec((tt, d), here),
        out_shape=jax.ShapeDtypeStruct((n, d), F32),
        scratch_shapes=[pltpu.VMEM((TOP_K * tt * SUBLANES, LANES), F32)] * 2 + [pltpu.SemaphoreType.DMA((2,))],
        input_output_aliases=aliases,
        compiler_params=_cparams(("arbitrary",), VMEM_LIMIT),
        name="moe_combine",
    )(*args)


def kernel(x, c, ctx, c_ctx, w_ada, b_ada, w_in, lb_raw, hg_norm_g, w_four_out, w_hg_out, w_o, ln1_g, ln1_b,
           w_router, b_router, w_gate_up, b_gate_up, w_down, b_down, ln2_g, ln2_b):
    b, s, d = x.shape
    lc = ctx.shape[1]
    n = b * s
    f_w = F_GROUPS * F_GROUP_DIM
    hk = HG_HEADS * HG_D
    assert w_ada.shape[0] == DEPTH and s % GRID_W == 0 and s % (HG_TILE * HG_GROUP) == 0 and lc % HG_TILE == 0
    assert s % ROW_TILE == 0 and ROW_TILE % MOE_TOKENS == 0 and MOE_TOKENS % COMBINE_ROWS == 0 and b % 2 == 0

    pad = (-(b + 1)) % 8
    c_rows = jnp.concatenate([c, c_ctx[None, :], jnp.zeros((pad, d), F32)], axis=0)
    mod = _ada(c_rows, w_ada[0], b_ada[0]).reshape(b + 1 + pad, N_MOD, 1, d)
    shift1, scale1, gate1, shift2, scale2, gate2 = (mod[:b, i] for i in range(N_MOD))
    cshift1 = jnp.broadcast_to(mod[b, 0], (b, 1, d))
    cscale1 = jnp.broadcast_to(mod[b, 1], (b, 1, d))

    lower_bounds = jnp.cumsum(jax.nn.softmax(lb_raw.astype(F32), axis=0), axis=0)[0]

    w_in_b = w_in[0].astype(BF16)
    o_f, o_q, o_z, o_v, o_g, o_gt = f_w, f_w + hk, f_w + 3 * hk, f_w + 4 * hk, f_w + 5 * hk, f_w + 5 * hk + 2 * d
    lb_row = lower_bounds.reshape(1, 2 * hk)
    pf, pq, pk, phi, plo, pv, pg, gates = _inproj(
        x, scale1, shift1, lb_row, w_in_b,
        [(0, f_w, "cast"), (o_f, hk, "silu"), (o_q, 2 * hk, "forget"), (o_z, hk, "cast"), (o_v, hk, "cast"),
         (o_g, 2 * d, "cast")], tm=ROW_TILE)
    w_ctx = w_in_b[:, o_q:o_v]
    ck, chi, clo, cv = _inproj(ctx, cscale1, cshift1, lb_row, w_ctx, [(0, 2 * hk, "forget"), (2 * hk, hk, "cast")], tm=lc)

    y_h = _hgrn(pq, pk, phi, plo, pv, pg, ck, chi, clo, cv, hg_norm_g[0].reshape(1, hk))
    y_f = _fourier(pf)

    wr = w_router[0].T.astype(BF16)
    br = b_router[0].reshape(N_EXPERTS, 1)
    x1, idx_t, wt_t = _merge(y_f, y_h, gates, x, gate1, scale2, shift2,
                         w_four_out[0].astype(BF16), w_hg_out[0].astype(BF16), w_o[0].astype(BF16),
                         ln1_g[0].reshape(1, d), ln1_b[0].reshape(1, d), wr, br, tm=ROW_TILE)

    tt = MOE_TOKENS
    x1_2d = x1.reshape(n, d)
    halves = []
    for hi in range(2):
        n_h = n // 2
        dest_t, counts, starts = _route(idx_t[:, hi * n_h:(hi + 1) * n_h], MOE_ROWS, ROW_TILE)
        n_blocks = -(-(n_h * TOP_K) // MOE_ROWS) + N_EXPERTS
        cnt, st = counts[:, 0], starts[:, 0]
        pad_ends = st + (cnt + MOE_ROWS - 1) // MOE_ROWS * MOE_ROWS
        block_first_row = jnp.arange(n_blocks, dtype=jnp.int32) * MOE_ROWS
        block_expert = jnp.minimum(jnp.sum((pad_ends[None, :] <= block_first_row[:, None]).astype(jnp.int32), axis=1),
                                   N_EXPERTS - 1)
        n_used = (pad_ends[-1] // MOE_ROWS).astype(jnp.int32).reshape(1)
        group_end_block = pad_ends[block_expert] // MOE_ROWS
        next_expert = jnp.where(group_end_block < n_used[0], block_expert[jnp.minimum(group_end_block, n_blocks - 1)], -1)
        flat = lambda t: dest_t.reshape(TOP_K, n_h // t, t).transpose(1, 0, 2).reshape(n_h // t, 1, TOP_K * t)
        halves.append(dict(cnt=cnt, st=st, meta=(block_expert, n_used, next_expert), n_rows=n_blocks * MOE_ROWS,
                           dest=flat(tt), dest_hosted=flat(HOSTED_TOKENS)))
    first, second = halves
    weights = (w_gate_up[0], b_gate_up[0], w_down[0], b_down[0])
    h_first = _dispatch(first["cnt"], first["st"], first["dest"], x1_2d, scale2, shift2, first["n_rows"], tt, 0)
    o_first, h_second = _experts(h_first, first["meta"], *weights,
                                 hosted=(second["cnt"], second["st"], second["dest_hosted"], x1_2d, scale2, shift2,
                                         second["n_rows"], (n // 2) // HOSTED_TOKENS))
    o_second, _ = _experts(h_second, second["meta"], *weights)
    ln2 = (gate2, ln2_g[0].reshape(1, d), ln2_b[0].reshape(1, d))
    out = _combine(first["dest"], wt_t.T, x1_2d, *ln2, o_first, tt, 0)
    out = _combine(second["dest"], wt_t.T, x1_2d, *ln2, o_second, tt, (n // 2) // tt, out=out)
    return out.reshape(b, s, d)
```

```python
import functools
import math

import jax
import jax.numpy as jnp
import numpy as np
from jax import lax
from jax.experimental import pallas as pl
from jax.experimental.pallas import tpu as pltpu

F32 = jnp.float32
BF16 = jnp.bfloat16

GRID_W = 64
N_MOD = 6
F_GROUPS = 4
F_GROUP_DIM = 128
HG_HEADS = 4
HG_D = 128
CHUNK = 64
N_EXPERTS = 32
TOP_K = 4
SWIGLU_LIMIT = 7.0
SWIGLU_ALPHA = 1.702
LN_EPS = 1e-5
RMS_EPS = 1e-6
DEPTH = 1
DEEPNORM_ALPHA = (2.0 * DEPTH) ** 0.25

LANES = 128
SUBLANES = 8
ROW_TILE = 1024
MERGE_ROWS = 512
MOE_TOKENS = 512
MOE_ROWS = 512
COMBINE_ROWS = 256
ISSUE_UNROLL = 4
DMA_QUEUES = 2
VMEM_LIMIT = 56 * 1024 * 1024


def _cparams(sem, vmem=None):
    return pltpu.CompilerParams(dimension_semantics=sem, vmem_limit_bytes=vmem)


def _dot(a, b):
    return jnp.dot(a, b, preferred_element_type=F32)


def _sigmoid(x):
    return 0.5 * jnp.tanh(0.5 * x) + 0.5


def _ada_kernel(c_ref, w_ref, b_ref, o_ref):
    c = c_ref[...]
    a = c * _sigmoid(c)
    o_ref[...] = jnp.dot(a, w_ref[...], preferred_element_type=F32,
                         precision=lax.Precision.HIGHEST) + b_ref[...]


def _ada(c_rows, w, b):
    r, d = c_rows.shape
    n = w.shape[1]
    tn = 512
    return pl.pallas_call(
        _ada_kernel,
        grid=(n // tn,),
        in_specs=[pl.BlockSpec((r, d), lambda j: (0, 0)),
                  pl.BlockSpec((d, tn), lambda j: (0, j)),
                  pl.BlockSpec((1, tn), lambda j: (0, j))],
        out_specs=pl.BlockSpec((r, tn), lambda j: (0, j)),
        out_shape=jax.ShapeDtypeStruct((r, n), F32),
        compiler_params=_cparams(("arbitrary",)),
        name="ada_mod",
    )(c_rows, w, b.reshape(1, n))


def _inproj_kernel(x_ref, sc_ref, sh_ref, lb_ref, w_ref, *o_refs, plan):
    u = (x_ref[0] * (1.0 + sc_ref[0]) + sh_ref[0]).astype(BF16)
    refs = iter(o_refs)
    for kind, chunks in plan:
        if kind == "forget":
            k_ref, hi_ref, lo_ref = next(refs), next(refs), next(refs)
            for (w0, o0, n) in chunks:
                lb = lb_ref[:, o0:o0 + n]
                f = lb + (1.0 - lb) * _sigmoid(_dot(u, w_ref[:, w0:w0 + n]))
                lf = jnp.log(f)
                hi = lf.astype(BF16)
                k_ref[0, :, o0:o0 + n] = (1.0 - f).astype(BF16)
                hi_ref[0, :, o0:o0 + n] = hi
                lo_ref[0, :, o0:o0 + n] = (lf - hi.astype(F32)).astype(BF16)
        else:
            o_ref = next(refs)
            for (w0, o0, n) in chunks:
                p = _dot(u, w_ref[:, w0:w0 + n])
                if kind == "silu":
                    p = p * _sigmoid(p)
                o_ref[0, :, o0:o0 + n] = p.astype(o_ref.dtype)


def _inproj(x, scale, shift, lb_row, w_bf16, outs, tm):
    b, s, d = x.shape
    n_w = w_bf16.shape[1]
    plan, widths = [], []
    for (c0, width, kind) in outs:
        step = min(width, 512)
        plan.append((kind, tuple((c0 + o, o, step) for o in range(0, width, step))))
        widths += [width] * (3 if kind == "forget" else 1)
    kern = functools.partial(_inproj_kernel, plan=tuple(plan))
    return pl.pallas_call(
        kern,
        grid=(b, s // tm),
        in_specs=[pl.BlockSpec((1, tm, d), lambda i, j: (i, j, 0)),
                  pl.BlockSpec((1, 1, d), lambda i, j: (i, 0, 0)),
                  pl.BlockSpec((1, 1, d), lambda i, j: (i, 0, 0)),
                  pl.BlockSpec(lb_row.shape, lambda i, j: (0, 0)),
                  pl.BlockSpec((d, n_w), lambda i, j: (0, 0), pipeline_mode=pl.Buffered(1))],
        out_specs=[pl.BlockSpec((1, tm, width), lambda i, j: (i, j, 0)) for width in widths],
        out_shape=[jax.ShapeDtypeStruct((b, s, width), BF16) for width in widths],
        compiler_params=_cparams(("parallel", "parallel"), VMEM_LIMIT),
        name="in_proj",
    )(x, scale, shift, lb_row, w_bf16)


HG_TILE = 256
HG_GROUP = 4


def _gla_group(jobs, states):
    states = list(states)
    nj = len(jobs)
    r, dk = jobs[0][1].shape
    dv = jobs[0][4].shape[1]
    nc = r // CHUNK
    zero = jnp.zeros((CHUNK, dk), BF16)
    contract_last = (((1,), (1,)), ((), ()))
    contract_rows = (((0,), (0,)), ((), ()))

    cums = []
    for (_, hi, lo, _, _, _, tri, _, _) in jobs:
        both = _dot(tri, jnp.concatenate([hi, lo], axis=1))
        cums.append(both[:, :dk] + both[:, dk:])

    k_inv_b, decays, k_end_blk, q_dec = [], [], [], []
    for cum, (_, _, _, k, _, q, _, _, forward) in zip(cums, jobs):
        k_inv = k.astype(F32) * jnp.exp(-cum)
        k_inv_b.append(k_inv.astype(BF16))
        tot_row = CHUNK - 1 if forward else 0
        dec_j, rows_j = [], []
        for ci in range(nc):
            dec = jnp.exp(cum[ci * CHUNK + tot_row:ci * CHUNK + tot_row + 1, :])
            k_end = (k_inv[ci * CHUNK:(ci + 1) * CHUNK] * dec).astype(BF16)
            dec_j.append(dec)
            rows_j.append(jnp.concatenate([k_end if cj == ci else zero for cj in range(nc)], axis=1))
        decays.append(dec_j)
        k_end_blk.append(jnp.concatenate(rows_j, axis=0))
        q_dec.append(None if q is None else (q.astype(F32) * jnp.exp(cum)).astype(BF16))

    upd_all = [lax.dot_general(job[4], blk, contract_rows, preferred_element_type=F32)
               for job, blk in zip(jobs, k_end_blk)]
    scores = [None if qd is None else lax.dot_general(qd, kb, contract_last, preferred_element_type=F32)
              for qd, kb in zip(q_dec, k_inv_b)]

    s_all = []
    for ji in range(nj):
        chain, forward = jobs[ji][0], jobs[ji][8]
        st = states[chain]
        entering = [None] * nc
        for ci in (range(nc) if forward else reversed(range(nc))):
            entering[ci] = st
            st = st * decays[ji][ci] + upd_all[ji][:, ci * dk:(ci + 1) * dk]
        states[chain] = st
        s_all.append(None if q_dec[ji] is None
                     else [jnp.concatenate([entering[ci].astype(BF16), entering[ci + 1].astype(BF16)], axis=0)
                           for ci in range(0, nc, 2)])

    outs = []
    for ji in range(nj):
        if q_dec[ji] is None:
            outs.append(None)
            continue
        sc = jnp.where(jobs[ji][7] > 0, scores[ji], 0.0).astype(BF16)
        intra = _dot(sc, jobs[ji][4])
        inter = []
        for pi, s_pair in enumerate(s_all[ji]):
            rows = slice(2 * pi * CHUNK, (2 * pi + 2) * CHUNK)
            both = lax.dot_general(q_dec[ji][rows], s_pair, contract_last, preferred_element_type=F32)
            inter += [both[:CHUNK, :dv], both[CHUNK:, dv:]]
        outs.append(intra + jnp.concatenate(inter, axis=0))
    return outs, states


def _hgrn_kernel(q_ref, kf_ref, kb_ref, hif_ref, hib_ref, lof_ref, lob_ref, v_ref, g_ref,
                 ckf_ref, ckb_ref, chif_ref, chib_ref, clof_ref, clob_ref, cv_ref, ng_ref, trif_ref, trib_ref,
                 keepf_ref, keepb_ref, y_ref, of_ref, ob_ref, *, seq, ctx_len):
    r = HG_TILE
    n_lat = seq // r
    n_ctx = ctx_len // r
    group = HG_GROUP

    def ctx_jobs():
        jobs = []
        for i in range(n_ctx):
            sl_f = pl.ds(i * r, r)
            sl_b = pl.ds((n_ctx - 1 - i) * r, r)
            jobs.append((0, chif_ref[0, sl_f, :], clof_ref[0, sl_f, :], ckf_ref[0, sl_f, :], cv_ref[0, sl_f, :], None,
                         trif_ref[...], None, True))
            jobs.append((1, chib_ref[0, sl_b, :], clob_ref[0, sl_b, :], ckb_ref[0, sl_b, :], cv_ref[0, sl_b, :], None,
                         trib_ref[...], None, False))
        return jobs

    ng = ng_ref[...]

    def readout(tile, o):
        sl = pl.ds(tile * r, r)
        o = o * lax.rsqrt(jnp.mean(o * o, axis=-1, keepdims=True) + RMS_EPS) * ng
        g = g_ref[0, sl, :].astype(F32)
        y_ref[0, sl, :] = (o * (g * _sigmoid(g))).astype(y_ref.dtype)

    def step_of(tile, forward):
        return (tile if forward else n_lat - 1 - tile) // group

    carry = (jnp.zeros((HG_D, HG_D), F32),) * 2
    for i in range(n_lat // group):
        jobs = ctx_jobs() if i == 0 else []
        meta = [None] * len(jobs)
        for t in range(group):
            for forward in (True, False):
                tile = i * group + t if forward else n_lat - 1 - (i * group + t)
                sl = pl.ds(tile * r, r)
                hi_ref, lo_ref, k_ref, tri_ref, keep_ref = ((hif_ref, lof_ref, kf_ref, trif_ref, keepf_ref) if forward
                                                            else (hib_ref, lob_ref, kb_ref, trib_ref, keepb_ref))
                jobs.append((0 if forward else 1, hi_ref[0, sl, :], lo_ref[0, sl, :], k_ref[0, sl, :], v_ref[0, sl, :],
                             q_ref[0, sl, :], tri_ref[...], keep_ref[...], forward))
                meta.append((tile, forward))
        outs, carry = _gla_group(jobs, carry)
        waiting = {}
        for m, o in zip(meta, outs):
            if m is None:
                continue
            tile, forward = m
            mine, other = (of_ref, ob_ref) if forward else (ob_ref, of_ref)
            other_step = step_of(tile, not forward)
            if other_step < i:
                readout(tile, o + other[pl.ds(tile * r, r), :])
            elif other_step > i:
                mine[pl.ds(tile * r, r), :] = o
            elif tile in waiting:
                readout(tile, o + waiting.pop(tile))
            else:
                waiting[tile] = o


def _chunk_tri(r, forward, dtype):
    i = np.arange(r)
    same = (i[:, None] // CHUNK) == (i[None, :] // CHUNK)
    order = (i[:, None] >= i[None, :]) if forward else (i[:, None] <= i[None, :])
    return jnp.asarray(same & order, dtype)


def _hgrn(q, k, hi, lo, v, g, ck, chi, clo, cv, ng):
    b, s, _ = q.shape
    lc = ck.shape[1]
    h = HG_HEADS
    d = HG_D
    r = HG_TILE
    kern = functools.partial(_hgrn_kernel, seq=s, ctx_len=lc)
    fwd = lambda i, j: (i, 0, j)
    bwd = lambda i, j: (i, 0, j + h)
    const = lambda i, j: (0, 0)
    lat = lambda im: pl.BlockSpec((1, s, d), im)
    cx = lambda im: pl.BlockSpec((1, lc, d), im)
    return pl.pallas_call(
        kern,
        grid=(b, h),
        in_specs=[lat(fwd), lat(fwd), lat(bwd), lat(fwd), lat(bwd), lat(fwd), lat(bwd), lat(fwd), lat(fwd),
                  cx(fwd), cx(bwd), cx(fwd), cx(bwd), cx(fwd), cx(bwd), cx(fwd),
                  pl.BlockSpec((1, d), lambda i, j: (0, j)),
                  pl.BlockSpec((r, r), const),
                  pl.BlockSpec((r, r), const),
                  pl.BlockSpec((r, r), const),
                  pl.BlockSpec((r, r), const)],
        out_specs=pl.BlockSpec((1, s, d), fwd),
        out_shape=jax.ShapeDtypeStruct((b, s, h * d), BF16),
        scratch_shapes=[pltpu.VMEM((s, d), F32), pltpu.VMEM((s, d), F32)],
        compiler_params=_cparams(("parallel", "parallel"), VMEM_LIMIT),
        name="hgrn2",
    )(q, k, k, hi, hi, lo, lo, v, g, ck, ck, chi, chi, clo, clo, cv, ng,
      _chunk_tri(r, True, BF16), _chunk_tri(r, False, BF16), _chunk_tri(r, True, F32), _chunk_tri(r, False, F32))


def _dft_tables(seq):
    rows = seq // GRID_W
    n = F_GROUP_DIM
    kc = np.outer(np.arange(n), np.arange(n)) % n
    ang = 2.0 * np.pi * kc / n
    norm = 1.0 / math.sqrt(rows * GRID_W * n)
    ch = np.concatenate([np.cos(ang), -np.sin(ang)], axis=1) * norm
    t = np.arange(seq)
    r, w = t // GRID_W, t % GRID_W
    m = (np.outer(r, r) * GRID_W + np.outer(w, w) * rows) % (rows * GRID_W)
    ang_t = 2.0 * np.pi * m / (rows * GRID_W)
    tok = np.concatenate([np.cos(ang_t), np.sin(ang_t)], axis=1)
    return jnp.asarray(ch, BF16), jnp.asarray(tok, BF16)


def _fourier_kernel(p_ref, ch_ref, tok_ref, y_ref, xs_ref, *, seq):
    for gi in range(F_GROUPS):
        cols = slice(gi * F_GROUP_DIM, (gi + 1) * F_GROUP_DIM)
        x1 = _dot(p_ref[0, :, cols], ch_ref[...]).astype(BF16)
        xs_ref[0:seq, cols] = x1[:, :F_GROUP_DIM]
        xs_ref[seq:2 * seq, cols] = x1[:, F_GROUP_DIM:]
    y_ref[0] = _dot(tok_ref[...], xs_ref[...]).astype(y_ref.dtype)


def _fourier(pf):
    b, s, wdt = pf.shape
    ch, tok = _dft_tables(s)
    kern = functools.partial(_fourier_kernel, seq=s)
    return pl.pallas_call(
        kern,
        grid=(b,),
        in_specs=[pl.BlockSpec((1, s, wdt), lambda i: (i, 0, 0)),
                  pl.BlockSpec(ch.shape, lambda i: (0, 0)),
                  pl.BlockSpec(tok.shape, lambda i: (0, 0), pipeline_mode=pl.Buffered(1))],
        out_specs=pl.BlockSpec((1, s, wdt), lambda i: (i, 0, 0)),
        out_shape=jax.ShapeDtypeStruct((b, s, wdt), BF16),
        scratch_shapes=[pltpu.VMEM((2 * s, wdt), BF16)],
        compiler_params=_cparams(("parallel",), VMEM_LIMIT),
        name="fourier",
    )(pf, ch, tok)


def _layer_norm(t, g, b):
    mu = jnp.mean(t, axis=-1, keepdims=True)
    tc = t - mu
    var = jnp.mean(tc * tc, axis=-1, keepdims=True)
    return tc * lax.rsqrt(var + LN_EPS) * g + b


def _merge_kernel(yf_ref, yh_ref, gt_ref, x_ref, g1_ref, sc2_ref, sh2_ref, wfo_ref, who_ref, wo_ref,
                  lg_ref, lbias_ref, wr_ref, br_ref, x1_ref, idx_ref, wt_ref):
    d = x_ref.shape[-1]
    tm = x_ref.shape[1]
    tiles = [slice(r0, r0 + MERGE_ROWS) for r0 in range(0, tm, MERGE_ROWS)]
    branch = [(_dot(yf_ref[0, rows, :], wfo_ref[...]), _dot(yh_ref[0, rows, :], who_ref[...])) for rows in tiles]
    merged = [(_sigmoid(gt_ref[0, rows, :d].astype(F32)) * pf + _sigmoid(gt_ref[0, rows, d:].astype(F32)) * ph).astype(BF16)
              for rows, (pf, ph) in zip(tiles, branch)]
    mixes = [_dot(m, wo_ref[...]) for m in merged]
    u2s = []
    for rows, mix in zip(tiles, mixes):
        x1 = _layer_norm(DEEPNORM_ALPHA * x_ref[0, rows, :] + g1_ref[0] * mix, lg_ref[...], lbias_ref[...])
        x1_ref[0, rows, :] = x1
        u2s.append((x1 * (1.0 + sc2_ref[0]) + sh2_ref[0]).astype(BF16))
    all_logits = [lax.dot_general(wr_ref[...], u2, (((1,), (1,)), ((), ())), preferred_element_type=F32) + br_ref[...]
                  for u2 in u2s]
    for rows, logits in zip(tiles, all_logits):
        expert = lax.broadcasted_iota(jnp.int32, logits.shape, 0)
        expert_f = expert.astype(F32)
        work = logits
        vals, idxs = [], []
        for _ in range(TOP_K):
            mx = jnp.max(work, axis=0, keepdims=True)
            sel = jnp.min(jnp.where(work == mx, expert_f, float(N_EXPERTS)), axis=0, keepdims=True).astype(jnp.int32)
            vals.append(mx)
            idxs.append(sel)
            work = jnp.where(expert == sel, -jnp.inf, work)
        exps = [jnp.exp(vv - vals[0]) for vv in vals]
        inv = 1.0 / (exps[0] + exps[1] + exps[2] + exps[3])
        for kk in range(TOP_K):
            idx_ref[kk:kk + 1, rows] = idxs[kk]
            wt_ref[kk:kk + 1, rows] = exps[kk] * inv


def _merge(yf, yh, gates, x, g1, sc2, sh2, wfo, who, wo, lg, lbias, wr, br, tm):
    b, s, d = x.shape
    tile = lambda i, j: (i, j, 0)
    per_b = lambda i, j: (i, 0, 0)
    const = lambda i, j: (0, 0)
    slots = lambda i, j: (0, i * (s // tm) + j)
    return pl.pallas_call(
        _merge_kernel,
        grid=(b, s // tm),
        in_specs=[pl.BlockSpec((1, tm, yf.shape[-1]), tile),
                  pl.BlockSpec((1, tm, yh.shape[-1]), tile),
                  pl.BlockSpec((1, tm, gates.shape[-1]), tile),
                  pl.BlockSpec((1, tm, d), tile),
                  pl.BlockSpec((1, 1, d), per_b),
                  pl.BlockSpec((1, 1, d), per_b),
                  pl.BlockSpec((1, 1, d), per_b),
                  pl.BlockSpec(wfo.shape, const),
                  pl.BlockSpec(who.shape, const),
                  pl.BlockSpec(wo.shape, const),
                  pl.BlockSpec((1, d), const),
                  pl.BlockSpec((1, d), const),
                  pl.BlockSpec(wr.shape, const),
                  pl.BlockSpec(br.shape, const)],
        out_specs=[pl.BlockSpec((1, tm, d), tile),
                   pl.BlockSpec((TOP_K, tm), slots),
                   pl.BlockSpec((TOP_K, tm), slots)],
        out_shape=[jax.ShapeDtypeStruct((b, s, d), F32),
                   jax.ShapeDtypeStruct((TOP_K, b * s), jnp.int32),
                   jax.ShapeDtypeStruct((TOP_K, b * s), F32)],
        compiler_params=_cparams(("parallel", "parallel"), VMEM_LIMIT),
        name="merge_ln_router",
    )(yf, yh, gates, x, g1, sc2, sh2, wfo, who, wo, lg, lbias, wr, br)


def _route_kernel(idx_ref, before_ref, dest_ref, cnt_ref, start_ref, run_ref, *, block_rows):
    phase = pl.program_id(0)
    t = pl.program_id(1)
    tt = idx_ref.shape[1]
    expert = lax.broadcasted_iota(jnp.int32, (N_EXPERTS, tt), 0)
    onehots = [jnp.where(expert == idx_ref[kk:kk + 1, :], 1.0, 0.0) for kk in range(TOP_K)]
    counts = [jnp.sum(oh, axis=1, keepdims=True) for oh in onehots]

    @pl.when(jnp.logical_and(phase == 0, t == 0))
    def _():
        run_ref[...] = jnp.zeros_like(run_ref)

    @pl.when(phase == 0)
    def _():
        run_ref[...] += counts[0] + counts[1] + counts[2] + counts[3]

    @pl.when(jnp.logical_and(phase == 1, t == 0))
    def _():
        cnt = run_ref[...]
        padded = jnp.floor((cnt + (block_rows - 1)) * (1.0 / block_rows)) * block_rows
        r = lax.broadcasted_iota(jnp.int32, (N_EXPERTS, N_EXPERTS), 0)
        c = lax.broadcasted_iota(jnp.int32, (N_EXPERTS, N_EXPERTS), 1)
        earlier = jnp.where(r > c, 1.0, 0.0)
        start = jnp.dot(earlier, padded, preferred_element_type=F32, precision=lax.Precision.HIGHEST)
        cnt_ref[...] = cnt.astype(jnp.int32)
        start_ref[...] = start.astype(jnp.int32)
        run_ref[...] = jnp.zeros_like(run_ref)

    @pl.when(phase == 1)
    def _():
        prefixes = _dot(jnp.concatenate([oh.astype(BF16) for oh in onehots], axis=0), before_ref[...])
        start = start_ref[:, 0:1].astype(F32)
        base = run_ref[:, 0:1] + start
        for kk in range(TOP_K):
            prefix = prefixes[kk * N_EXPERTS:(kk + 1) * N_EXPERTS, :]
            dest_ref[kk:kk + 1, :] = jnp.sum(onehots[kk] * (prefix + base), axis=0, keepdims=True).astype(jnp.int32)
            base = base + counts[kk]
        run_ref[...] = jnp.broadcast_to(base - start, run_ref.shape)


def _route(idx_t, block_rows, tt):
    n = idx_t.shape[1]
    kern = functools.partial(_route_kernel, block_rows=block_rows)
    before = jnp.asarray(np.triu(np.ones((tt, tt)), 1), BF16)
    stat = pl.BlockSpec((N_EXPERTS, LANES), lambda p, t: (0, 0))
    return pl.pallas_call(
        kern,
        grid=(2, n // tt),
        in_specs=[pl.BlockSpec((TOP_K, tt), lambda p, t: (0, t)),
                  pl.BlockSpec((tt, tt), lambda p, t: (0, 0))],
        out_specs=[pl.BlockSpec((TOP_K, tt), lambda p, t: (0, t * p)), stat, stat],
        out_shape=[jax.ShapeDtypeStruct((TOP_K, n), jnp.int32),
                   jax.ShapeDtypeStruct((N_EXPERTS, LANES), jnp.int32),
                   jax.ShapeDtypeStruct((N_EXPERTS, LANES), jnp.int32)],
        scratch_shapes=[pltpu.VMEM((N_EXPERTS, LANES), F32)],
        compiler_params=_cparams(("arbitrary", "arbitrary")),
        name="route_ranks",
    )(idx_t, before)


def _to_row_tiles(ref, base, val):
    rows = val.shape[0]
    for s in range(SUBLANES):
        ref[pl.ds(base + s, rows, stride=SUBLANES), :] = val[:, s * LANES:(s + 1) * LANES]


def _from_row_tiles(ref, base, rows):
    return jnp.concatenate([ref[pl.ds(base + s, rows, stride=SUBLANES), :] for s in range(SUBLANES)], axis=1)


def _row_tile(ref, r):
    return ref.at[pl.ds(pl.multiple_of(r * SUBLANES, SUBLANES), SUBLANES), :]


def _row_tiles(ref, r, n):
    return ref.at[pl.ds(pl.multiple_of(r * SUBLANES, SUBLANES), n * SUBLANES), :]


def _dispatch_kernel(cnt_ref, start_ref, dest_ref, x1_ref, sc_ref, sh_ref, h_ref, u_ref, zero_ref, sems, *, block_rows):
    t = pl.program_id(0)
    nt = pl.num_programs(0)
    tt = x1_ref.shape[0]
    slot = t % 2
    ubase = pl.multiple_of(slot * (tt * SUBLANES), tt * SUBLANES)

    _to_row_tiles(u_ref, ubase, x1_ref[...] * (1.0 + sc_ref[0]) + sh_ref[0])

    def issue(i, _):
        for kk in range(TOP_K):
            d = dest_ref[0, 0, kk * tt + i]
            pltpu.make_async_copy(_row_tile(u_ref, slot * tt + i), _row_tile(h_ref, d),
                                  sems.at[slot]).start(priority=kk % DMA_QUEUES)
        return 0

    lax.fori_loop(0, tt, issue, 0, unroll=ISSUE_UNROLL)

    def drain(sl):
        for _ in range(TOP_K):
            pltpu.make_async_copy(_row_tiles(u_ref, sl * tt, tt), _row_tiles(h_ref, 0, tt), sems.at[sl]).wait()

    @pl.when(t > 0)
    def _():
        drain(1 - slot)

    @pl.when(t == nt - 1)
    def _():
        drain(slot)
        zero_ref[...] = jnp.zeros_like(zero_ref)

        def pad_expert(e, _):
            cnt = cnt_ref[e]
            rem = (block_rows - (cnt & (block_rows - 1))) & (block_rows - 1)
            pos = start_ref[e] + cnt
            size = block_rows // 2
            while size >= 1:
                take = rem & size

                @pl.when(take != 0)
                def _(pos=pos, size=size):
                    cp = pltpu.make_async_copy(_row_tiles(zero_ref, 0, size), _row_tiles(h_ref, pos, size), sems.at[2])
                    cp.start()
                    cp.wait()

                pos = pos + take
                size //= 2
            return 0

        lax.fori_loop(0, N_EXPERTS, pad_expert, 0)


def _dispatch(counts, starts, dest_flat, x1, sc2, sh2, n_rows, tt):
    n, d = x1.shape
    b = sc2.shape[0]
    tiles_per_b = (n // b) // tt
    per_b = lambda t, c, s: (t // tiles_per_b, 0, 0)
    grid_spec = pltpu.PrefetchScalarGridSpec(
        num_scalar_prefetch=2,
        grid=(n // tt,),
        in_specs=[pl.BlockSpec((1, 1, tt * TOP_K), lambda t, c, s: (t, 0, 0), memory_space=pltpu.SMEM),
                  pl.BlockSpec((tt, d), lambda t, c, s: (t, 0)),
                  pl.BlockSpec((1, 1, d), per_b),
                  pl.BlockSpec((1, 1, d), per_b)],
        out_specs=pl.BlockSpec(memory_space=pl.ANY),
        scratch_shapes=[pltpu.VMEM((2 * tt * SUBLANES, LANES), F32), pltpu.VMEM((MOE_ROWS // 2 * SUBLANES, LANES), F32),
                        pltpu.SemaphoreType.DMA((3,))],
    )
    return pl.pallas_call(
        functools.partial(_dispatch_kernel, block_rows=MOE_ROWS),
        grid_spec=grid_spec,
        out_shape=jax.ShapeDtypeStruct((n_rows * SUBLANES, LANES), F32),
        compiler_params=_cparams(("arbitrary",)),
        name="moe_dispatch",
    )(counts, starts, dest_flat, x1, sc2, sh2)


def _expert_kernel(be_ref, nu_ref, nxt_ref, h_ref, wgu_hbm, bgu_ref, wdn_hbm, bdn_ref, o_ref,
                   wgu_f_ref, wdn_f_ref, wgu_b_ref, wdn_b_ref, sems):
    j = pl.program_id(0)
    d = wdn_b_ref.shape[-1]
    bm = h_ref.shape[0] // SUBLANES

    def weight_copies(e):
        return (pltpu.make_async_copy(wgu_hbm.at[e], wgu_f_ref, sems.at[0]),
                pltpu.make_async_copy(wdn_hbm.at[e], wdn_f_ref, sems.at[1]))

    @pl.when(j < nu_ref[0])
    def _():
        @pl.when(j == 0)
        def _():
            for cp in weight_copies(be_ref[0]):
                cp.start()

        @pl.when(jnp.logical_or(j == 0, be_ref[j] != be_ref[jnp.maximum(j - 1, 0)]))
        def _():
            for cp in weight_copies(be_ref[j]):
                cp.wait()
            wgu_b_ref[...] = wgu_f_ref[...].astype(BF16)
            wdn_b_ref[...] = wdn_f_ref[...].astype(BF16)

            @pl.when(nxt_ref[j] >= 0)
            def _():
                for cp in weight_copies(nxt_ref[j]):
                    cp.start()

        gu = _dot(_from_row_tiles(h_ref, 0, bm).astype(BF16), wgu_b_ref[...]) + bgu_ref[0]
        gate = jnp.minimum(gu[:, :d], SWIGLU_LIMIT)
        up = jnp.clip(gu[:, d:], -SWIGLU_LIMIT, SWIGLU_LIMIT)
        act = (up + 1.0) * gate * _sigmoid(SWIGLU_ALPHA * gate)
        _to_row_tiles(o_ref, 0, _dot(act.astype(BF16), wdn_b_ref[...]) + bdn_ref[0])


def _experts(h_sorted, block_expert, n_used, next_expert, wgu, bgu, wdn, bdn):
    n_rows = h_sorted.shape[0] // SUBLANES
    e, d, _ = wdn.shape
    bm = MOE_ROWS
    n_blocks = n_rows // bm
    blk = lambda j, be, nu, nx: (jnp.minimum(j, nu[0] - 1), 0)
    exp = lambda j, be, nu, nx: (be[jnp.minimum(j, nu[0] - 1)], 0, 0)
    grid_spec = pltpu.PrefetchScalarGridSpec(
        num_scalar_prefetch=3,
        grid=(n_blocks,),
        in_specs=[pl.BlockSpec((bm * SUBLANES, LANES), blk),
                  pl.BlockSpec(memory_space=pl.ANY),
                  pl.BlockSpec((1, 1, 2 * d), exp),
                  pl.BlockSpec(memory_space=pl.ANY),
                  pl.BlockSpec((1, 1, d), exp)],
        out_specs=pl.BlockSpec((bm * SUBLANES, LANES), blk),
        scratch_shapes=[pltpu.VMEM((d, 2 * d), F32), pltpu.VMEM((d, d), F32),
                        pltpu.VMEM((d, 2 * d), BF16), pltpu.VMEM((d, d), BF16), pltpu.SemaphoreType.DMA((2,))],
    )
    return pl.pallas_call(
        _expert_kernel,
        grid_spec=grid_spec,
        out_shape=jax.ShapeDtypeStruct((n_rows * SUBLANES, LANES), F32),
        compiler_params=_cparams(("arbitrary",), VMEM_LIMIT),
        name="moe_experts",
    )(block_expert, n_used, next_expert, h_sorted, wgu, bgu.reshape(e, 1, 2 * d), wdn, bdn.reshape(e, 1, d))


def _combine_kernel(dest_ref, dest_next_ref, wt_ref, x1_ref, g2_ref, lg_ref, lbias_ref, o_hbm, y_ref, buf0_ref, buf1_ref, sems):
    t = pl.program_id(0)
    nt = pl.num_programs(0)
    tt = x1_ref.shape[0]
    slot = t % 2

    bufs = (buf0_ref, buf1_ref)

    def issue_row(d_ref, sl, i):
        for kk in range(TOP_K):
            d = d_ref[0, 0, kk * tt + i]
            pltpu.make_async_copy(_row_tile(o_hbm, d), _row_tile(bufs[sl], kk * tt + i),
                                  sems.at[sl]).start(priority=kk % DMA_QUEUES)

    def drain(sl):
        for kk in range(TOP_K):
            pltpu.make_async_copy(_row_tiles(o_hbm, 0, tt), _row_tiles(bufs[sl], kk * tt, tt), sems.at[sl]).wait()

    @pl.when(t == 0)
    def _():
        def first(i, _):
            issue_row(dest_ref, 0, i)
            return 0

        lax.fori_loop(0, tt, first, 0, unroll=ISSUE_UNROLL)

    def fused(sl):
        drain(sl)
        for r0 in range(0, tt, COMBINE_ROWS):
            for i in range(r0, r0 + COMBINE_ROWS):
                issue_row(dest_next_ref, 1 - sl, i)
            rows = pl.ds(r0, COMBINE_ROWS)
            ff = None
            for kk in range(TOP_K):
                term = wt_ref[rows, kk:kk + 1] * _from_row_tiles(bufs[sl], (kk * tt + r0) * SUBLANES, COMBINE_ROWS)
                ff = term if ff is None else ff + term
            y_ref[rows, :] = _layer_norm(DEEPNORM_ALPHA * x1_ref[rows, :] + g2_ref[0] * ff, lg_ref[...], lbias_ref[...])

        @pl.when(t == nt - 1)
        def _():
            drain(1 - sl)

    for sl in range(2):
        pl.when(slot == sl)(functools.partial(fused, sl))


def _combine(dest_flat, wt, x1, g2, lg, lbias, o_sorted, tt):
    n, d = x1.shape
    b = g2.shape[0]
    nt = n // tt
    tiles_per_b = (n // b) // tt
    const = lambda t: (0, 0)
    dest_spec = lambda im: pl.BlockSpec((1, 1, tt * TOP_K), im, memory_space=pltpu.SMEM)
    return pl.pallas_call(
        _combine_kernel,
        grid=(nt,),
        in_specs=[dest_spec(lambda t: (t, 0, 0)),
                  dest_spec(lambda t: (jnp.minimum(t + 1, nt - 1), 0, 0)),
                  pl.BlockSpec((tt, TOP_K), lambda t: (t, 0)),
                  pl.BlockSpec((tt, d), lambda t: (t, 0)),
                  pl.BlockSpec((1, 1, d), lambda t: (t // tiles_per_b, 0, 0)),
                  pl.BlockSpec((1, d), const),
                  pl.BlockSpec((1, d), const),
                  pl.BlockSpec(memory_space=pl.ANY)],
        out_specs=pl.BlockSpec((tt, d), lambda t: (t, 0)),
        out_shape=jax.ShapeDtypeStruct((n, d), F32),
        scratch_shapes=[pltpu.VMEM((TOP_K * tt * SUBLANES, LANES), F32)] * 2 + [pltpu.SemaphoreType.DMA((2,))],
        compiler_params=_cparams(("arbitrary",), VMEM_LIMIT),
        name="moe_combine",
    )(dest_flat, dest_flat, wt, x1, g2, lg, lbias, o_sorted)


def kernel(x, c, ctx, c_ctx, w_ada, b_ada, w_in, lb_raw, hg_norm_g, w_four_out, w_hg_out, w_o, ln1_g, ln1_b,
           w_router, b_router, w_gate_up, b_gate_up, w_down, b_down, ln2_g, ln2_b):
    b, s, d = x.shape
    lc = ctx.shape[1]
    n = b * s
    f_w = F_GROUPS * F_GROUP_DIM
    hk = HG_HEADS * HG_D
    assert w_ada.shape[0] == DEPTH and s % GRID_W == 0 and s % (HG_TILE * HG_GROUP) == 0 and lc % HG_TILE == 0
    assert s % ROW_TILE == 0 and ROW_TILE % MOE_TOKENS == 0 and MOE_TOKENS % COMBINE_ROWS == 0

    pad = (-(b + 1)) % 8
    c_rows = jnp.concatenate([c, c_ctx[None, :], jnp.zeros((pad, d), F32)], axis=0)
    mod = _ada(c_rows, w_ada[0], b_ada[0]).reshape(b + 1 + pad, N_MOD, 1, d)
    shift1, scale1, gate1, shift2, scale2, gate2 = (mod[:b, i] for i in range(N_MOD))
    cshift1 = jnp.broadcast_to(mod[b, 0], (b, 1, d))
    cscale1 = jnp.broadcast_to(mod[b, 1], (b, 1, d))

    lower_bounds = jnp.cumsum(jax.nn.softmax(lb_raw.astype(F32), axis=0), axis=0)[0]

    w_in_b = w_in[0].astype(BF16)
    o_f, o_q, o_z, o_v, o_g, o_gt = f_w, f_w + hk, f_w + 3 * hk, f_w + 4 * hk, f_w + 5 * hk, f_w + 5 * hk + 2 * d
    lb_row = lower_bounds.reshape(1, 2 * hk)
    pf, pq, pk, phi, plo, pv, pg, gates = _inproj(
        x, scale1, shift1, lb_row, w_in_b,
        [(0, f_w, "cast"), (o_f, hk, "silu"), (o_q, 2 * hk, "forget"), (o_z, hk, "cast"), (o_v, hk, "cast"),
         (o_g, 2 * d, "cast")], tm=ROW_TILE)
    w_ctx = w_in_b[:, o_q:o_v]
    ck, chi, clo, cv = _inproj(ctx, cscale1, cshift1, lb_row, w_ctx, [(0, 2 * hk, "forget"), (2 * hk, hk, "cast")], tm=lc)

    y_h = _hgrn(pq, pk, phi, plo, pv, pg, ck, chi, clo, cv, hg_norm_g[0].reshape(1, hk))
    y_f = _fourier(pf)

    wr = w_router[0].T.astype(BF16)
    br = b_router[0].reshape(N_EXPERTS, 1)
    x1, idx_t, wt_t = _merge(y_f, y_h, gates, x, gate1, scale2, shift2,
                         w_four_out[0].astype(BF16), w_hg_out[0].astype(BF16), w_o[0].astype(BF16),
                         ln1_g[0].reshape(1, d), ln1_b[0].reshape(1, d), wr, br, tm=ROW_TILE)

    dest_t, counts, starts = _route(idx_t, MOE_ROWS, ROW_TILE)
    n_blocks = -(-(n * TOP_K) // MOE_ROWS) + N_EXPERTS
    n_rows = n_blocks * MOE_ROWS
    cnt = counts[:, 0]
    st = starts[:, 0]
    pad_ends = st + (cnt + MOE_ROWS - 1) // MOE_ROWS * MOE_ROWS
    block_first_row = jnp.arange(n_blocks, dtype=jnp.int32) * MOE_ROWS
    block_expert = jnp.minimum(jnp.sum((pad_ends[None, :] <= block_first_row[:, None]).astype(jnp.int32), axis=1),
                               N_EXPERTS - 1)
    n_used = (pad_ends[-1] // MOE_ROWS).astype(jnp.int32).reshape(1)
    group_end_block = pad_ends[block_expert] // MOE_ROWS
    next_expert = jnp.where(group_end_block < n_used[0], block_expert[jnp.minimum(group_end_block, n_blocks - 1)], -1)

    tt = MOE_TOKENS
    dest_flat = dest_t.reshape(TOP_K, n // tt, tt).transpose(1, 0, 2).reshape(n // tt, 1, TOP_K * tt)
    x1_2d = x1.reshape(n, d)
    h_sorted = _dispatch(cnt, st, dest_flat, x1_2d, scale2, shift2, n_rows, tt)
    o_sorted = _experts(h_sorted, block_expert, n_used, next_expert, w_gate_up[0], b_gate_up[0], w_down[0], b_down[0])
    out = _combine(dest_flat, wt_t.T, x1_2d, gate2, ln2_g[0].reshape(1, d), ln2_b[0].reshape(1, d),
                   o_sorted, tt)
    return out.reshape(b, s, d)
```

```python
import functools
import math

import jax
import jax.numpy as jnp
import numpy as np
from jax import lax
from jax.experimental import pallas as pl
from jax.experimental.pallas import tpu as pltpu

F32 = jnp.float32
BF16 = jnp.bfloat16

GRID_W = 64
N_MOD = 6
F_GROUPS = 4
F_GROUP_DIM = 128
HG_HEADS = 4
HG_D = 128
CHUNK = 64
N_EXPERTS = 32
TOP_K = 4
SWIGLU_LIMIT = 7.0
SWIGLU_ALPHA = 1.702
LN_EPS = 1e-5
RMS_EPS = 1e-6
DEPTH = 1
DEEPNORM_ALPHA = (2.0 * DEPTH) ** 0.25

LANES = 128
SUBLANES = 8
ROW_TILE = 1024
MERGE_ROWS = 512
MOE_TOKENS = 512
MOE_ROWS = 1024
COMBINE_ROWS = 256
ISSUE_UNROLL = 4
DMA_QUEUES = 2
VMEM_LIMIT = 56 * 1024 * 1024


def _cparams(sem, vmem=None):
    return pltpu.CompilerParams(dimension_semantics=sem, vmem_limit_bytes=vmem)


def _dot(a, b):
    return jnp.dot(a, b, preferred_element_type=F32)


def _sigmoid(x):
    return 0.5 * jnp.tanh(0.5 * x) + 0.5


def _ada_kernel(c_ref, w_ref, b_ref, o_ref):
    c = c_ref[...]
    a = c * _sigmoid(c)
    o_ref[...] = jnp.dot(a, w_ref[...], preferred_element_type=F32,
                         precision=lax.Precision.HIGHEST) + b_ref[...]


def _ada(c_rows, w, b):
    r, d = c_rows.shape
    n = w.shape[1]
    tn = 512
    return pl.pallas_call(
        _ada_kernel,
        grid=(n // tn,),
        in_specs=[pl.BlockSpec((r, d), lambda j: (0, 0)),
                  pl.BlockSpec((d, tn), lambda j: (0, j)),
                  pl.BlockSpec((1, tn), lambda j: (0, j))],
        out_specs=pl.BlockSpec((r, tn), lambda j: (0, j)),
        out_shape=jax.ShapeDtypeStruct((r, n), F32),
        compiler_params=_cparams(("arbitrary",)),
        name="ada_mod",
    )(c_rows, w, b.reshape(1, n))


def _inproj_kernel(x_ref, sc_ref, sh_ref, lb_ref, w_ref, *o_refs, plan):
    u = (x_ref[0] * (1.0 + sc_ref[0]) + sh_ref[0]).astype(BF16)
    refs = iter(o_refs)
    for kind, chunks in plan:
        if kind == "forget":
            k_ref, hi_ref, lo_ref = next(refs), next(refs), next(refs)
            for (w0, o0, n) in chunks:
                lb = lb_ref[:, o0:o0 + n]
                f = lb + (1.0 - lb) * _sigmoid(_dot(u, w_ref[:, w0:w0 + n]))
                lf = jnp.log(f)
                hi = lf.astype(BF16)
                k_ref[0, :, o0:o0 + n] = (1.0 - f).astype(BF16)
                hi_ref[0, :, o0:o0 + n] = hi
                lo_ref[0, :, o0:o0 + n] = (lf - hi.astype(F32)).astype(BF16)
        else:
            o_ref = next(refs)
            for (w0, o0, n) in chunks:
                p = _dot(u, w_ref[:, w0:w0 + n])
                if kind == "silu":
                    p = p * _sigmoid(p)
                o_ref[0, :, o0:o0 + n] = p.astype(o_ref.dtype)


def _inproj(x, scale, shift, lb_row, w_bf16, outs, tm):
    b, s, d = x.shape
    n_w = w_bf16.shape[1]
    plan, widths = [], []
    for (c0, width, kind) in outs:
        step = min(width, 512)
        plan.append((kind, tuple((c0 + o, o, step) for o in range(0, width, step))))
        widths += [width] * (3 if kind == "forget" else 1)
    kern = functools.partial(_inproj_kernel, plan=tuple(plan))
    return pl.pallas_call(
        kern,
        grid=(b, s // tm),
        in_specs=[pl.BlockSpec((1, tm, d), lambda i, j: (i, j, 0)),
                  pl.BlockSpec((1, 1, d), lambda i, j: (i, 0, 0)),
                  pl.BlockSpec((1, 1, d), lambda i, j: (i, 0, 0)),
                  pl.BlockSpec(lb_row.shape, lambda i, j: (0, 0)),
                  pl.BlockSpec((d, n_w), lambda i, j: (0, 0), pipeline_mode=pl.Buffered(1))],
        out_specs=[pl.BlockSpec((1, tm, width), lambda i, j: (i, j, 0)) for width in widths],
        out_shape=[jax.ShapeDtypeStruct((b, s, width), BF16) for width in widths],
        compiler_params=_cparams(("parallel", "parallel"), VMEM_LIMIT),
        name="in_proj",
    )(x, scale, shift, lb_row, w_bf16)


HG_TILE = 256
HG_GROUP = 4


def _gla_group(jobs, states):
    states = list(states)
    nj = len(jobs)
    r, dk = jobs[0][1].shape
    dv = jobs[0][4].shape[1]
    nc = r // CHUNK
    zero = jnp.zeros((CHUNK, dk), BF16)
    contract_last = (((1,), (1,)), ((), ()))
    contract_rows = (((0,), (0,)), ((), ()))

    cums = []
    for (_, hi, lo, _, _, _, tri, _, _) in jobs:
        both = _dot(tri, jnp.concatenate([hi, lo], axis=1))
        cums.append(both[:, :dk] + both[:, dk:])

    k_inv_b, decays, k_end_blk, q_dec = [], [], [], []
    for cum, (_, _, _, k, _, q, _, _, forward) in zip(cums, jobs):
        k_inv = k.astype(F32) * jnp.exp(-cum)
        k_inv_b.append(k_inv.astype(BF16))
        tot_row = CHUNK - 1 if forward else 0
        dec_j, rows_j = [], []
        for ci in range(nc):
            dec = jnp.exp(cum[ci * CHUNK + tot_row:ci * CHUNK + tot_row + 1, :])
            k_end = (k_inv[ci * CHUNK:(ci + 1) * CHUNK] * dec).astype(BF16)
            dec_j.append(dec)
            rows_j.append(jnp.concatenate([k_end if cj == ci else zero for cj in range(nc)], axis=1))
        decays.append(dec_j)
        k_end_blk.append(jnp.concatenate(rows_j, axis=0))
        q_dec.append(None if q is None else (q.astype(F32) * jnp.exp(cum)).astype(BF16))

    upd_all = [lax.dot_general(job[4], blk, contract_rows, preferred_element_type=F32)
               for job, blk in zip(jobs, k_end_blk)]
    scores = [None if qd is None else lax.dot_general(qd, kb, contract_last, preferred_element_type=F32)
              for qd, kb in zip(q_dec, k_inv_b)]

    s_all = []
    for ji in range(nj):
        chain, forward = jobs[ji][0], jobs[ji][8]
        st = states[chain]
        entering = [None] * nc
        for ci in (range(nc) if forward else reversed(range(nc))):
            entering[ci] = st
            st = st * decays[ji][ci] + upd_all[ji][:, ci * dk:(ci + 1) * dk]
        states[chain] = st
        s_all.append(None if q_dec[ji] is None
                     else [jnp.concatenate([entering[ci].astype(BF16), entering[ci + 1].astype(BF16)], axis=0)
                           for ci in range(0, nc, 2)])

    outs = []
    for ji in range(nj):
        if q_dec[ji] is None:
            outs.append(None)
            continue
        sc = jnp.where(jobs[ji][7] > 0, scores[ji], 0.0).astype(BF16)
        intra = _dot(sc, jobs[ji][4])
        inter = []
        for pi, s_pair in enumerate(s_all[ji]):
            rows = slice(2 * pi * CHUNK, (2 * pi + 2) * CHUNK)
            both = lax.dot_general(q_dec[ji][rows], s_pair, contract_last, preferred_element_type=F32)
            inter += [both[:CHUNK, :dv], both[CHUNK:, dv:]]
        outs.append(intra + jnp.concatenate(inter, axis=0))
    return outs, states


def _hgrn_kernel(q_ref, kf_ref, kb_ref, hif_ref, hib_ref, lof_ref, lob_ref, v_ref, g_ref,
                 ckf_ref, ckb_ref, chif_ref, chib_ref, clof_ref, clob_ref, cv_ref, ng_ref, trif_ref, trib_ref,
                 keepf_ref, keepb_ref, y_ref, of_ref, ob_ref, *, seq, ctx_len):
    r = HG_TILE
    n_lat = seq // r
    n_ctx = ctx_len // r
    group = HG_GROUP

    def ctx_jobs():
        jobs = []
        for i in range(n_ctx):
            sl_f = pl.ds(i * r, r)
            sl_b = pl.ds((n_ctx - 1 - i) * r, r)
            jobs.append((0, chif_ref[0, sl_f, :], clof_ref[0, sl_f, :], ckf_ref[0, sl_f, :], cv_ref[0, sl_f, :], None,
                         trif_ref[...], None, True))
            jobs.append((1, chib_ref[0, sl_b, :], clob_ref[0, sl_b, :], ckb_ref[0, sl_b, :], cv_ref[0, sl_b, :], None,
                         trib_ref[...], None, False))
        return jobs

    ng = ng_ref[...]

    def readout(tile, o):
        sl = pl.ds(tile * r, r)
        o = o * lax.rsqrt(jnp.mean(o * o, axis=-1, keepdims=True) + RMS_EPS) * ng
        g = g_ref[0, sl, :].astype(F32)
        y_ref[0, sl, :] = (o * (g * _sigmoid(g))).astype(y_ref.dtype)

    def step_of(tile, forward):
        return (tile if forward else n_lat - 1 - tile) // group

    carry = (jnp.zeros((HG_D, HG_D), F32),) * 2
    for i in range(n_lat // group):
        jobs = ctx_jobs() if i == 0 else []
        meta = [None] * len(jobs)
        for t in range(group):
            for forward in (True, False):
                tile = i * group + t if forward else n_lat - 1 - (i * group + t)
                sl = pl.ds(tile * r, r)
                hi_ref, lo_ref, k_ref, tri_ref, keep_ref = ((hif_ref, lof_ref, kf_ref, trif_ref, keepf_ref) if forward
                                                            else (hib_ref, lob_ref, kb_ref, trib_ref, keepb_ref))
                jobs.append((0 if forward else 1, hi_ref[0, sl, :], lo_ref[0, sl, :], k_ref[0, sl, :], v_ref[0, sl, :],
                             q_ref[0, sl, :], tri_ref[...], keep_ref[...], forward))
                meta.append((tile, forward))
        outs, carry = _gla_group(jobs, carry)
        waiting = {}
        for m, o in zip(meta, outs):
            if m is None:
                continue
            tile, forward = m
            mine, other = (of_ref, ob_ref) if forward else (ob_ref, of_ref)
            other_step = step_of(tile, not forward)
            if other_step < i:
                readout(tile, o + other[pl.ds(tile * r, r), :])
            elif other_step > i:
                mine[pl.ds(tile * r, r), :] = o
            elif tile in waiting:
                readout(tile, o + waiting.pop(tile))
            else:
                waiting[tile] = o


def _chunk_tri(r, forward, dtype):
    i = np.arange(r)
    same = (i[:, None] // CHUNK) == (i[None, :] // CHUNK)
    order = (i[:, None] >= i[None, :]) if forward else (i[:, None] <= i[None, :])
    return jnp.asarray(same & order, dtype)


def _hgrn(q, k, hi, lo, v, g, ck, chi, clo, cv, ng):
    b, s, _ = q.shape
    lc = ck.shape[1]
    h = HG_HEADS
    d = HG_D
    r = HG_TILE
    kern = functools.partial(_hgrn_kernel, seq=s, ctx_len=lc)
    fwd = lambda i, j: (i, 0, j)
    bwd = lambda i, j: (i, 0, j + h)
    const = lambda i, j: (0, 0)
    lat = lambda im: pl.BlockSpec((1, s, d), im)
    cx = lambda im: pl.BlockSpec((1, lc, d), im)
    return pl.pallas_call(
        kern,
        grid=(b, h),
        in_specs=[lat(fwd), lat(fwd), lat(bwd), lat(fwd), lat(bwd), lat(fwd), lat(bwd), lat(fwd), lat(fwd),
                  cx(fwd), cx(bwd), cx(fwd), cx(bwd), cx(fwd), cx(bwd), cx(fwd),
                  pl.BlockSpec((1, d), lambda i, j: (0, j)),
                  pl.BlockSpec((r, r), const),
                  pl.BlockSpec((r, r), const),
                  pl.BlockSpec((r, r), const),
                  pl.BlockSpec((r, r), const)],
        out_specs=pl.BlockSpec((1, s, d), fwd),
        out_shape=jax.ShapeDtypeStruct((b, s, h * d), BF16),
        scratch_shapes=[pltpu.VMEM((s, d), F32), pltpu.VMEM((s, d), F32)],
        compiler_params=_cparams(("parallel", "parallel"), VMEM_LIMIT),
        name="hgrn2",
    )(q, k, k, hi, hi, lo, lo, v, g, ck, ck, chi, chi, clo, clo, cv, ng,
      _chunk_tri(r, True, BF16), _chunk_tri(r, False, BF16), _chunk_tri(r, True, F32), _chunk_tri(r, False, F32))


def _dft_tables(seq):
    rows = seq // GRID_W
    n = F_GROUP_DIM
    kc = np.outer(np.arange(n), np.arange(n)) % n
    ang = 2.0 * np.pi * kc / n
    norm = 1.0 / math.sqrt(rows * GRID_W * n)
    ch = np.concatenate([np.cos(ang), -np.sin(ang)], axis=1) * norm
    t = np.arange(seq)
    r, w = t // GRID_W, t % GRID_W
    m = (np.outer(r, r) * GRID_W + np.outer(w, w) * rows) % (rows * GRID_W)
    ang_t = 2.0 * np.pi * m / (rows * GRID_W)
    tok = np.concatenate([np.cos(ang_t), np.sin(ang_t)], axis=1)
    return jnp.asarray(ch, BF16), jnp.asarray(tok, BF16)


def _fourier_kernel(p_ref, ch_ref, tok_ref, y_ref, xs_ref, *, seq):
    for gi in range(F_GROUPS):
        cols = slice(gi * F_GROUP_DIM, (gi + 1) * F_GROUP_DIM)
        x1 = _dot(p_ref[0, :, cols], ch_ref[...]).astype(BF16)
        xs_ref[0:seq, cols] = x1[:, :F_GROUP_DIM]
        xs_ref[seq:2 * seq, cols] = x1[:, F_GROUP_DIM:]
    y_ref[0] = _dot(tok_ref[...], xs_ref[...]).astype(y_ref.dtype)


def _fourier(pf):
    b, s, wdt = pf.shape
    ch, tok = _dft_tables(s)
    kern = functools.partial(_fourier_kernel, seq=s)
    return pl.pallas_call(
        kern,
        grid=(b,),
        in_specs=[pl.BlockSpec((1, s, wdt), lambda i: (i, 0, 0)),
                  pl.BlockSpec(ch.shape, lambda i: (0, 0)),
                  pl.BlockSpec(tok.shape, lambda i: (0, 0), pipeline_mode=pl.Buffered(1))],
        out_specs=pl.BlockSpec((1, s, wdt), lambda i: (i, 0, 0)),
        out_shape=jax.ShapeDtypeStruct((b, s, wdt), BF16),
        scratch_shapes=[pltpu.VMEM((2 * s, wdt), BF16)],
        compiler_params=_cparams(("parallel",), VMEM_LIMIT),
        name="fourier",
    )(pf, ch, tok)


def _layer_norm(t, g, b):
    mu = jnp.mean(t, axis=-1, keepdims=True)
    tc = t - mu
    var = jnp.mean(tc * tc, axis=-1, keepdims=True)
    return tc * lax.rsqrt(var + LN_EPS) * g + b


def _merge_kernel(yf_ref, yh_ref, gt_ref, x_ref, g1_ref, sc2_ref, sh2_ref, wfo_ref, who_ref, wo_ref,
                  lg_ref, lbias_ref, wr_ref, br_ref, x1_ref, idx_ref, wt_ref):
    d = x_ref.shape[-1]
    tm = x_ref.shape[1]
    tiles = [slice(r0, r0 + MERGE_ROWS) for r0 in range(0, tm, MERGE_ROWS)]
    branch = [(_dot(yf_ref[0, rows, :], wfo_ref[...]), _dot(yh_ref[0, rows, :], who_ref[...])) for rows in tiles]
    merged = [(_sigmoid(gt_ref[0, rows, :d].astype(F32)) * pf + _sigmoid(gt_ref[0, rows, d:].astype(F32)) * ph).astype(BF16)
              for rows, (pf, ph) in zip(tiles, branch)]
    mixes = [_dot(m, wo_ref[...]) for m in merged]
    u2s = []
    for rows, mix in zip(tiles, mixes):
        x1 = _layer_norm(DEEPNORM_ALPHA * x_ref[0, rows, :] + g1_ref[0] * mix, lg_ref[...], lbias_ref[...])
        x1_ref[0, rows, :] = x1
        u2s.append((x1 * (1.0 + sc2_ref[0]) + sh2_ref[0]).astype(BF16))
    all_logits = [lax.dot_general(wr_ref[...], u2, (((1,), (1,)), ((), ())), preferred_element_type=F32) + br_ref[...]
                  for u2 in u2s]
    for rows, logits in zip(tiles, all_logits):
        expert = lax.broadcasted_iota(jnp.int32, logits.shape, 0)
        expert_f = expert.astype(F32)
        work = logits
        vals, idxs = [], []
        for _ in range(TOP_K):
            mx = jnp.max(work, axis=0, keepdims=True)
            sel = jnp.min(jnp.where(work == mx, expert_f, float(N_EXPERTS)), axis=0, keepdims=True).astype(jnp.int32)
            vals.append(mx)
            idxs.append(sel)
            work = jnp.where(expert == sel, -jnp.inf, work)
        exps = [jnp.exp(vv - vals[0]) for vv in vals]
        inv = 1.0 / (exps[0] + exps[1] + exps[2] + exps[3])
        for kk in range(TOP_K):
            idx_ref[kk:kk + 1, rows] = idxs[kk]
            wt_ref[kk:kk + 1, rows] = exps[kk] * inv


def _merge(yf, yh, gates, x, g1, sc2, sh2, wfo, who, wo, lg, lbias, wr, br, tm):
    b, s, d = x.shape
    tile = lambda i, j: (i, j, 0)
    per_b = lambda i, j: (i, 0, 0)
    const = lambda i, j: (0, 0)
    slots = lambda i, j: (0, i * (s // tm) + j)
    return pl.pallas_call(
        _merge_kernel,
        grid=(b, s // tm),
        in_specs=[pl.BlockSpec((1, tm, yf.shape[-1]), tile),
                  pl.BlockSpec((1, tm, yh.shape[-1]), tile),
                  pl.BlockSpec((1, tm, gates.shape[-1]), tile),
                  pl.BlockSpec((1, tm, d), tile),
                  pl.BlockSpec((1, 1, d), per_b),
                  pl.BlockSpec((1, 1, d), per_b),
                  pl.BlockSpec((1, 1, d), per_b),
                  pl.BlockSpec(wfo.shape, const),
                  pl.BlockSpec(who.shape, const),
                  pl.BlockSpec(wo.shape, const),
                  pl.BlockSpec((1, d), const),
                  pl.BlockSpec((1, d), const),
                  pl.BlockSpec(wr.shape, const),
                  pl.BlockSpec(br.shape, const)],
        out_specs=[pl.BlockSpec((1, tm, d), tile),
                   pl.BlockSpec((TOP_K, tm), slots),
                   pl.BlockSpec((TOP_K, tm), slots)],
        out_shape=[jax.ShapeDtypeStruct((b, s, d), F32),
                   jax.ShapeDtypeStruct((TOP_K, b * s), jnp.int32),
                   jax.ShapeDtypeStruct((TOP_K, b * s), F32)],
        compiler_params=_cparams(("parallel", "parallel"), VMEM_LIMIT),
        name="merge_ln_router",
    )(yf, yh, gates, x, g1, sc2, sh2, wfo, who, wo, lg, lbias, wr, br)


def _route_kernel(idx_ref, before_ref, dest_ref, cnt_ref, start_ref, run_ref, *, block_rows):
    phase = pl.program_id(0)
    t = pl.program_id(1)
    tt = idx_ref.shape[1]
    expert = lax.broadcasted_iota(jnp.int32, (N_EXPERTS, tt), 0)
    onehots = [jnp.where(expert == idx_ref[kk:kk + 1, :], 1.0, 0.0) for kk in range(TOP_K)]
    counts = [jnp.sum(oh, axis=1, keepdims=True) for oh in onehots]

    @pl.when(jnp.logical_and(phase == 0, t == 0))
    def _():
        run_ref[...] = jnp.zeros_like(run_ref)

    @pl.when(phase == 0)
    def _():
        run_ref[...] += counts[0] + counts[1] + counts[2] + counts[3]

    @pl.when(jnp.logical_and(phase == 1, t == 0))
    def _():
        cnt = run_ref[...]
        padded = jnp.floor((cnt + (block_rows - 1)) * (1.0 / block_rows)) * block_rows
        r = lax.broadcasted_iota(jnp.int32, (N_EXPERTS, N_EXPERTS), 0)
        c = lax.broadcasted_iota(jnp.int32, (N_EXPERTS, N_EXPERTS), 1)
        earlier = jnp.where(r > c, 1.0, 0.0)
        start = jnp.dot(earlier, padded, preferred_element_type=F32, precision=lax.Precision.HIGHEST)
        cnt_ref[...] = cnt.astype(jnp.int32)
        start_ref[...] = start.astype(jnp.int32)
        run_ref[...] = jnp.zeros_like(run_ref)

    @pl.when(phase == 1)
    def _():
        prefixes = _dot(jnp.concatenate([oh.astype(BF16) for oh in onehots], axis=0), before_ref[...])
        start = start_ref[:, 0:1].astype(F32)
        base = run_ref[:, 0:1] + start
        for kk in range(TOP_K):
            prefix = prefixes[kk * N_EXPERTS:(kk + 1) * N_EXPERTS, :]
            dest_ref[kk:kk + 1, :] = jnp.sum(onehots[kk] * (prefix + base), axis=0, keepdims=True).astype(jnp.int32)
            base = base + counts[kk]
        run_ref[...] = jnp.broadcast_to(base - start, run_ref.shape)


def _route(idx_t, block_rows, tt):
    n = idx_t.shape[1]
    kern = functools.partial(_route_kernel, block_rows=block_rows)
    before = jnp.asarray(np.triu(np.ones((tt, tt)), 1), BF16)
    stat = pl.BlockSpec((N_EXPERTS, LANES), lambda p, t: (0, 0))
    return pl.pallas_call(
        kern,
        grid=(2, n // tt),
        in_specs=[pl.BlockSpec((TOP_K, tt), lambda p, t: (0, t)),
                  pl.BlockSpec((tt, tt), lambda p, t: (0, 0))],
        out_specs=[pl.BlockSpec((TOP_K, tt), lambda p, t: (0, t * p)), stat, stat],
        out_shape=[jax.ShapeDtypeStruct((TOP_K, n), jnp.int32),
                   jax.ShapeDtypeStruct((N_EXPERTS, LANES), jnp.int32),
                   jax.ShapeDtypeStruct((N_EXPERTS, LANES), jnp.int32)],
        scratch_shapes=[pltpu.VMEM((N_EXPERTS, LANES), F32)],
        compiler_params=_cparams(("arbitrary", "arbitrary")),
        name="route_ranks",
    )(idx_t, before)


def _to_row_tiles(ref, base, val):
    rows = val.shape[0]
    for s in range(SUBLANES):
        ref[pl.ds(base + s, rows, stride=SUBLANES), :] = val[:, s * LANES:(s + 1) * LANES]


def _from_row_tiles(ref, base, rows):
    return jnp.concatenate([ref[pl.ds(base + s, rows, stride=SUBLANES), :] for s in range(SUBLANES)], axis=1)


def _row_tile(ref, r):
    return ref.at[pl.ds(pl.multiple_of(r * SUBLANES, SUBLANES), SUBLANES), :]


def _row_tiles(ref, r, n):
    return ref.at[pl.ds(pl.multiple_of(r * SUBLANES, SUBLANES), n * SUBLANES), :]


def _dispatch_kernel(cnt_ref, start_ref, dest_ref, x1_ref, sc_ref, sh_ref, h_ref, u_ref, zero_ref, sems, *, block_rows):
    t = pl.program_id(0)
    nt = pl.num_programs(0)
    tt = x1_ref.shape[0]
    slot = t % 2
    ubase = pl.multiple_of(slot * (tt * SUBLANES), tt * SUBLANES)

    _to_row_tiles(u_ref, ubase, x1_ref[...] * (1.0 + sc_ref[0]) + sh_ref[0])

    def issue(i, _):
        for kk in range(TOP_K):
            d = dest_ref[0, 0, kk * tt + i]
            pltpu.make_async_copy(_row_tile(u_ref, slot * tt + i), _row_tile(h_ref, d),
                                  sems.at[slot]).start(priority=kk % DMA_QUEUES)
        return 0

    lax.fori_loop(0, tt, issue, 0, unroll=ISSUE_UNROLL)

    def drain(sl):
        for _ in range(TOP_K):
            pltpu.make_async_copy(_row_tiles(u_ref, sl * tt, tt), _row_tiles(h_ref, 0, tt), sems.at[sl]).wait()

    @pl.when(t > 0)
    def _():
        drain(1 - slot)

    @pl.when(t == nt - 1)
    def _():
        drain(slot)
        zero_ref[...] = jnp.zeros_like(zero_ref)

        def pad_expert(e, _):
            cnt = cnt_ref[e]
            rem = (block_rows - (cnt & (block_rows - 1))) & (block_rows - 1)
            pos = start_ref[e] + cnt
            size = block_rows // 2
            while size >= 1:
                take = rem & size

                @pl.when(take != 0)
                def _(pos=pos, size=size):
                    cp = pltpu.make_async_copy(_row_tiles(zero_ref, 0, size), _row_tiles(h_ref, pos, size), sems.at[2])
                    cp.start()
                    cp.wait()

                pos = pos + take
                size //= 2
            return 0

        lax.fori_loop(0, N_EXPERTS, pad_expert, 0)


def _dispatch(counts, starts, dest_flat, x1, sc2, sh2, n_rows, tt):
    n, d = x1.shape
    b = sc2.shape[0]
    tiles_per_b = (n // b) // tt
    per_b = lambda t, c, s: (t // tiles_per_b, 0, 0)
    grid_spec = pltpu.PrefetchScalarGridSpec(
        num_scalar_prefetch=2,
        grid=(n // tt,),
        in_specs=[pl.BlockSpec((1, 1, tt * TOP_K), lambda t, c, s: (t, 0, 0), memory_space=pltpu.SMEM),
                  pl.BlockSpec((tt, d), lambda t, c, s: (t, 0)),
                  pl.BlockSpec((1, 1, d), per_b),
                  pl.BlockSpec((1, 1, d), per_b)],
        out_specs=pl.BlockSpec(memory_space=pl.ANY),
        scratch_shapes=[pltpu.VMEM((2 * tt * SUBLANES, LANES), F32), pltpu.VMEM((MOE_ROWS // 2 * SUBLANES, LANES), F32),
                        pltpu.SemaphoreType.DMA((3,))],
    )
    return pl.pallas_call(
        functools.partial(_dispatch_kernel, block_rows=MOE_ROWS),
        grid_spec=grid_spec,
        out_shape=jax.ShapeDtypeStruct((n_rows * SUBLANES, LANES), F32),
        compiler_params=_cparams(("arbitrary",)),
        name="moe_dispatch",
    )(counts, starts, dest_flat, x1, sc2, sh2)


def _expert_kernel(be_ref, nu_ref, nxt_ref, h_ref, wgu_hbm, bgu_ref, wdn_hbm, bdn_ref, o_ref,
                   wgu_f_ref, wdn_f_ref, wgu_b_ref, wdn_b_ref, sems):
    j = pl.program_id(0)
    d = wdn_b_ref.shape[-1]
    bm = h_ref.shape[0] // SUBLANES

    def weight_copies(e):
        return (pltpu.make_async_copy(wgu_hbm.at[e], wgu_f_ref, sems.at[0]),
                pltpu.make_async_copy(wdn_hbm.at[e], wdn_f_ref, sems.at[1]))

    @pl.when(j < nu_ref[0])
    def _():
        @pl.when(j == 0)
        def _():
            for cp in weight_copies(be_ref[0]):
                cp.start()

        @pl.when(jnp.logical_or(j == 0, be_ref[j] != be_ref[jnp.maximum(j - 1, 0)]))
        def _():
            for cp in weight_copies(be_ref[j]):
                cp.wait()
            wgu_b_ref[...] = wgu_f_ref[...].astype(BF16)
            wdn_b_ref[...] = wdn_f_ref[...].astype(BF16)

            @pl.when(nxt_ref[j] >= 0)
            def _():
                for cp in weight_copies(nxt_ref[j]):
                    cp.start()

        gu = _dot(_from_row_tiles(h_ref, 0, bm).astype(BF16), wgu_b_ref[...]) + bgu_ref[0]
        gate = jnp.minimum(gu[:, :d], SWIGLU_LIMIT)
        up = jnp.clip(gu[:, d:], -SWIGLU_LIMIT, SWIGLU_LIMIT)
        act = (up + 1.0) * gate * _sigmoid(SWIGLU_ALPHA * gate)
        _to_row_tiles(o_ref, 0, _dot(act.astype(BF16), wdn_b_ref[...]) + bdn_ref[0])


def _experts(h_sorted, block_expert, n_used, next_expert, wgu, bgu, wdn, bdn):
    n_rows = h_sorted.shape[0] // SUBLANES
    e, d, _ = wdn.shape
    bm = MOE_ROWS
    n_blocks = n_rows // bm
    blk = lambda j, be, nu, nx: (jnp.minimum(j, nu[0] - 1), 0)
    exp = lambda j, be, nu, nx: (be[jnp.minimum(j, nu[0] - 1)], 0, 0)
    grid_spec = pltpu.PrefetchScalarGridSpec(
        num_scalar_prefetch=3,
        grid=(n_blocks,),
        in_specs=[pl.BlockSpec((bm * SUBLANES, LANES), blk),
                  pl.BlockSpec(memory_space=pl.ANY),
                  pl.BlockSpec((1, 1, 2 * d), exp),
                  pl.BlockSpec(memory_space=pl.ANY),
                  pl.BlockSpec((1, 1, d), exp)],
        out_specs=pl.BlockSpec((bm * SUBLANES, LANES), blk),
        scratch_shapes=[pltpu.VMEM((d, 2 * d), F32), pltpu.VMEM((d, d), F32),
                        pltpu.VMEM((d, 2 * d), BF16), pltpu.VMEM((d, d), BF16), pltpu.SemaphoreType.DMA((2,))],
    )
    return pl.pallas_call(
        _expert_kernel,
        grid_spec=grid_spec,
        out_shape=jax.ShapeDtypeStruct((n_rows * SUBLANES, LANES), F32),
        compiler_params=_cparams(("arbitrary",), VMEM_LIMIT),
        name="moe_experts",
    )(block_expert, n_used, next_expert, h_sorted, wgu, bgu.reshape(e, 1, 2 * d), wdn, bdn.reshape(e, 1, d))


def _combine_kernel(dest_ref, dest_next_ref, wt_ref, x1_ref, g2_ref, lg_ref, lbias_ref, o_hbm, y_ref, buf0_ref, buf1_ref, sems):
    t = pl.program_id(0)
    nt = pl.num_programs(0)
    tt = x1_ref.shape[0]
    slot = t % 2

    bufs = (buf0_ref, buf1_ref)

    def issue_row(d_ref, sl, i):
        for kk in range(TOP_K):
            d = d_ref[0, 0, kk * tt + i]
            pltpu.make_async_copy(_row_tile(o_hbm, d), _row_tile(bufs[sl], kk * tt + i),
                                  sems.at[sl]).start(priority=kk % DMA_QUEUES)

    def drain(sl):
        for kk in range(TOP_K):
            pltpu.make_async_copy(_row_tiles(o_hbm, 0, tt), _row_tiles(bufs[sl], kk * tt, tt), sems.at[sl]).wait()

    @pl.when(t == 0)
    def _():
        def first(i, _):
            issue_row(dest_ref, 0, i)
            return 0

        lax.fori_loop(0, tt, first, 0, unroll=ISSUE_UNROLL)

    def fused(sl):
        drain(sl)
        for r0 in range(0, tt, COMBINE_ROWS):
            for i in range(r0, r0 + COMBINE_ROWS):
                issue_row(dest_next_ref, 1 - sl, i)
            rows = pl.ds(r0, COMBINE_ROWS)
            ff = None
            for kk in range(TOP_K):
                term = wt_ref[rows, kk:kk + 1] * _from_row_tiles(bufs[sl], (kk * tt + r0) * SUBLANES, COMBINE_ROWS)
                ff = term if ff is None else ff + term
            y_ref[rows, :] = _layer_norm(DEEPNORM_ALPHA * x1_ref[rows, :] + g2_ref[0] * ff, lg_ref[...], lbias_ref[...])

        @pl.when(t == nt - 1)
        def _():
            drain(1 - sl)

    for sl in range(2):
        pl.when(slot == sl)(functools.partial(fused, sl))


def _combine(dest_flat, wt, x1, g2, lg, lbias, o_sorted, tt):
    n, d = x1.shape
    b = g2.shape[0]
    nt = n // tt
    tiles_per_b = (n // b) // tt
    const = lambda t: (0, 0)
    dest_spec = lambda im: pl.BlockSpec((1, 1, tt * TOP_K), im, memory_space=pltpu.SMEM)
    return pl.pallas_call(
        _combine_kernel,
        grid=(nt,),
        in_specs=[dest_spec(lambda t: (t, 0, 0)),
                  dest_spec(lambda t: (jnp.minimum(t + 1, nt - 1), 0, 0)),
                  pl.BlockSpec((tt, TOP_K), lambda t: (t, 0)),
                  pl.BlockSpec((tt, d), lambda t: (t, 0)),
                  pl.BlockSpec((1, 1, d), lambda t: (t // tiles_per_b, 0, 0)),
                  pl.BlockSpec((1, d), const),
                  pl.BlockSpec((1, d), const),
                  pl.BlockSpec(memory_space=pl.ANY)],
        out_specs=pl.BlockSpec((tt, d), lambda t: (t, 0)),
        out_shape=jax.ShapeDtypeStruct((n, d), F32),
        scratch_shapes=[pltpu.VMEM((TOP_K * tt * SUBLANES, LANES), F32)] * 2 + [pltpu.SemaphoreType.DMA((2,))],
        compiler_params=_cparams(("arbitrary",), VMEM_LIMIT),
        name="moe_combine",
    )(dest_flat, dest_flat, wt, x1, g2, lg, lbias, o_sorted)


def kernel(x, c, ctx, c_ctx, w_ada, b_ada, w_in, lb_raw, hg_norm_g, w_four_out, w_hg_out, w_o, ln1_g, ln1_b,
           w_router, b_router, w_gate_up, b_gate_up, w_down, b_down, ln2_g, ln2_b):
    b, s, d = x.shape
    lc = ctx.shape[1]
    n = b * s
    f_w = F_GROUPS * F_GROUP_DIM
    hk = HG_HEADS * HG_D
    assert w_ada.shape[0] == DEPTH and s % GRID_W == 0 and s % (HG_TILE * HG_GROUP) == 0 and lc % HG_TILE == 0
    assert s % ROW_TILE == 0 and ROW_TILE % MOE_TOKENS == 0 and MOE_TOKENS % COMBINE_ROWS == 0

    pad = (-(b + 1)) % 8
    c_rows = jnp.concatenate([c, c_ctx[None, :], jnp.zeros((pad, d), F32)], axis=0)
    mod = _ada(c_rows, w_ada[0], b_ada[0]).reshape(b + 1 + pad, N_MOD, 1, d)
    shift1, scale1, gate1, shift2, scale2, gate2 = (mod[:b, i] for i in range(N_MOD))
    cshift1 = jnp.broadcast_to(mod[b, 0], (b, 1, d))
    cscale1 = jnp.broadcast_to(mod[b, 1], (b, 1, d))

    lower_bounds = jnp.cumsum(jax.nn.softmax(lb_raw.astype(F32), axis=0), axis=0)[0]

    w_in_b = w_in[0].astype(BF16)
    o_f, o_q, o_z, o_v, o_g, o_gt = f_w, f_w + hk, f_w + 3 * hk, f_w + 4 * hk, f_w + 5 * hk, f_w + 5 * hk + 2 * d
    lb_row = lower_bounds.reshape(1, 2 * hk)
    pf, pq, pk, phi, plo, pv, pg, gates = _inproj(
        x, scale1, shift1, lb_row, w_in_b,
        [(0, f_w, "cast"), (o_f, hk, "silu"), (o_q, 2 * hk, "forget"), (o_z, hk, "cast"), (o_v, hk, "cast"),
         (o_g, 2 * d, "cast")], tm=ROW_TILE)
    w_ctx = w_in_b[:, o_q:o_v]
    ck, chi, clo, cv = _inproj(ctx, cscale1, cshift1, lb_row, w_ctx, [(0, 2 * hk, "forget"), (2 * hk, hk, "cast")], tm=lc)

    y_h = _hgrn(pq, pk, phi, plo, pv, pg, ck, chi, clo, cv, hg_norm_g[0].reshape(1, hk))
    y_f = _fourier(pf)

    wr = w_router[0].T.astype(BF16)
    br = b_router[0].reshape(N_EXPERTS, 1)
    x1, idx_t, wt_t = _merge(y_f, y_h, gates, x, gate1, scale2, shift2,
                         w_four_out[0].astype(BF16), w_hg_out[0].astype(BF16), w_o[0].astype(BF16),
                         ln1_g[0].reshape(1, d), ln1_b[0].reshape(1, d), wr, br, tm=ROW_TILE)

    dest_t, counts, starts = _route(idx_t, MOE_ROWS, ROW_TILE)
    n_blocks = -(-(n * TOP_K) // MOE_ROWS) + N_EXPERTS
    n_rows = n_blocks * MOE_ROWS
    cnt = counts[:, 0]
    st = starts[:, 0]
    pad_ends = st + (cnt + MOE_ROWS - 1) // MOE_ROWS * MOE_ROWS
    block_first_row = jnp.arange(n_blocks, dtype=jnp.int32) * MOE_ROWS
    block_expert = jnp.minimum(jnp.sum((pad_ends[None, :] <= block_first_row[:, None]).astype(jnp.int32), axis=1),
                               N_EXPERTS - 1)
    n_used = (pad_ends[-1] // MOE_ROWS).astype(jnp.int32).reshape(1)
    group_end_block = pad_ends[block_expert] // MOE_ROWS
    next_expert = jnp.where(group_end_block < n_used[0], block_expert[jnp.minimum(group_end_block, n_blocks - 1)], -1)

    tt = MOE_TOKENS
    dest_flat = dest_t.reshape(TOP_K, n // tt, tt).transpose(1, 0, 2).reshape(n // tt, 1, TOP_K * tt)
    x1_2d = x1.reshape(n, d)
    h_sorted = _dispatch(cnt, st, dest_flat, x1_2d, scale2, shift2, n_rows, tt)
    o_sorted = _experts(h_sorted, block_expert, n_used, next_expert, w_gate_up[0], b_gate_up[0], w_down[0], b_down[0])
    out = _combine(dest_flat, wt_t.T, x1_2d, gate2, ln2_g[0].reshape(1, d), ln2_b[0].reshape(1, d),
                   o_sorted, tt)
    return out.reshape(b, s, d)
```

```python
import functools
import math

import jax
import jax.numpy as jnp
import numpy as np
from jax import lax
from jax.experimental import pallas as pl
from jax.experimental.pallas import tpu as pltpu

F32 = jnp.float32
BF16 = jnp.bfloat16

GRID_W = 64
N_MOD = 6
F_GROUPS = 4
F_GROUP_DIM = 128
HG_HEADS = 4
HG_D = 128
CHUNK = 64
N_EXPERTS = 32
TOP_K = 4
SWIGLU_LIMIT = 7.0
SWIGLU_ALPHA = 1.702
LN_EPS = 1e-5
RMS_EPS = 1e-6
DEPTH = 1
DEEPNORM_ALPHA = (2.0 * DEPTH) ** 0.25

LANES = 128
SUBLANES = 8
ROW_TILE = 1024
MERGE_ROWS = 512
MOE_TOKENS = 512
MOE_ROWS = 512
COMBINE_ROWS = 256
ISSUE_UNROLL = 4
DMA_QUEUES = 2
VMEM_LIMIT = 56 * 1024 * 1024


def _cparams(sem, vmem=None):
    return pltpu.CompilerParams(dimension_semantics=sem, vmem_limit_bytes=vmem)


def _dot(a, b):
    return jnp.dot(a, b, preferred_element_type=F32)


def _sigmoid(x):
    return 0.5 * jnp.tanh(0.5 * x) + 0.5


def _ada_kernel(c_ref, w_ref, b_ref, o_ref):
    c = c_ref[...]
    a = c * _sigmoid(c)
    o_ref[...] = jnp.dot(a, w_ref[...], preferred_element_type=F32,
                         precision=lax.Precision.HIGHEST) + b_ref[...]


def _ada(c_rows, w, b):
    r, d = c_rows.shape
    n = w.shape[1]
    tn = 512
    return pl.pallas_call(
        _ada_kernel,
        grid=(n // tn,),
        in_specs=[pl.BlockSpec((r, d), lambda j: (0, 0)),
                  pl.BlockSpec((d, tn), lambda j: (0, j)),
                  pl.BlockSpec((1, tn), lambda j: (0, j))],
        out_specs=pl.BlockSpec((r, tn), lambda j: (0, j)),
        out_shape=jax.ShapeDtypeStruct((r, n), F32),
        compiler_params=_cparams(("arbitrary",)),
        name="ada_mod",
    )(c_rows, w, b.reshape(1, n))


def _inproj_kernel(x_ref, sc_ref, sh_ref, lb_ref, w_ref, *o_refs, plan):
    u = (x_ref[0] * (1.0 + sc_ref[0]) + sh_ref[0]).astype(BF16)
    refs = iter(o_refs)
    for kind, chunks in plan:
        if kind == "forget":
            k_ref, hi_ref, lo_ref = next(refs), next(refs), next(refs)
            for (w0, o0, n) in chunks:
                lb = lb_ref[:, o0:o0 + n]
                f = lb + (1.0 - lb) * _sigmoid(_dot(u, w_ref[:, w0:w0 + n]))
                lf = jnp.log(f)
                hi = lf.astype(BF16)
                k_ref[0, :, o0:o0 + n] = (1.0 - f).astype(BF16)
                hi_ref[0, :, o0:o0 + n] = hi
                lo_ref[0, :, o0:o0 + n] = (lf - hi.astype(F32)).astype(BF16)
        else:
            o_ref = next(refs)
            for (w0, o0, n) in chunks:
                p = _dot(u, w_ref[:, w0:w0 + n])
                if kind == "silu":
                    p = p * _sigmoid(p)
                o_ref[0, :, o0:o0 + n] = p.astype(o_ref.dtype)


def _inproj(x, scale, shift, lb_row, w_bf16, outs, tm):
    b, s, d = x.shape
    n_w = w_bf16.shape[1]
    plan, widths = [], []
    for (c0, width, kind) in outs:
        step = min(width, 512)
        plan.append((kind, tuple((c0 + o, o, step) for o in range(0, width, step))))
        widths += [width] * (3 if kind == "forget" else 1)
    kern = functools.partial(_inproj_kernel, plan=tuple(plan))
    return pl.pallas_call(
        kern,
        grid=(b, s // tm),
        in_specs=[pl.BlockSpec((1, tm, d), lambda i, j: (i, j, 0)),
                  pl.BlockSpec((1, 1, d), lambda i, j: (i, 0, 0)),
                  pl.BlockSpec((1, 1, d), lambda i, j: (i, 0, 0)),
                  pl.BlockSpec(lb_row.shape, lambda i, j: (0, 0)),
                  pl.BlockSpec((d, n_w), lambda i, j: (0, 0), pipeline_mode=pl.Buffered(1))],
        out_specs=[pl.BlockSpec((1, tm, width), lambda i, j: (i, j, 0)) for width in widths],
        out_shape=[jax.ShapeDtypeStruct((b, s, width), BF16) for width in widths],
        compiler_params=_cparams(("parallel", "parallel"), VMEM_LIMIT),
        name="in_proj",
    )(x, scale, shift, lb_row, w_bf16)


HG_TILE = 256
HG_GROUP = 4


def _gla_group(jobs, states):
    states = list(states)
    nj = len(jobs)
    r, dk = jobs[0][1].shape
    dv = jobs[0][4].shape[1]
    nc = r // CHUNK
    zero = jnp.zeros((CHUNK, dk), BF16)
    contract_last = (((1,), (1,)), ((), ()))
    contract_rows = (((0,), (0,)), ((), ()))

    cums = []
    for (_, hi, lo, _, _, _, tri, _, _) in jobs:
        both = _dot(tri, jnp.concatenate([hi, lo], axis=1))
        cums.append(both[:, :dk] + both[:, dk:])

    k_inv_b, decays, k_end_blk, q_dec = [], [], [], []
    for cum, (_, _, _, k, _, q, _, _, forward) in zip(cums, jobs):
        k_inv = k.astype(F32) * jnp.exp(-cum)
        k_inv_b.append(k_inv.astype(BF16))
        tot_row = CHUNK - 1 if forward else 0
        dec_j, rows_j = [], []
        for ci in range(nc):
            dec = jnp.exp(cum[ci * CHUNK + tot_row:ci * CHUNK + tot_row + 1, :])
            k_end = (k_inv[ci * CHUNK:(ci + 1) * CHUNK] * dec).astype(BF16)
            dec_j.append(dec)
            rows_j.append(jnp.concatenate([k_end if cj == ci else zero for cj in range(nc)], axis=1))
        decays.append(dec_j)
        k_end_blk.append(jnp.concatenate(rows_j, axis=0))
        q_dec.append(None if q is None else (q.astype(F32) * jnp.exp(cum)).astype(BF16))

    upd_all = [lax.dot_general(job[4], blk, contract_rows, preferred_element_type=F32)
               for job, blk in zip(jobs, k_end_blk)]
    scores = [None if qd is None else lax.dot_general(qd, kb, contract_last, preferred_element_type=F32)
              for qd, kb in zip(q_dec, k_inv_b)]

    s_all = []
    for ji in range(nj):
        chain, forward = jobs[ji][0], jobs[ji][8]
        st = states[chain]
        entering = [None] * nc
        for ci in (range(nc) if forward else reversed(range(nc))):
            entering[ci] = st
            st = st * decays[ji][ci] + upd_all[ji][:, ci * dk:(ci + 1) * dk]
        states[chain] = st
        s_all.append(None if q_dec[ji] is None
                     else [jnp.concatenate([entering[ci].astype(BF16), entering[ci + 1].astype(BF16)], axis=0)
                           for ci in range(0, nc, 2)])

    outs = []
    for ji in range(nj):
        if q_dec[ji] is None:
            outs.append(None)
            continue
        sc = jnp.where(jobs[ji][7] > 0, scores[ji], 0.0).astype(BF16)
        intra = _dot(sc, jobs[ji][4])
        inter = []
        for pi, s_pair in enumerate(s_all[ji]):
            rows = slice(2 * pi * CHUNK, (2 * pi + 2) * CHUNK)
            both = lax.dot_general(q_dec[ji][rows], s_pair, contract_last, preferred_element_type=F32)
            inter += [both[:CHUNK, :dv], both[CHUNK:, dv:]]
        outs.append(intra + jnp.concatenate(inter, axis=0))
    return outs, states


def _hgrn_kernel(q_ref, kf_ref, kb_ref, hif_ref, hib_ref, lof_ref, lob_ref, v_ref, g_ref,
                 ckf_ref, ckb_ref, chif_ref, chib_ref, clof_ref, clob_ref, cv_ref, ng_ref, trif_ref, trib_ref,
                 keepf_ref, keepb_ref, y_ref, of_ref, ob_ref, *, seq, ctx_len):
    r = HG_TILE
    n_lat = seq // r
    n_ctx = ctx_len // r
    group = HG_GROUP

    def ctx_jobs():
        jobs = []
        for i in range(n_ctx):
            sl_f = pl.ds(i * r, r)
            sl_b = pl.ds((n_ctx - 1 - i) * r, r)
            jobs.append((0, chif_ref[0, sl_f, :], clof_ref[0, sl_f, :], ckf_ref[0, sl_f, :], cv_ref[0, sl_f, :], None,
                         trif_ref[...], None, True))
            jobs.append((1, chib_ref[0, sl_b, :], clob_ref[0, sl_b, :], ckb_ref[0, sl_b, :], cv_ref[0, sl_b, :], None,
                         trib_ref[...], None, False))
        return jobs

    ng = ng_ref[...]

    def readout(tile, o):
        sl = pl.ds(tile * r, r)
        o = o * lax.rsqrt(jnp.mean(o * o, axis=-1, keepdims=True) + RMS_EPS) * ng
        g = g_ref[0, sl, :].astype(F32)
        y_ref[0, sl, :] = (o * (g * _sigmoid(g))).astype(y_ref.dtype)

    def step_of(tile, forward):
        return (tile if forward else n_lat - 1 - tile) // group

    carry = (jnp.zeros((HG_D, HG_D), F32),) * 2
    for i in range(n_lat // group):
        jobs = ctx_jobs() if i == 0 else []
        meta = [None] * len(jobs)
        for t in range(group):
            for forward in (True, False):
                tile = i * group + t if forward else n_lat - 1 - (i * group + t)
                sl = pl.ds(tile * r, r)
                hi_ref, lo_ref, k_ref, tri_ref, keep_ref = ((hif_ref, lof_ref, kf_ref, trif_ref, keepf_ref) if forward
                                                            else (hib_ref, lob_ref, kb_ref, trib_ref, keepb_ref))
                jobs.append((0 if forward else 1, hi_ref[0, sl, :], lo_ref[0, sl, :], k_ref[0, sl, :], v_ref[0, sl, :],
                             q_ref[0, sl, :], tri_ref[...], keep_ref[...], forward))
                meta.append((tile, forward))
        outs, carry = _gla_group(jobs, carry)
        waiting = {}
        for m, o in zip(meta, outs):
            if m is None:
                continue
            tile, forward = m
            mine, other = (of_ref, ob_ref) if forward else (ob_ref, of_ref)
            other_step = step_of(tile, not forward)
            if other_step < i:
                readout(tile, o + other[pl.ds(tile * r, r), :])
            elif other_step > i:
                mine[pl.ds(tile * r, r), :] = o
            elif tile in waiting:
                readout(tile, o + waiting.pop(tile))
            else:
                waiting[tile] = o


def _chunk_tri(r, forward, dtype):
    i = np.arange(r)
    same = (i[:, None] // CHUNK) == (i[None, :] // CHUNK)
    order = (i[:, None] >= i[None, :]) if forward else (i[:, None] <= i[None, :])
    return jnp.asarray(same & order, dtype)


def _hgrn(q, k, hi, lo, v, g, ck, chi, clo, cv, ng):
    b, s, _ = q.shape
    lc = ck.shape[1]
    h = HG_HEADS
    d = HG_D
    r = HG_TILE
    kern = functools.partial(_hgrn_kernel, seq=s, ctx_len=lc)
    fwd = lambda i, j: (i, 0, j)
    bwd = lambda i, j: (i, 0, j + h)
    const = lambda i, j: (0, 0)
    lat = lambda im: pl.BlockSpec((1, s, d), im)
    cx = lambda im: pl.BlockSpec((1, lc, d), im)
    return pl.pallas_call(
        kern,
        grid=(b, h),
        in_specs=[lat(fwd), lat(fwd), lat(bwd), lat(fwd), lat(bwd), lat(fwd), lat(bwd), lat(fwd), lat(fwd),
                  cx(fwd), cx(bwd), cx(fwd), cx(bwd), cx(fwd), cx(bwd), cx(fwd),
                  pl.BlockSpec((1, d), lambda i, j: (0, j)),
                  pl.BlockSpec((r, r), const),
                  pl.BlockSpec((r, r), const),
                  pl.BlockSpec((r, r), const),
                  pl.BlockSpec((r, r), const)],
        out_specs=pl.BlockSpec((1, s, d), fwd),
        out_shape=jax.ShapeDtypeStruct((b, s, h * d), BF16),
        scratch_shapes=[pltpu.VMEM((s, d), F32), pltpu.VMEM((s, d), F32)],
        compiler_params=_cparams(("parallel", "parallel"), VMEM_LIMIT),
        name="hgrn2",
    )(q, k, k, hi, hi, lo, lo, v, g, ck, ck, chi, chi, clo, clo, cv, ng,
      _chunk_tri(r, True, BF16), _chunk_tri(r, False, BF16), _chunk_tri(r, True, F32), _chunk_tri(r, False, F32))


def _dft_tables(seq):
    rows = seq // GRID_W
    n = F_GROUP_DIM
    kc = np.outer(np.arange(n), np.arange(n)) % n
    ang = 2.0 * np.pi * kc / n
    norm = 1.0 / math.sqrt(rows * GRID_W * n)
    ch = np.concatenate([np.cos(ang), -np.sin(ang)], axis=1) * norm
    t = np.arange(seq)
    r, w = t // GRID_W, t % GRID_W
    m = (np.outer(r, r) * GRID_W + np.outer(w, w) * rows) % (rows * GRID_W)
    ang_t = 2.0 * np.pi * m / (rows * GRID_W)
    tok = np.concatenate([np.cos(ang_t), np.sin(ang_t)], axis=1)
    return jnp.asarray(ch, BF16), jnp.asarray(tok, BF16)


def _fourier_kernel(p_ref, ch_ref, tok_ref, y_ref, xs_ref, *, seq):
    for gi in range(F_GROUPS):
        cols = slice(gi * F_GROUP_DIM, (gi + 1) * F_GROUP_DIM)
        x1 = _dot(p_ref[0, :, cols], ch_ref[...]).astype(BF16)
        xs_ref[0:seq, cols] = x1[:, :F_GROUP_DIM]
        xs_ref[seq:2 * seq, cols] = x1[:, F_GROUP_DIM:]
    y_ref[0] = _dot(tok_ref[...], xs_ref[...]).astype(y_ref.dtype)


def _fourier(pf):
    b, s, wdt = pf.shape
    ch, tok = _dft_tables(s)
    kern = functools.partial(_fourier_kernel, seq=s)
    return pl.pallas_call(
        kern,
        grid=(b,),
        in_specs=[pl.BlockSpec((1, s, wdt), lambda i: (i, 0, 0)),
                  pl.BlockSpec(ch.shape, lambda i: (0, 0)),
                  pl.BlockSpec(tok.shape, lambda i: (0, 0), pipeline_mode=pl.Buffered(1))],
        out_specs=pl.BlockSpec((1, s, wdt), lambda i: (i, 0, 0)),
        out_shape=jax.ShapeDtypeStruct((b, s, wdt), BF16),
        scratch_shapes=[pltpu.VMEM((2 * s, wdt), BF16)],
        compiler_params=_cparams(("parallel",), VMEM_LIMIT),
        name="fourier",
    )(pf, ch, tok)


def _layer_norm(t, g, b):
    mu = jnp.mean(t, axis=-1, keepdims=True)
    tc = t - mu
    var = jnp.mean(tc * tc, axis=-1, keepdims=True)
    return tc * lax.rsqrt(var + LN_EPS) * g + b


def _merge_kernel(yf_ref, yh_ref, gt_ref, x_ref, g1_ref, sc2_ref, sh2_ref, wfo_ref, who_ref, wo_ref,
                  lg_ref, lbias_ref, wr_ref, br_ref, x1_ref, idx_ref, wt_ref):
    d = x_ref.shape[-1]
    tm = x_ref.shape[1]
    tiles = [slice(r0, r0 + MERGE_ROWS) for r0 in range(0, tm, MERGE_ROWS)]
    branch = [(_dot(yf_ref[0, rows, :], wfo_ref[...]), _dot(yh_ref[0, rows, :], who_ref[...])) for rows in tiles]
    merged = [(_sigmoid(gt_ref[0, rows, :d].astype(F32)) * pf + _sigmoid(gt_ref[0, rows, d:].astype(F32)) * ph).astype(BF16)
              for rows, (pf, ph) in zip(tiles, branch)]
    mixes = [_dot(m, wo_ref[...]) for m in merged]
    u2s = []
    for rows, mix in zip(tiles, mixes):
        x1 = _layer_norm(DEEPNORM_ALPHA * x_ref[0, rows, :] + g1_ref[0] * mix, lg_ref[...], lbias_ref[...])
        x1_ref[0, rows, :] = x1
        u2s.append((x1 * (1.0 + sc2_ref[0]) + sh2_ref[0]).astype(BF16))
    all_logits = [lax.dot_general(wr_ref[...], u2, (((1,), (1,)), ((), ())), preferred_element_type=F32) + br_ref[...]
                  for u2 in u2s]
    for rows, logits in zip(tiles, all_logits):
        expert = lax.broadcasted_iota(jnp.int32, logits.shape, 0)
        expert_f = expert.astype(F32)
        work = logits
        vals, idxs = [], []
        for _ in range(TOP_K):
            mx = jnp.max(work, axis=0, keepdims=True)
            sel = jnp.min(jnp.where(work == mx, expert_f, float(N_EXPERTS)), axis=0, keepdims=True).astype(jnp.int32)
            vals.append(mx)
            idxs.append(sel)
            work = jnp.where(expert == sel, -jnp.inf, work)
        exps = [jnp.exp(vv - vals[0]) for vv in vals]
        inv = 1.0 / (exps[0] + exps[1] + exps[2] + exps[3])
        for kk in range(TOP_K):
            idx_ref[kk:kk + 1, rows] = idxs[kk]
            wt_ref[kk:kk + 1, rows] = exps[kk] * inv


def _merge(yf, yh, gates, x, g1, sc2, sh2, wfo, who, wo, lg, lbias, wr, br, tm):
    b, s, d = x.shape
    tile = lambda i, j: (i, j, 0)
    per_b = lambda i, j: (i, 0, 0)
    const = lambda i, j: (0, 0)
    slots = lambda i, j: (0, i * (s // tm) + j)
    return pl.pallas_call(
        _merge_kernel,
        grid=(b, s // tm),
        in_specs=[pl.BlockSpec((1, tm, yf.shape[-1]), tile),
                  pl.BlockSpec((1, tm, yh.shape[-1]), tile),
                  pl.BlockSpec((1, tm, gates.shape[-1]), tile),
                  pl.BlockSpec((1, tm, d), tile),
                  pl.BlockSpec((1, 1, d), per_b),
                  pl.BlockSpec((1, 1, d), per_b),
                  pl.BlockSpec((1, 1, d), per_b),
                  pl.BlockSpec(wfo.shape, const),
                  pl.BlockSpec(who.shape, const),
                  pl.BlockSpec(wo.shape, const),
                  pl.BlockSpec((1, d), const),
                  pl.BlockSpec((1, d), const),
                  pl.BlockSpec(wr.shape, const),
                  pl.BlockSpec(br.shape, const)],
        out_specs=[pl.BlockSpec((1, tm, d), tile),
                   pl.BlockSpec((TOP_K, tm), slots),
                   pl.BlockSpec((TOP_K, tm), slots)],
        out_shape=[jax.ShapeDtypeStruct((b, s, d), F32),
                   jax.ShapeDtypeStruct((TOP_K, b * s), jnp.int32),
                   jax.ShapeDtypeStruct((TOP_K, b * s), F32)],
        compiler_params=_cparams(("parallel", "parallel"), VMEM_LIMIT),
        name="merge_ln_router",
    )(yf, yh, gates, x, g1, sc2, sh2, wfo, who, wo, lg, lbias, wr, br)


def _route_kernel(idx_ref, before_ref, dest_ref, cnt_ref, start_ref, run_ref, *, block_rows):
    phase = pl.program_id(0)
    t = pl.program_id(1)
    tt = idx_ref.shape[1]
    expert = lax.broadcasted_iota(jnp.int32, (N_EXPERTS, tt), 0)
    onehots = [jnp.where(expert == idx_ref[kk:kk + 1, :], 1.0, 0.0) for kk in range(TOP_K)]
    counts = [jnp.sum(oh, axis=1, keepdims=True) for oh in onehots]

    @pl.when(jnp.logical_and(phase == 0, t == 0))
    def _():
        run_ref[...] = jnp.zeros_like(run_ref)

    @pl.when(phase == 0)
    def _():
        run_ref[...] += counts[0] + counts[1] + counts[2] + counts[3]

    @pl.when(jnp.logical_and(phase == 1, t == 0))
    def _():
        cnt = run_ref[...]
        padded = jnp.floor((cnt + (block_rows - 1)) * (1.0 / block_rows)) * block_rows
        r = lax.broadcasted_iota(jnp.int32, (N_EXPERTS, N_EXPERTS), 0)
        c = lax.broadcasted_iota(jnp.int32, (N_EXPERTS, N_EXPERTS), 1)
        earlier = jnp.where(r > c, 1.0, 0.0)
        start = jnp.dot(earlier, padded, preferred_element_type=F32, precision=lax.Precision.HIGHEST)
        cnt_ref[...] = cnt.astype(jnp.int32)
        start_ref[...] = start.astype(jnp.int32)
        run_ref[...] = jnp.zeros_like(run_ref)

    @pl.when(phase == 1)
    def _():
        prefixes = _dot(jnp.concatenate([oh.astype(BF16) for oh in onehots], axis=0), before_ref[...])
        start = start_ref[:, 0:1].astype(F32)
        base = run_ref[:, 0:1] + start
        for kk in range(TOP_K):
            prefix = prefixes[kk * N_EXPERTS:(kk + 1) * N_EXPERTS, :]
            dest_ref[kk:kk + 1, :] = jnp.sum(onehots[kk] * (prefix + base), axis=0, keepdims=True).astype(jnp.int32)
            base = base + counts[kk]
        run_ref[...] = jnp.broadcast_to(base - start, run_ref.shape)


def _route(idx_t, block_rows, tt):
    n = idx_t.shape[1]
    kern = functools.partial(_route_kernel, block_rows=block_rows)
    before = jnp.asarray(np.triu(np.ones((tt, tt)), 1), BF16)
    stat = pl.BlockSpec((N_EXPERTS, LANES), lambda p, t: (0, 0))
    return pl.pallas_call(
        kern,
        grid=(2, n // tt),
        in_specs=[pl.BlockSpec((TOP_K, tt), lambda p, t: (0, t)),
                  pl.BlockSpec((tt, tt), lambda p, t: (0, 0))],
        out_specs=[pl.BlockSpec((TOP_K, tt), lambda p, t: (0, t * p)), stat, stat],
        out_shape=[jax.ShapeDtypeStruct((TOP_K, n), jnp.int32),
                   jax.ShapeDtypeStruct((N_EXPERTS, LANES), jnp.int32),
                   jax.ShapeDtypeStruct((N_EXPERTS, LANES), jnp.int32)],
        scratch_shapes=[pltpu.VMEM((N_EXPERTS, LANES), F32)],
        compiler_params=_cparams(("arbitrary", "arbitrary")),
        name="route_ranks",
    )(idx_t, before)


def _to_row_tiles(ref, base, val):
    rows = val.shape[0]
    for s in range(SUBLANES):
        ref[pl.ds(base + s, rows, stride=SUBLANES), :] = val[:, s * LANES:(s + 1) * LANES]


def _from_row_tiles(ref, base, rows):
    return jnp.concatenate([ref[pl.ds(base + s, rows, stride=SUBLANES), :] for s in range(SUBLANES)], axis=1)


def _row_tile(ref, r):
    return ref.at[pl.ds(pl.multiple_of(r * SUBLANES, SUBLANES), SUBLANES), :]


def _row_tiles(ref, r, n):
    return ref.at[pl.ds(pl.multiple_of(r * SUBLANES, SUBLANES), n * SUBLANES), :]


def _dispatch_kernel(cnt_ref, start_ref, dest_ref, x1_ref, sc_ref, sh_ref, h_ref, u_ref, zero_ref, sems, *, block_rows):
    t = pl.program_id(0)
    nt = pl.num_programs(0)
    tt = x1_ref.shape[0]
    slot = t % 2
    ubase = pl.multiple_of(slot * (tt * SUBLANES), tt * SUBLANES)

    _to_row_tiles(u_ref, ubase, x1_ref[...] * (1.0 + sc_ref[0]) + sh_ref[0])

    def issue(i, _):
        for kk in range(TOP_K):
            d = dest_ref[0, 0, kk * tt + i]
            pltpu.make_async_copy(_row_tile(u_ref, slot * tt + i), _row_tile(h_ref, d),
                                  sems.at[slot]).start(priority=kk % DMA_QUEUES)
        return 0

    lax.fori_loop(0, tt, issue, 0, unroll=ISSUE_UNROLL)

    def drain(sl):
        for _ in range(TOP_K):
            pltpu.make_async_copy(_row_tiles(u_ref, sl * tt, tt), _row_tiles(h_ref, 0, tt), sems.at[sl]).wait()

    @pl.when(t > 0)
    def _():
        drain(1 - slot)

    @pl.when(t == nt - 1)
    def _():
        drain(slot)
        zero_ref[...] = jnp.zeros_like(zero_ref)

        def pad_expert(e, _):
            cnt = cnt_ref[e]
            rem = (block_rows - (cnt & (block_rows - 1))) & (block_rows - 1)
            pos = start_ref[e] + cnt
            size = block_rows // 2
            while size >= 1:
                take = rem & size

                @pl.when(take != 0)
                def _(pos=pos, size=size):
                    cp = pltpu.make_async_copy(_row_tiles(zero_ref, 0, size), _row_tiles(h_ref, pos, size), sems.at[2])
                    cp.start()
                    cp.wait()

                pos = pos + take
                size //= 2
            return 0

        lax.fori_loop(0, N_EXPERTS, pad_expert, 0)


def _dispatch(counts, starts, dest_flat, x1, sc2, sh2, n_rows, tt):
    n, d = x1.shape
    b = sc2.shape[0]
    tiles_per_b = (n // b) // tt
    per_b = lambda t, c, s: (t // tiles_per_b, 0, 0)
    grid_spec = pltpu.PrefetchScalarGridSpec(
        num_scalar_prefetch=2,
        grid=(n // tt,),
        in_specs=[pl.BlockSpec((1, 1, tt * TOP_K), lambda t, c, s: (t, 0, 0), memory_space=pltpu.SMEM),
                  pl.BlockSpec((tt, d), lambda t, c, s: (t, 0)),
                  pl.BlockSpec((1, 1, d), per_b),
                  pl.BlockSpec((1, 1, d), per_b)],
        out_specs=pl.BlockSpec(memory_space=pl.ANY),
        scratch_shapes=[pltpu.VMEM((2 * tt * SUBLANES, LANES), F32), pltpu.VMEM((MOE_ROWS // 2 * SUBLANES, LANES), F32),
                        pltpu.SemaphoreType.DMA((3,))],
    )
    return pl.pallas_call(
        functools.partial(_dispatch_kernel, block_rows=MOE_ROWS),
        grid_spec=grid_spec,
        out_shape=jax.ShapeDtypeStruct((n_rows * SUBLANES, LANES), F32),
        compiler_params=_cparams(("arbitrary",)),
        name="moe_dispatch",
    )(counts, starts, dest_flat, x1, sc2, sh2)


def _expert_kernel(be_ref, nu_ref, nxt_ref, h_ref, wgu_hbm, bgu_ref, wdn_hbm, bdn_ref, o_ref,
                   wgu_f_ref, wdn_f_ref, wgu_b_ref, wdn_b_ref, sems):
    j = pl.program_id(0)
    d = wdn_b_ref.shape[-1]
    bm = h_ref.shape[0] // SUBLANES

    def weight_copies(e):
        return (pltpu.make_async_copy(wgu_hbm.at[e], wgu_f_ref, sems.at[0]),
                pltpu.make_async_copy(wdn_hbm.at[e], wdn_f_ref, sems.at[1]))

    @pl.when(j < nu_ref[0])
    def _():
        @pl.when(j == 0)
        def _():
            for cp in weight_copies(be_ref[0]):
                cp.start()

        @pl.when(jnp.logical_or(j == 0, be_ref[j] != be_ref[jnp.maximum(j - 1, 0)]))
        def _():
            for cp in weight_copies(be_ref[j]):
                cp.wait()
            wgu_b_ref[...] = wgu_f_ref[...].astype(BF16)
            wdn_b_ref[...] = wdn_f_ref[...].astype(BF16)

            @pl.when(nxt_ref[j] >= 0)
            def _():
                for cp in weight_copies(nxt_ref[j]):
                    cp.start()

        gu = _dot(_from_row_tiles(h_ref, 0, bm).astype(BF16), wgu_b_ref[...]) + bgu_ref[0]
        gate = jnp.minimum(gu[:, :d], SWIGLU_LIMIT)
        up = jnp.clip(gu[:, d:], -SWIGLU_LIMIT, SWIGLU_LIMIT)
        act = (up + 1.0) * gate * _sigmoid(SWIGLU_ALPHA * gate)
        _to_row_tiles(o_ref, 0, _dot(act.astype(BF16), wdn_b_ref[...]) + bdn_ref[0])


def _experts(h_sorted, block_expert, n_used, next_expert, wgu, bgu, wdn, bdn):
    n_rows = h_sorted.shape[0] // SUBLANES
    e, d, _ = wdn.shape
    bm = MOE_ROWS
    n_blocks = n_rows // bm
    blk = lambda j, be, nu, nx: (jnp.minimum(j, nu[0] - 1), 0)
    exp = lambda j, be, nu, nx: (be[jnp.minimum(j, nu[0] - 1)], 0, 0)
    grid_spec = pltpu.PrefetchScalarGridSpec(
        num_scalar_prefetch=3,
        grid=(n_blocks,),
        in_specs=[pl.BlockSpec((bm * SUBLANES, LANES), blk),
                  pl.BlockSpec(memory_space=pl.ANY),
                  pl.BlockSpec((1, 1, 2 * d), exp),
                  pl.BlockSpec(memory_space=pl.ANY),
                  pl.BlockSpec((1, 1, d), exp)],
        out_specs=pl.BlockSpec((bm * SUBLANES, LANES), blk),
        scratch_shapes=[pltpu.VMEM((d, 2 * d), F32), pltpu.VMEM((d, d), F32),
                        pltpu.VMEM((d, 2 * d), BF16), pltpu.VMEM((d, d), BF16), pltpu.SemaphoreType.DMA((2,))],
    )
    return pl.pallas_call(
        _expert_kernel,
        grid_spec=grid_spec,
        out_shape=jax.ShapeDtypeStruct((n_rows * SUBLANES, LANES), F32),
        compiler_params=_cparams(("arbitrary",), VMEM_LIMIT),
        name="moe_experts",
    )(block_expert, n_used, next_expert, h_sorted, wgu, bgu.reshape(e, 1, 2 * d), wdn, bdn.reshape(e, 1, d))


def _combine_kernel(dest_ref, dest_next_ref, wt_ref, x1_ref, g2_ref, lg_ref, lbias_ref, o_hbm, y_ref, buf0_ref, buf1_ref, sems):
    t = pl.program_id(0)
    nt = pl.num_programs(0)
    tt = x1_ref.shape[0]
    slot = t % 2

    bufs = (buf0_ref, buf1_ref)

    def issue_row(d_ref, sl, i):
        for kk in range(TOP_K):
            d = d_ref[0, 0, kk * tt + i]
            pltpu.make_async_copy(_row_tile(o_hbm, d), _row_tile(bufs[sl], kk * tt + i),
                                  sems.at[sl]).start(priority=kk % DMA_QUEUES)

    def drain(sl):
        for kk in range(TOP_K):
            pltpu.make_async_copy(_row_tiles(o_hbm, 0, tt), _row_tiles(bufs[sl], kk * tt, tt), sems.at[sl]).wait()

    @pl.when(t == 0)
    def _():
        def first(i, _):
            issue_row(dest_ref, 0, i)
            return 0

        lax.fori_loop(0, tt, first, 0, unroll=ISSUE_UNROLL)

    def fused(sl):
        drain(sl)
        for r0 in range(0, tt, COMBINE_ROWS):
            for i in range(r0, r0 + COMBINE_ROWS):
                issue_row(dest_next_ref, 1 - sl, i)
            rows = pl.ds(r0, COMBINE_ROWS)
            ff = None
            for kk in range(TOP_K):
                term = wt_ref[rows, kk:kk + 1] * _from_row_tiles(bufs[sl], (kk * tt + r0) * SUBLANES, COMBINE_ROWS)
                ff = term if ff is None else ff + term
            y_ref[rows, :] = _layer_norm(DEEPNORM_ALPHA * x1_ref[rows, :] + g2_ref[0] * ff, lg_ref[...], lbias_ref[...])

        @pl.when(t == nt - 1)
        def _():
            drain(1 - sl)

    for sl in range(2):
        pl.when(slot == sl)(functools.partial(fused, sl))


def _combine(dest_flat, wt, x1, g2, lg, lbias, o_sorted, tt):
    n, d = x1.shape
    b = g2.shape[0]
    nt = n // tt
    tiles_per_b = (n // b) // tt
    const = lambda t: (0, 0)
    dest_spec = lambda im: pl.BlockSpec((1, 1, tt * TOP_K), im, memory_space=pltpu.SMEM)
    return pl.pallas_call(
        _combine_kernel,
        grid=(nt,),
        in_specs=[dest_spec(lambda t: (t, 0, 0)),
                  dest_spec(lambda t: (jnp.minimum(t + 1, nt - 1), 0, 0)),
                  pl.BlockSpec((tt, TOP_K), lambda t: (t, 0)),
                  pl.BlockSpec((tt, d), lambda t: (t, 0)),
                  pl.BlockSpec((1, 1, d), lambda t: (t // tiles_per_b, 0, 0)),
                  pl.BlockSpec((1, d), const),
                  pl.BlockSpec((1, d), const),
                  pl.BlockSpec(memory_space=pl.ANY)],
        out_specs=pl.BlockSpec((tt, d), lambda t: (t, 0)),
        out_shape=jax.ShapeDtypeStruct((n, d), F32),
        scratch_shapes=[pltpu.VMEM((TOP_K * tt * SUBLANES, LANES), F32)] * 2 + [pltpu.SemaphoreType.DMA((2,))],
        compiler_params=_cparams(("arbitrary",), VMEM_LIMIT),
        name="moe_combine",
    )(dest_flat, dest_flat, wt, x1, g2, lg, lbias, o_sorted)


def kernel(x, c, ctx, c_ctx, w_ada, b_ada, w_in, lb_raw, hg_norm_g, w_four_out, w_hg_out, w_o, ln1_g, ln1_b,
           w_router, b_router, w_gate_up, b_gate_up, w_down, b_down, ln2_g, ln2_b):
    b, s, d = x.shape
    lc = ctx.shape[1]
    n = b * s
    f_w = F_GROUPS * F_GROUP_DIM
    hk = HG_HEADS * HG_D
    assert w_ada.shape[0] == DEPTH and s % GRID_W == 0 and s % (HG_TILE * HG_GROUP) == 0 and lc % HG_TILE == 0
    assert s % ROW_TILE == 0 and ROW_TILE % MOE_TOKENS == 0 and MOE_TOKENS % COMBINE_ROWS == 0

    pad = (-(b + 1)) % 8
    c_rows = jnp.concatenate([c, c_ctx[None, :], jnp.zeros((pad, d), F32)], axis=0)
    mod = _ada(c_rows, w_ada[0], b_ada[0]).reshape(b + 1 + pad, N_MOD, 1, d)
    shift1, scale1, gate1, shift2, scale2, gate2 = (mod[:b, i] for i in range(N_MOD))
    cshift1 = jnp.broadcast_to(mod[b, 0], (b, 1, d))
    cscale1 = jnp.broadcast_to(mod[b, 1], (b, 1, d))

    lower_bounds = jnp.cumsum(jax.nn.softmax(lb_raw.astype(F32), axis=0), axis=0)[0]

    w_in_b = w_in[0].astype(BF16)
    o_f, o_q, o_z, o_v, o_g, o_gt = f_w, f_w + hk, f_w + 3 * hk, f_w + 4 * hk, f_w + 5 * hk, f_w + 5 * hk + 2 * d
    lb_row = lower_bounds.reshape(1, 2 * hk)
    pf, pq, pk, phi, plo, pv, pg, gates = _inproj(
        x, scale1, shift1, lb_row, w_in_b,
        [(0, f_w, "cast"), (o_f, hk, "silu"), (o_q, 2 * hk, "forget"), (o_z, hk, "cast"), (o_v, hk, "cast"),
         (o_g, 2 * d, "cast")], tm=ROW_TILE)
    w_ctx = w_in_b[:, o_q:o_v]
    ctx_tile = ROW_TILE if (b * lc) % ROW_TILE == 0 else lc
    ck, chi, clo, cv = (t.reshape(b, lc, -1) for t in _inproj(
        ctx.reshape(1, b * lc, d), cscale1[:1], cshift1[:1], lb_row, w_ctx,
        [(0, 2 * hk, "forget"), (2 * hk, hk, "cast")], tm=ctx_tile))

    y_h = _hgrn(pq, pk, phi, plo, pv, pg, ck, chi, clo, cv, hg_norm_g[0].reshape(1, hk))
    y_f = _fourier(pf)

    wr = w_router[0].T.astype(BF16)
    br = b_router[0].reshape(N_EXPERTS, 1)
    x1, idx_t, wt_t = _merge(y_f, y_h, gates, x, gate1, scale2, shift2,
                         w_four_out[0].astype(BF16), w_hg_out[0].astype(BF16), w_o[0].astype(BF16),
                         ln1_g[0].reshape(1, d), ln1_b[0].reshape(1, d), wr, br, tm=ROW_TILE)

    dest_t, counts, starts = _route(idx_t, MOE_ROWS, ROW_TILE)
    n_blocks = -(-(n * TOP_K) // MOE_ROWS) + N_EXPERTS
    n_rows = n_blocks * MOE_ROWS
    cnt = counts[:, 0]
    st = starts[:, 0]
    pad_ends = st + (cnt + MOE_ROWS - 1) // MOE_ROWS * MOE_ROWS
    block_first_row = jnp.arange(n_blocks, dtype=jnp.int32) * MOE_ROWS
    block_expert = jnp.minimum(jnp.sum((pad_ends[None, :] <= block_first_row[:, None]).astype(jnp.int32), axis=1),
                               N_EXPERTS - 1)
    n_used = (pad_ends[-1] // MOE_ROWS).astype(jnp.int32).reshape(1)
    group_end_block = pad_ends[block_expert] // MOE_ROWS
    next_expert = jnp.where(group_end_block < n_used[0], block_expert[jnp.minimum(group_end_block, n_blocks - 1)], -1)

    tt = MOE_TOKENS
    dest_flat = dest_t.reshape(TOP_K, n // tt, tt).transpose(1, 0, 2).reshape(n // tt, 1, TOP_K * tt)
    x1_2d = x1.reshape(n, d)
    h_sorted = _dispatch(cnt, st, dest_flat, x1_2d, scale2, shift2, n_rows, tt)
    o_sorted = _experts(h_sorted, block_expert, n_used, next_expert, w_gate_up[0], b_gate_up[0], w_down[0], b_down[0])
    out = _combine(dest_flat, wt_t.T, x1_2d, gate2, ln2_g[0].reshape(1, d), ln2_b[0].reshape(1, d),
                   o_sorted, tt)
    return out.reshape(b, s, d)
```

```python
import functools
import math

import jax
import jax.numpy as jnp
import numpy as np
from jax import lax
from jax.experimental import pallas as pl
from jax.experimental.pallas import tpu as pltpu

F32 = jnp.float32
BF16 = jnp.bfloat16

GRID_W = 64
N_MOD = 6
F_GROUPS = 4
F_GROUP_DIM = 128
HG_HEADS = 4
HG_D = 128
CHUNK = 64
N_EXPERTS = 32
TOP_K = 4
SWIGLU_LIMIT = 7.0
SWIGLU_ALPHA = 1.702
LN_EPS = 1e-5
RMS_EPS = 1e-6
DEPTH = 1
DEEPNORM_ALPHA = (2.0 * DEPTH) ** 0.25

LANES = 128
SUBLANES = 8
ROW_TILE = 1024
MERGE_ROWS = 512
MOE_TOKENS = 512
MOE_ROWS = 512
COMBINE_ROWS = 256
ISSUE_UNROLL = 4
DMA_QUEUES = 2
VMEM_LIMIT = 56 * 1024 * 1024


def _cparams(sem, vmem=None):
    return pltpu.CompilerParams(dimension_semantics=sem, vmem_limit_bytes=vmem)


def _dot(a, b):
    return jnp.dot(a, b, preferred_element_type=F32)


def _sigmoid(x):
    return 0.5 * jnp.tanh(0.5 * x) + 0.5


def _ada_kernel(c_ref, w_ref, b_ref, o_ref):
    c = c_ref[...]
    a = c * _sigmoid(c)
    o_ref[...] = jnp.dot(a, w_ref[...], preferred_element_type=F32,
                         precision=lax.Precision.HIGHEST) + b_ref[...]


def _ada(c_rows, w, b):
    r, d = c_rows.shape
    n = w.shape[1]
    tn = 512
    return pl.pallas_call(
        _ada_kernel,
        grid=(n // tn,),
        in_specs=[pl.BlockSpec((r, d), lambda j: (0, 0)),
                  pl.BlockSpec((d, tn), lambda j: (0, j)),
                  pl.BlockSpec((1, tn), lambda j: (0, j))],
        out_specs=pl.BlockSpec((r, tn), lambda j: (0, j)),
        out_shape=jax.ShapeDtypeStruct((r, n), F32),
        compiler_params=_cparams(("arbitrary",)),
        name="ada_mod",
    )(c_rows, w, b.reshape(1, n))


def _inproj_kernel(x_ref, sc_ref, sh_ref, lb_ref, w_ref, *o_refs, plan):
    u = (x_ref[0] * (1.0 + sc_ref[0]) + sh_ref[0]).astype(BF16)
    refs = iter(o_refs)
    for kind, chunks in plan:
        if kind == "forget":
            k_ref, hi_ref, lo_ref = next(refs), next(refs), next(refs)
            for (w0, o0, n) in chunks:
                lb = lb_ref[:, o0:o0 + n]
                f = lb + (1.0 - lb) * _sigmoid(_dot(u, w_ref[:, w0:w0 + n]))
                lf = jnp.log(f)
                hi = lf.astype(BF16)
                k_ref[0, :, o0:o0 + n] = (1.0 - f).astype(BF16)
                hi_ref[0, :, o0:o0 + n] = hi
                lo_ref[0, :, o0:o0 + n] = (lf - hi.astype(F32)).astype(BF16)
        else:
            o_ref = next(refs)
            for (w0, o0, n) in chunks:
                p = _dot(u, w_ref[:, w0:w0 + n])
                if kind == "silu":
                    p = p * _sigmoid(p)
                o_ref[0, :, o0:o0 + n] = p.astype(o_ref.dtype)


def _inproj(x, scale, shift, lb_row, w_bf16, outs, tm):
    b, s, d = x.shape
    n_w = w_bf16.shape[1]
    plan, widths = [], []
    for (c0, width, kind) in outs:
        step = min(width, 512)
        plan.append((kind, tuple((c0 + o, o, step) for o in range(0, width, step))))
        widths += [width] * (3 if kind == "forget" else 1)
    kern = functools.partial(_inproj_kernel, plan=tuple(plan))
    return pl.pallas_call(
        kern,
        grid=(b, s // tm),
        in_specs=[pl.BlockSpec((1, tm, d), lambda i, j: (i, j, 0)),
                  pl.BlockSpec((1, 1, d), lambda i, j: (i, 0, 0)),
                  pl.BlockSpec((1, 1, d), lambda i, j: (i, 0, 0)),
                  pl.BlockSpec(lb_row.shape, lambda i, j: (0, 0)),
                  pl.BlockSpec((d, n_w), lambda i, j: (0, 0), pipeline_mode=pl.Buffered(1))],
        out_specs=[pl.BlockSpec((1, tm, width), lambda i, j: (i, j, 0)) for width in widths],
        out_shape=[jax.ShapeDtypeStruct((b, s, width), BF16) for width in widths],
        compiler_params=_cparams(("parallel", "parallel"), VMEM_LIMIT),
        name="in_proj",
    )(x, scale, shift, lb_row, w_bf16)


HG_TILE = 256
HG_GROUP = 4


def _gla_group(jobs, states):
    states = list(states)
    nj = len(jobs)
    r, dk = jobs[0][1].shape
    dv = jobs[0][4].shape[1]
    nc = r // CHUNK
    zero = jnp.zeros((CHUNK, dk), BF16)
    contract_last = (((1,), (1,)), ((), ()))
    contract_rows = (((0,), (0,)), ((), ()))

    cums = []
    for (_, hi, lo, _, _, _, tri, _, _) in jobs:
        both = _dot(tri, jnp.concatenate([hi, lo], axis=1))
        cums.append(both[:, :dk] + both[:, dk:])

    k_inv_b, decays, k_end_blk, q_dec = [], [], [], []
    for cum, (_, _, _, k, _, q, _, _, forward) in zip(cums, jobs):
        k_inv = k.astype(F32) * jnp.exp(-cum)
        k_inv_b.append(k_inv.astype(BF16))
        tot_row = CHUNK - 1 if forward else 0
        dec_j, rows_j = [], []
        for ci in range(nc):
            dec = jnp.exp(cum[ci * CHUNK + tot_row:ci * CHUNK + tot_row + 1, :])
            k_end = (k_inv[ci * CHUNK:(ci + 1) * CHUNK] * dec).astype(BF16)
            dec_j.append(dec)
            rows_j.append(jnp.concatenate([k_end if cj == ci else zero for cj in range(nc)], axis=1))
        decays.append(dec_j)
        k_end_blk.append(jnp.concatenate(rows_j, axis=0))
        q_dec.append(None if q is None else (q.astype(F32) * jnp.exp(cum)).astype(BF16))

    upd_all = [lax.dot_general(job[4], blk, contract_rows, preferred_element_type=F32)
               for job, blk in zip(jobs, k_end_blk)]
    scores = [None if qd is None else lax.dot_general(qd, kb, contract_last, preferred_element_type=F32)
              for qd, kb in zip(q_dec, k_inv_b)]

    s_all = []
    for ji in range(nj):
        chain, forward = jobs[ji][0], jobs[ji][8]
        st = states[chain]
        entering = [None] * nc
        for ci in (range(nc) if forward else reversed(range(nc))):
            entering[ci] = st
            st = st * decays[ji][ci] + upd_all[ji][:, ci * dk:(ci + 1) * dk]
        states[chain] = st
        s_all.append(None if q_dec[ji] is None
                     else [jnp.concatenate([entering[ci].astype(BF16), entering[ci + 1].astype(BF16)], axis=0)
                           for ci in range(0, nc, 2)])

    outs = []
    for ji in range(nj):
        if q_dec[ji] is None:
            outs.append(None)
            continue
        sc = jnp.where(jobs[ji][7] > 0, scores[ji], 0.0).astype(BF16)
        intra = _dot(sc, jobs[ji][4])
        inter = []
        for pi, s_pair in enumerate(s_all[ji]):
            rows = slice(2 * pi * CHUNK, (2 * pi + 2) * CHUNK)
            both = lax.dot_general(q_dec[ji][rows], s_pair, contract_last, preferred_element_type=F32)
            inter += [both[:CHUNK, :dv], both[CHUNK:, dv:]]
        outs.append(intra + jnp.concatenate(inter, axis=0))
    return outs, states


def _hgrn_kernel(q_ref, kf_ref, kb_ref, hif_ref, hib_ref, lof_ref, lob_ref, v_ref, g_ref,
                 ckf_ref, ckb_ref, chif_ref, chib_ref, clof_ref, clob_ref, cv_ref, ng_ref, trif_ref, trib_ref,
                 keepf_ref, keepb_ref, y_ref, of_ref, ob_ref, *, seq, ctx_len):
    r = HG_TILE
    n_lat = seq // r
    n_ctx = ctx_len // r
    group = HG_GROUP

    def ctx_jobs():
        jobs = []
        for i in range(n_ctx):
            sl_f = pl.ds(i * r, r)
            sl_b = pl.ds((n_ctx - 1 - i) * r, r)
            jobs.append((0, chif_ref[0, sl_f, :], clof_ref[0, sl_f, :], ckf_ref[0, sl_f, :], cv_ref[0, sl_f, :], None,
                         trif_ref[...], None, True))
            jobs.append((1, chib_ref[0, sl_b, :], clob_ref[0, sl_b, :], ckb_ref[0, sl_b, :], cv_ref[0, sl_b, :], None,
                         trib_ref[...], None, False))
        return jobs

    ng = ng_ref[...]

    def readout(tile, o):
        sl = pl.ds(tile * r, r)
        o = o * lax.rsqrt(jnp.mean(o * o, axis=-1, keepdims=True) + RMS_EPS) * ng
        g = g_ref[0, sl, :].astype(F32)
        y_ref[0, sl, :] = (o * (g * _sigmoid(g))).astype(y_ref.dtype)

    def step_of(tile, forward):
        return (tile if forward else n_lat - 1 - tile) // group

    carry = (jnp.zeros((HG_D, HG_D), F32),) * 2
    for i in range(n_lat // group):
        jobs = ctx_jobs() if i == 0 else []
        meta = [None] * len(jobs)
        for t in range(group):
            for forward in (True, False):
                tile = i * group + t if forward else n_lat - 1 - (i * group + t)
                sl = pl.ds(tile * r, r)
                hi_ref, lo_ref, k_ref, tri_ref, keep_ref = ((hif_ref, lof_ref, kf_ref, trif_ref, keepf_ref) if forward
                                                            else (hib_ref, lob_ref, kb_ref, trib_ref, keepb_ref))
                jobs.append((0 if forward else 1, hi_ref[0, sl, :], lo_ref[0, sl, :], k_ref[0, sl, :], v_ref[0, sl, :],
                             q_ref[0, sl, :], tri_ref[...], keep_ref[...], forward))
                meta.append((tile, forward))
        outs, carry = _gla_group(jobs, carry)
        waiting = {}
        for m, o in zip(meta, outs):
            if m is None:
                continue
            tile, forward = m
            mine, other = (of_ref, ob_ref) if forward else (ob_ref, of_ref)
            other_step = step_of(tile, not forward)
            if other_step < i:
                readout(tile, o + other[pl.ds(tile * r, r), :])
            elif other_step > i:
                mine[pl.ds(tile * r, r), :] = o
            elif tile in waiting:
                readout(tile, o + waiting.pop(tile))
            else:
                waiting[tile] = o


def _chunk_tri(r, forward, dtype):
    i = np.arange(r)
    same = (i[:, None] // CHUNK) == (i[None, :] // CHUNK)
    order = (i[:, None] >= i[None, :]) if forward else (i[:, None] <= i[None, :])
    return jnp.asarray(same & order, dtype)


def _hgrn(q, k, hi, lo, v, g, ck, chi, clo, cv, ng):
    b, s, _ = q.shape
    lc = ck.shape[1]
    h = HG_HEADS
    d = HG_D
    r = HG_TILE
    kern = functools.partial(_hgrn_kernel, seq=s, ctx_len=lc)
    fwd = lambda i, j: (i, 0, j)
    bwd = lambda i, j: (i, 0, j + h)
    const = lambda i, j: (0, 0)
    lat = lambda im: pl.BlockSpec((1, s, d), im)
    cx = lambda im: pl.BlockSpec((1, lc, d), im)
    return pl.pallas_call(
        kern,
        grid=(b, h),
        in_specs=[lat(fwd), lat(fwd), lat(bwd), lat(fwd), lat(bwd), lat(fwd), lat(bwd), lat(fwd), lat(fwd),
                  cx(fwd), cx(bwd), cx(fwd), cx(bwd), cx(fwd), cx(bwd), cx(fwd),
                  pl.BlockSpec((1, d), lambda i, j: (0, j)),
                  pl.BlockSpec((r, r), const),
                  pl.BlockSpec((r, r), const),
                  pl.BlockSpec((r, r), const),
                  pl.BlockSpec((r, r), const)],
        out_specs=pl.BlockSpec((1, s, d), fwd),
        out_shape=jax.ShapeDtypeStruct((b, s, h * d), BF16),
        scratch_shapes=[pltpu.VMEM((s, d), F32), pltpu.VMEM((s, d), F32)],
        compiler_params=_cparams(("parallel", "parallel"), VMEM_LIMIT),
        name="hgrn2",
    )(q, k, k, hi, hi, lo, lo, v, g, ck, ck, chi, chi, clo, clo, cv, ng,
      _chunk_tri(r, True, BF16), _chunk_tri(r, False, BF16), _chunk_tri(r, True, F32), _chunk_tri(r, False, F32))


def _dft_tables(seq):
    rows = seq // GRID_W
    n = F_GROUP_DIM
    kc = np.outer(np.arange(n), np.arange(n)) % n
    ang = 2.0 * np.pi * kc / n
    norm = 1.0 / math.sqrt(rows * GRID_W * n)
    ch = np.concatenate([np.cos(ang), -np.sin(ang)], axis=1) * norm
    t = np.arange(seq)
    r, w = t // GRID_W, t % GRID_W
    m = (np.outer(r, r) * GRID_W + np.outer(w, w) * rows) % (rows * GRID_W)
    ang_t = 2.0 * np.pi * m / (rows * GRID_W)
    tok = np.concatenate([np.cos(ang_t), np.sin(ang_t)], axis=1)
    return jnp.asarray(ch, BF16), jnp.asarray(tok, BF16)


def _fourier_kernel(p_ref, ch_ref, tok_ref, y_ref, xs_ref, *, seq):
    for gi in range(F_GROUPS):
        cols = slice(gi * F_GROUP_DIM, (gi + 1) * F_GROUP_DIM)
        x1 = _dot(p_ref[0, :, cols], ch_ref[...]).astype(BF16)
        xs_ref[0:seq, cols] = x1[:, :F_GROUP_DIM]
        xs_ref[seq:2 * seq, cols] = x1[:, F_GROUP_DIM:]
    y_ref[0] = _dot(tok_ref[...], xs_ref[...]).astype(y_ref.dtype)


def _fourier(pf):
    b, s, wdt = pf.shape
    ch, tok = _dft_tables(s)
    kern = functools.partial(_fourier_kernel, seq=s)
    return pl.pallas_call(
        kern,
        grid=(b,),
        in_specs=[pl.BlockSpec((1, s, wdt), lambda i: (i, 0, 0)),
                  pl.BlockSpec(ch.shape, lambda i: (0, 0)),
                  pl.BlockSpec(tok.shape, lambda i: (0, 0), pipeline_mode=pl.Buffered(1))],
        out_specs=pl.BlockSpec((1, s, wdt), lambda i: (i, 0, 0)),
        out_shape=jax.ShapeDtypeStruct((b, s, wdt), BF16),
        scratch_shapes=[pltpu.VMEM((2 * s, wdt), BF16)],
        compiler_params=_cparams(("parallel",), VMEM_LIMIT),
        name="fourier",
    )(pf, ch, tok)


def _layer_norm(t, g, b):
    mu = jnp.mean(t, axis=-1, keepdims=True)
    tc = t - mu
    var = jnp.mean(tc * tc, axis=-1, keepdims=True)
    return tc * lax.rsqrt(var + LN_EPS) * g + b


def _merge_kernel(yf_ref, yh_ref, gt_ref, x_ref, g1_ref, sc2_ref, sh2_ref, wfo_ref, who_ref, wo_ref,
                  lg_ref, lbias_ref, wr_ref, br_ref, x1_ref, idx_ref, wt_ref, cnt_ref):
    d = x_ref.shape[-1]
    tm = x_ref.shape[1]

    @pl.when(jnp.logical_and(pl.program_id(0) == 0, pl.program_id(1) == 0))
    def _():
        cnt_ref[...] = jnp.zeros_like(cnt_ref)

    tiles = [slice(r0, r0 + MERGE_ROWS) for r0 in range(0, tm, MERGE_ROWS)]
    branch = [(_dot(yf_ref[0, rows, :], wfo_ref[...]), _dot(yh_ref[0, rows, :], who_ref[...])) for rows in tiles]
    merged = [(_sigmoid(gt_ref[0, rows, :d].astype(F32)) * pf + _sigmoid(gt_ref[0, rows, d:].astype(F32)) * ph).astype(BF16)
              for rows, (pf, ph) in zip(tiles, branch)]
    mixes = [_dot(m, wo_ref[...]) for m in merged]
    u2s = []
    for rows, mix in zip(tiles, mixes):
        x1 = _layer_norm(DEEPNORM_ALPHA * x_ref[0, rows, :] + g1_ref[0] * mix, lg_ref[...], lbias_ref[...])
        x1_ref[0, rows, :] = x1
        u2s.append((x1 * (1.0 + sc2_ref[0]) + sh2_ref[0]).astype(BF16))
    all_logits = [lax.dot_general(wr_ref[...], u2, (((1,), (1,)), ((), ())), preferred_element_type=F32) + br_ref[...]
                  for u2 in u2s]
    for rows, logits in zip(tiles, all_logits):
        expert = lax.broadcasted_iota(jnp.int32, logits.shape, 0)
        expert_f = expert.astype(F32)
        work = logits
        vals, idxs = [], []
        for _ in range(TOP_K):
            mx = jnp.max(work, axis=0, keepdims=True)
            sel = jnp.min(jnp.where(work == mx, expert_f, float(N_EXPERTS)), axis=0, keepdims=True).astype(jnp.int32)
            vals.append(mx)
            idxs.append(sel)
            work = jnp.where(expert == sel, -jnp.inf, work)
        exps = [jnp.exp(vv - vals[0]) for vv in vals]
        inv = 1.0 / (exps[0] + exps[1] + exps[2] + exps[3])
        for kk in range(TOP_K):
            idx_ref[kk:kk + 1, rows] = idxs[kk]
            wt_ref[kk:kk + 1, rows] = exps[kk] * inv
            cnt_ref[...] += jnp.sum(jnp.where(expert == idxs[kk], 1.0, 0.0), axis=1, keepdims=True)


def _merge(yf, yh, gates, x, g1, sc2, sh2, wfo, who, wo, lg, lbias, wr, br, tm):
    b, s, d = x.shape
    tile = lambda i, j: (i, j, 0)
    per_b = lambda i, j: (i, 0, 0)
    const = lambda i, j: (0, 0)
    slots = lambda i, j: (0, i * (s // tm) + j)
    return pl.pallas_call(
        _merge_kernel,
        grid=(b, s // tm),
        in_specs=[pl.BlockSpec((1, tm, yf.shape[-1]), tile),
                  pl.BlockSpec((1, tm, yh.shape[-1]), tile),
                  pl.BlockSpec((1, tm, gates.shape[-1]), tile),
                  pl.BlockSpec((1, tm, d), tile),
                  pl.BlockSpec((1, 1, d), per_b),
                  pl.BlockSpec((1, 1, d), per_b),
                  pl.BlockSpec((1, 1, d), per_b),
                  pl.BlockSpec(wfo.shape, const),
                  pl.BlockSpec(who.shape, const),
                  pl.BlockSpec(wo.shape, const),
                  pl.BlockSpec((1, d), const),
                  pl.BlockSpec((1, d), const),
                  pl.BlockSpec(wr.shape, const),
                  pl.BlockSpec(br.shape, const)],
        out_specs=[pl.BlockSpec((1, tm, d), tile),
                   pl.BlockSpec((TOP_K, tm), slots),
                   pl.BlockSpec((TOP_K, tm), slots),
                   pl.BlockSpec((N_EXPERTS, LANES), const)],
        out_shape=[jax.ShapeDtypeStruct((b, s, d), F32),
                   jax.ShapeDtypeStruct((TOP_K, b * s), jnp.int32),
                   jax.ShapeDtypeStruct((TOP_K, b * s), F32),
                   jax.ShapeDtypeStruct((N_EXPERTS, LANES), F32)],
        compiler_params=_cparams(("arbitrary", "arbitrary"), VMEM_LIMIT),
        name="merge_ln_router",
    )(yf, yh, gates, x, g1, sc2, sh2, wfo, who, wo, lg, lbias, wr, br)


def _route_kernel(idx_ref, before_ref, total_ref, dest_ref, cnt_ref, start_ref, run_ref, *, block_rows):
    t = pl.program_id(0)
    tt = idx_ref.shape[1]
    expert = lax.broadcasted_iota(jnp.int32, (N_EXPERTS, tt), 0)
    onehots = [jnp.where(expert == idx_ref[kk:kk + 1, :], 1.0, 0.0) for kk in range(TOP_K)]
    counts = [jnp.sum(oh, axis=1, keepdims=True) for oh in onehots]

    @pl.when(t == 0)
    def _():
        cnt = total_ref[...]
        padded = jnp.floor((cnt + (block_rows - 1)) * (1.0 / block_rows)) * block_rows
        r = lax.broadcasted_iota(jnp.int32, (N_EXPERTS, N_EXPERTS), 0)
        c = lax.broadcasted_iota(jnp.int32, (N_EXPERTS, N_EXPERTS), 1)
        earlier = jnp.where(r > c, 1.0, 0.0)
        start = jnp.dot(earlier, padded, preferred_element_type=F32, precision=lax.Precision.HIGHEST)
        cnt_ref[...] = cnt.astype(jnp.int32)
        start_ref[...] = start.astype(jnp.int32)
        run_ref[...] = jnp.zeros_like(run_ref)

    prefixes = _dot(jnp.concatenate([oh.astype(BF16) for oh in onehots], axis=0), before_ref[...])
    start = start_ref[:, 0:1].astype(F32)
    base = run_ref[:, 0:1] + start
    for kk in range(TOP_K):
        prefix = prefixes[kk * N_EXPERTS:(kk + 1) * N_EXPERTS, :]
        dest_ref[kk:kk + 1, :] = jnp.sum(onehots[kk] * (prefix + base), axis=0, keepdims=True).astype(jnp.int32)
        base = base + counts[kk]
    run_ref[...] = jnp.broadcast_to(base - start, run_ref.shape)


def _route(idx_t, totals, block_rows, tt):
    n = idx_t.shape[1]
    kern = functools.partial(_route_kernel, block_rows=block_rows)
    before = jnp.asarray(np.triu(np.ones((tt, tt)), 1), BF16)
    stat = pl.BlockSpec((N_EXPERTS, LANES), lambda t: (0, 0))
    return pl.pallas_call(
        kern,
        grid=(n // tt,),
        in_specs=[pl.BlockSpec((TOP_K, tt), lambda t: (0, t)),
                  pl.BlockSpec((tt, tt), lambda t: (0, 0)),
                  stat],
        out_specs=[pl.BlockSpec((TOP_K, tt), lambda t: (0, t)), stat, stat],
        out_shape=[jax.ShapeDtypeStruct((TOP_K, n), jnp.int32),
                   jax.ShapeDtypeStruct((N_EXPERTS, LANES), jnp.int32),
                   jax.ShapeDtypeStruct((N_EXPERTS, LANES), jnp.int32)],
        scratch_shapes=[pltpu.VMEM((N_EXPERTS, LANES), F32)],
        compiler_params=_cparams(("arbitrary",)),
        name="route_ranks",
    )(idx_t, before, totals)


def _to_row_tiles(ref, base, val):
    rows = val.shape[0]
    for s in range(SUBLANES):
        ref[pl.ds(base + s, rows, stride=SUBLANES), :] = val[:, s * LANES:(s + 1) * LANES]


def _from_row_tiles(ref, base, rows):
    return jnp.concatenate([ref[pl.ds(base + s, rows, stride=SUBLANES), :] for s in range(SUBLANES)], axis=1)


def _row_tile(ref, r):
    return ref.at[pl.ds(pl.multiple_of(r * SUBLANES, SUBLANES), SUBLANES), :]


def _row_tiles(ref, r, n):
    return ref.at[pl.ds(pl.multiple_of(r * SUBLANES, SUBLANES), n * SUBLANES), :]


def _dispatch_kernel(cnt_ref, start_ref, dest_ref, x1_ref, sc_ref, sh_ref, h_ref, u_ref, zero_ref, sems, *, block_rows):
    t = pl.program_id(0)
    nt = pl.num_programs(0)
    tt = x1_ref.shape[0]
    slot = t % 2
    ubase = pl.multiple_of(slot * (tt * SUBLANES), tt * SUBLANES)

    _to_row_tiles(u_ref, ubase, x1_ref[...] * (1.0 + sc_ref[0]) + sh_ref[0])

    def issue(i, _):
        for kk in range(TOP_K):
            d = dest_ref[0, 0, kk * tt + i]
            pltpu.make_async_copy(_row_tile(u_ref, slot * tt + i), _row_tile(h_ref, d),
                                  sems.at[slot]).start(priority=kk % DMA_QUEUES)
        return 0

    lax.fori_loop(0, tt, issue, 0, unroll=ISSUE_UNROLL)

    def drain(sl):
        for _ in range(TOP_K):
            pltpu.make_async_copy(_row_tiles(u_ref, sl * tt, tt), _row_tiles(h_ref, 0, tt), sems.at[sl]).wait()

    @pl.when(t > 0)
    def _():
        drain(1 - slot)

    @pl.when(t == nt - 1)
    def _():
        drain(slot)
        zero_ref[...] = jnp.zeros_like(zero_ref)

        def pad_expert(e, _):
            cnt = cnt_ref[e]
            rem = (block_rows - (cnt & (block_rows - 1))) & (block_rows - 1)
            pos = start_ref[e] + cnt
            size = block_rows // 2
            while size >= 1:
                take = rem & size

                @pl.when(take != 0)
                def _(pos=pos, size=size):
                    cp = pltpu.make_async_copy(_row_tiles(zero_ref, 0, size), _row_tiles(h_ref, pos, size), sems.at[2])
                    cp.start()
                    cp.wait()

                pos = pos + take
                size //= 2
            return 0

        lax.fori_loop(0, N_EXPERTS, pad_expert, 0)


def _dispatch(counts, starts, dest_flat, x1, sc2, sh2, n_rows, tt):
    n, d = x1.shape
    b = sc2.shape[0]
    tiles_per_b = (n // b) // tt
    per_b = lambda t, c, s: (t // tiles_per_b, 0, 0)
    grid_spec = pltpu.PrefetchScalarGridSpec(
        num_scalar_prefetch=2,
        grid=(n // tt,),
        in_specs=[pl.BlockSpec((1, 1, tt * TOP_K), lambda t, c, s: (t, 0, 0), memory_space=pltpu.SMEM),
                  pl.BlockSpec((tt, d), lambda t, c, s: (t, 0)),
                  pl.BlockSpec((1, 1, d), per_b),
                  pl.BlockSpec((1, 1, d), per_b)],
        out_specs=pl.BlockSpec(memory_space=pl.ANY),
        scratch_shapes=[pltpu.VMEM((2 * tt * SUBLANES, LANES), F32), pltpu.VMEM((MOE_ROWS // 2 * SUBLANES, LANES), F32),
                        pltpu.SemaphoreType.DMA((3,))],
    )
    return pl.pallas_call(
        functools.partial(_dispatch_kernel, block_rows=MOE_ROWS),
        grid_spec=grid_spec,
        out_shape=jax.ShapeDtypeStruct((n_rows * SUBLANES, LANES), F32),
        compiler_params=_cparams(("arbitrary",)),
        name="moe_dispatch",
    )(counts, starts, dest_flat, x1, sc2, sh2)


def _expert_kernel(be_ref, nu_ref, nxt_ref, h_ref, wgu_hbm, bgu_ref, wdn_hbm, bdn_ref, o_ref,
                   wgu_f_ref, wdn_f_ref, wgu_b_ref, wdn_b_ref, sems):
    j = pl.program_id(0)
    d = wdn_b_ref.shape[-1]
    bm = h_ref.shape[0] // SUBLANES

    def weight_copies(e):
        return (pltpu.make_async_copy(wgu_hbm.at[e], wgu_f_ref, sems.at[0]),
                pltpu.make_async_copy(wdn_hbm.at[e], wdn_f_ref, sems.at[1]))

    @pl.when(j < nu_ref[0])
    def _():
        @pl.when(j == 0)
        def _():
            for cp in weight_copies(be_ref[0]):
                cp.start()

        @pl.when(jnp.logical_or(j == 0, be_ref[j] != be_ref[jnp.maximum(j - 1, 0)]))
        def _():
            for cp in weight_copies(be_ref[j]):
                cp.wait()
            wgu_b_ref[...] = wgu_f_ref[...].astype(BF16)
            wdn_b_ref[...] = wdn_f_ref[...].astype(BF16)

            @pl.when(nxt_ref[j] >= 0)
            def _():
                for cp in weight_copies(nxt_ref[j]):
                    cp.start()

        gu = _dot(_from_row_tiles(h_ref, 0, bm).astype(BF16), wgu_b_ref[...]) + bgu_ref[0]
        gate = jnp.minimum(gu[:, :d], SWIGLU_LIMIT)
        up = jnp.clip(gu[:, d:], -SWIGLU_LIMIT, SWIGLU_LIMIT)
        act = (up + 1.0) * gate * _sigmoid(SWIGLU_ALPHA * gate)
        _to_row_tiles(o_ref, 0, _dot(act.astype(BF16), wdn_b_ref[...]) + bdn_ref[0])


def _experts(h_sorted, block_expert, n_used, next_expert, wgu, bgu, wdn, bdn):
    n_rows = h_sorted.shape[0] // SUBLANES
    e, d, _ = wdn.shape
    bm = MOE_ROWS
    n_blocks = n_rows // bm
    blk = lambda j, be, nu, nx: (jnp.minimum(j, nu[0] - 1), 0)
    exp = lambda j, be, nu, nx: (be[jnp.minimum(j, nu[0] - 1)], 0, 0)
    grid_spec = pltpu.PrefetchScalarGridSpec(
        num_scalar_prefetch=3,
        grid=(n_blocks,),
        in_specs=[pl.BlockSpec((bm * SUBLANES, LANES), blk),
                  pl.BlockSpec(memory_space=pl.ANY),
                  pl.BlockSpec((1, 1, 2 * d), exp),
                  pl.BlockSpec(memory_space=pl.ANY),
                  pl.BlockSpec((1, 1, d), exp)],
        out_specs=pl.BlockSpec((bm * SUBLANES, LANES), blk),
        scratch_shapes=[pltpu.VMEM((d, 2 * d), F32), pltpu.VMEM((d, d), F32),
                        pltpu.VMEM((d, 2 * d), BF16), pltpu.VMEM((d, d), BF16), pltpu.SemaphoreType.DMA((2,))],
    )
    return pl.pallas_call(
        _expert_kernel,
        grid_spec=grid_spec,
        out_shape=jax.ShapeDtypeStruct((n_rows * SUBLANES, LANES), F32),
        compiler_params=_cparams(("arbitrary",), VMEM_LIMIT),
        name="moe_experts",
    )(block_expert, n_used, next_expert, h_sorted, wgu, bgu.reshape(e, 1, 2 * d), wdn, bdn.reshape(e, 1, d))


def _combine_kernel(dest_ref, dest_next_ref, wt_ref, x1_ref, g2_ref, lg_ref, lbias_ref, o_hbm, y_ref, buf0_ref, buf1_ref, sems):
    t = pl.program_id(0)
    nt = pl.num_programs(0)
    tt = x1_ref.shape[0]
    slot = t % 2

    bufs = (buf0_ref, buf1_ref)

    def issue_row(d_ref, sl, i):
        for kk in range(TOP_K):
            d = d_ref[0, 0, kk * tt + i]
            pltpu.make_async_copy(_row_tile(o_hbm, d), _row_tile(bufs[sl], kk * tt + i),
                                  sems.at[sl]).start(priority=kk % DMA_QUEUES)

    def drain(sl):
        for kk in range(TOP_K):
            pltpu.make_async_copy(_row_tiles(o_hbm, 0, tt), _row_tiles(bufs[sl], kk * tt, tt), sems.at[sl]).wait()

    @pl.when(t == 0)
    def _():
        def first(i, _):
            issue_row(dest_ref, 0, i)
            return 0

        lax.fori_loop(0, tt, first, 0, unroll=ISSUE_UNROLL)

    def fused(sl):
        drain(sl)
        for r0 in range(0, tt, COMBINE_ROWS):
            for i in range(r0, r0 + COMBINE_ROWS):
                issue_row(dest_next_ref, 1 - sl, i)
            rows = pl.ds(r0, COMBINE_ROWS)
            ff = None
            for kk in range(TOP_K):
                term = wt_ref[rows, kk:kk + 1] * _from_row_tiles(bufs[sl], (kk * tt + r0) * SUBLANES, COMBINE_ROWS)
                ff = term if ff is None else ff + term
            y_ref[rows, :] = _layer_norm(DEEPNORM_ALPHA * x1_ref[rows, :] + g2_ref[0] * ff, lg_ref[...], lbias_ref[...])

        @pl.when(t == nt - 1)
        def _():
            drain(1 - sl)

    for sl in range(2):
        pl.when(slot == sl)(functools.partial(fused, sl))


def _combine(dest_flat, wt, x1, g2, lg, lbias, o_sorted, tt):
    n, d = x1.shape
    b = g2.shape[0]
    nt = n // tt
    tiles_per_b = (n // b) // tt
    const = lambda t: (0, 0)
    dest_spec = lambda im: pl.BlockSpec((1, 1, tt * TOP_K), im, memory_space=pltpu.SMEM)
    return pl.pallas_call(
        _combine_kernel,
        grid=(nt,),
        in_specs=[dest_spec(lambda t: (t, 0, 0)),
                  dest_spec(lambda t: (jnp.minimum(t + 1, nt - 1), 0, 0)),
                  pl.BlockSpec((tt, TOP_K), lambda t: (t, 0)),
                  pl.BlockSpec((tt, d), lambda t: (t, 0)),
                  pl.BlockSpec((1, 1, d), lambda t: (t // tiles_per_b, 0, 0)),
                  pl.BlockSpec((1, d), const),
                  pl.BlockSpec((1, d), const),
                  pl.BlockSpec(memory_space=pl.ANY)],
        out_specs=pl.BlockSpec((tt, d), lambda t: (t, 0)),
        out_shape=jax.ShapeDtypeStruct((n, d), F32),
        scratch_shapes=[pltpu.VMEM((TOP_K * tt * SUBLANES, LANES), F32)] * 2 + [pltpu.SemaphoreType.DMA((2,))],
        compiler_params=_cparams(("arbitrary",), VMEM_LIMIT),
        name="moe_combine",
    )(dest_flat, dest_flat, wt, x1, g2, lg, lbias, o_sorted)


def kernel(x, c, ctx, c_ctx, w_ada, b_ada, w_in, lb_raw, hg_norm_g, w_four_out, w_hg_out, w_o, ln1_g, ln1_b,
           w_router, b_router, w_gate_up, b_gate_up, w_down, b_down, ln2_g, ln2_b):
    b, s, d = x.shape
    lc = ctx.shape[1]
    n = b * s
    f_w = F_GROUPS * F_GROUP_DIM
    hk = HG_HEADS * HG_D
    assert w_ada.shape[0] == DEPTH and s % GRID_W == 0 and s % (HG_TILE * HG_GROUP) == 0 and lc % HG_TILE == 0
    assert s % ROW_TILE == 0 and ROW_TILE % MOE_TOKENS == 0 and MOE_TOKENS % COMBINE_ROWS == 0

    pad = (-(b + 1)) % 8
    c_rows = jnp.concatenate([c, c_ctx[None, :], jnp.zeros((pad, d), F32)], axis=0)
    mod = _ada(c_rows, w_ada[0], b_ada[0]).reshape(b + 1 + pad, N_MOD, 1, d)
    shift1, scale1, gate1, shift2, scale2, gate2 = (mod[:b, i] for i in range(N_MOD))
    cshift1 = jnp.broadcast_to(mod[b, 0], (b, 1, d))
    cscale1 = jnp.broadcast_to(mod[b, 1], (b, 1, d))

    lower_bounds = jnp.cumsum(jax.nn.softmax(lb_raw.astype(F32), axis=0), axis=0)[0]

    w_in_b = w_in[0].astype(BF16)
    o_f, o_q, o_z, o_v, o_g, o_gt = f_w, f_w + hk, f_w + 3 * hk, f_w + 4 * hk, f_w + 5 * hk, f_w + 5 * hk + 2 * d
    lb_row = lower_bounds.reshape(1, 2 * hk)
    pf, pq, pk, phi, plo, pv, pg, gates = _inproj(
        x, scale1, shift1, lb_row, w_in_b,
        [(0, f_w, "cast"), (o_f, hk, "silu"), (o_q, 2 * hk, "forget"), (o_z, hk, "cast"), (o_v, hk, "cast"),
         (o_g, 2 * d, "cast")], tm=ROW_TILE)
    w_ctx = w_in_b[:, o_q:o_v]
    ctx_tile = ROW_TILE if (b * lc) % ROW_TILE == 0 else lc
    ck, chi, clo, cv = (t.reshape(b, lc, -1) for t in _inproj(
        ctx.reshape(1, b * lc, d), cscale1[:1], cshift1[:1], lb_row, w_ctx,
        [(0, 2 * hk, "forget"), (2 * hk, hk, "cast")], tm=ctx_tile))

    y_h = _hgrn(pq, pk, phi, plo, pv, pg, ck, chi, clo, cv, hg_norm_g[0].reshape(1, hk))
    y_f = _fourier(pf)

    wr = w_router[0].T.astype(BF16)
    br = b_router[0].reshape(N_EXPERTS, 1)
    x1, idx_t, wt_t, totals = _merge(y_f, y_h, gates, x, gate1, scale2, shift2,
                         w_four_out[0].astype(BF16), w_hg_out[0].astype(BF16), w_o[0].astype(BF16),
                         ln1_g[0].reshape(1, d), ln1_b[0].reshape(1, d), wr, br, tm=ROW_TILE)

    dest_t, counts, starts = _route(idx_t, totals, MOE_ROWS, ROW_TILE)
    n_blocks = -(-(n * TOP_K) // MOE_ROWS) + N_EXPERTS
    n_rows = n_blocks * MOE_ROWS
    cnt = counts[:, 0]
    st = starts[:, 0]
    pad_ends = st + (cnt + MOE_ROWS - 1) // MOE_ROWS * MOE_ROWS
    block_first_row = jnp.arange(n_blocks, dtype=jnp.int32) * MOE_ROWS
    block_expert = jnp.minimum(jnp.sum((pad_ends[None, :] <= block_first_row[:, None]).astype(jnp.int32), axis=1),
                               N_EXPERTS - 1)
    n_used = (pad_ends[-1] // MOE_ROWS).astype(jnp.int32).reshape(1)
    group_end_block = pad_ends[block_expert] // MOE_ROWS
    next_expert = jnp.where(group_end_block < n_used[0], block_expert[jnp.minimum(group_end_block, n_blocks - 1)], -1)

    tt = MOE_TOKENS
    dest_flat = dest_t.reshape(TOP_K, n // tt, tt).transpose(1, 0, 2).reshape(n // tt, 1, TOP_K * tt)
    x1_2d = x1.reshape(n, d)
    h_sorted = _dispatch(cnt, st, dest_flat, x1_2d, scale2, shift2, n_rows, tt)
    o_sorted = _experts(h_sorted, block_expert, n_used, next_expert, w_gate_up[0], b_gate_up[0], w_down[0], b_down[0])
    out = _combine(dest_flat, wt_t.T, x1_2d, gate2, ln2_g[0].reshape(1, d), ln2_b[0].reshape(1, d),
                   o_sorted, tt)
    return out.reshape(b, s, d)
```
